```python
import math
import jax, jax.numpy as jnp
from jax import lax
import numpy as np

D_MODEL = 1024
BATCH = 8
SEQ = 8192
DEPTH = 2

MEM_TOKENS = 256
DN_HEADS = 4
DN_HEAD_DIM = 128
DN_WIDTH = DN_HEADS * DN_HEAD_DIM
DN_CONV = 4
DN_CHUNK = 64
SWA_HEADS = 4
SWA_HEAD_DIM = 64
SWA_WIDTH = SWA_HEADS * SWA_HEAD_DIM
SWA_PATTERNS = ((128, 1), (512, 4), (2048, 16))
SWA_BLOCK = 128
SG_GROUPS = 4
SG_GROUP_DIM = 64
SG_WIDTH = SG_GROUPS * SG_GROUP_DIM
SG_CHUNK = 128
IN_SIZES = (3 * DN_WIDTH, DN_WIDTH, DN_HEADS, DN_HEADS, 3 * SWA_WIDTH, 2 * SG_WIDTH)
IN_COLS = 3 * DN_WIDTH + DN_WIDTH + 2 * DN_HEADS + 3 * SWA_WIDTH + 2 * SG_WIDTH
MIX_WIDTH = DN_WIDTH + SWA_WIDTH + SG_WIDTH
XA_HEADS = 4
XA_HEAD_DIM = D_MODEL // XA_HEADS
D_FF = 2816
NORM_EPS = 1e-6

kernel_name = 'hymba_style_delta_dilated_gmlp_hybrid'


def rms_norm(x, gain):
    xf = x.astype(jnp.float32)
    y = xf * lax.rsqrt(jnp.mean(xf * xf, axis=-1, keepdims=True) + NORM_EPS)
    return (y * gain.astype(jnp.float32)).astype(x.dtype)


def layer_norm(x, gain, bias):
    xf = x.astype(jnp.float32)
    mu = jnp.mean(xf, axis=-1, keepdims=True)
    var = jnp.mean(jnp.square(xf - mu), axis=-1, keepdims=True)
    y = (xf - mu) * lax.rsqrt(var + NORM_EPS) * gain.astype(jnp.float32) + bias.astype(jnp.float32)
    return y.astype(x.dtype)


def swiglu(x, w_gate_up, w_down):
    gate, up = jnp.split(x @ w_gate_up, 2, axis=-1)
    return (jax.nn.silu(gate) * up) @ w_down


def alibi_slopes(n):
    return jnp.asarray([2.0 ** (-8.0 * (i + 1) / n) for i in range(n)], dtype=jnp.float32)


def causal_depthwise_conv(x, w):
    c = x.shape[-1]
    return lax.conv_general_dilated(
        x, w[:, None, :], window_strides=(1,), padding=[(w.shape[0] - 1, 0)],
        dimension_numbers=('NWC', 'WIO', 'NWC'), feature_group_count=c)


def l2_normalize(x):
    return x * lax.rsqrt(jnp.sum(x * x, axis=-1, keepdims=True) + NORM_EPS)


def chunk_gated_delta_rule(q, k, v, g, beta):
    B, S, H, Dk = q.shape
    Dv = v.shape[-1]
    C = DN_CHUNK
    N = S // C

    def to_chunks(t):
        return t.reshape(B, N, C, H, -1).transpose(0, 3, 1, 2, 4)

    q, k, v = to_chunks(q), to_chunks(k), to_chunks(v)
    g = g.reshape(B, N, C, H).transpose(0, 3, 1, 2)
    beta = beta.reshape(B, N, C, H).transpose(0, 3, 1, 2)
    gc = jnp.cumsum(g, axis=-1)
    tri_incl = jnp.tril(jnp.ones((C, C), dtype=bool))
    tri_strict = jnp.tril(jnp.ones((C, C), dtype=bool), -1)
    diff = gc[..., :, None] - gc[..., None, :]
    decay = jnp.where(tri_incl, jnp.exp(jnp.where(tri_incl, diff, 0.0)), 0.0)
    kb = k * beta[..., None]
    lower = jnp.where(tri_strict, jnp.einsum('bhncd,bhnsd->bhncs', kb, k) * decay, 0.0)
    eye = jnp.eye(C, dtype=q.dtype)
    t_inv = lax.linalg.triangular_solve(eye + lower, jnp.broadcast_to(eye, lower.shape),
                                        left_side=True, lower=True, unit_diagonal=True)
    u = jnp.einsum('bhncs,bhnse->bhnce', t_inv, v * beta[..., None])
    w = jnp.einsum('bhncs,bhnsd->bhncd', t_inv, kb * jnp.exp(gc)[..., None])
    a_qk = jnp.where(tri_incl, jnp.einsum('bhncd,bhnsd->bhncs', q, k) * decay, 0.0)
    g_last = gc[..., -1]
    k_tail = k * jnp.exp(g_last[..., None] - gc)[..., None]
    q_dec = q * jnp.exp(gc)[..., None]

    xs = tuple(jnp.moveaxis(t, 2, 0) for t in (q_dec, k_tail, u, w, a_qk, g_last))

    def step(state, inp):
        qd, kt, uc, wc, aqk, gl = inp
        v_new = uc - jnp.einsum('bhcd,bhde->bhce', wc, state)
        o = jnp.einsum('bhcd,bhde->bhce', qd, state) + jnp.einsum('bhcs,bhse->bhce', aqk, v_new)
        state = state * jnp.exp(gl)[..., None, None] + jnp.einsum('bhcd,bhce->bhde', kt, v_new)
        return state, o

    state0 = jnp.zeros((B, H, Dk, Dv), dtype=q.dtype)
    _, o = lax.scan(step, state0, xs)
    return o.transpose(1, 0, 3, 2, 4).reshape(B, S, H, Dv)


def gated_deltanet(qkv, z, b_raw, a_raw, conv_w, a_log, dt_bias, out_gain):
    B, S, _ = qkv.shape
    f32 = jnp.float32
    qkv = jax.nn.silu(causal_depthwise_conv(qkv.astype(f32), conv_w.astype(f32)))
    q, k, v = jnp.split(qkv, 3, axis=-1)
    q = l2_normalize(q.reshape(B, S, DN_HEADS, DN_HEAD_DIM)) * (DN_HEAD_DIM ** -0.5)
    k = l2_normalize(k.reshape(B, S, DN_HEADS, DN_HEAD_DIM))
    v = v.reshape(B, S, DN_HEADS, DN_HEAD_DIM)
    beta = jax.nn.sigmoid(b_raw.astype(f32))
    g = -jnp.exp(a_log.astype(f32)) * jax.nn.softplus(a_raw.astype(f32) + dt_bias.astype(f32))
    o = chunk_gated_delta_rule(q, k, v, g, beta)
    o = o * lax.rsqrt(jnp.mean(o * o, axis=-1, keepdims=True) + NORM_EPS) * out_gain.astype(f32)
    o = o * jax.nn.silu(z.astype(f32).reshape(B, S, DN_HEADS, DN_HEAD_DIM))
    return o.reshape(B, S, DN_WIDTH)


def dilated_window_attention(q, k, v, slopes, window, dilation):
    B, S, H, Dh = q.shape
    f32 = jnp.float32
    span = window // dilation
    L = S // dilation
    nb = -(-L // SWA_BLOCK)
    Lp = nb * SWA_BLOCK

    def to_sub(t):
        t = t.reshape(B, L, dilation, H, Dh).transpose(0, 2, 1, 3, 4)
        t = jnp.pad(t, ((0, 0), (0, 0), (0, Lp - L), (0, 0), (0, 0)))
        return t.reshape(B, dilation, nb, SWA_BLOCK, H, Dh)

    def with_prev(t):
        prev = jnp.pad(t[:, :, :-1], ((0, 0), (0, 0), (1, 0), (0, 0), (0, 0), (0, 0)))
        return jnp.concatenate([prev, t], axis=3)

    qs = to_sub(q)
    kw = with_prev(to_sub(k))
    vw = with_prev(to_sub(v))
    qi = jnp.arange(SWA_BLOCK)[:, None]
    kj = jnp.arange(2 * SWA_BLOCK)[None, :]
    rel = SWA_BLOCK + qi - kj
    first = (jnp.arange(nb) == 0)[:, None, None] & (kj < SWA_BLOCK)[None]
    valid = ((rel >= 0) & (rel <= span))[None] & ~first
    bias = -slopes[:, None, None] * (rel * dilation).astype(f32)[None]
    s = jnp.einsum('bdnqhe,bdnkhe->bdnhqk', qs, kw).astype(f32) * (Dh ** -0.5) + bias
    s = jnp.where(valid[:, None], s, -jnp.inf)
    m = jnp.max(s, axis=-1, keepdims=True)
    p = jnp.exp(s - m)
    den = jnp.sum(p, axis=-1)
    o = jnp.einsum('bdnhqk,bdnkhe->bdnqhe', p, vw.astype(f32)) / jnp.swapaxes(den, -1, -2)[..., None]
    lse = jnp.swapaxes(m[..., 0] + jnp.log(den), -1, -2)

    def from_sub(t):
        t = t.reshape((B, dilation, Lp) + t.shape[4:])[:, :, :L]
        t = jnp.swapaxes(t, 1, 2)
        return t.reshape((B, S) + t.shape[3:])

    return from_sub(o), from_sub(lse)


def dilated_mixture_attention(qkv):
    B, S, _ = qkv.shape
    q, k, v = (t.reshape(B, S, SWA_HEADS, SWA_HEAD_DIM) for t in jnp.split(qkv, 3, axis=-1))
    slopes = alibi_slopes(SWA_HEADS)
    outs, lses = [], []
    for window, dilation in SWA_PATTERNS:
        o, lse = dilated_window_attention(q, k, v, slopes, window, dilation)
        outs.append(o)
        lses.append(lse)
    alpha = jax.nn.softmax(jnp.stack(lses), axis=0)
    o = jnp.einsum('pbsh,pbshe->bshe', alpha, jnp.stack(outs))
    return o.reshape(B, S, SWA_WIDTH)


def chunked_spatial_gating(uv, ln_gain, ln_bias, w_spatial, b_spatial):
    B, S, _ = uv.shape
    n_chunks = S // SG_CHUNK
    u, v = jnp.split(jax.nn.gelu(uv), 2, axis=-1)
    v = layer_norm(v, ln_gain, ln_bias).reshape(B, n_chunks, SG_CHUNK, SG_GROUPS, SG_GROUP_DIM)
    w_causal = jnp.tril(w_spatial)
    mixed = jnp.einsum('gts,bnsgd->bntgd', w_causal, v) + b_spatial.T[:, :, None]
    gated = u.reshape(B, n_chunks, SG_CHUNK, SG_GROUPS, SG_GROUP_DIM) * mixed
    return gated.reshape(B, S, SG_WIDTH)


def parallel_mixing(n, w_in, conv_w, a_log, dt_bias, dn_gain, sg_gain, sg_bias, w_sp, b_sp, w_out):
    proj = n @ w_in
    cuts = []
    acc = 0
    for size in IN_SIZES[:-1]:
        acc += size
        cuts.append(acc)
    dn_qkv, dn_z, dn_b, dn_a, swa_qkv, sg_uv = jnp.split(proj, cuts, axis=-1)
    out_a = gated_deltanet(dn_qkv, dn_z, dn_b, dn_a, conv_w, a_log, dt_bias, dn_gain)
    out_b = dilated_mixture_attention(swa_qkv)
    out_c = chunked_spatial_gating(sg_uv, sg_gain, sg_bias, w_sp, b_sp)
    merged = jnp.concatenate([out_a.astype(n.dtype), out_b.astype(n.dtype), out_c.astype(n.dtype)], axis=-1)
    return merged @ w_out


def memory_cross_attention(h, mem, w_q, w_kv, w_o):
    B, S, _ = h.shape
    M = mem.shape[1]
    q = (h @ w_q).reshape(B, S, XA_HEADS, XA_HEAD_DIM)
    k, v = jnp.split(mem @ w_kv, 2, axis=-1)
    k = k.reshape(B, M, XA_HEADS, XA_HEAD_DIM)
    v = v.reshape(B, M, XA_HEADS, XA_HEAD_DIM)
    s = jnp.einsum('bshe,bmhe->bhsm', q, k).astype(jnp.float32) * (XA_HEAD_DIM ** -0.5)
    p = jax.nn.softmax(s, axis=-1)
    o = jnp.einsum('bhsm,bmhe->bshe', p.astype(v.dtype), v).reshape(B, S, D_MODEL)
    return o @ w_o


def _fwd_setup_inputs(seed: int = 0) -> dict:
    key = jax.random.key(seed)
    ks = jax.random.split(key, 32)
    f32 = jnp.float32

    def nrm(k, shape, scale):
        return jax.random.normal(k, shape, f32) * scale

    def gain(k, shape):
        return 1.0 + 0.02 * jax.random.normal(k, shape, f32)

    dt = jnp.exp(jax.random.uniform(ks[9], (DEPTH, DN_HEADS), f32, math.log(1e-3), math.log(1e-1)))
    return {
        'x': nrm(ks[0], (BATCH, SEQ, D_MODEL), 1.0),
        'mem': nrm(ks[1], (BATCH, MEM_TOKENS, D_MODEL), 1.0),
        'ffn1_norm': gain(ks[2], (DEPTH, D_MODEL)),
        'ffn1_w_gate_up': nrm(ks[3], (DEPTH, D_MODEL, 2 * D_FF), D_MODEL ** -0.5),
        'ffn1_w_down': nrm(ks[4], (DEPTH, D_FF, D_MODEL), D_FF ** -0.5),
        'mix_norm': gain(ks[5], (DEPTH, D_MODEL)),
        'mix_w_in': nrm(ks[6], (DEPTH, D_MODEL, IN_COLS), D_MODEL ** -0.5),
        'dn_conv_w': nrm(ks[7], (DEPTH, DN_CONV, 3 * DN_WIDTH), DN_CONV ** -0.5),
        'dn_a_log': jnp.log(jax.random.uniform(ks[8], (DEPTH, DN_HEADS), f32, 1.0, 16.0)),
        'dn_dt_bias': dt + jnp.log(-jnp.expm1(-dt)),
        'dn_out_norm': gain(ks[10], (DEPTH, DN_HEAD_DIM)),
        'sg_norm_gain': gain(ks[11], (DEPTH, SG_WIDTH)),
        'sg_norm_bias': nrm(ks[12], (DEPTH, SG_WIDTH), 0.02),
        'sg_w_spatial': nrm(ks[13], (DEPTH, SG_GROUPS, SG_CHUNK, SG_CHUNK), SG_CHUNK ** -0.5),
        'sg_b_spatial': gain(ks[14], (DEPTH, SG_GROUPS, SG_CHUNK)),
        'mix_w_out': nrm(ks[15], (DEPTH, MIX_WIDTH, D_MODEL), MIX_WIDTH ** -0.5),
        'xa_norm': gain(ks[16], (DEPTH, D_MODEL)),
        'xa_mem_norm': gain(ks[17], (DEPTH, D_MODEL)),
        'xa_w_q': nrm(ks[18], (DEPTH, D_MODEL, D_MODEL), D_MODEL ** -0.5),
        'xa_w_kv': nrm(ks[19], (DEPTH, D_MODEL, 2 * D_MODEL), D_MODEL ** -0.5),
        'xa_w_o': nrm(ks[20], (DEPTH, D_MODEL, D_MODEL), D_MODEL ** -0.5),
        'ffn2_norm': gain(ks[21], (DEPTH, D_MODEL)),
        'ffn2_w_gate_up': nrm(ks[22], (DEPTH, D_MODEL, 2 * D_FF), D_MODEL ** -0.5),
        'ffn2_w_down': nrm(ks[23], (DEPTH, D_FF, D_MODEL), D_FF ** -0.5),
        'final_norm': gain(ks[24], (D_MODEL,)),
    }


def _fwd_reference(x, mem, ffn1_norm, ffn1_w_gate_up, ffn1_w_down, mix_norm, mix_w_in, dn_conv_w,
              dn_a_log, dn_dt_bias, dn_out_norm, sg_norm_gain, sg_norm_bias, sg_w_spatial,
              sg_b_spatial, mix_w_out, xa_norm, xa_mem_norm, xa_w_q, xa_w_kv, xa_w_o,
              ffn2_norm, ffn2_w_gate_up, ffn2_w_down, final_norm):
    h = x
    for i in range(DEPTH):
        h = h + 0.5 * swiglu(rms_norm(h, ffn1_norm[i]), ffn1_w_gate_up[i], ffn1_w_down[i])
        h = h + parallel_mixing(rms_norm(h, mix_norm[i]), mix_w_in[i], dn_conv_w[i], dn_a_log[i],
                                dn_dt_bias[i], dn_out_norm[i], sg_norm_gain[i], sg_norm_bias[i],
                                sg_w_spatial[i], sg_b_spatial[i], mix_w_out[i])
        h = h + memory_cross_attention(rms_norm(h, xa_norm[i]), rms_norm(mem, xa_mem_norm[i]),
                                       xa_w_q[i], xa_w_kv[i], xa_w_o[i])
        h = h + 0.5 * swiglu(rms_norm(h, ffn2_norm[i]), ffn2_w_gate_up[i], ffn2_w_down[i])
    return rms_norm(h, final_norm)


import jax as _jax
import jax.numpy as _jnp

TWIN_FORMAT = 'train_step'
FWD_PARAMS = ['x', 'mem', 'ffn1_norm', 'ffn1_w_gate_up', 'ffn1_w_down', 'mix_norm', 'mix_w_in', 'dn_conv_w', 'dn_a_log', 'dn_dt_bias', 'dn_out_norm', 'sg_norm_gain', 'sg_norm_bias', 'sg_w_spatial', 'sg_b_spatial', 'mix_w_out', 'xa_norm', 'xa_mem_norm', 'xa_w_q', 'xa_w_kv', 'xa_w_o', 'ffn2_norm', 'ffn2_w_gate_up', 'ffn2_w_down', 'final_norm']
TWIN_WEIGHTS = ['ffn1_norm', 'ffn1_w_gate_up', 'ffn1_w_down', 'mix_norm', 'mix_w_in', 'dn_conv_w', 'dn_a_log', 'dn_dt_bias', 'dn_out_norm', 'sg_norm_gain', 'sg_norm_bias', 'sg_w_spatial', 'sg_b_spatial', 'mix_w_out', 'xa_norm', 'xa_mem_norm', 'xa_w_q', 'xa_w_kv', 'xa_w_o', 'ffn2_norm', 'ffn2_w_gate_up', 'ffn2_w_down', 'final_norm']
TWIN_DIFF_INPUT = 'x'
TWIN_INPUTS = ['x', 'mem', 'ffn1_norm', 'ffn1_w_gate_up', 'ffn1_w_down', 'mix_norm', 'mix_w_in', 'dn_conv_w', 'dn_a_log', 'dn_dt_bias', 'dn_out_norm', 'sg_norm_gain', 'sg_norm_bias', 'sg_w_spatial', 'sg_b_spatial', 'mix_w_out', 'xa_norm', 'xa_mem_norm', 'xa_w_q', 'xa_w_kv', 'xa_w_o', 'ffn2_norm', 'ffn2_w_gate_up', 'ffn2_w_down', 'final_norm', 'loss_target', 'm_ffn1_norm', 'm_ffn1_w_gate_up', 'm_ffn1_w_down', 'm_mix_norm', 'm_mix_w_in', 'm_dn_conv_w', 'm_dn_a_log', 'm_dn_dt_bias', 'm_dn_out_norm', 'm_sg_norm_gain', 'm_sg_norm_bias', 'm_sg_w_spatial', 'm_sg_b_spatial', 'm_mix_w_out', 'm_xa_norm', 'm_xa_mem_norm', 'm_xa_w_q', 'm_xa_w_kv', 'm_xa_w_o', 'm_ffn2_norm', 'm_ffn2_w_gate_up', 'm_ffn2_w_down', 'm_final_norm', 'v_ffn1_norm', 'v_ffn1_w_gate_up', 'v_ffn1_w_down', 'v_mix_norm', 'v_mix_w_in', 'v_dn_conv_w', 'v_dn_a_log', 'v_dn_dt_bias', 'v_dn_out_norm', 'v_sg_norm_gain', 'v_sg_norm_bias', 'v_sg_w_spatial', 'v_sg_b_spatial', 'v_mix_w_out', 'v_xa_norm', 'v_xa_mem_norm', 'v_xa_w_q', 'v_xa_w_kv', 'v_xa_w_o', 'v_ffn2_norm', 'v_ffn2_w_gate_up', 'v_ffn2_w_down', 'v_final_norm']
TWIN_OUTPUTS = ['loss', 'grad_x', 'grad_ffn1_norm', 'grad_ffn1_w_gate_up', 'grad_ffn1_w_down', 'grad_mix_norm', 'grad_mix_w_in', 'grad_dn_conv_w', 'grad_dn_a_log', 'grad_dn_dt_bias', 'grad_dn_out_norm', 'grad_sg_norm_gain', 'grad_sg_norm_bias', 'grad_sg_w_spatial', 'grad_sg_b_spatial', 'grad_mix_w_out', 'grad_xa_norm', 'grad_xa_mem_norm', 'grad_xa_w_q', 'grad_xa_w_kv', 'grad_xa_w_o', 'grad_ffn2_norm', 'grad_ffn2_w_gate_up', 'grad_ffn2_w_down', 'grad_final_norm', 'delta_ffn1_norm', 'delta_ffn1_w_gate_up', 'delta_ffn1_w_down', 'delta_mix_norm', 'delta_mix_w_in', 'delta_dn_conv_w', 'delta_dn_a_log', 'delta_dn_dt_bias', 'delta_dn_out_norm', 'delta_sg_norm_gain', 'delta_sg_norm_bias', 'delta_sg_w_spatial', 'delta_sg_b_spatial', 'delta_mix_w_out', 'delta_xa_norm', 'delta_xa_mem_norm', 'delta_xa_w_q', 'delta_xa_w_kv', 'delta_xa_w_o', 'delta_ffn2_norm', 'delta_ffn2_w_gate_up', 'delta_ffn2_w_down', 'delta_final_norm', 'new_m_ffn1_norm', 'new_m_ffn1_w_gate_up', 'new_m_ffn1_w_down', 'new_m_mix_norm', 'new_m_mix_w_in', 'new_m_dn_conv_w', 'new_m_dn_a_log', 'new_m_dn_dt_bias', 'new_m_dn_out_norm', 'new_m_sg_norm_gain', 'new_m_sg_norm_bias', 'new_m_sg_w_spatial', 'new_m_sg_b_spatial', 'new_m_mix_w_out', 'new_m_xa_norm', 'new_m_xa_mem_norm', 'new_m_xa_w_q', 'new_m_xa_w_kv', 'new_m_xa_w_o', 'new_m_ffn2_norm', 'new_m_ffn2_w_gate_up', 'new_m_ffn2_w_down', 'new_m_final_norm', 'new_v_ffn1_norm', 'new_v_ffn1_w_gate_up', 'new_v_ffn1_w_down', 'new_v_mix_norm', 'new_v_mix_w_in', 'new_v_dn_conv_w', 'new_v_dn_a_log', 'new_v_dn_dt_bias', 'new_v_dn_out_norm', 'new_v_sg_norm_gain', 'new_v_sg_norm_bias', 'new_v_sg_w_spatial', 'new_v_sg_b_spatial', 'new_v_mix_w_out', 'new_v_xa_norm', 'new_v_xa_mem_norm', 'new_v_xa_w_q', 'new_v_xa_w_kv', 'new_v_xa_w_o', 'new_v_ffn2_norm', 'new_v_ffn2_w_gate_up', 'new_v_ffn2_w_down', 'new_v_final_norm']
TWIN_LEAF_KINDS = {'loss': 'loss', 'grad_x': 'grad_x', 'grad_ffn1_norm': 'grad_w', 'grad_ffn1_w_gate_up': 'grad_w', 'grad_ffn1_w_down': 'grad_w', 'grad_mix_norm': 'grad_w', 'grad_mix_w_in': 'grad_w', 'grad_dn_conv_w': 'grad_w', 'grad_dn_a_log': 'grad_w', 'grad_dn_dt_bias': 'grad_w', 'grad_dn_out_norm': 'grad_w', 'grad_sg_norm_gain': 'grad_w', 'grad_sg_norm_bias': 'grad_w', 'grad_sg_w_spatial': 'grad_w', 'grad_sg_b_spatial': 'grad_w', 'grad_mix_w_out': 'grad_w', 'grad_xa_norm': 'grad_w', 'grad_xa_mem_norm': 'grad_w', 'grad_xa_w_q': 'grad_w', 'grad_xa_w_kv': 'grad_w', 'grad_xa_w_o': 'grad_w', 'grad_ffn2_norm': 'grad_w', 'grad_ffn2_w_gate_up': 'grad_w', 'grad_ffn2_w_down': 'grad_w', 'grad_final_norm': 'grad_w', 'delta_ffn1_norm': 'delta_w', 'delta_ffn1_w_gate_up': 'delta_w', 'delta_ffn1_w_down': 'delta_w', 'delta_mix_norm': 'delta_w', 'delta_mix_w_in': 'delta_w', 'delta_dn_conv_w': 'delta_w', 'delta_dn_a_log': 'delta_w', 'delta_dn_dt_bias': 'delta_w', 'delta_dn_out_norm': 'delta_w', 'delta_sg_norm_gain': 'delta_w', 'delta_sg_norm_bias': 'delta_w', 'delta_sg_w_spatial': 'delta_w', 'delta_sg_b_spatial': 'delta_w', 'delta_mix_w_out': 'delta_w', 'delta_xa_norm': 'delta_w', 'delta_xa_mem_norm': 'delta_w', 'delta_xa_w_q': 'delta_w', 'delta_xa_w_kv': 'delta_w', 'delta_xa_w_o': 'delta_w', 'delta_ffn2_norm': 'delta_w', 'delta_ffn2_w_gate_up': 'delta_w', 'delta_ffn2_w_down': 'delta_w', 'delta_final_norm': 'delta_w', 'new_m_ffn1_norm': 'new_m', 'new_m_ffn1_w_gate_up': 'new_m', 'new_m_ffn1_w_down': 'new_m', 'new_m_mix_norm': 'new_m', 'new_m_mix_w_in': 'new_m', 'new_m_dn_conv_w': 'new_m', 'new_m_dn_a_log': 'new_m', 'new_m_dn_dt_bias': 'new_m', 'new_m_dn_out_norm': 'new_m', 'new_m_sg_norm_gain': 'new_m', 'new_m_sg_norm_bias': 'new_m', 'new_m_sg_w_spatial': 'new_m', 'new_m_sg_b_spatial': 'new_m', 'new_m_mix_w_out': 'new_m', 'new_m_xa_norm': 'new_m', 'new_m_xa_mem_norm': 'new_m', 'new_m_xa_w_q': 'new_m', 'new_m_xa_w_kv': 'new_m', 'new_m_xa_w_o': 'new_m', 'new_m_ffn2_norm': 'new_m', 'new_m_ffn2_w_gate_up': 'new_m', 'new_m_ffn2_w_down': 'new_m', 'new_m_final_norm': 'new_m', 'new_v_ffn1_norm': 'new_v', 'new_v_ffn1_w_gate_up': 'new_v', 'new_v_ffn1_w_down': 'new_v', 'new_v_mix_norm': 'new_v', 'new_v_mix_w_in': 'new_v', 'new_v_dn_conv_w': 'new_v', 'new_v_dn_a_log': 'new_v', 'new_v_dn_dt_bias': 'new_v', 'new_v_dn_out_norm': 'new_v', 'new_v_sg_norm_gain': 'new_v', 'new_v_sg_norm_bias': 'new_v', 'new_v_sg_w_spatial': 'new_v', 'new_v_sg_b_spatial': 'new_v', 'new_v_mix_w_out': 'new_v', 'new_v_xa_norm': 'new_v', 'new_v_xa_mem_norm': 'new_v', 'new_v_xa_w_q': 'new_v', 'new_v_xa_w_kv': 'new_v', 'new_v_xa_w_o': 'new_v', 'new_v_ffn2_norm': 'new_v', 'new_v_ffn2_w_gate_up': 'new_v', 'new_v_ffn2_w_down': 'new_v', 'new_v_final_norm': 'new_v'}


def _forward(args):
    return _fwd_reference(*[args[k] for k in FWD_PARAMS])


def _output_shape():
    def fwd():
        inp = _fwd_setup_inputs(0)
        return _fwd_reference(*[inp[k] for k in FWD_PARAMS])
    out = _jax.eval_shape(fwd)
    return out.shape, out.dtype

N_MICROBATCH = 1
ADAM_LR = 0.001
ADAM_B1 = 0.9
ADAM_B2 = 0.999
ADAM_EPS = 1e-08
ADAM_WD = 0.01
ADAM_STEP = 10
PER_EXAMPLE_BATCH_AXIS = {'x': 0, 'mem': 0, 'loss_target': 0}
SHARED_INPUTS = []
_WEIGHT_DTYPES = {'ffn1_norm': _jnp.float32, 'ffn1_w_gate_up': _jnp.float32, 'ffn1_w_down': _jnp.float32, 'mix_norm': _jnp.float32, 'mix_w_in': _jnp.float32, 'dn_conv_w': _jnp.float32, 'dn_a_log': _jnp.float32, 'dn_dt_bias': _jnp.float32, 'dn_out_norm': _jnp.float32, 'sg_norm_gain': _jnp.float32, 'sg_norm_bias': _jnp.float32, 'sg_w_spatial': _jnp.float32, 'sg_b_spatial': _jnp.float32, 'mix_w_out': _jnp.float32, 'xa_norm': _jnp.float32, 'xa_mem_norm': _jnp.float32, 'xa_w_q': _jnp.float32, 'xa_w_kv': _jnp.float32, 'xa_w_o': _jnp.float32, 'ffn2_norm': _jnp.float32, 'ffn2_w_gate_up': _jnp.float32, 'ffn2_w_down': _jnp.float32, 'final_norm': _jnp.float32}
MOMENT_SCALE = {'ffn1_norm': 1.223585e-01, 'ffn1_w_gate_up': 5.103914e-02, 'ffn1_w_down': 8.336716e-02, 'mix_norm': 1.979034e-01, 'mix_w_in': 1.073378e-01, 'dn_conv_w': 1.045342e-01, 'dn_a_log': 7.755049e-01, 'dn_dt_bias': 7.695590e-01, 'dn_out_norm': 2.625913e-01, 'sg_norm_gain': 9.452005e-02, 'sg_norm_bias': 1.042296e-01, 'sg_w_spatial': 6.997483e-02, 'sg_b_spatial': 9.852331e-02, 'mix_w_out': 1.322257e-01, 'xa_norm': 2.487004e-02, 'xa_mem_norm': 3.667274e-02, 'xa_w_q': 2.289234e-02, 'xa_w_kv': 2.327971e-02, 'xa_w_o': 2.354308e-02, 'ffn2_norm': 9.000614e-02, 'ffn2_w_gate_up': 3.775794e-02, 'ffn2_w_down': 6.166478e-02, 'final_norm': 6.403660e+01}


def _to_microbatches(a, axis):
    t = _jnp.moveaxis(a, axis, 0)
    t = t.reshape((N_MICROBATCH, t.shape[0] // N_MICROBATCH) + t.shape[1:])
    return _jnp.moveaxis(t, 1, axis + 1)


def setup_inputs(seed: int = 0) -> dict:
    inp = _fwd_setup_inputs(seed)
    key = _jax.random.fold_in(_jax.random.key(seed), 7919)
    shape, _ = _output_shape()
    out = dict(inp)
    out["loss_target"] = _jax.random.normal(_jax.random.fold_in(key, 0), shape, _jnp.float32)
    for i, name in enumerate(TWIN_WEIGHTS):
        w = inp[name].astype(_jnp.float32)
        if MOMENT_SCALE is None:
            s = _jnp.sqrt(_jnp.mean(_jnp.square(w)) + 1e-30)
        else:
            s = MOMENT_SCALE[name]
        km, kv = _jax.random.split(_jax.random.fold_in(key, i + 1))
        out[name] = w
        out["m_" + name] = s * _jax.random.normal(km, w.shape, _jnp.float32)
        out["v_" + name] = (s * s) * _jax.random.uniform(kv, w.shape, _jnp.float32, 0.5, 1.5)
    if N_MICROBATCH > 1:
        for name, axis in PER_EXAMPLE_BATCH_AXIS.items():
            out[name] = _to_microbatches(out[name], axis)
    return {'x': out['x'], 'mem': out['mem'], 'ffn1_norm': out['ffn1_norm'], 'ffn1_w_gate_up': out['ffn1_w_gate_up'], 'ffn1_w_down': out['ffn1_w_down'], 'mix_norm': out['mix_norm'], 'mix_w_in': out['mix_w_in'], 'dn_conv_w': out['dn_conv_w'], 'dn_a_log': out['dn_a_log'], 'dn_dt_bias': out['dn_dt_bias'], 'dn_out_norm': out['dn_out_norm'], 'sg_norm_gain': out['sg_norm_gain'], 'sg_norm_bias': out['sg_norm_bias'], 'sg_w_spatial': out['sg_w_spatial'], 'sg_b_spatial': out['sg_b_spatial'], 'mix_w_out': out['mix_w_out'], 'xa_norm': out['xa_norm'], 'xa_mem_norm': out['xa_mem_norm'], 'xa_w_q': out['xa_w_q'], 'xa_w_kv': out['xa_w_kv'], 'xa_w_o': out['xa_w_o'], 'ffn2_norm': out['ffn2_norm'], 'ffn2_w_gate_up': out['ffn2_w_gate_up'], 'ffn2_w_down': out['ffn2_w_down'], 'final_norm': out['final_norm'], 'loss_target': out['loss_target'], 'm_ffn1_norm': out['m_ffn1_norm'], 'm_ffn1_w_gate_up': out['m_ffn1_w_gate_up'], 'm_ffn1_w_down': out['m_ffn1_w_down'], 'm_mix_norm': out['m_mix_norm'], 'm_mix_w_in': out['m_mix_w_in'], 'm_dn_conv_w': out['m_dn_conv_w'], 'm_dn_a_log': out['m_dn_a_log'], 'm_dn_dt_bias': out['m_dn_dt_bias'], 'm_dn_out_norm': out['m_dn_out_norm'], 'm_sg_norm_gain': out['m_sg_norm_gain'], 'm_sg_norm_bias': out['m_sg_norm_bias'], 'm_sg_w_spatial': out['m_sg_w_spatial'], 'm_sg_b_spatial': out['m_sg_b_spatial'], 'm_mix_w_out': out['m_mix_w_out'], 'm_xa_norm': out['m_xa_norm'], 'm_xa_mem_norm': out['m_xa_mem_norm'], 'm_xa_w_q': out['m_xa_w_q'], 'm_xa_w_kv': out['m_xa_w_kv'], 'm_xa_w_o': out['m_xa_w_o'], 'm_ffn2_norm': out['m_ffn2_norm'], 'm_ffn2_w_gate_up': out['m_ffn2_w_gate_up'], 'm_ffn2_w_down': out['m_ffn2_w_down'], 'm_final_norm': out['m_final_norm'], 'v_ffn1_norm': out['v_ffn1_norm'], 'v_ffn1_w_gate_up': out['v_ffn1_w_gate_up'], 'v_ffn1_w_down': out['v_ffn1_w_down'], 'v_mix_norm': out['v_mix_norm'], 'v_mix_w_in': out['v_mix_w_in'], 'v_dn_conv_w': out['v_dn_conv_w'], 'v_dn_a_log': out['v_dn_a_log'], 'v_dn_dt_bias': out['v_dn_dt_bias'], 'v_dn_out_norm': out['v_dn_out_norm'], 'v_sg_norm_gain': out['v_sg_norm_gain'], 'v_sg_norm_bias': out['v_sg_norm_bias'], 'v_sg_w_spatial': out['v_sg_w_spatial'], 'v_sg_b_spatial': out['v_sg_b_spatial'], 'v_mix_w_out': out['v_mix_w_out'], 'v_xa_norm': out['v_xa_norm'], 'v_xa_mem_norm': out['v_xa_mem_norm'], 'v_xa_w_q': out['v_xa_w_q'], 'v_xa_w_kv': out['v_xa_w_kv'], 'v_xa_w_o': out['v_xa_w_o'], 'v_ffn2_norm': out['v_ffn2_norm'], 'v_ffn2_w_gate_up': out['v_ffn2_w_gate_up'], 'v_ffn2_w_down': out['v_ffn2_w_down'], 'v_final_norm': out['v_final_norm']}


def _loss(weights, diff, rest, loss_target):
    with _jax.named_scope("forward"):
        args = {**rest, TWIN_DIFF_INPUT: diff, **{k: w.astype(_WEIGHT_DTYPES[k]) for k, w in weights.items()}}
        y = _forward(args)
    with _jax.named_scope("loss_head"):
        err = _jnp.square(y.astype(_jnp.float32) - loss_target)
        return 0.5 * _jnp.sum(_jnp.mean(err, axis=-1)) if err.ndim else 0.5 * err


def _adamw(w, g, m, v):
    m = ADAM_B1 * m + (1.0 - ADAM_B1) * g
    v = ADAM_B2 * v + (1.0 - ADAM_B2) * _jnp.square(g)
    m_hat = m / (1.0 - ADAM_B1 ** ADAM_STEP)
    v_hat = v / (1.0 - ADAM_B2 ** ADAM_STEP)
    delta = -ADAM_LR * (m_hat / (_jnp.sqrt(v_hat) + ADAM_EPS) + ADAM_WD * w)
    return delta, m, v


def reference(x, mem, ffn1_norm, ffn1_w_gate_up, ffn1_w_down, mix_norm, mix_w_in, dn_conv_w, dn_a_log, dn_dt_bias, dn_out_norm, sg_norm_gain, sg_norm_bias, sg_w_spatial, sg_b_spatial, mix_w_out, xa_norm, xa_mem_norm, xa_w_q, xa_w_kv, xa_w_o, ffn2_norm, ffn2_w_gate_up, ffn2_w_down, final_norm, loss_target, m_ffn1_norm, m_ffn1_w_gate_up, m_ffn1_w_down, m_mix_norm, m_mix_w_in, m_dn_conv_w, m_dn_a_log, m_dn_dt_bias, m_dn_out_norm, m_sg_norm_gain, m_sg_norm_bias, m_sg_w_spatial, m_sg_b_spatial, m_mix_w_out, m_xa_norm, m_xa_mem_norm, m_xa_w_q, m_xa_w_kv, m_xa_w_o, m_ffn2_norm, m_ffn2_w_gate_up, m_ffn2_w_down, m_final_norm, v_ffn1_norm, v_ffn1_w_gate_up, v_ffn1_w_down, v_mix_norm, v_mix_w_in, v_dn_conv_w, v_dn_a_log, v_dn_dt_bias, v_dn_out_norm, v_sg_norm_gain, v_sg_norm_bias, v_sg_w_spatial, v_sg_b_spatial, v_mix_w_out, v_xa_norm, v_xa_mem_norm, v_xa_w_q, v_xa_w_kv, v_xa_w_o, v_ffn2_norm, v_ffn2_w_gate_up, v_ffn2_w_down, v_final_norm):
    given = dict(x=x, mem=mem, ffn1_norm=ffn1_norm, ffn1_w_gate_up=ffn1_w_gate_up, ffn1_w_down=ffn1_w_down, mix_norm=mix_norm, mix_w_in=mix_w_in, dn_conv_w=dn_conv_w, dn_a_log=dn_a_log, dn_dt_bias=dn_dt_bias, dn_out_norm=dn_out_norm, sg_norm_gain=sg_norm_gain, sg_norm_bias=sg_norm_bias, sg_w_spatial=sg_w_spatial, sg_b_spatial=sg_b_spatial, mix_w_out=mix_w_out, xa_norm=xa_norm, xa_mem_norm=xa_mem_norm, xa_w_q=xa_w_q, xa_w_kv=xa_w_kv, xa_w_o=xa_w_o, ffn2_norm=ffn2_norm, ffn2_w_gate_up=ffn2_w_gate_up, ffn2_w_down=ffn2_w_down, final_norm=final_norm, loss_target=loss_target, m_ffn1_norm=m_ffn1_norm, m_ffn1_w_gate_up=m_ffn1_w_gate_up, m_ffn1_w_down=m_ffn1_w_down, m_mix_norm=m_mix_norm, m_mix_w_in=m_mix_w_in, m_dn_conv_w=m_dn_conv_w, m_dn_a_log=m_dn_a_log, m_dn_dt_bias=m_dn_dt_bias, m_dn_out_norm=m_dn_out_norm, m_sg_norm_gain=m_sg_norm_gain, m_sg_norm_bias=m_sg_norm_bias, m_sg_w_spatial=m_sg_w_spatial, m_sg_b_spatial=m_sg_b_spatial, m_mix_w_out=m_mix_w_out, m_xa_norm=m_xa_norm, m_xa_mem_norm=m_xa_mem_norm, m_xa_w_q=m_xa_w_q, m_xa_w_kv=m_xa_w_kv, m_xa_w_o=m_xa_w_o, m_ffn2_norm=m_ffn2_norm, m_ffn2_w_gate_up=m_ffn2_w_gate_up, m_ffn2_w_down=m_ffn2_w_down, m_final_norm=m_final_norm, v_ffn1_norm=v_ffn1_norm, v_ffn1_w_gate_up=v_ffn1_w_gate_up, v_ffn1_w_down=v_ffn1_w_down, v_mix_norm=v_mix_norm, v_mix_w_in=v_mix_w_in, v_dn_conv_w=v_dn_conv_w, v_dn_a_log=v_dn_a_log, v_dn_dt_bias=v_dn_dt_bias, v_dn_out_norm=v_dn_out_norm, v_sg_norm_gain=v_sg_norm_gain, v_sg_norm_bias=v_sg_norm_bias, v_sg_w_spatial=v_sg_w_spatial, v_sg_b_spatial=v_sg_b_spatial, v_mix_w_out=v_mix_w_out, v_xa_norm=v_xa_norm, v_xa_mem_norm=v_xa_mem_norm, v_xa_w_q=v_xa_w_q, v_xa_w_kv=v_xa_w_kv, v_xa_w_o=v_xa_w_o, v_ffn2_norm=v_ffn2_norm, v_ffn2_w_gate_up=v_ffn2_w_gate_up, v_ffn2_w_down=v_ffn2_w_down, v_final_norm=v_final_norm)
    weights = {n: given[n] for n in TWIN_WEIGHTS}
    shared = {n: given[n] for n in SHARED_INPUTS}
    per_example = {n: given[n] for n in ['x', 'mem']}
    grad_fn = _jax.value_and_grad(_loss, argnums=(0, 1))

    def one_microbatch(ex, loss_target):
        ex = dict(ex)
        diff = ex.pop(TWIN_DIFF_INPUT)
        return grad_fn(weights, diff, {**shared, **ex}, loss_target)

    if N_MICROBATCH == 1:
        loss, (grad_w, grad_x) = one_microbatch(per_example, given["loss_target"])
    else:
        def body(carry, xs):
            loss_sum, grad_sum = carry
            l_k, (gw_k, gx_k) = one_microbatch(xs[0], xs[1])
            with _jax.named_scope("update"):
                return (loss_sum + l_k, _jax.tree.map(_jnp.add, grad_sum, gw_k)), gx_k

        init = (_jnp.zeros((), _jnp.float32), _jax.tree.map(_jnp.zeros_like, weights))
        (loss, grad_w), grad_x = _jax.lax.scan(body, init, (per_example, given["loss_target"]))
    with _jax.named_scope("update"):
        delta_w, new_m, new_v = {}, {}, {}
        for n in TWIN_WEIGHTS:
            delta_w[n], new_m[n], new_v[n] = _adamw(weights[n], grad_w[n], given["m_" + n], given["v_" + n])
    return (loss, grad_x, *[grad_w[n] for n in TWIN_WEIGHTS], *[delta_w[n] for n in TWIN_WEIGHTS],
            *[new_m[n] for n in TWIN_WEIGHTS], *[new_v[n] for n in TWIN_WEIGHTS])
```

```python
import functools
import math

import jax
import jax.numpy as jnp
from jax import lax
from jax.experimental import pallas as pl
from jax.experimental.pallas import tpu as pltpu

F32, BF16 = jnp.float32, jnp.bfloat16
HI = lax.Precision.HIGHEST
NN, NT, TN = ((1,), (0,)), ((1,), (1,)), ((0,), (0,))

NORM_EPS = 1e-6
LANES = 128
V7X_VMEM_BYTES = 64 * 2**20
VMEM_LIMIT = V7X_VMEM_BYTES * 3 // 4

DN_HEADS, DN_DIM, DN_CHUNK, DN_CONV, HALO = 4, 128, 64, 4, 8
DN_WIDTH = DN_HEADS * DN_DIM
SWA_HEADS, SWA_DIM, SWA_BLOCK = 4, 64, 128
SWA_WIDTH = SWA_HEADS * SWA_DIM
SWA_PATTERNS = ((128, 1), (512, 4), (2048, 16))
SG_GROUPS, SG_DIM, SG_CHUNK = 4, 64, 128
SG_WIDTH = SG_GROUPS * SG_DIM
XA_HEADS = 4
IN_SIZES = (3 * DN_WIDTH, DN_WIDTH, DN_HEADS, DN_HEADS, 3 * SWA_WIDTH, 2 * SG_WIDTH)
ADAM_LR, ADAM_B1, ADAM_B2, ADAM_EPS, ADAM_WD, ADAM_STEP = 0.001, 0.9, 0.999, 1e-08, 0.01, 10
N_CHIPS, N_DEV = 4, 8
PACK_ROWS = 1024
MESH_ID = pl.DeviceIdType.MESH

BIG = ("ffn1_w_gate_up", "ffn1_w_down", "mix_w_in", "mix_w_out", "xa_w_q", "xa_w_kv", "xa_w_o",
       "ffn2_w_gate_up", "ffn2_w_down")
BIG_AXIS = {"ffn1_w_gate_up": 2, "ffn1_w_down": 1, "mix_w_in": 2, "mix_w_out": 1, "xa_w_q": 1, "xa_w_kv": 2,
            "xa_w_o": 1, "ffn2_w_gate_up": 2, "ffn2_w_down": 1}
SMALL = ("ffn1_norm", "mix_norm", "dn_a_log", "dn_dt_bias", "dn_out_norm", "sg_norm_gain", "sg_norm_bias",
         "sg_w_spatial", "sg_b_spatial", "xa_norm", "xa_mem_norm", "ffn2_norm", "final_norm")
WEIGHTS = ("ffn1_norm", "ffn1_w_gate_up", "ffn1_w_down", "mix_norm", "mix_w_in", "dn_conv_w", "dn_a_log",
           "dn_dt_bias", "dn_out_norm", "sg_norm_gain", "sg_norm_bias", "sg_w_spatial", "sg_b_spatial",
           "mix_w_out", "xa_norm", "xa_mem_norm", "xa_w_q", "xa_w_kv", "xa_w_o", "ffn2_norm", "ffn2_w_gate_up",
           "ffn2_w_down", "final_norm")


@functools.partial(jax.custom_vjp, nondiff_argnums=(2,))
def _dlo(a, b, dims):
    return lax.dot_general(a.astype(BF16), b.astype(BF16), (dims, ((), ())), preferred_element_type=F32)


def _dlo_fwd(a, b, dims):
    return _dlo(a, b, dims), (a, b)


def _dlo_bwd(dims, saved, g):
    a, b = saved
    if dims == NN:
        da, db = _dlo(g, b, NT), _dlo(a, g, TN)
    elif dims == NT:
        da, db = _dlo(g, b, NN), _dlo(g, a, TN)
    else:
        da, db = _dlo(b, g, NT), _dlo(a, g, NN)
    return da.astype(a.dtype), db.astype(b.dtype)


_dlo.defvjp(_dlo_fwd, _dlo_bwd)


def _dhi(a, b, dims):
    return lax.dot_general(a, b, (dims, ((), ())), preferred_element_type=F32, precision=HI)


def _sigmoid(x):
    return 1.0 / (1.0 + jnp.exp(-x))


def _silu(x):
    return x * _sigmoid(x)


def _softplus(x):
    return jnp.maximum(x, 0.0) + jnp.log(1.0 + jnp.exp(-jnp.abs(x)))


def _rms(x, gain):
    x = x.astype(F32)
    return x * lax.rsqrt(jnp.mean(x * x, axis=-1, keepdims=True) + NORM_EPS) * gain


def _tile(n, target, unit=LANES):
    best = None
    for t in range(unit, min(n, target) + 1, unit):
        if n % t == 0:
            best = t
    return best if best is not None else n


def _cparams(sem):
    return pltpu.CompilerParams(dimension_semantics=sem, vmem_limit_bytes=VMEM_LIMIT)


def _mm(a, b, dims, *, out_dtype, name, res=None, scale=1.0, tm=512, tn=1408, tk=512):
    if dims == NN:
        (m, k), n = a.shape, b.shape[1]
    elif dims == NT:
        (m, k), n = a.shape, b.shape[0]
    else:
        (k, m), n = a.shape, b.shape[1]
    tm, tn, tk = _tile(m, tm, LANES if dims == TN else 8), _tile(n, tn), _tile(k, tk)
    nk = k // tk
    a_spec = pl.BlockSpec((tk, tm), lambda i, j, kk: (kk, i)) if dims == TN else pl.BlockSpec((tm, tk), lambda i, j, kk: (i, kk))
    b_spec = pl.BlockSpec((tn, tk), lambda i, j, kk: (j, kk)) if dims == NT else pl.BlockSpec((tk, tn), lambda i, j, kk: (kk, j))
    o_spec = pl.BlockSpec((tm, tn), lambda i, j, kk: (i, j))
    has_res = res is not None

    def body(*refs):
        a_ref, b_ref = refs[0], refs[1]
        r_ref = refs[2] if has_res else None
        o_ref, acc_ref = refs[-2], refs[-1]
        kk = pl.program_id(2)

        @pl.when(kk == 0)
        def _():
            acc_ref[...] = jnp.zeros_like(acc_ref)

        acc_ref[...] += lax.dot_general(a_ref[...].astype(BF16), b_ref[...].astype(BF16), (dims, ((), ())),
                                        preferred_element_type=F32)

        @pl.when(kk == nk - 1)
        def _():
            val = acc_ref[...] * scale if scale != 1.0 else acc_ref[...]
            if has_res:
                val = r_ref[...].astype(F32) + val
            o_ref[...] = val.astype(o_ref.dtype)

    return pl.pallas_call(
        body, name=name, grid=(m // tm, n // tn, nk),
        in_specs=[a_spec, b_spec] + ([o_spec] if has_res else []), out_specs=o_spec,
        out_shape=jax.ShapeDtypeStruct((m, n), out_dtype),
        scratch_shapes=[pltpu.VMEM((tm, tn), F32)],
        compiler_params=_cparams(("parallel", "parallel", "arbitrary")),
    )(*([a, b] + ([res] if has_res else [])))


def _full_spec(p):
    nd = p.ndim
    return pl.BlockSpec(p.shape, lambda i, _nd=nd: (0,) * _nd)


def _rows(f, rows, params, outs, *, tile, name):
    t = rows[0].shape[0]
    tile = min(tile, t)
    nr, npar = len(rows), len(params)

    def body(*refs):
        vals = f(*[r[...] for r in refs[:nr + npar]])
        for o_ref, v in zip(refs[nr + npar:], vals, strict=True):
            o_ref[...] = v.astype(o_ref.dtype)

    res = pl.pallas_call(
        body, name=name, grid=(t // tile,),
        in_specs=[pl.BlockSpec((tile, r.shape[1]), lambda i: (i, 0)) for r in rows] + [_full_spec(p) for p in params],
        out_specs=[pl.BlockSpec((tile, w), lambda i: (i, 0)) for w, _ in outs],
        out_shape=[jax.ShapeDtypeStruct((t, w), d) for w, d in outs],
        compiler_params=_cparams(("parallel",)),
    )(*rows, *params)
    return tuple(res)


def _rows_vjp(f, rows, params, cts, *, diff, grad_dtypes, tile, name, add=None):
    t = rows[0].shape[0]
    tile = min(tile, t)
    nr, npar, nct = len(rows), len(params), len(cts)
    didx = [i for i, d in enumerate(diff) if d]
    add = [None] * len(didx) if add is None else add
    adds = [a for a in add if a is not None]

    def body(*refs):
        row_refs, par_refs = refs[:nr], refs[nr:nr + npar]
        ct_refs = refs[nr + npar:nr + npar + nct]
        add_refs = list(refs[nr + npar + nct:nr + npar + nct + len(adds)])
        out_refs = refs[nr + npar + nct + len(adds):]
        rv = [r[...] for r in row_refs]
        pv = [p[...] for p in par_refs]

        def g(*args):
            full = list(rv)
            for k, i in enumerate(didx):
                full[i] = args[k]
            return f(*full, *args[len(didx):])

        outs, pull = jax.vjp(g, *[rv[i] for i in didx], *pv)
        grads = pull(tuple(c[...].astype(o.dtype) for c, o in zip(ct_refs, outs, strict=True)))
        for k in range(len(didx)):
            val = grads[k].astype(F32)
            if add[k] is not None:
                val = val + add_refs.pop(0)[...].astype(F32)
            out_refs[k][...] = val.astype(out_refs[k].dtype)

        @pl.when(pl.program_id(0) == 0)
        def _():
            for o_ref in out_refs[len(didx):]:
                o_ref[...] = jnp.zeros_like(o_ref)

        for o_ref, gp in zip(out_refs[len(didx):], grads[len(didx):], strict=True):
            o_ref[...] += gp.astype(F32)

    row_spec = lambda a: pl.BlockSpec((tile, a.shape[1]), lambda i: (i, 0))
    res = pl.pallas_call(
        body, name=name, grid=(t // tile,),
        in_specs=[row_spec(r) for r in rows] + [_full_spec(p) for p in params] + [row_spec(c) for c in cts]
        + [row_spec(a) for a in adds],
        out_specs=[row_spec(rows[i]) for i in didx] + [_full_spec(p) for p in params],
        out_shape=[jax.ShapeDtypeStruct(rows[i].shape, d) for i, d in zip(didx, grad_dtypes, strict=True)]
        + [jax.ShapeDtypeStruct(p.shape, F32) for p in params],
        compiler_params=_cparams(("arbitrary",)),
    )(*rows, *params, *cts, *adds)
    return tuple(res)


def _sum_slots(x, name, tile=1024):
    n, r, w = x.shape
    tile = _tile(r, tile, 8)

    def body(x_ref, o_ref):
        acc = x_ref[0].astype(F32)
        for s in range(1, n):
            acc = acc + x_ref[s].astype(F32)
        o_ref[...] = acc

    return pl.pallas_call(
        body, name=name, grid=(r // tile,),
        in_specs=[pl.BlockSpec((n, tile, w), lambda i: (0, i, 0))], out_specs=pl.BlockSpec((tile, w), lambda i: (i, 0)),
        out_shape=jax.ShapeDtypeStruct((r, w), F32), compiler_params=_cparams(("parallel",)),
    )(x)


def _adamw(w, g_a, g_b, m, v, name):
    r, c = w.shape
    tile = _tile(r, max(8, (1 << 18) // c), 8)

    def body(w_ref, ga_ref, gb_ref, m_ref, v_ref, g_out, d_out, m_out, v_out):
        g = ga_ref[...] + gb_ref[...]
        mn = ADAM_B1 * m_ref[...] + (1.0 - ADAM_B1) * g
        vn = ADAM_B2 * v_ref[...] + (1.0 - ADAM_B2) * (g * g)
        m_hat = mn / (1.0 - ADAM_B1 ** ADAM_STEP)
        v_hat = vn / (1.0 - ADAM_B2 ** ADAM_STEP)
        g_out[...] = g
        d_out[...] = -ADAM_LR * (m_hat / (jnp.sqrt(v_hat) + ADAM_EPS) + ADAM_WD * w_ref[...])
        m_out[...] = mn
        v_out[...] = vn

    spec = pl.BlockSpec((tile, c), lambda i: (i, 0))
    return pl.pallas_call(
        body, name=name, grid=(r // tile,), in_specs=[spec] * 5, out_specs=[spec] * 4,
        out_shape=[jax.ShapeDtypeStruct((r, c), F32)] * 4, compiler_params=_cparams(("parallel",)),
    )(w, g_a, g_b, m, v)


def _place():
    return lax.axis_index("x"), lax.axis_index("y"), lax.axis_index("c")


def _flip(v, bit):
    return 1 - v if bit else v


_ANY = pl.BlockSpec(memory_space=pl.ANY)


def _gather_chips(x, name):
    r, w = x.shape

    def body(x_ref, o_ref, send_sems, recv_sems, local_sem):
        mx, my, mc = _place()
        local = pltpu.make_async_copy(x_ref, o_ref.at[2 * mx + my], local_sem)
        local.start()
        copies = []
        for k in range(1, N_CHIPS):
            px, py = _flip(mx, k >> 1), _flip(my, k & 1)
            copies.append(pltpu.make_async_remote_copy(
                src_ref=x_ref, dst_ref=o_ref.at[2 * mx + my], send_sem=send_sems.at[k - 1], recv_sem=recv_sems.at[k - 1],
                device_id=(px, py, mc), device_id_type=MESH_ID))
            copies[-1].start()
        for k in range(1, N_CHIPS):
            px, py = _flip(mx, k >> 1), _flip(my, k & 1)
            pltpu.make_async_remote_copy(
                src_ref=x_ref, dst_ref=o_ref.at[2 * px + py], send_sem=send_sems.at[k - 1], recv_sem=recv_sems.at[k - 1],
                device_id=(px, py, mc), device_id_type=MESH_ID).wait_recv()
        for cp in copies:
            cp.wait_send()
        local.wait()

    return pl.pallas_call(
        body, name=name, in_specs=[_ANY], out_specs=_ANY, out_shape=jax.ShapeDtypeStruct((N_CHIPS, r, w), x.dtype),
        scratch_shapes=[pltpu.SemaphoreType.DMA((N_CHIPS - 1,)), pltpu.SemaphoreType.DMA((N_CHIPS - 1,)),
                        pltpu.SemaphoreType.DMA],
    )(x)


def _scatter_chips(g, name):
    _, r, w = g.shape

    def body(g_ref, o_ref, send_sems, recv_sems, local_sem):
        mx, my, mc = _place()
        local = pltpu.make_async_copy(g_ref.at[2 * mx + my], o_ref.at[0], local_sem)
        local.start()
        copies = []
        for k in range(1, N_CHIPS):
            px, py = _flip(mx, k >> 1), _flip(my, k & 1)
            copies.append(pltpu.make_async_remote_copy(
                src_ref=g_ref.at[2 * px + py], dst_ref=o_ref.at[k], send_sem=send_sems.at[k - 1],
                recv_sem=recv_sems.at[k - 1], device_id=(px, py, mc), device_id_type=MESH_ID))
            copies[-1].start()
        for k in range(1, N_CHIPS):
            px, py = _flip(mx, k >> 1), _flip(my, k & 1)
            pltpu.make_async_remote_copy(
                src_ref=g_ref.at[2 * px + py], dst_ref=o_ref.at[k], send_sem=send_sems.at[k - 1],
                recv_sem=recv_sems.at[k - 1], device_id=(px, py, mc), device_id_type=MESH_ID).wait_recv()
        for cp in copies:
            cp.wait_send()
        local.wait()

    return pl.pallas_call(
        body, name=name, in_specs=[_ANY], out_specs=_ANY, out_shape=jax.ShapeDtypeStruct((N_CHIPS, r, w), g.dtype),
        scratch_shapes=[pltpu.SemaphoreType.DMA((N_CHIPS - 1,)), pltpu.SemaphoreType.DMA((N_CHIPS - 1,)),
                        pltpu.SemaphoreType.DMA],
    )(g)


def _swap_cores(p, name):
    def body(p_ref, o_ref, send_sem, recv_sem):
        mx, my, mc = _place()
        cp = pltpu.make_async_remote_copy(src_ref=p_ref, dst_ref=o_ref, send_sem=send_sem, recv_sem=recv_sem,
                                          device_id=(mx, my, 1 - mc), device_id_type=MESH_ID)
        cp.start()
        cp.wait()

    return pl.pallas_call(
        body, name=name, in_specs=[_ANY], out_specs=_ANY, out_shape=jax.ShapeDtypeStruct(p.shape, p.dtype),
        scratch_shapes=[pltpu.SemaphoreType.DMA, pltpu.SemaphoreType.DMA],
    )(p)


def _gather_all(x, name):
    r, w = x.shape

    def body(x_ref, o_ref, send_sems, recv_sems, local_sem):
        mx, my, mc = _place()
        mine = 4 * mx + 2 * my + mc
        local = pltpu.make_async_copy(x_ref, o_ref.at[mine], local_sem)
        local.start()
        copies = []
        for k in range(1, N_DEV):
            peer = (_flip(mx, k >> 2), _flip(my, (k >> 1) & 1), _flip(mc, k & 1))
            copies.append(pltpu.make_async_remote_copy(
                src_ref=x_ref, dst_ref=o_ref.at[mine], send_sem=send_sems.at[k - 1], recv_sem=recv_sems.at[k - 1],
                device_id=peer, device_id_type=MESH_ID))
            copies[-1].start()
        for k in range(1, N_DEV):
            peer = (_flip(mx, k >> 2), _flip(my, (k >> 1) & 1), _flip(mc, k & 1))
            pltpu.make_async_remote_copy(
                src_ref=x_ref, dst_ref=o_ref.at[4 * peer[0] + 2 * peer[1] + peer[2]], send_sem=send_sems.at[k - 1],
                recv_sem=recv_sems.at[k - 1], device_id=peer, device_id_type=MESH_ID).wait_recv()
        for cp in copies:
            cp.wait_send()
        local.wait()

    return pl.pallas_call(
        body, name=name, in_specs=[_ANY], out_specs=_ANY, out_shape=jax.ShapeDtypeStruct((N_DEV, r, w), x.dtype),
        scratch_shapes=[pltpu.SemaphoreType.DMA((N_DEV - 1,)), pltpu.SemaphoreType.DMA((N_DEV - 1,)),
                        pltpu.SemaphoreType.DMA],
    )(x)


def _pack(parts, dtype, row_unit):
    flat = jnp.concatenate([p.astype(dtype).reshape(-1) for p in parts])
    unit = row_unit * LANES
    pad = (-flat.shape[0]) % unit
    if pad:
        flat = jnp.concatenate([flat, jnp.zeros((pad,), dtype)])
    return flat.reshape(-1, LANES)


def _unpack(packed, shapes):
    flat = packed.reshape(-1)
    out, off = [], 0
    for s in shapes:
        n = math.prod(s)
        out.append(flat[off:off + n].reshape(s))
        off += n
    return out


def _f_rms(x, gain):
    return (_rms(x, gain),)


def _f_swiglu(gu):
    f = gu.shape[1] // 2
    return (_silu(gu[:, :f].astype(F32)) * gu[:, f:].astype(F32),)


def _f_xattn(q, kv):
    d = q.shape[1]
    hd = d // XA_HEADS
    outs = []
    for h in range(XA_HEADS):
        qh, kh, vh = q[:, h * hd:(h + 1) * hd], kv[:, h * hd:(h + 1) * hd], kv[:, d + h * hd:d + (h + 1) * hd]
        s = _dlo(qh, kh, NT) * (hd ** -0.5)
        s = s - jnp.max(s, axis=-1, keepdims=True)
        p = jnp.exp(s)
        p = p / jnp.sum(p, axis=-1, keepdims=True)
        outs.append(_dlo(p, vh, NN))
    return (jnp.concatenate(outs, axis=1),)


def _f_gmlp(uv, gain, bias, w_sp, b_sp):
    r = uv.shape[0]
    act = jax.nn.gelu(uv.astype(F32))
    u, v = act[:, :SG_WIDTH], act[:, SG_WIDTH:]
    mu = jnp.mean(v, axis=-1, keepdims=True)
    var = jnp.mean(jnp.square(v - mu), axis=-1, keepdims=True)
    v = (v - mu) * lax.rsqrt(var + NORM_EPS) * gain + bias
    row = lax.broadcasted_iota(jnp.int32, (SG_CHUNK, SG_CHUNK), 0)
    col = lax.broadcasted_iota(jnp.int32, (SG_CHUNK, SG_CHUNK), 1)
    lane_grp = lax.broadcasted_iota(jnp.int32, (b_sp.shape[0], SG_WIDTH), 1) // SG_DIM
    grp_row = lax.broadcasted_iota(jnp.int32, (b_sp.shape[0], SG_WIDTH), 0)
    spread = jnp.where(lane_grp == grp_row, 1.0, 0.0).astype(F32)
    bias_t = _dhi(b_sp, spread, TN)
    chunks = []
    for c in range(r // SG_CHUNK):
        vc = v[c * SG_CHUNK:(c + 1) * SG_CHUNK]
        parts = []
        for g in range(SG_GROUPS):
            wg = jnp.where(row >= col, w_sp[g], 0.0)
            parts.append(_dlo(wg, vc[:, g * SG_DIM:(g + 1) * SG_DIM], NN))
        chunks.append(jnp.concatenate(parts, axis=1) + bias_t)
    mixed = jnp.concatenate(chunks, axis=0) if len(chunks) > 1 else chunks[0]
    return (u * mixed,)


def _f_swa_mix(o1, o2, o3, l1, l2, l3):
    outs = []
    for h in range(SWA_HEADS):
        ls = [l[:, h:h + 1] for l in (l1, l2, l3)]
        mx = jnp.maximum(jnp.maximum(ls[0], ls[1]), ls[2])
        es = [jnp.exp(l - mx) for l in ls]
        den = es[0] + es[1] + es[2]
        sl = slice(h * SWA_DIM, (h + 1) * SWA_DIM)
        outs.append((es[0] * o1[:, sl] + es[1] * o2[:, sl] + es[2] * o3[:, sl]) / den)
    return (jnp.concatenate(outs, axis=1),)


def _swa_block(q, kp, kc, vp, vc, first, window, dilation):
    span = window // dilation
    qi = lax.broadcasted_iota(jnp.int32, (SWA_BLOCK, 2 * SWA_BLOCK), 0)
    kj = lax.broadcasted_iota(jnp.int32, (SWA_BLOCK, 2 * SWA_BLOCK), 1)
    rel = SWA_BLOCK + qi - kj
    valid = (rel >= 0) & (rel <= span) & jnp.logical_not(jnp.logical_and(first, kj < SWA_BLOCK))
    relf = (rel * dilation).astype(F32)
    kw = jnp.concatenate([kp, kc], axis=0)
    vw = jnp.concatenate([vp, vc], axis=0)
    lane = lax.broadcasted_iota(jnp.int32, (SWA_BLOCK, LANES), 1)
    outs, lse = [], jnp.zeros((SWA_BLOCK, LANES), F32)
    for h in range(SWA_HEADS):
        sl = slice(h * SWA_DIM, (h + 1) * SWA_DIM)
        slope = 2.0 ** (-8.0 * (h + 1) / SWA_HEADS)
        s = _dlo(q[:, sl], kw[:, sl], NT) * (SWA_DIM ** -0.5) - slope * relf
        s = jnp.where(valid, s, -1e30)
        m = jnp.max(s, axis=-1, keepdims=True)
        p = jnp.exp(s - m)
        den = jnp.sum(p, axis=-1, keepdims=True)
        outs.append(_dlo(p, vw[:, sl], NN) / den)
        lse = lse + jnp.where(lane == h, m + jnp.log(den), 0.0)
    return jnp.concatenate(outs, axis=1), lse


def _dn_chunk(xx, z, ba, state, conv_w, a_log, dt_bias, gain):
    c = z.shape[0]
    acc = conv_w[0:1] * xx[HALO - 3:HALO - 3 + c]
    for j in range(1, DN_CONV):
        acc = acc + conv_w[j:j + 1] * xx[HALO - 3 + j:HALO - 3 + j + c]
    qkv = _silu(acc)
    beta_all = _sigmoid(ba)
    g_all = -jnp.exp(a_log) * _softplus(ba + dt_bias)
    row = lax.broadcasted_iota(jnp.int32, (c, c), 0)
    col = lax.broadcasted_iota(jnp.int32, (c, c), 1)
    incl, strict = row >= col, row > col
    eye = jnp.where(row == col, 1.0, 0.0).astype(F32)
    gc_all = _dhi(jnp.where(incl, 1.0, 0.0).astype(F32), g_all, NN)
    lane = lax.broadcasted_iota(jnp.int32, (c, LANES), 1)
    outs, new_state = [], []
    for h in range(DN_HEADS):
        q = qkv[:, h * DN_DIM:(h + 1) * DN_DIM]
        k = qkv[:, DN_WIDTH + h * DN_DIM:DN_WIDTH + (h + 1) * DN_DIM]
        v = qkv[:, 2 * DN_WIDTH + h * DN_DIM:2 * DN_WIDTH + (h + 1) * DN_DIM]
        q = q * lax.rsqrt(jnp.sum(q * q, axis=-1, keepdims=True) + NORM_EPS) * (DN_DIM ** -0.5)
        k = k * lax.rsqrt(jnp.sum(k * k, axis=-1, keepdims=True) + NORM_EPS)
        beta = beta_all[:, h:h + 1]
        gc = gc_all[:, DN_HEADS + h:DN_HEADS + h + 1]
        g_last = gc[c - 1:c]
        gc_cols = _dhi(jnp.where(lane == DN_HEADS + h, 1.0, 0.0).astype(F32), gc_all, NT)
        diff = gc - gc_cols
        decay = jnp.where(incl, jnp.exp(jnp.where(incl, diff, 0.0)), 0.0)
        kb = k * beta
        lower = jnp.where(strict, _dlo(kb, k, NT) * decay, 0.0)
        pw = -lower
        t_inv = eye + pw
        for _ in range(int(math.log2(c)) - 1):
            pw = _dhi(pw, pw, NN)
            t_inv = t_inv + _dhi(t_inv, pw, NN)
        e_gc = jnp.exp(gc)
        u = _dlo(t_inv, v * beta, NN)
        w = _dlo(t_inv, kb * e_gc, NN)
        a_qk = jnp.where(incl, _dlo(q, k, NT) * decay, 0.0)
        k_tail = k * jnp.exp(g_last - gc)
        q_dec = q * e_gc
        s_h = state[h]
        v_new = u - _dlo(w, s_h, NN)
        o = _dlo(q_dec, s_h, NN) + _dlo(a_qk, v_new, NN)
        new_state.append(s_h * jnp.exp(g_last) + _dlo(k_tail, v_new, TN))
        o = o * lax.rsqrt(jnp.mean(o * o, axis=-1, keepdims=True) + NORM_EPS) * gain
        outs.append(o * _silu(z[:, h * DN_DIM:(h + 1) * DN_DIM]))
    return jnp.concatenate(outs, axis=1), jnp.stack(new_state)


def _dn_specs(n_of):
    c = DN_CHUNK
    return [pl.BlockSpec((c, 3 * DN_WIDTH), lambda i: (n_of(i), 0)),
            pl.BlockSpec((HALO, 3 * DN_WIDTH), lambda i: (jnp.maximum(n_of(i) * (c // HALO) - 1, 0), 0)),
            pl.BlockSpec((c, DN_WIDTH), lambda i: (n_of(i), 0)),
            pl.BlockSpec((c, LANES), lambda i: (n_of(i), 0))]


def _dn_forward(xq, xz, xba, params, name):
    t = xq.shape[0]
    n_chunks = t // DN_CHUNK

    def body(x_ref, halo_ref, z_ref, ba_ref, cw_ref, al_ref, dt_ref, gn_ref, o_ref, s_all_ref, s_ref):
        n = pl.program_id(0)

        @pl.when(n == 0)
        def _():
            s_ref[...] = jnp.zeros_like(s_ref)

        halo = jnp.where(n > 0, halo_ref[...], 0.0)
        xx = jnp.concatenate([halo, x_ref[...]], axis=0)
        s_all_ref[0] = s_ref[...]
        o, s_new = _dn_chunk(xx, z_ref[...], ba_ref[...], s_ref[...], cw_ref[...], al_ref[...], dt_ref[...], gn_ref[...])
        o_ref[...] = o.astype(o_ref.dtype)
        s_ref[...] = s_new

    return pl.pallas_call(
        body, name=name, grid=(n_chunks,),
        in_specs=_dn_specs(lambda i: i) + [_full_spec(p) for p in params],
        out_specs=[pl.BlockSpec((DN_CHUNK, DN_WIDTH), lambda i: (i, 0)),
                   pl.BlockSpec((1, DN_HEADS, DN_DIM, DN_DIM), lambda i: (i, 0, 0, 0))],
        out_shape=[jax.ShapeDtypeStruct((t, DN_WIDTH), BF16),
                   jax.ShapeDtypeStruct((n_chunks, DN_HEADS, DN_DIM, DN_DIM), F32)],
        scratch_shapes=[pltpu.VMEM((DN_HEADS, DN_DIM, DN_DIM), F32)],
        compiler_params=_cparams(("arbitrary",)),
    )(xq, xq, xz, xba, *params)


def _dn_backward(xq, xz, xba, params, s_all, d_out, name):
    t = xq.shape[0]
    n_chunks = t // DN_CHUNK
    c = DN_CHUNK
    rev = lambda i: n_chunks - 1 - i

    def body(x_ref, halo_ref, z_ref, ba_ref, cw_ref, al_ref, dt_ref, gn_ref, s_ref, do_ref,
             dx_ref, dz_ref, dba_ref, dcw_ref, dal_ref, ddt_ref, dgn_ref, ds_ref, dhalo_ref):
        i = pl.program_id(0)
        n = n_chunks - 1 - i

        @pl.when(i == 0)
        def _():
            ds_ref[...] = jnp.zeros_like(ds_ref)
            dhalo_ref[...] = jnp.zeros_like(dhalo_ref)
            for r in (dcw_ref, dal_ref, ddt_ref, dgn_ref):
                r[...] = jnp.zeros_like(r)

        halo = jnp.where(n > 0, halo_ref[...], 0.0)
        xx = jnp.concatenate([halo, x_ref[...]], axis=0)
        _, pull = jax.vjp(_dn_chunk, xx, z_ref[...], ba_ref[...], s_ref[0], cw_ref[...], al_ref[...], dt_ref[...],
                          gn_ref[...])
        dxx, dz, dba, ds, dcw, dal, ddt, dgn = pull((do_ref[...].astype(F32), ds_ref[...]))
        dx_ref[...] = jnp.concatenate([dxx[HALO:c], dxx[c:] + dhalo_ref[...]], axis=0).astype(dx_ref.dtype)
        dhalo_ref[...] = dxx[:HALO]
        dz_ref[...] = dz.astype(dz_ref.dtype)
        dba_ref[...] = dba.astype(dba_ref.dtype)
        ds_ref[...] = ds
        dcw_ref[...] += dcw
        dal_ref[...] += dal
        ddt_ref[...] += ddt
        dgn_ref[...] += dgn

    return pl.pallas_call(
        body, name=name, grid=(n_chunks,),
        in_specs=_dn_specs(rev) + [_full_spec(p) for p in params]
        + [pl.BlockSpec((1, DN_HEADS, DN_DIM, DN_DIM), lambda i: (rev(i), 0, 0, 0)),
           pl.BlockSpec((c, DN_WIDTH), lambda i: (rev(i), 0))],
        out_specs=[pl.BlockSpec((c, 3 * DN_WIDTH), lambda i: (rev(i), 0)), pl.BlockSpec((c, DN_WIDTH), lambda i: (rev(i), 0)),
                   pl.BlockSpec((c, LANES), lambda i: (rev(i), 0))] + [_full_spec(p) for p in params],
        out_shape=[jax.ShapeDtypeStruct(xq.shape, BF16), jax.ShapeDtypeStruct(xz.shape, BF16),
                   jax.ShapeDtypeStruct(xba.shape, BF16)] + [jax.ShapeDtypeStruct(p.shape, F32) for p in params],
        scratch_shapes=[pltpu.VMEM((DN_HEADS, DN_DIM, DN_DIM), F32), pltpu.VMEM((HALO, 3 * DN_WIDTH), F32)],
        compiler_params=_cparams(("arbitrary",)),
    )(xq, xq, xz, xba, *params, s_all, d_out)


def _swa_forward(xs, window, dilation, name):
    t = xs.shape[0]
    d, l = dilation, t // dilation
    nb = l // SWA_BLOCK
    view = xs.reshape(l, d * 3 * SWA_WIDTH)
    blk = (SWA_BLOCK, SWA_WIDTH)

    def body(q_ref, kp_ref, kc_ref, vp_ref, vc_ref, o_ref, l_ref):
        o, lse = _swa_block(q_ref[...], kp_ref[...], kc_ref[...], vp_ref[...], vc_ref[...], pl.program_id(1) == 0,
                            window, dilation)
        o_ref[...] = o
        l_ref[...] = lse

    prev = lambda n: jnp.maximum(n - 1, 0)
    o, lse = pl.pallas_call(
        body, name=name, grid=(d, nb),
        in_specs=[pl.BlockSpec(blk, lambda r, n: (n, 3 * r)), pl.BlockSpec(blk, lambda r, n: (prev(n), 3 * r + 1)),
                  pl.BlockSpec(blk, lambda r, n: (n, 3 * r + 1)), pl.BlockSpec(blk, lambda r, n: (prev(n), 3 * r + 2)),
                  pl.BlockSpec(blk, lambda r, n: (n, 3 * r + 2))],
        out_specs=[pl.BlockSpec(blk, lambda r, n: (n, r)), pl.BlockSpec((SWA_BLOCK, LANES), lambda r, n: (n, r))],
        out_shape=[jax.ShapeDtypeStruct((l, d * SWA_WIDTH), F32), jax.ShapeDtypeStruct((l, d * LANES), F32)],
        compiler_params=_cparams(("parallel", "parallel")),
    )(view, view, view, view, view)
    return o.reshape(t, SWA_WIDTH), lse.reshape(t, LANES)


def _swa_backward(xs, d_o, d_lse, acc, window, dilation, name):
    t = xs.shape[0]
    d, l = dilation, t // dilation
    nb = l // SWA_BLOCK
    view = xs.reshape(l, d * 3 * SWA_WIDTH)
    blk = (SWA_BLOCK, SWA_WIDTH)
    has_acc = acc is not None

    def body(*refs):
        q_ref, kp_ref, kc_ref, vp_ref, vc_ref, do_ref, dl_ref = refs[:7]
        acc_refs = refs[7:10] if has_acc else None
        dq_ref, dk_ref, dv_ref, ck_ref, cv_ref = refs[-5:]
        i = pl.program_id(1)
        n = nb - 1 - i

        @pl.when(i == 0)
        def _():
            ck_ref[...] = jnp.zeros_like(ck_ref)
            cv_ref[...] = jnp.zeros_like(cv_ref)

        f = functools.partial(_swa_block, first=n == 0, window=window, dilation=dilation)
        _, pull = jax.vjp(f, q_ref[...], kp_ref[...], kc_ref[...], vp_ref[...], vc_ref[...])
        dq, dkp, dkc, dvp, dvc = pull((do_ref[...], dl_ref[...]))
        dk = dkc + ck_ref[...]
        dv = dvc + cv_ref[...]
        if has_acc:
            dq, dk, dv = dq + acc_refs[0][...], dk + acc_refs[1][...], dv + acc_refs[2][...]
        dq_ref[...] = dq
        dk_ref[...] = dk
        dv_ref[...] = dv
        ck_ref[...] = dkp
        cv_ref[...] = dvp

    cur = lambda i: nb - 1 - i
    prev = lambda i: jnp.maximum(nb - 2 - i, 0)
    own = pl.BlockSpec(blk, lambda r, i: (cur(i), r))
    accs = [a.reshape(l, d * SWA_WIDTH) for a in acc] if has_acc else []
    outs = pl.pallas_call(
        body, name=name, grid=(d, nb),
        in_specs=[pl.BlockSpec(blk, lambda r, i: (cur(i), 3 * r)), pl.BlockSpec(blk, lambda r, i: (prev(i), 3 * r + 1)),
                  pl.BlockSpec(blk, lambda r, i: (cur(i), 3 * r + 1)), pl.BlockSpec(blk, lambda r, i: (prev(i), 3 * r + 2)),
                  pl.BlockSpec(blk, lambda r, i: (cur(i), 3 * r + 2)), own,
                  pl.BlockSpec((SWA_BLOCK, LANES), lambda r, i: (cur(i), r))] + [own] * len(accs),
        out_specs=[own] * 3, out_shape=[jax.ShapeDtypeStruct((l, d * SWA_WIDTH), F32)] * 3,
        scratch_shapes=[pltpu.VMEM(blk, F32), pltpu.VMEM(blk, F32)],
        compiler_params=_cparams(("parallel", "arbitrary")),
    )(view, view, view, view, view, d_o.reshape(l, d * SWA_WIDTH), d_lse.reshape(l, d * LANES), *accs)
    return tuple(o.reshape(t, SWA_WIDTH) for o in outs)


def _loss_head(h, target, gain, name, tile=256):
    t, d = h.shape
    tile = min(tile, t)

    def body(h_ref, t_ref, g_ref, loss_ref, dh_ref, dg_ref):
        def f(hv, gv):
            err = _rms(hv, gv) - t_ref[...]
            return 0.5 * jnp.sum(jnp.mean(err * err, axis=-1, keepdims=True), axis=0, keepdims=True)

        val, pull = jax.vjp(f, h_ref[...], g_ref[...])
        dh, dg = pull(jnp.ones((1, 1), F32))
        dh_ref[...] = dh

        @pl.when(pl.program_id(0) == 0)
        def _():
            loss_ref[...] = jnp.zeros_like(loss_ref)
            dg_ref[...] = jnp.zeros_like(dg_ref)

        loss_ref[...] += jnp.broadcast_to(val, loss_ref.shape)
        dg_ref[...] += dg

    return pl.pallas_call(
        body, name=name, grid=(t // tile,),
        in_specs=[pl.BlockSpec((tile, d), lambda i: (i, 0)), pl.BlockSpec((tile, d), lambda i: (i, 0)), _full_spec(gain)],
        out_specs=[pl.BlockSpec((1, LANES), lambda i: (0, 0)), pl.BlockSpec((tile, d), lambda i: (i, 0)), _full_spec(gain)],
        out_shape=[jax.ShapeDtypeStruct((1, LANES), F32), jax.ShapeDtypeStruct((t, d), F32),
                   jax.ShapeDtypeStruct(gain.shape, F32)],
        compiler_params=_cparams(("arbitrary",)),
    )(h, target, gain)


def _split_w_in(w_in):
    cuts = [0]
    for s in IN_SIZES:
        cuts.append(cuts[-1] + s)
    qkv, z = w_in[:, cuts[0]:cuts[1]], w_in[:, cuts[1]:cuts[2]]
    ba = jnp.pad(w_in[:, cuts[2]:cuts[4]], ((0, 0), (0, LANES - 2 * DN_HEADS)))
    return qkv, z, ba, w_in[:, cuts[4]:cuts[5]], w_in[:, cuts[5]:cuts[6]]


def _lane_pad(v, offset):
    return jnp.pad(v.reshape(1, -1), ((0, 0), (offset, LANES - offset - v.shape[0])))


def _layer_params(sm, i):
    return dict(
        ffn1_norm=sm["ffn1_norm"][i][None], mix_norm=sm["mix_norm"][i][None], xa_norm=sm["xa_norm"][i][None],
        xa_mem_norm=sm["xa_mem_norm"][i][None], ffn2_norm=sm["ffn2_norm"][i][None],
        dn=(sm["dn_conv_w"][i], _lane_pad(sm["dn_a_log"][i], DN_HEADS), _lane_pad(sm["dn_dt_bias"][i], DN_HEADS),
            sm["dn_out_norm"][i][None]),
        sg=(sm["sg_norm_gain"][i][None], sm["sg_norm_bias"][i][None], sm["sg_w_spatial"][i],
            jnp.pad(sm["sg_b_spatial"][i], ((0, 8 - SG_GROUPS), (0, 0)))),
    )


def _ffn_fwd(h, gain, w_gu, w_d, tag):
    n = _rows(_f_rms, [h], [gain], [(h.shape[1], BF16)], tile=512, name=f"{tag}_norm")[0]
    gu = _mm(n, w_gu, NN, out_dtype=F32, name=f"{tag}_gate_up")
    a = _rows(_f_swiglu, [gu], [], [(gu.shape[1] // 2, BF16)], tile=256, name=f"{tag}_act")[0]
    out = _mm(a, w_d, NN, out_dtype=F32, res=h, scale=0.5, name=f"{tag}_down")
    return out, (h, n, gu, a)


def _ffn_bwd(dh, saved, gain, w_gu, w_d, tag):
    h, n, gu, a = saved
    da = _mm(dh, w_d, NT, out_dtype=F32, scale=0.5, name=f"{tag}_down_dx")
    dw_d = _mm(a, dh, TN, out_dtype=F32, scale=0.5, name=f"{tag}_down_dw")
    dgu = _rows_vjp(_f_swiglu, [gu], [], [da], diff=[True], grad_dtypes=[BF16], tile=256, name=f"{tag}_act_bwd")[0]
    dn = _mm(dgu, w_gu, NT, out_dtype=F32, name=f"{tag}_gate_up_dx")
    dw_gu = _mm(n, dgu, TN, out_dtype=F32, name=f"{tag}_gate_up_dw")
    dh_in, dgain = _rows_vjp(_f_rms, [h], [gain], [dn], diff=[True], grad_dtypes=[F32], tile=512, add=[dh],
                             name=f"{tag}_norm_bwd")
    return dh_in, dgain, dw_gu, dw_d


def _mixer_fwd(h, p, w_in, w_out, tag):
    d = h.shape[1]
    n = _rows(_f_rms, [h], [p["mix_norm"]], [(d, BF16)], tile=512, name=f"{tag}_norm")[0]
    w_parts = _split_w_in(w_in)
    xq, xz, xba, xs, xg = (_mm(n, w, NN, out_dtype=F32, name=f"{tag}_in{j}") for j, w in enumerate(w_parts))
    oa, s_all = _dn_forward(xq, xz, xba, p["dn"], name=f"{tag}_dn")
    swa = [_swa_forward(xs, wnd, dil, name=f"{tag}_swa{j}") for j, (wnd, dil) in enumerate(SWA_PATTERNS)]
    ob = _rows(_f_swa_mix, [o for o, _ in swa] + [l for _, l in swa], [], [(SWA_WIDTH, BF16)], tile=512,
               name=f"{tag}_swa_mix")[0]
    oc = _rows(_f_gmlp, [xg], list(p["sg"]), [(SG_WIDTH, BF16)], tile=256, name=f"{tag}_gmlp")[0]
    merged = jnp.concatenate([oa, ob, oc], axis=1)
    out = _mm(merged, w_out, NN, out_dtype=F32, res=h, name=f"{tag}_out")
    return out, (h, n, xq, xz, xba, xs, xg, s_all, swa, merged)


def _mixer_bwd(dh, saved, p, w_in, w_out, tag):
    h, n, xq, xz, xba, xs, xg, s_all, swa, merged = saved
    dw_out = _mm(merged, dh, TN, out_dtype=F32, name=f"{tag}_out_dw")
    doa = _mm(dh, w_out[:DN_WIDTH], NT, out_dtype=F32, name=f"{tag}_out_dxa")
    dob = _mm(dh, w_out[DN_WIDTH:DN_WIDTH + SWA_WIDTH], NT, out_dtype=F32, name=f"{tag}_out_dxb")
    doc = _mm(dh, w_out[DN_WIDTH + SWA_WIDTH:], NT, out_dtype=F32, name=f"{tag}_out_dxc")
    res = _rows_vjp(_f_gmlp, [xg], list(p["sg"]), [doc], diff=[True], grad_dtypes=[BF16], tile=256, name=f"{tag}_gmlp_bwd")
    dxg, d_sg = res[0], res[1:]
    mix_in = [o for o, _ in swa] + [l for _, l in swa]
    d_mix = _rows_vjp(_f_swa_mix, mix_in, [], [dob], diff=[True] * 6, grad_dtypes=[F32] * 6, tile=512,
                      name=f"{tag}_swa_mix_bwd")
    acc = None
    for j, (wnd, dil) in enumerate(SWA_PATTERNS):
        acc = _swa_backward(xs, d_mix[j], d_mix[3 + j], acc, wnd, dil, name=f"{tag}_swa{j}_bwd")
    dxs = jnp.concatenate([a.astype(BF16) for a in acc], axis=1)
    res = _dn_backward(xq, xz, xba, p["dn"], s_all, doa, name=f"{tag}_dn_bwd")
    (dxq, dxz, dxba), d_dn = res[:3], res[3:]
    w_parts = _split_w_in(w_in)
    dn = None
    dws = []
    for j, (dx, w) in enumerate(zip((dxq, dxz, dxba, dxs, dxg), w_parts, strict=True)):
        dn = _mm(dx, w, NT, out_dtype=F32, res=dn, name=f"{tag}_in{j}_dx")
        dws.append(_mm(n, dx, TN, out_dtype=F32, name=f"{tag}_in{j}_dw"))
    dws[2] = dws[2][:, :2 * DN_HEADS]
    dw_in = jnp.concatenate(dws, axis=1)
    dh_in, dgain = _rows_vjp(_f_rms, [h], [p["mix_norm"]], [dn], diff=[True], grad_dtypes=[F32], tile=512, add=[dh],
                             name=f"{tag}_norm_bwd")
    return dh_in, dgain, dw_in, dw_out, d_dn, d_sg


def _xattn_fwd(h, mem, p, w_q, w_kv, w_o, tag):
    d = h.shape[1]
    n = _rows(_f_rms, [h], [p["xa_norm"]], [(d, BF16)], tile=512, name=f"{tag}_norm")[0]
    mn = _rows(_f_rms, [mem], [p["xa_mem_norm"]], [(d, BF16)], tile=512, name=f"{tag}_mem_norm")[0]
    q = _mm(n, w_q, NN, out_dtype=F32, name=f"{tag}_q")
    kv = _mm(mn, w_kv, NN, out_dtype=F32, name=f"{tag}_kv")
    o = _rows(_f_xattn, [q], [kv], [(d, BF16)], tile=256, name=f"{tag}_core")[0]
    out = _mm(o, w_o, NN, out_dtype=F32, res=h, name=f"{tag}_o")
    return out, (h, n, mn, q, kv, o)


def _xattn_bwd(dh, saved, mem, p, w_q, w_kv, w_o, tag):
    h, n, mn, q, kv, o = saved
    do = _mm(dh, w_o, NT, out_dtype=F32, name=f"{tag}_o_dx")
    dw_o = _mm(o, dh, TN, out_dtype=F32, name=f"{tag}_o_dw")
    dq, dkv = _rows_vjp(_f_xattn, [q], [kv], [do], diff=[True], grad_dtypes=[BF16], tile=256, name=f"{tag}_core_bwd")
    dn = _mm(dq, w_q, NT, out_dtype=F32, name=f"{tag}_q_dx")
    dw_q = _mm(n, dq, TN, out_dtype=F32, name=f"{tag}_q_dw")
    dmn = _mm(dkv, w_kv, NT, out_dtype=F32, name=f"{tag}_kv_dx")
    dw_kv = _mm(mn, dkv, TN, out_dtype=F32, name=f"{tag}_kv_dw")
    dmem_gain = _rows_vjp(_f_rms, [mem], [p["xa_mem_norm"]], [dmn], diff=[False], grad_dtypes=[], tile=512,
                          name=f"{tag}_mem_norm_bwd")[0]
    dh_in, dgain = _rows_vjp(_f_rms, [h], [p["xa_norm"]], [dn], diff=[True], grad_dtypes=[F32], tile=512, add=[dh],
                             name=f"{tag}_norm_bwd")
    return dh_in, dgain, dmem_gain, dw_q, dw_kv, dw_o


def kernel(x, mem, ffn1_norm, ffn1_w_gate_up, ffn1_w_down, mix_norm, mix_w_in, dn_conv_w, dn_a_log, dn_dt_bias, dn_out_norm, sg_norm_gain, sg_norm_bias, sg_w_spatial, sg_b_spatial, mix_w_out, xa_norm, xa_mem_norm, xa_w_q, xa_w_kv, xa_w_o, ffn2_norm, ffn2_w_gate_up, ffn2_w_down, final_norm, loss_target, m_ffn1_norm, m_ffn1_w_gate_up, m_ffn1_w_down, m_mix_norm, m_mix_w_in, m_dn_conv_w, m_dn_a_log, m_dn_dt_bias, m_dn_out_norm, m_sg_norm_gain, m_sg_norm_bias, m_sg_w_spatial, m_sg_b_spatial, m_mix_w_out, m_xa_norm, m_xa_mem_norm, m_xa_w_q, m_xa_w_kv, m_xa_w_o, m_ffn2_norm, m_ffn2_w_gate_up, m_ffn2_w_down, m_final_norm, v_ffn1_norm, v_ffn1_w_gate_up, v_ffn1_w_down, v_mix_norm, v_mix_w_in, v_dn_conv_w, v_dn_a_log, v_dn_dt_bias, v_dn_out_norm, v_sg_norm_gain, v_sg_norm_bias, v_sg_w_spatial, v_sg_b_spatial, v_mix_w_out, v_xa_norm, v_xa_mem_norm, v_xa_w_q, v_xa_w_kv, v_xa_w_o, v_ffn2_norm, v_ffn2_w_gate_up, v_ffn2_w_down, v_final_norm):
    args = dict(locals())
    wts = {k: args[k] for k in WEIGHTS}
    mom_m = {k: args["m_" + k] for k in WEIGHTS}
    mom_v = {k: args["v_" + k] for k in WEIGHTS}
    depth = ffn1_norm.shape[0]
    h = x[0]
    mem2 = mem[0]
    target = loss_target[0]

    shard_shapes = [wts[k].shape for k in BIG]
    gathered = _gather_chips(_pack([wts[k] for k in BIG], BF16, PACK_ROWS), name="gather_weights")
    pieces = [_unpack(gathered[j], shard_shapes) for j in range(N_CHIPS)]
    full = {k: jnp.concatenate([pieces[j][i] for j in range(N_CHIPS)], axis=BIG_AXIS[k]) for i, k in enumerate(BIG)}
    conv_all = _gather_chips(_pack([dn_conv_w], F32, 8), name="gather_conv")
    conv_full = jnp.concatenate([_unpack(conv_all[j], [dn_conv_w.shape])[0] for j in range(N_CHIPS)], axis=2)
    small = {k: wts[k] for k in SMALL}
    small["dn_conv_w"] = conv_full

    saved = []
    for i in range(depth):
        p = _layer_params(small, i)
        h, s1 = _ffn_fwd(h, p["ffn1_norm"], full["ffn1_w_gate_up"][i], full["ffn1_w_down"][i], f"l{i}_ffn1")
        h, s2 = _mixer_fwd(h, p, full["mix_w_in"][i], full["mix_w_out"][i], f"l{i}_mix")
        h, s3 = _xattn_fwd(h, mem2, p, full["xa_w_q"][i], full["xa_w_kv"][i], full["xa_w_o"][i], f"l{i}_xa")
        h, s4 = _ffn_fwd(h, p["ffn2_norm"], full["ffn2_w_gate_up"][i], full["ffn2_w_down"][i], f"l{i}_ffn2")
        saved.append((p, s1, s2, s3, s4))
    loss_part, dh, d_final = _loss_head(h, target, final_norm[None], name="loss_head")
    loss = lax.psum(loss_part[0, 0], ("x", "y", "c"))

    g_big = {k: [None] * depth for k in BIG}
    g_small = {k: [None] * depth for k in SMALL if k != "final_norm"}
    g_small["dn_conv_w"] = [None] * depth
    for i in reversed(range(depth)):
        p, s1, s2, s3, s4 = saved[i]
        dh, dg, dw_gu, dw_d = _ffn_bwd(dh, s4, p["ffn2_norm"], full["ffn2_w_gate_up"][i], full["ffn2_w_down"][i], f"l{i}_ffn2")
        g_small["ffn2_norm"][i], g_big["ffn2_w_gate_up"][i], g_big["ffn2_w_down"][i] = dg[0], dw_gu, dw_d
        dh, dg, dmg, dw_q, dw_kv, dw_o = _xattn_bwd(dh, s3, mem2, p, full["xa_w_q"][i], full["xa_w_kv"][i],
                                                    full["xa_w_o"][i], f"l{i}_xa")
        g_small["xa_norm"][i], g_small["xa_mem_norm"][i] = dg[0], dmg[0]
        g_big["xa_w_q"][i], g_big["xa_w_kv"][i], g_big["xa_w_o"][i] = dw_q, dw_kv, dw_o
        dh, dg, dw_in, dw_out, d_dn, d_sg = _mixer_bwd(dh, s2, p, full["mix_w_in"][i], full["mix_w_out"][i], f"l{i}_mix")
        g_small["mix_norm"][i], g_big["mix_w_in"][i], g_big["mix_w_out"][i] = dg[0], dw_in, dw_out
        g_small["dn_conv_w"][i] = d_dn[0]
        g_small["dn_a_log"][i] = d_dn[1][0, DN_HEADS:2 * DN_HEADS]
        g_small["dn_dt_bias"][i] = d_dn[2][0, DN_HEADS:2 * DN_HEADS]
        g_small["dn_out_norm"][i] = d_dn[3][0]
        g_small["sg_norm_gain"][i], g_small["sg_norm_bias"][i] = d_sg[0][0], d_sg[1][0]
        g_small["sg_w_spatial"][i], g_small["sg_b_spatial"][i] = d_sg[2], d_sg[3][:SG_GROUPS]
        dh, dg, dw_gu, dw_d = _ffn_bwd(dh, s1, p["ffn1_norm"], full["ffn1_w_gate_up"][i], full["ffn1_w_down"][i], f"l{i}_ffn1")
        g_small["ffn1_norm"][i], g_big["ffn1_w_gate_up"][i], g_big["ffn1_w_down"][i] = dg[0], dw_gu, dw_d
    grad_x = dh[None]
    g_big = {k: jnp.stack(v) for k, v in g_big.items()}
    g_small = {k: jnp.stack(v) for k, v in g_small.items()}
    g_small["final_norm"] = d_final[0]

    def shard_of(k, j):
        n = wts[k].shape[BIG_AXIS[k]]
        return lax.slice_in_dim(g_big[k], j * n, (j + 1) * n, axis=BIG_AXIS[k])

    contrib = jnp.stack([_pack([shard_of(k, j) for k in BIG], F32, PACK_ROWS) for j in range(N_CHIPS)])
    part = _sum_slots(_scatter_chips(contrib, name="scatter_grads"), name="sum_chips")
    other = _swap_cores(part, name="swap_cores")
    small_names = list(SMALL) + ["dn_conv_w"]
    small_shapes = [g_small[k].shape for k in small_names]
    small_sum = _sum_slots(_gather_all(_pack([g_small[k] for k in small_names], F32, 64), name="gather_small"), name="sum_small")
    gs = dict(zip(small_names, _unpack(small_sum, small_shapes), strict=True))
    n_conv = dn_conv_w.shape[2]
    conv_mine = lax.dynamic_slice_in_dim(gs["dn_conv_w"], (2 * lax.axis_index("x") + lax.axis_index("y")) * n_conv, n_conv, axis=2)

    results = {}
    g_own = dict(zip(BIG, _unpack(part, shard_shapes), strict=True))
    g_other = dict(zip(BIG, _unpack(other, shard_shapes), strict=True))
    for k in BIG:
        shp = wts[k].shape
        two_d = lambda a, _s=shp: a.reshape(-1, _s[-1])
        res = _adamw(two_d(wts[k]), two_d(g_own[k]), two_d(g_other[k]), two_d(mom_m[k]), two_d(mom_v[k]), name=f"adamw_{k}")
        results[k] = [r.reshape(shp) for r in res]
    sm_names = list(SMALL) + ["dn_conv_w"]
    sm_grads = [gs[k] for k in SMALL] + [conv_mine]
    sm_shapes = [wts[k].shape for k in sm_names]
    pk = lambda d: _pack([d[k] for k in sm_names], F32, 64)
    g_pk = _pack(sm_grads, F32, 64)
    res = _adamw(pk(wts), g_pk, jnp.zeros_like(g_pk), pk(mom_m), pk(mom_v), name="adamw_small")
    res = [_unpack(r, sm_shapes) for r in res]
    for i, k in enumerate(sm_names):
        results[k] = [res[j][i] for j in range(4)]

    out = [loss, grad_x]
    for j in range(4):
        out += [results[k][j] for k in WEIGHTS]
    return tuple(out)
```

```python
import functools
import math

import jax
import jax.numpy as jnp
from jax import lax
from jax.experimental import pallas as pl
from jax.experimental.pallas import tpu as pltpu

F32, BF16 = jnp.float32, jnp.bfloat16
HI = lax.Precision.HIGHEST
NN, NT, TN = ((1,), (0,)), ((1,), (1,)), ((0,), (0,))

NORM_EPS = 1e-6
LANES = 128
V7X_VMEM_BYTES = 64 * 2**20
VMEM_LIMIT = V7X_VMEM_BYTES * 3 // 4

DN_HEADS, DN_DIM, DN_CHUNK, DN_CONV, HALO = 4, 128, 64, 4, 8
DN_GROUP = 4
DN_WIDTH = DN_HEADS * DN_DIM
SWA_HEADS, SWA_DIM, SWA_BLOCK = 4, 64, 128
SWA_WIDTH = SWA_HEADS * SWA_DIM
SWA_PATTERNS = ((128, 1), (512, 4), (2048, 16))
SG_GROUPS, SG_DIM, SG_CHUNK = 4, 64, 128
SG_WIDTH = SG_GROUPS * SG_DIM
XA_HEADS = 4
IN_SIZES = (3 * DN_WIDTH, DN_WIDTH, DN_HEADS, DN_HEADS, 3 * SWA_WIDTH, 2 * SG_WIDTH)
ADAM_LR, ADAM_B1, ADAM_B2, ADAM_EPS, ADAM_WD, ADAM_STEP = 0.001, 0.9, 0.999, 1e-08, 0.01, 10
N_CHIPS, N_DEV = 4, 8
PACK_ROWS = 1024
MESH_ID = pl.DeviceIdType.MESH

BIG = ("ffn1_w_gate_up", "ffn1_w_down", "mix_w_in", "mix_w_out", "xa_w_q", "xa_w_kv", "xa_w_o",
       "ffn2_w_gate_up", "ffn2_w_down")
BIG_AXIS = {"ffn1_w_gate_up": 2, "ffn1_w_down": 1, "mix_w_in": 2, "mix_w_out": 1, "xa_w_q": 1, "xa_w_kv": 2,
            "xa_w_o": 1, "ffn2_w_gate_up": 2, "ffn2_w_down": 1}
SMALL = ("ffn1_norm", "mix_norm", "dn_a_log", "dn_dt_bias", "dn_out_norm", "sg_norm_gain", "sg_norm_bias",
         "sg_w_spatial", "sg_b_spatial", "xa_norm", "xa_mem_norm", "ffn2_norm", "final_norm")
WEIGHTS = ("ffn1_norm", "ffn1_w_gate_up", "ffn1_w_down", "mix_norm", "mix_w_in", "dn_conv_w", "dn_a_log",
           "dn_dt_bias", "dn_out_norm", "sg_norm_gain", "sg_norm_bias", "sg_w_spatial", "sg_b_spatial",
           "mix_w_out", "xa_norm", "xa_mem_norm", "xa_w_q", "xa_w_kv", "xa_w_o", "ffn2_norm", "ffn2_w_gate_up",
           "ffn2_w_down", "final_norm")


@functools.partial(jax.custom_vjp, nondiff_argnums=(2,))
def _dlo(a, b, dims):
    return lax.dot_general(a.astype(BF16), b.astype(BF16), (dims, ((), ())), preferred_element_type=F32)


def _dlo_fwd(a, b, dims):
    return _dlo(a, b, dims), (a, b)


def _dlo_bwd(dims, saved, g):
    a, b = saved
    if dims == NN:
        da, db = _dlo(g, b, NT), _dlo(a, g, TN)
    elif dims == NT:
        da, db = _dlo(g, b, NN), _dlo(g, a, TN)
    else:
        da, db = _dlo(b, g, NT), _dlo(a, g, NN)
    return da.astype(a.dtype), db.astype(b.dtype)


_dlo.defvjp(_dlo_fwd, _dlo_bwd)


def _dhi(a, b, dims):
    return lax.dot_general(a, b, (dims, ((), ())), preferred_element_type=F32, precision=HI)


def _sigmoid(x):
    return 1.0 / (1.0 + jnp.exp(-x))


def _silu(x):
    return x * _sigmoid(x)


def _softplus(x):
    return jnp.maximum(x, 0.0) + jnp.log(1.0 + jnp.exp(-jnp.abs(x)))


def _rms(x, gain):
    x = x.astype(F32)
    return x * lax.rsqrt(jnp.mean(x * x, axis=-1, keepdims=True) + NORM_EPS) * gain


def _tile(n, target, unit=LANES):
    best = None
    for t in range(unit, min(n, target) + 1, unit):
        if n % t == 0:
            best = t
    return best if best is not None else n


def _cparams(sem):
    return pltpu.CompilerParams(dimension_semantics=sem, vmem_limit_bytes=VMEM_LIMIT)


def _mm(a, b, dims, *, out_dtype, name, res=None, scale=1.0, tm=512, tn=1408, tk=2816):
    if dims == NN:
        (m, k), n = a.shape, b.shape[1]
    elif dims == NT:
        (m, k), n = a.shape, b.shape[0]
    else:
        (k, m), n = a.shape, b.shape[1]
    tm, tn, tk = _tile(m, tm, LANES if dims == TN else 8), _tile(n, tn), _tile(k, tk)
    nk = k // tk
    a_spec = pl.BlockSpec((tk, tm), lambda i, j, kk: (kk, i)) if dims == TN else pl.BlockSpec((tm, tk), lambda i, j, kk: (i, kk))
    b_spec = pl.BlockSpec((tn, tk), lambda i, j, kk: (j, kk)) if dims == NT else pl.BlockSpec((tk, tn), lambda i, j, kk: (kk, j))
    o_spec = pl.BlockSpec((tm, tn), lambda i, j, kk: (i, j))
    has_res = res is not None

    def finish(acc, r_ref, o_ref):
        val = acc * scale if scale != 1.0 else acc
        if has_res:
            val = r_ref[...].astype(F32) + val
        o_ref[...] = val.astype(o_ref.dtype)

    def body(*refs):
        a_ref, b_ref = refs[0], refs[1]
        r_ref = refs[2] if has_res else None
        part = lax.dot_general(a_ref[...].astype(BF16), b_ref[...].astype(BF16), (dims, ((), ())),
                               preferred_element_type=F32)
        if nk == 1:
            finish(part, r_ref, refs[-1])
            return
        o_ref, acc_ref = refs[-2], refs[-1]
        kk = pl.program_id(2)

        @pl.when(kk == 0)
        def _():
            acc_ref[...] = part

        @pl.when(jnp.logical_and(kk > 0, kk < nk - 1))
        def _():
            acc_ref[...] += part

        @pl.when(kk == nk - 1)
        def _():
            finish(acc_ref[...] + part, r_ref, o_ref)

    return pl.pallas_call(
        body, name=name, grid=(m // tm, n // tn, nk),
        in_specs=[a_spec, b_spec] + ([o_spec] if has_res else []), out_specs=o_spec,
        out_shape=jax.ShapeDtypeStruct((m, n), out_dtype),
        scratch_shapes=[pltpu.VMEM((tm, tn), F32)] if nk > 1 else [],
        compiler_params=_cparams(("parallel", "parallel", "arbitrary")),
    )(*([a, b] + ([res] if has_res else [])))


def _full_spec(p):
    nd = p.ndim
    return pl.BlockSpec(p.shape, lambda i, _nd=nd: (0,) * _nd)


def _rows(f, rows, params, outs, *, tile, name):
    t = rows[0].shape[0]
    tile = min(tile, t)
    nr, npar = len(rows), len(params)

    def body(*refs):
        vals = f(*[r[...] for r in refs[:nr + npar]])
        for o_ref, v in zip(refs[nr + npar:], vals, strict=True):
            o_ref[...] = v.astype(o_ref.dtype)

    res = pl.pallas_call(
        body, name=name, grid=(t // tile,),
        in_specs=[pl.BlockSpec((tile, r.shape[1]), lambda i: (i, 0)) for r in rows] + [_full_spec(p) for p in params],
        out_specs=[pl.BlockSpec((tile, w), lambda i: (i, 0)) for w, _ in outs],
        out_shape=[jax.ShapeDtypeStruct((t, w), d) for w, d in outs],
        compiler_params=_cparams(("parallel",)),
    )(*rows, *params)
    return tuple(res)


def _rows_vjp(f, rows, params, cts, *, diff, grad_dtypes, tile, name, add=None):
    t = rows[0].shape[0]
    tile = min(tile, t)
    nr, npar, nct = len(rows), len(params), len(cts)
    didx = [i for i, d in enumerate(diff) if d]
    add = [None] * len(didx) if add is None else add
    adds = [a for a in add if a is not None]

    def body(*refs):
        row_refs, par_refs = refs[:nr], refs[nr:nr + npar]
        ct_refs = refs[nr + npar:nr + npar + nct]
        add_refs = list(refs[nr + npar + nct:nr + npar + nct + len(adds)])
        out_refs = refs[nr + npar + nct + len(adds):]
        rv = [r[...] for r in row_refs]
        pv = [p[...].astype(F32) for p in par_refs]

        def g(*args):
            full = list(rv)
            for k, i in enumerate(didx):
                full[i] = args[k]
            return f(*full, *args[len(didx):])

        outs, pull = jax.vjp(g, *[rv[i] for i in didx], *pv)
        grads = pull(tuple(c[...].astype(o.dtype) for c, o in zip(ct_refs, outs, strict=True)))
        for k in range(len(didx)):
            val = grads[k].astype(F32)
            if add[k] is not None:
                val = val + add_refs.pop(0)[...].astype(F32)
            out_refs[k][...] = val.astype(out_refs[k].dtype)

        @pl.when(pl.program_id(0) == 0)
        def _():
            for o_ref in out_refs[len(didx):]:
                o_ref[...] = jnp.zeros_like(o_ref)

        for o_ref, gp in zip(out_refs[len(didx):], grads[len(didx):], strict=True):
            o_ref[...] += gp.astype(F32)

    row_spec = lambda a: pl.BlockSpec((tile, a.shape[1]), lambda i: (i, 0))
    res = pl.pallas_call(
        body, name=name, grid=(t // tile,),
        in_specs=[row_spec(r) for r in rows] + [_full_spec(p) for p in params] + [row_spec(c) for c in cts]
        + [row_spec(a) for a in adds],
        out_specs=[row_spec(rows[i]) for i in didx] + [_full_spec(p) for p in params],
        out_shape=[jax.ShapeDtypeStruct(rows[i].shape, d) for i, d in zip(didx, grad_dtypes, strict=True)]
        + [jax.ShapeDtypeStruct(p.shape, F32) for p in params],
        compiler_params=_cparams(("arbitrary",)),
    )(*rows, *params, *cts, *adds)
    return tuple(res)


def _sum_slots(x, name, tile=1024):
    n, r, w = x.shape
    tile = _tile(r, tile, 8)

    def body(x_ref, o_ref):
        acc = x_ref[0].astype(F32)
        for s in range(1, n):
            acc = acc + x_ref[s].astype(F32)
        o_ref[...] = acc

    return pl.pallas_call(
        body, name=name, grid=(r // tile,),
        in_specs=[pl.BlockSpec((n, tile, w), lambda i: (0, i, 0))], out_specs=pl.BlockSpec((tile, w), lambda i: (i, 0)),
        out_shape=jax.ShapeDtypeStruct((r, w), F32), compiler_params=_cparams(("parallel",)),
    )(x)


def _adamw(w, g_a, g_b, m, v, name):
    r, c = w.shape
    tile = _tile(r, max(8, (1 << 18) // c), 8)

    def body(w_ref, ga_ref, gb_ref, m_ref, v_ref, g_out, d_out, m_out, v_out):
        g = ga_ref[...] + gb_ref[...]
        mn = ADAM_B1 * m_ref[...] + (1.0 - ADAM_B1) * g
        vn = ADAM_B2 * v_ref[...] + (1.0 - ADAM_B2) * (g * g)
        m_hat = mn / (1.0 - ADAM_B1 ** ADAM_STEP)
        v_hat = vn / (1.0 - ADAM_B2 ** ADAM_STEP)
        g_out[...] = g
        d_out[...] = -ADAM_LR * (m_hat / (jnp.sqrt(v_hat) + ADAM_EPS) + ADAM_WD * w_ref[...])
        m_out[...] = mn
        v_out[...] = vn

    spec = pl.BlockSpec((tile, c), lambda i: (i, 0))
    return pl.pallas_call(
        body, name=name, grid=(r // tile,), in_specs=[spec] * 5, out_specs=[spec] * 4,
        out_shape=[jax.ShapeDtypeStruct((r, c), F32)] * 4, compiler_params=_cparams(("parallel",)),
    )(w, g_a, g_b, m, v)


def _place():
    return lax.axis_index("x"), lax.axis_index("y"), lax.axis_index("c")


def _flip(v, bit):
    return 1 - v if bit else v


_ANY = pl.BlockSpec(memory_space=pl.ANY)


def _gather_chips(x, name):
    r, w = x.shape

    def body(x_ref, o_ref, send_sems, recv_sems, local_sem):
        mx, my, mc = _place()
        local = pltpu.make_async_copy(x_ref, o_ref.at[2 * mx + my], local_sem)
        local.start()
        copies = []
        for k in range(1, N_CHIPS):
            px, py = _flip(mx, k >> 1), _flip(my, k & 1)
            copies.append(pltpu.make_async_remote_copy(
                src_ref=x_ref, dst_ref=o_ref.at[2 * mx + my], send_sem=send_sems.at[k - 1], recv_sem=recv_sems.at[k - 1],
                device_id=(px, py, mc), device_id_type=MESH_ID))
            copies[-1].start()
        for k in range(1, N_CHIPS):
            px, py = _flip(mx, k >> 1), _flip(my, k & 1)
            pltpu.make_async_remote_copy(
                src_ref=x_ref, dst_ref=o_ref.at[2 * px + py], send_sem=send_sems.at[k - 1], recv_sem=recv_sems.at[k - 1],
                device_id=(px, py, mc), device_id_type=MESH_ID).wait_recv()
        for cp in copies:
            cp.wait_send()
        local.wait()

    return pl.pallas_call(
        body, name=name, in_specs=[_ANY], out_specs=_ANY, out_shape=jax.ShapeDtypeStruct((N_CHIPS, r, w), x.dtype),
        scratch_shapes=[pltpu.SemaphoreType.DMA((N_CHIPS - 1,)), pltpu.SemaphoreType.DMA((N_CHIPS - 1,)),
                        pltpu.SemaphoreType.DMA],
    )(x)


def _scatter_chips(g, name):
    _, r, w = g.shape

    def body(g_ref, o_ref, send_sems, recv_sems, local_sem):
        mx, my, mc = _place()
        local = pltpu.make_async_copy(g_ref.at[2 * mx + my], o_ref.at[0], local_sem)
        local.start()
        copies = []
        for k in range(1, N_CHIPS):
            px, py = _flip(mx, k >> 1), _flip(my, k & 1)
            copies.append(pltpu.make_async_remote_copy(
                src_ref=g_ref.at[2 * px + py], dst_ref=o_ref.at[k], send_sem=send_sems.at[k - 1],
                recv_sem=recv_sems.at[k - 1], device_id=(px, py, mc), device_id_type=MESH_ID))
            copies[-1].start()
        for k in range(1, N_CHIPS):
            px, py = _flip(mx, k >> 1), _flip(my, k & 1)
            pltpu.make_async_remote_copy(
                src_ref=g_ref.at[2 * px + py], dst_ref=o_ref.at[k], send_sem=send_sems.at[k - 1],
                recv_sem=recv_sems.at[k - 1], device_id=(px, py, mc), device_id_type=MESH_ID).wait_recv()
        for cp in copies:
            cp.wait_send()
        local.wait()

    return pl.pallas_call(
        body, name=name, in_specs=[_ANY], out_specs=_ANY, out_shape=jax.ShapeDtypeStruct((N_CHIPS, r, w), g.dtype),
        scratch_shapes=[pltpu.SemaphoreType.DMA((N_CHIPS - 1,)), pltpu.SemaphoreType.DMA((N_CHIPS - 1,)),
                        pltpu.SemaphoreType.DMA],
    )(g)


def _swap_cores(p, name):
    def body(p_ref, o_ref, send_sem, recv_sem):
        mx, my, mc = _place()
        cp = pltpu.make_async_remote_copy(src_ref=p_ref, dst_ref=o_ref, send_sem=send_sem, recv_sem=recv_sem,
                                          device_id=(mx, my, 1 - mc), device_id_type=MESH_ID)
        cp.start()
        cp.wait()

    return pl.pallas_call(
        body, name=name, in_specs=[_ANY], out_specs=_ANY, out_shape=jax.ShapeDtypeStruct(p.shape, p.dtype),
        scratch_shapes=[pltpu.SemaphoreType.DMA, pltpu.SemaphoreType.DMA],
    )(p)


def _gather_all(x, name):
    r, w = x.shape

    def body(x_ref, o_ref, send_sems, recv_sems, local_sem):
        mx, my, mc = _place()
        mine = 4 * mx + 2 * my + mc
        local = pltpu.make_async_copy(x_ref, o_ref.at[mine], local_sem)
        local.start()
        copies = []
        for k in range(1, N_DEV):
            peer = (_flip(mx, k >> 2), _flip(my, (k >> 1) & 1), _flip(mc, k & 1))
            copies.append(pltpu.make_async_remote_copy(
                src_ref=x_ref, dst_ref=o_ref.at[mine], send_sem=send_sems.at[k - 1], recv_sem=recv_sems.at[k - 1],
                device_id=peer, device_id_type=MESH_ID))
            copies[-1].start()
        for k in range(1, N_DEV):
            peer = (_flip(mx, k >> 2), _flip(my, (k >> 1) & 1), _flip(mc, k & 1))
            pltpu.make_async_remote_copy(
                src_ref=x_ref, dst_ref=o_ref.at[4 * peer[0] + 2 * peer[1] + peer[2]], send_sem=send_sems.at[k - 1],
                recv_sem=recv_sems.at[k - 1], device_id=peer, device_id_type=MESH_ID).wait_recv()
        for cp in copies:
            cp.wait_send()
        local.wait()

    return pl.pallas_call(
        body, name=name, in_specs=[_ANY], out_specs=_ANY, out_shape=jax.ShapeDtypeStruct((N_DEV, r, w), x.dtype),
        scratch_shapes=[pltpu.SemaphoreType.DMA((N_DEV - 1,)), pltpu.SemaphoreType.DMA((N_DEV - 1,)),
                        pltpu.SemaphoreType.DMA],
    )(x)


def _pack(parts, dtype, row_unit):
    flat = jnp.concatenate([p.astype(dtype).reshape(-1) for p in parts])
    unit = row_unit * LANES
    pad = (-flat.shape[0]) % unit
    if pad:
        flat = jnp.concatenate([flat, jnp.zeros((pad,), dtype)])
    return flat.reshape(-1, LANES)


def _unpack(packed, shapes):
    flat = packed.reshape(-1)
    out, off = [], 0
    for s in shapes:
        n = math.prod(s)
        out.append(flat[off:off + n].reshape(s))
        off += n
    return out


def _f_rms(x, gain):
    return (_rms(x, gain),)


def _f_swiglu(gu):
    f = gu.shape[1] // 2
    return (_silu(gu[:, :f].astype(F32)) * gu[:, f:].astype(F32),)


def _f_xattn(q, kv):
    d = q.shape[1]
    hd = d // XA_HEADS
    outs = []
    for h in range(XA_HEADS):
        qh, kh, vh = q[:, h * hd:(h + 1) * hd], kv[:, h * hd:(h + 1) * hd], kv[:, d + h * hd:d + (h + 1) * hd]
        s = _dlo(qh, kh, NT) * (hd ** -0.5)
        s = s - jnp.max(s, axis=-1, keepdims=True)
        p = jnp.exp(s)
        p = p / jnp.sum(p, axis=-1, keepdims=True)
        outs.append(_dlo(p, vh, NN))
    return (jnp.concatenate(outs, axis=1),)


def _f_gmlp(uv, gain, bias, w_sp, b_sp):
    r = uv.shape[0]
    act = jax.nn.gelu(uv.astype(F32))
    u, v = act[:, :SG_WIDTH], act[:, SG_WIDTH:]
    mu = jnp.mean(v, axis=-1, keepdims=True)
    var = jnp.mean(jnp.square(v - mu), axis=-1, keepdims=True)
    v = (v - mu) * lax.rsqrt(var + NORM_EPS) * gain + bias
    row = lax.broadcasted_iota(jnp.int32, (SG_CHUNK, SG_CHUNK), 0)
    col = lax.broadcasted_iota(jnp.int32, (SG_CHUNK, SG_CHUNK), 1)
    lane_grp = lax.broadcasted_iota(jnp.int32, (b_sp.shape[0], SG_WIDTH), 1) // SG_DIM
    grp_row = lax.broadcasted_iota(jnp.int32, (b_sp.shape[0], SG_WIDTH), 0)
    spread = jnp.where(lane_grp == grp_row, 1.0, 0.0).astype(F32)
    bias_t = _dhi(b_sp, spread, TN)
    chunks = []
    for c in range(r // SG_CHUNK):
        vc = v[c * SG_CHUNK:(c + 1) * SG_CHUNK]
        parts = []
        for g in range(SG_GROUPS):
            wg = jnp.where(row >= col, w_sp[g], 0.0)
            parts.append(_dlo(wg, vc[:, g * SG_DIM:(g + 1) * SG_DIM], NN))
        chunks.append(jnp.concatenate(parts, axis=1) + bias_t)
    mixed = jnp.concatenate(chunks, axis=0) if len(chunks) > 1 else chunks[0]
    return (u * mixed,)


def _f_swa_mix(o1, o2, o3, l1, l2, l3):
    outs = []
    for h in range(SWA_HEADS):
        ls = [l[:, h:h + 1] for l in (l1, l2, l3)]
        mx = jnp.maximum(jnp.maximum(ls[0], ls[1]), ls[2])
        es = [jnp.exp(l - mx) for l in ls]
        den = es[0] + es[1] + es[2]
        sl = slice(h * SWA_DIM, (h + 1) * SWA_DIM)
        outs.append((es[0] * o1[:, sl] + es[1] * o2[:, sl] + es[2] * o3[:, sl]) / den)
    return (jnp.concatenate(outs, axis=1),)


def _swa_block(q, kp, kc, vp, vc, first, window, dilation):
    span = window // dilation
    qi = lax.broadcasted_iota(jnp.int32, (SWA_BLOCK, 2 * SWA_BLOCK), 0)
    kj = lax.broadcasted_iota(jnp.int32, (SWA_BLOCK, 2 * SWA_BLOCK), 1)
    rel = SWA_BLOCK + qi - kj
    valid = (rel >= 0) & (rel <= span) & jnp.logical_not(jnp.logical_and(first, kj < SWA_BLOCK))
    relf = (rel * dilation).astype(F32)
    kw = jnp.concatenate([kp, kc], axis=0)
    vw = jnp.concatenate([vp, vc], axis=0)
    lane = lax.broadcasted_iota(jnp.int32, (SWA_BLOCK, LANES), 1)
    outs, lse = [], jnp.zeros((SWA_BLOCK, LANES), F32)
    for h in range(SWA_HEADS):
        sl = slice(h * SWA_DIM, (h + 1) * SWA_DIM)
        slope = 2.0 ** (-8.0 * (h + 1) / SWA_HEADS)
        s = _dlo(q[:, sl], kw[:, sl], NT) * (SWA_DIM ** -0.5) - slope * relf
        s = jnp.where(valid, s, -1e30)
        m = jnp.max(s, axis=-1, keepdims=True)
        p = jnp.exp(s - m)
        den = jnp.sum(p, axis=-1, keepdims=True)
        outs.append(_dlo(p, vw[:, sl], NN) / den)
        lse = lse + jnp.where(lane == h, m + jnp.log(den), 0.0)
    return jnp.concatenate(outs, axis=1), lse


@jax.custom_vjp
def _unit_lower_inv(lower):
    c = lower.shape[0]
    row = lax.broadcasted_iota(jnp.int32, (c, c), 0)
    col = lax.broadcasted_iota(jnp.int32, (c, c), 1)
    pw = -lower
    t_inv = jnp.where(row == col, 1.0, 0.0).astype(F32) + pw
    for _ in range(int(math.log2(c)) - 1):
        pw = _dhi(pw, pw, NN)
        t_inv = t_inv + _dhi(t_inv, pw, NN)
    return t_inv


def _unit_lower_inv_fwd(lower):
    t_inv = _unit_lower_inv(lower)
    return t_inv, t_inv


def _unit_lower_inv_bwd(t_inv, g):
    return (-_dhi(_dhi(t_inv, g, TN), t_inv, NT),)


_unit_lower_inv.defvjp(_unit_lower_inv_fwd, _unit_lower_inv_bwd)


def _dn_group(xx, z, ba, state, conv_w, a_log, dt_bias, gain):
    rows, c = z.shape[0], DN_CHUNK
    acc = conv_w[0:1] * xx[HALO - 3:HALO - 3 + rows]
    for j in range(1, DN_CONV):
        acc = acc + conv_w[j:j + 1] * xx[HALO - 3 + j:HALO - 3 + j + rows]
    qkv = _silu(acc)
    beta_all = _sigmoid(ba)
    g_all = -jnp.exp(a_log) * _softplus(ba + dt_bias)
    row = lax.broadcasted_iota(jnp.int32, (c, c), 0)
    col = lax.broadcasted_iota(jnp.int32, (c, c), 1)
    incl, strict = row >= col, row > col
    tri = jnp.where(incl, 1.0, 0.0).astype(F32)
    local = []
    for ci in range(rows // c):
        r0 = ci * c
        gc_all = _dhi(tri, g_all[r0:r0 + c], NN)
        gc_t = gc_all.T
        heads = []
        for h in range(DN_HEADS):
            q = qkv[r0:r0 + c, h * DN_DIM:(h + 1) * DN_DIM]
            k = qkv[r0:r0 + c, DN_WIDTH + h * DN_DIM:DN_WIDTH + (h + 1) * DN_DIM]
            v = qkv[r0:r0 + c, 2 * DN_WIDTH + h * DN_DIM:2 * DN_WIDTH + (h + 1) * DN_DIM]
            q = q * lax.rsqrt(jnp.sum(q * q, axis=-1, keepdims=True) + NORM_EPS) * (DN_DIM ** -0.5)
            k = k * lax.rsqrt(jnp.sum(k * k, axis=-1, keepdims=True) + NORM_EPS)
            beta = beta_all[r0:r0 + c, h:h + 1]
            gc = gc_all[:, DN_HEADS + h:DN_HEADS + h + 1]
            g_last = gc[c - 1:c]
            diff = gc - gc_t[DN_HEADS + h:DN_HEADS + h + 1, :]
            decay = jnp.where(incl, jnp.exp(jnp.where(incl, diff, 0.0)), 0.0)
            kb = k * beta
            t_inv = _unit_lower_inv(jnp.where(strict, _dlo(kb, k, NT) * decay, 0.0))
            e_gc = jnp.exp(gc)
            u = _dlo(t_inv, v * beta, NN)
            w = _dlo(t_inv, kb * e_gc, NN)
            a_qk = jnp.where(incl, _dlo(q, k, NT) * decay, 0.0)
            heads.append((q * e_gc, k * jnp.exp(g_last - gc), u, w, a_qk, jnp.exp(g_last)))
        local.append(heads)
    s = list(state)
    out_rows = []
    for ci in range(rows // c):
        r0 = ci * c
        outs = []
        for h in range(DN_HEADS):
            q_dec, k_tail, u, w, a_qk, e_last = local[ci][h]
            v_new = u - _dlo(w, s[h], NN)
            o = _dlo(q_dec, s[h], NN) + _dlo(a_qk, v_new, NN)
            s[h] = s[h] * e_last + _dlo(k_tail, v_new, TN)
            o = o * lax.rsqrt(jnp.mean(o * o, axis=-1, keepdims=True) + NORM_EPS) * gain
            outs.append(o * _silu(z[r0:r0 + c, h * DN_DIM:(h + 1) * DN_DIM]))
        out_rows.append(jnp.concatenate(outs, axis=1))
    out = jnp.concatenate(out_rows, axis=0) if len(out_rows) > 1 else out_rows[0]
    return out, tuple(s)


def _dn_specs(n_of, rows):
    return [pl.BlockSpec((rows, 3 * DN_WIDTH), lambda i: (n_of(i), 0)),
            pl.BlockSpec((HALO, 3 * DN_WIDTH), lambda i: (jnp.maximum(n_of(i) * (rows // HALO) - 1, 0), 0)),
            pl.BlockSpec((rows, DN_WIDTH), lambda i: (n_of(i), 0)),
            pl.BlockSpec((rows, LANES), lambda i: (n_of(i), 0))]


def _dn_forward(xq, xz, xba, params, name):
    t = xq.shape[0]
    rows = min(DN_GROUP * DN_CHUNK, t)
    n_groups = t // rows

    def body(x_ref, halo_ref, z_ref, ba_ref, cw_ref, al_ref, dt_ref, gn_ref, o_ref, s_all_ref, s_ref):
        n = pl.program_id(0)

        @pl.when(n == 0)
        def _():
            s_ref[...] = jnp.zeros_like(s_ref)

        halo = jnp.where(n > 0, halo_ref[...], 0.0)
        xx = jnp.concatenate([halo, x_ref[...]], axis=0)
        s_all_ref[0] = s_ref[...]
        o, s_new = _dn_group(xx, z_ref[...], ba_ref[...], tuple(s_ref[h] for h in range(DN_HEADS)), cw_ref[...],
                             al_ref[...], dt_ref[...], gn_ref[...])
        o_ref[...] = o.astype(o_ref.dtype)
        for h in range(DN_HEADS):
            s_ref[h] = s_new[h]

    return pl.pallas_call(
        body, name=name, grid=(n_groups,),
        in_specs=_dn_specs(lambda i: i, rows) + [_full_spec(p) for p in params],
        out_specs=[pl.BlockSpec((rows, DN_WIDTH), lambda i: (i, 0)),
                   pl.BlockSpec((1, DN_HEADS, DN_DIM, DN_DIM), lambda i: (i, 0, 0, 0))],
        out_shape=[jax.ShapeDtypeStruct((t, DN_WIDTH), BF16),
                   jax.ShapeDtypeStruct((n_groups, DN_HEADS, DN_DIM, DN_DIM), F32)],
        scratch_shapes=[pltpu.VMEM((DN_HEADS, DN_DIM, DN_DIM), F32)],
        compiler_params=_cparams(("arbitrary",)),
    )(xq, xq, xz, xba, *params)


def _dn_backward(xq, xz, xba, params, s_all, d_out, name):
    t = xq.shape[0]
    rows = min(DN_GROUP * DN_CHUNK, t)
    n_groups = t // rows
    rev = lambda i: n_groups - 1 - i

    def body(x_ref, halo_ref, z_ref, ba_ref, cw_ref, al_ref, dt_ref, gn_ref, s_ref, do_ref,
             dx_ref, dz_ref, dba_ref, dcw_ref, dal_ref, ddt_ref, dgn_ref, ds_ref, dhalo_ref):
        i = pl.program_id(0)
        n = n_groups - 1 - i

        @pl.when(i == 0)
        def _():
            ds_ref[...] = jnp.zeros_like(ds_ref)
            dhalo_ref[...] = jnp.zeros_like(dhalo_ref)
            for r in (dcw_ref, dal_ref, ddt_ref, dgn_ref):
                r[...] = jnp.zeros_like(r)

        halo = jnp.where(n > 0, halo_ref[...], 0.0)
        xx = jnp.concatenate([halo, x_ref[...]], axis=0)
        _, pull = jax.vjp(_dn_group, xx, z_ref[...], ba_ref[...], tuple(s_ref[0, h] for h in range(DN_HEADS)),
                          cw_ref[...], al_ref[...], dt_ref[...], gn_ref[...])
        dxx, dz, dba, ds, dcw, dal, ddt, dgn = pull((do_ref[...].astype(F32), tuple(ds_ref[h] for h in range(DN_HEADS))))
        dx_ref[...] = jnp.concatenate([dxx[HALO:rows], dxx[rows:] + dhalo_ref[...]], axis=0).astype(dx_ref.dtype)
        dhalo_ref[...] = dxx[:HALO]
        dz_ref[...] = dz.astype(dz_ref.dtype)
        dba_ref[...] = dba.astype(dba_ref.dtype)
        for h in range(DN_HEADS):
            ds_ref[h] = ds[h]
        dcw_ref[...] += dcw
        dal_ref[...] += dal
        ddt_ref[...] += ddt
        dgn_ref[...] += dgn

    return pl.pallas_call(
        body, name=name, grid=(n_groups,),
        in_specs=_dn_specs(rev, rows) + [_full_spec(p) for p in params]
        + [pl.BlockSpec((1, DN_HEADS, DN_DIM, DN_DIM), lambda i: (rev(i), 0, 0, 0)),
           pl.BlockSpec((rows, DN_WIDTH), lambda i: (rev(i), 0))],
        out_specs=[pl.BlockSpec((rows, 3 * DN_WIDTH), lambda i: (rev(i), 0)),
                   pl.BlockSpec((rows, DN_WIDTH), lambda i: (rev(i), 0)),
                   pl.BlockSpec((rows, LANES), lambda i: (rev(i), 0))] + [_full_spec(p) for p in params],
        out_shape=[jax.ShapeDtypeStruct(xq.shape, BF16), jax.ShapeDtypeStruct(xz.shape, BF16),
                   jax.ShapeDtypeStruct(xba.shape, BF16)] + [jax.ShapeDtypeStruct(p.shape, F32) for p in params],
        scratch_shapes=[pltpu.VMEM((DN_HEADS, DN_DIM, DN_DIM), F32), pltpu.VMEM((HALO, 3 * DN_WIDTH), F32)],
        compiler_params=_cparams(("arbitrary",)),
    )(xq, xq, xz, xba, *params, s_all, d_out)


def _swa_forward(xs, window, dilation, name):
    t = xs.shape[0]
    d, l = dilation, t // dilation
    nb = l // SWA_BLOCK
    view = xs.reshape(l, d * 3 * SWA_WIDTH)
    blk = (SWA_BLOCK, SWA_WIDTH)

    def body(q_ref, kp_ref, kc_ref, vp_ref, vc_ref, o_ref, l_ref):
        blocks = [r[...].astype(F32) for r in (q_ref, kp_ref, kc_ref, vp_ref, vc_ref)]
        o, lse = _swa_block(*blocks, pl.program_id(1) == 0, window, dilation)
        o_ref[...] = o
        l_ref[...] = lse

    prev = lambda n: jnp.maximum(n - 1, 0)
    o, lse = pl.pallas_call(
        body, name=name, grid=(d, nb),
        in_specs=[pl.BlockSpec(blk, lambda r, n: (n, 3 * r)), pl.BlockSpec(blk, lambda r, n: (prev(n), 3 * r + 1)),
                  pl.BlockSpec(blk, lambda r, n: (n, 3 * r + 1)), pl.BlockSpec(blk, lambda r, n: (prev(n), 3 * r + 2)),
                  pl.BlockSpec(blk, lambda r, n: (n, 3 * r + 2))],
        out_specs=[pl.BlockSpec(blk, lambda r, n: (n, r)), pl.BlockSpec((SWA_BLOCK, LANES), lambda r, n: (n, r))],
        out_shape=[jax.ShapeDtypeStruct((l, d * SWA_WIDTH), F32), jax.ShapeDtypeStruct((l, d * LANES), F32)],
        compiler_params=_cparams(("parallel", "parallel")),
    )(view, view, view, view, view)
    return o.reshape(t, SWA_WIDTH), lse.reshape(t, LANES)


def _swa_backward(xs, d_o, d_lse, acc, window, dilation, name):
    t = xs.shape[0]
    d, l = dilation, t // dilation
    nb = l // SWA_BLOCK
    view = xs.reshape(l, d * 3 * SWA_WIDTH)
    blk = (SWA_BLOCK, SWA_WIDTH)
    has_acc = acc is not None

    def body(*refs):
        q_ref, kp_ref, kc_ref, vp_ref, vc_ref, do_ref, dl_ref = refs[:7]
        acc_refs = refs[7:10] if has_acc else None
        dq_ref, dk_ref, dv_ref, ck_ref, cv_ref = refs[-5:]
        i = pl.program_id(1)
        n = nb - 1 - i

        @pl.when(i == 0)
        def _():
            ck_ref[...] = jnp.zeros_like(ck_ref)
            cv_ref[...] = jnp.zeros_like(cv_ref)

        f = functools.partial(_swa_block, first=n == 0, window=window, dilation=dilation)
        _, pull = jax.vjp(f, *[r[...].astype(F32) for r in (q_ref, kp_ref, kc_ref, vp_ref, vc_ref)])
        dq, dkp, dkc, dvp, dvc = pull((do_ref[...], dl_ref[...]))
        dk = dkc + ck_ref[...]
        dv = dvc + cv_ref[...]
        if has_acc:
            dq, dk, dv = dq + acc_refs[0][...], dk + acc_refs[1][...], dv + acc_refs[2][...]
        dq_ref[...] = dq
        dk_ref[...] = dk
        dv_ref[...] = dv
        ck_ref[...] = dkp
        cv_ref[...] = dvp

    cur = lambda i: nb - 1 - i
    prev = lambda i: jnp.maximum(nb - 2 - i, 0)
    own = pl.BlockSpec(blk, lambda r, i: (cur(i), r))
    accs = [a.reshape(l, d * SWA_WIDTH) for a in acc] if has_acc else []
    outs = pl.pallas_call(
        body, name=name, grid=(d, nb),
        in_specs=[pl.BlockSpec(blk, lambda r, i: (cur(i), 3 * r)), pl.BlockSpec(blk, lambda r, i: (prev(i), 3 * r + 1)),
                  pl.BlockSpec(blk, lambda r, i: (cur(i), 3 * r + 1)), pl.BlockSpec(blk, lambda r, i: (prev(i), 3 * r + 2)),
                  pl.BlockSpec(blk, lambda r, i: (cur(i), 3 * r + 2)), own,
                  pl.BlockSpec((SWA_BLOCK, LANES), lambda r, i: (cur(i), r))] + [own] * len(accs),
        out_specs=[own] * 3, out_shape=[jax.ShapeDtypeStruct((l, d * SWA_WIDTH), F32)] * 3,
        scratch_shapes=[pltpu.VMEM(blk, F32), pltpu.VMEM(blk, F32)],
        compiler_params=_cparams(("parallel", "arbitrary")),
    )(view, view, view, view, view, d_o.reshape(l, d * SWA_WIDTH), d_lse.reshape(l, d * LANES), *accs)
    return tuple(o.reshape(t, SWA_WIDTH) for o in outs)


def _loss_head(h, target, gain, name, tile=256):
    t, d = h.shape
    tile = min(tile, t)

    def body(h_ref, t_ref, g_ref, loss_ref, dh_ref, dg_ref):
        def f(hv, gv):
            err = _rms(hv, gv) - t_ref[...]
            return 0.5 * jnp.sum(jnp.mean(err * err, axis=-1, keepdims=True), axis=0, keepdims=True)

        val, pull = jax.vjp(f, h_ref[...], g_ref[...])
        dh, dg = pull(jnp.ones((1, 1), F32))
        dh_ref[...] = dh

        @pl.when(pl.program_id(0) == 0)
        def _():
            loss_ref[...] = jnp.zeros_like(loss_ref)
            dg_ref[...] = jnp.zeros_like(dg_ref)

        loss_ref[...] += jnp.broadcast_to(val, loss_ref.shape)
        dg_ref[...] += dg

    return pl.pallas_call(
        body, name=name, grid=(t // tile,),
        in_specs=[pl.BlockSpec((tile, d), lambda i: (i, 0)), pl.BlockSpec((tile, d), lambda i: (i, 0)), _full_spec(gain)],
        out_specs=[pl.BlockSpec((1, LANES), lambda i: (0, 0)), pl.BlockSpec((tile, d), lambda i: (i, 0)), _full_spec(gain)],
        out_shape=[jax.ShapeDtypeStruct((1, LANES), F32), jax.ShapeDtypeStruct((t, d), F32),
                   jax.ShapeDtypeStruct(gain.shape, F32)],
        compiler_params=_cparams(("arbitrary",)),
    )(h, target, gain)


def _split_w_in(w_in):
    cuts = [0]
    for s in IN_SIZES:
        cuts.append(cuts[-1] + s)
    qkv, z = w_in[:, cuts[0]:cuts[1]], w_in[:, cuts[1]:cuts[2]]
    ba = jnp.pad(w_in[:, cuts[2]:cuts[4]], ((0, 0), (0, LANES - 2 * DN_HEADS)))
    return qkv, z, ba, w_in[:, cuts[4]:cuts[5]], w_in[:, cuts[5]:cuts[6]]


def _lane_pad(v, offset):
    return jnp.pad(v.reshape(1, -1), ((0, 0), (offset, LANES - offset - v.shape[0])))


def _layer_params(sm, i):
    return dict(
        ffn1_norm=sm["ffn1_norm"][i][None], mix_norm=sm["mix_norm"][i][None], xa_norm=sm["xa_norm"][i][None],
        xa_mem_norm=sm["xa_mem_norm"][i][None], ffn2_norm=sm["ffn2_norm"][i][None],
        dn=(sm["dn_conv_w"][i], _lane_pad(sm["dn_a_log"][i], DN_HEADS), _lane_pad(sm["dn_dt_bias"][i], DN_HEADS),
            sm["dn_out_norm"][i][None]),
        sg=(sm["sg_norm_gain"][i][None], sm["sg_norm_bias"][i][None], sm["sg_w_spatial"][i],
            jnp.pad(sm["sg_b_spatial"][i], ((0, 8 - SG_GROUPS), (0, 0)))),
    )


def _ffn_fwd(h, gain, w_gu, w_d, tag):
    n = _rows(_f_rms, [h], [gain], [(h.shape[1], BF16)], tile=512, name=f"{tag}_norm")[0]
    gu = _mm(n, w_gu, NN, out_dtype=BF16, name=f"{tag}_gate_up")
    a = _rows(_f_swiglu, [gu], [], [(gu.shape[1] // 2, BF16)], tile=256, name=f"{tag}_act")[0]
    out = _mm(a, w_d, NN, out_dtype=F32, res=h, scale=0.5, name=f"{tag}_down")
    return out, (h, n, gu, a)


def _ffn_bwd(dh, saved, gain, w_gu, w_d, tag):
    h, n, gu, a = saved
    da = _mm(dh, w_d, NT, out_dtype=F32, scale=0.5, name=f"{tag}_down_dx")
    dw_d = _mm(a, dh, TN, out_dtype=F32, scale=0.5, name=f"{tag}_down_dw")
    dgu = _rows_vjp(_f_swiglu, [gu], [], [da], diff=[True], grad_dtypes=[BF16], tile=256, name=f"{tag}_act_bwd")[0]
    dn = _mm(dgu, w_gu, NT, out_dtype=F32, name=f"{tag}_gate_up_dx")
    dw_gu = _mm(n, dgu, TN, out_dtype=F32, name=f"{tag}_gate_up_dw")
    dh_in, dgain = _rows_vjp(_f_rms, [h], [gain], [dn], diff=[True], grad_dtypes=[F32], tile=512, add=[dh],
                             name=f"{tag}_norm_bwd")
    return dh_in, dgain, dw_gu, dw_d


def _mixer_fwd(h, p, w_in, w_out, tag):
    d = h.shape[1]
    n = _rows(_f_rms, [h], [p["mix_norm"]], [(d, BF16)], tile=512, name=f"{tag}_norm")[0]
    w_parts = _split_w_in(w_in)
    xq, xz, xba, xs, xg = (_mm(n, w, NN, out_dtype=BF16 if j == 3 else F32, name=f"{tag}_in{j}") for j, w in enumerate(w_parts))
    oa, s_all = _dn_forward(xq, xz, xba, p["dn"], name=f"{tag}_dn")
    swa = [_swa_forward(xs, wnd, dil, name=f"{tag}_swa{j}") for j, (wnd, dil) in enumerate(SWA_PATTERNS)]
    ob = _rows(_f_swa_mix, [o for o, _ in swa] + [l for _, l in swa], [], [(SWA_WIDTH, BF16)], tile=512,
               name=f"{tag}_swa_mix")[0]
    oc = _rows(_f_gmlp, [xg], list(p["sg"]), [(SG_WIDTH, BF16)], tile=256, name=f"{tag}_gmlp")[0]
    merged = jnp.concatenate([oa, ob, oc], axis=1)
    out = _mm(merged, w_out, NN, out_dtype=F32, res=h, name=f"{tag}_out")
    return out, (h, n, xq, xz, xba, xs, xg, s_all, swa, merged)


def _mixer_bwd(dh, saved, p, w_in, w_out, tag):
    h, n, xq, xz, xba, xs, xg, s_all, swa, merged = saved
    dw_out = _mm(merged, dh, TN, out_dtype=F32, name=f"{tag}_out_dw")
    doa = _mm(dh, w_out[:DN_WIDTH], NT, out_dtype=F32, name=f"{tag}_out_dxa")
    dob = _mm(dh, w_out[DN_WIDTH:DN_WIDTH + SWA_WIDTH], NT, out_dtype=F32, name=f"{tag}_out_dxb")
    doc = _mm(dh, w_out[DN_WIDTH + SWA_WIDTH:], NT, out_dtype=F32, name=f"{tag}_out_dxc")
    res = _rows_vjp(_f_gmlp, [xg], list(p["sg"]), [doc], diff=[True], grad_dtypes=[BF16], tile=256, name=f"{tag}_gmlp_bwd")
    dxg, d_sg = res[0], res[1:]
    mix_in = [o for o, _ in swa] + [l for _, l in swa]
    d_mix = _rows_vjp(_f_swa_mix, mix_in, [], [dob], diff=[True] * 6, grad_dtypes=[F32] * 6, tile=512,
                      name=f"{tag}_swa_mix_bwd")
    acc = None
    for j, (wnd, dil) in enumerate(SWA_PATTERNS):
        acc = _swa_backward(xs, d_mix[j], d_mix[3 + j], acc, wnd, dil, name=f"{tag}_swa{j}_bwd")
    dxs = jnp.concatenate([a.astype(BF16) for a in acc], axis=1)
    res = _dn_backward(xq, xz, xba, p["dn"], s_all, doa, name=f"{tag}_dn_bwd")
    (dxq, dxz, dxba), d_dn = res[:3], res[3:]
    w_parts = _split_w_in(w_in)
    dn = None
    dws = []
    for j, (dx, w) in enumerate(zip((dxq, dxz, dxba, dxs, dxg), w_parts, strict=True)):
        dn = _mm(dx, w, NT, out_dtype=F32, res=dn, name=f"{tag}_in{j}_dx")
        dws.append(_mm(n, dx, TN, out_dtype=F32, name=f"{tag}_in{j}_dw"))
    dws[2] = dws[2][:, :2 * DN_HEADS]
    dw_in = jnp.concatenate(dws, axis=1)
    dh_in, dgain = _rows_vjp(_f_rms, [h], [p["mix_norm"]], [dn], diff=[True], grad_dtypes=[F32], tile=512, add=[dh],
                             name=f"{tag}_norm_bwd")
    return dh_in, dgain, dw_in, dw_out, d_dn, d_sg


def _xattn_fwd(h, mem, p, w_q, w_kv, w_o, tag):
    d = h.shape[1]
    n = _rows(_f_rms, [h], [p["xa_norm"]], [(d, BF16)], tile=512, name=f"{tag}_norm")[0]
    mn = _rows(_f_rms, [mem], [p["xa_mem_norm"]], [(d, BF16)], tile=512, name=f"{tag}_mem_norm")[0]
    q = _mm(n, w_q, NN, out_dtype=BF16, name=f"{tag}_q")
    kv = _mm(mn, w_kv, NN, out_dtype=BF16, name=f"{tag}_kv")
    o = _rows(_f_xattn, [q], [kv], [(d, BF16)], tile=256, name=f"{tag}_core")[0]
    out = _mm(o, w_o, NN, out_dtype=F32, res=h, name=f"{tag}_o")
    return out, (h, n, mn, q, kv, o)


def _xattn_bwd(dh, saved, mem, p, w_q, w_kv, w_o, tag):
    h, n, mn, q, kv, o = saved
    do = _mm(dh, w_o, NT, out_dtype=BF16, name=f"{tag}_o_dx")
    dw_o = _mm(o, dh, TN, out_dtype=F32, name=f"{tag}_o_dw")
    dq, dkv = _rows_vjp(_f_xattn, [q], [kv], [do], diff=[True], grad_dtypes=[BF16], tile=256, name=f"{tag}_core_bwd")
    dn = _mm(dq, w_q, NT, out_dtype=F32, name=f"{tag}_q_dx")
    dw_q = _mm(n, dq, TN, out_dtype=F32, name=f"{tag}_q_dw")
    dmn = _mm(dkv, w_kv, NT, out_dtype=F32, name=f"{tag}_kv_dx")
    dw_kv = _mm(mn, dkv, TN, out_dtype=F32, name=f"{tag}_kv_dw")
    dmem_gain = _rows_vjp(_f_rms, [mem], [p["xa_mem_norm"]], [dmn], diff=[False], grad_dtypes=[], tile=512,
                          name=f"{tag}_mem_norm_bwd")[0]
    dh_in, dgain = _rows_vjp(_f_rms, [h], [p["xa_norm"]], [dn], diff=[True], grad_dtypes=[F32], tile=512, add=[dh],
                             name=f"{tag}_norm_bwd")
    return dh_in, dgain, dmem_gain, dw_q, dw_kv, dw_o


def kernel(x, mem, ffn1_norm, ffn1_w_gate_up, ffn1_w_down, mix_norm, mix_w_in, dn_conv_w, dn_a_log, dn_dt_bias, dn_out_norm, sg_norm_gain, sg_norm_bias, sg_w_spatial, sg_b_spatial, mix_w_out, xa_norm, xa_mem_norm, xa_w_q, xa_w_kv, xa_w_o, ffn2_norm, ffn2_w_gate_up, ffn2_w_down, final_norm, loss_target, m_ffn1_norm, m_ffn1_w_gate_up, m_ffn1_w_down, m_mix_norm, m_mix_w_in, m_dn_conv_w, m_dn_a_log, m_dn_dt_bias, m_dn_out_norm, m_sg_norm_gain, m_sg_norm_bias, m_sg_w_spatial, m_sg_b_spatial, m_mix_w_out, m_xa_norm, m_xa_mem_norm, m_xa_w_q, m_xa_w_kv, m_xa_w_o, m_ffn2_norm, m_ffn2_w_gate_up, m_ffn2_w_down, m_final_norm, v_ffn1_norm, v_ffn1_w_gate_up, v_ffn1_w_down, v_mix_norm, v_mix_w_in, v_dn_conv_w, v_dn_a_log, v_dn_dt_bias, v_dn_out_norm, v_sg_norm_gain, v_sg_norm_bias, v_sg_w_spatial, v_sg_b_spatial, v_mix_w_out, v_xa_norm, v_xa_mem_norm, v_xa_w_q, v_xa_w_kv, v_xa_w_o, v_ffn2_norm, v_ffn2_w_gate_up, v_ffn2_w_down, v_final_norm):
    args = dict(locals())
    wts = {k: args[k] for k in WEIGHTS}
    mom_m = {k: args["m_" + k] for k in WEIGHTS}
    mom_v = {k: args["v_" + k] for k in WEIGHTS}
    depth = ffn1_norm.shape[0]
    h = x[0]
    mem2 = mem[0]
    target = loss_target[0]

    shard_shapes = [wts[k].shape for k in BIG]
    gathered = _gather_chips(_pack([wts[k] for k in BIG], BF16, PACK_ROWS), name="gather_weights")
    pieces = [_unpack(gathered[j], shard_shapes) for j in range(N_CHIPS)]
    full = {k: jnp.concatenate([pieces[j][i] for j in range(N_CHIPS)], axis=BIG_AXIS[k]) for i, k in enumerate(BIG)}
    conv_all = _gather_chips(_pack([dn_conv_w], F32, 8), name="gather_conv")
    conv_full = jnp.concatenate([_unpack(conv_all[j], [dn_conv_w.shape])[0] for j in range(N_CHIPS)], axis=2)
    small = {k: wts[k] for k in SMALL}
    small["dn_conv_w"] = conv_full

    saved = []
    for i in range(depth):
        p = _layer_params(small, i)
        h, s1 = _ffn_fwd(h, p["ffn1_norm"], full["ffn1_w_gate_up"][i], full["ffn1_w_down"][i], f"l{i}_ffn1")
        h, s2 = _mixer_fwd(h, p, full["mix_w_in"][i], full["mix_w_out"][i], f"l{i}_mix")
        h, s3 = _xattn_fwd(h, mem2, p, full["xa_w_q"][i], full["xa_w_kv"][i], full["xa_w_o"][i], f"l{i}_xa")
        h, s4 = _ffn_fwd(h, p["ffn2_norm"], full["ffn2_w_gate_up"][i], full["ffn2_w_down"][i], f"l{i}_ffn2")
        saved.append((p, s1, s2, s3, s4))
    loss_part, dh, d_final = _loss_head(h, target, final_norm[None], name="loss_head")
    loss = lax.psum(loss_part[0, 0], ("x", "y", "c"))

    g_big = {k: [None] * depth for k in BIG}
    g_small = {k: [None] * depth for k in SMALL if k != "final_norm"}
    g_small["dn_conv_w"] = [None] * depth
    for i in reversed(range(depth)):
        p, s1, s2, s3, s4 = saved[i]
        dh, dg, dw_gu, dw_d = _ffn_bwd(dh, s4, p["ffn2_norm"], full["ffn2_w_gate_up"][i], full["ffn2_w_down"][i], f"l{i}_ffn2")
        g_small["ffn2_norm"][i], g_big["ffn2_w_gate_up"][i], g_big["ffn2_w_down"][i] = dg[0], dw_gu, dw_d
        dh, dg, dmg, dw_q, dw_kv, dw_o = _xattn_bwd(dh, s3, mem2, p, full["xa_w_q"][i], full["xa_w_kv"][i],
                                                    full["xa_w_o"][i], f"l{i}_xa")
        g_small["xa_norm"][i], g_small["xa_mem_norm"][i] = dg[0], dmg[0]
        g_big["xa_w_q"][i], g_big["xa_w_kv"][i], g_big["xa_w_o"][i] = dw_q, dw_kv, dw_o
        dh, dg, dw_in, dw_out, d_dn, d_sg = _mixer_bwd(dh, s2, p, full["mix_w_in"][i], full["mix_w_out"][i], f"l{i}_mix")
        g_small["mix_norm"][i], g_big["mix_w_in"][i], g_big["mix_w_out"][i] = dg[0], dw_in, dw_out
        g_small["dn_conv_w"][i] = d_dn[0]
        g_small["dn_a_log"][i] = d_dn[1][0, DN_HEADS:2 * DN_HEADS]
        g_small["dn_dt_bias"][i] = d_dn[2][0, DN_HEADS:2 * DN_HEADS]
        g_small["dn_out_norm"][i] = d_dn[3][0]
        g_small["sg_norm_gain"][i], g_small["sg_norm_bias"][i] = d_sg[0][0], d_sg[1][0]
        g_small["sg_w_spatial"][i], g_small["sg_b_spatial"][i] = d_sg[2], d_sg[3][:SG_GROUPS]
        dh, dg, dw_gu, dw_d = _ffn_bwd(dh, s1, p["ffn1_norm"], full["ffn1_w_gate_up"][i], full["ffn1_w_down"][i], f"l{i}_ffn1")
        g_small["ffn1_norm"][i], g_big["ffn1_w_gate_up"][i], g_big["ffn1_w_down"][i] = dg[0], dw_gu, dw_d
    grad_x = dh[None]
    g_big = {k: jnp.stack(v) for k, v in g_big.items()}
    g_small = {k: jnp.stack(v) for k, v in g_small.items()}
    g_small["final_norm"] = d_final[0]

    def shard_of(k, j):
        n = wts[k].shape[BIG_AXIS[k]]
        return lax.slice_in_dim(g_big[k], j * n, (j + 1) * n, axis=BIG_AXIS[k])

    contrib = jnp.stack([_pack([shard_of(k, j) for k in BIG], F32, PACK_ROWS) for j in range(N_CHIPS)])
    part = _sum_slots(_scatter_chips(contrib, name="scatter_grads"), name="sum_chips")
    other = _swap_cores(part, name="swap_cores")
    small_names = list(SMALL) + ["dn_conv_w"]
    small_shapes = [g_small[k].shape for k in small_names]
    small_sum = _sum_slots(_gather_all(_pack([g_small[k] for k in small_names], F32, 64), name="gather_small"), name="sum_small")
    gs = dict(zip(small_names, _unpack(small_sum, small_shapes), strict=True))
    n_conv = dn_conv_w.shape[2]
    conv_mine = lax.dynamic_slice_in_dim(gs["dn_conv_w"], (2 * lax.axis_index("x") + lax.axis_index("y")) * n_conv, n_conv, axis=2)

    results = {}
    g_own = dict(zip(BIG, _unpack(part, shard_shapes), strict=True))
    g_other = dict(zip(BIG, _unpack(other, shard_shapes), strict=True))
    for k in BIG:
        shp = wts[k].shape
        two_d = lambda a, _s=shp: a.reshape(-1, _s[-1])
        res = _adamw(two_d(wts[k]), two_d(g_own[k]), two_d(g_other[k]), two_d(mom_m[k]), two_d(mom_v[k]), name=f"adamw_{k}")
        results[k] = [r.reshape(shp) for r in res]
    sm_names = list(SMALL) + ["dn_conv_w"]
    sm_grads = [gs[k] for k in SMALL] + [conv_mine]
    sm_shapes = [wts[k].shape for k in sm_names]
    pk = lambda d: _pack([d[k] for k in sm_names], F32, 64)
    g_pk = _pack(sm_grads, F32, 64)
    res = _adamw(pk(wts), g_pk, jnp.zeros_like(g_pk), pk(mom_m), pk(mom_v), name="adamw_small")
    res = [_unpack(r, sm_shapes) for r in res]
    for i, k in enumerate(sm_names):
        results[k] = [res[j][i] for j in range(4)]

    out = [loss, grad_x]
    for j in range(4):
        out += [results[k][j] for k in WEIGHTS]
    return tuple(out)
```

```python
import functools
import math

import jax
import jax.numpy as jnp
from jax import lax
from jax.experimental import pallas as pl
from jax.experimental.pallas import tpu as pltpu

F32, BF16 = jnp.float32, jnp.bfloat16
HI = lax.Precision.HIGHEST
NN, NT, TN = ((1,), (0,)), ((1,), (1,)), ((0,), (0,))

NORM_EPS = 1e-6
LANES = 128
V7X_VMEM_BYTES = 64 * 2**20
VMEM_LIMIT = V7X_VMEM_BYTES * 3 // 4

DN_HEADS, DN_DIM, DN_CHUNK, DN_CONV, HALO = 4, 128, 64, 4, 8
DN_GROUP = 4
DN_WIDTH = DN_HEADS * DN_DIM
SWA_HEADS, SWA_DIM, SWA_BLOCK = 4, 64, 128
SWA_WIDTH = SWA_HEADS * SWA_DIM
SWA_PATTERNS = ((128, 1), (512, 4), (2048, 16))
SG_GROUPS, SG_DIM, SG_CHUNK = 4, 64, 128
SG_WIDTH = SG_GROUPS * SG_DIM
XA_HEADS = 4
IN_SIZES = (3 * DN_WIDTH, DN_WIDTH, DN_HEADS, DN_HEADS, 3 * SWA_WIDTH, 2 * SG_WIDTH)
ADAM_LR, ADAM_B1, ADAM_B2, ADAM_EPS, ADAM_WD, ADAM_STEP = 0.001, 0.9, 0.999, 1e-08, 0.01, 10
N_CHIPS, N_DEV = 4, 8
MESH_ID = pl.DeviceIdType.MESH

BIG = ("ffn1_w_gate_up", "ffn1_w_down", "mix_w_in", "mix_w_out", "xa_w_q", "xa_w_kv", "xa_w_o",
       "ffn2_w_gate_up", "ffn2_w_down")
BIG_AXIS = {"ffn1_w_gate_up": 2, "ffn1_w_down": 1, "mix_w_in": 2, "mix_w_out": 1, "xa_w_q": 1, "xa_w_kv": 2,
            "xa_w_o": 1, "ffn2_w_gate_up": 2, "ffn2_w_down": 1}
SMALL = ("ffn1_norm", "mix_norm", "dn_a_log", "dn_dt_bias", "dn_out_norm", "sg_norm_gain", "sg_norm_bias",
         "sg_w_spatial", "sg_b_spatial", "xa_norm", "xa_mem_norm", "ffn2_norm", "final_norm")
WEIGHTS = ("ffn1_norm", "ffn1_w_gate_up", "ffn1_w_down", "mix_norm", "mix_w_in", "dn_conv_w", "dn_a_log",
           "dn_dt_bias", "dn_out_norm", "sg_norm_gain", "sg_norm_bias", "sg_w_spatial", "sg_b_spatial",
           "mix_w_out", "xa_norm", "xa_mem_norm", "xa_w_q", "xa_w_kv", "xa_w_o", "ffn2_norm", "ffn2_w_gate_up",
           "ffn2_w_down", "final_norm")


@functools.partial(jax.custom_vjp, nondiff_argnums=(2,))
def _dlo(a, b, dims):
    return lax.dot_general(a.astype(BF16), b.astype(BF16), (dims, ((), ())), preferred_element_type=F32)


def _dlo_fwd(a, b, dims):
    return _dlo(a, b, dims), (a, b)


def _dlo_bwd(dims, saved, g):
    a, b = saved
    if dims == NN:
        da, db = _dlo(g, b, NT), _dlo(a, g, TN)
    elif dims == NT:
        da, db = _dlo(g, b, NN), _dlo(g, a, TN)
    else:
        da, db = _dlo(b, g, NT), _dlo(a, g, NN)
    return da.astype(a.dtype), db.astype(b.dtype)


_dlo.defvjp(_dlo_fwd, _dlo_bwd)


def _dhi(a, b, dims):
    return lax.dot_general(a, b, (dims, ((), ())), preferred_element_type=F32, precision=HI)


def _sigmoid(x):
    return 1.0 / (1.0 + jnp.exp(-x))


def _silu(x):
    return x * _sigmoid(x)


def _softplus(x):
    return jnp.maximum(x, 0.0) + jnp.log(1.0 + jnp.exp(-jnp.abs(x)))


def _rms(x, gain):
    x = x.astype(F32)
    return x * lax.rsqrt(jnp.mean(x * x, axis=-1, keepdims=True) + NORM_EPS) * gain


def _tile(n, target, unit=LANES):
    best = None
    for t in range(unit, min(n, target) + 1, unit):
        if n % t == 0:
            best = t
    return best if best is not None else n


def _cparams(sem):
    return pltpu.CompilerParams(dimension_semantics=sem, vmem_limit_bytes=VMEM_LIMIT)


def _mm(a, b, dims, *, out_dtype, name, res=None, scale=1.0, tm=512, tn=1408, tk=2816):
    if dims == NN:
        (m, k), n = a.shape, b.shape[1]
    elif dims == NT:
        (m, k), n = a.shape, b.shape[0]
    else:
        (k, m), n = a.shape, b.shape[1]
    tm, tn, tk = _tile(m, tm, LANES if dims == TN else 8), _tile(n, tn), _tile(k, tk)
    nk = k // tk
    a_spec = pl.BlockSpec((tk, tm), lambda i, j, kk: (kk, i)) if dims == TN else pl.BlockSpec((tm, tk), lambda i, j, kk: (i, kk))
    b_spec = pl.BlockSpec((tn, tk), lambda i, j, kk: (j, kk)) if dims == NT else pl.BlockSpec((tk, tn), lambda i, j, kk: (kk, j))
    o_spec = pl.BlockSpec((tm, tn), lambda i, j, kk: (i, j))
    has_res = res is not None

    def finish(acc, r_ref, o_ref):
        val = acc * scale if scale != 1.0 else acc
        if has_res:
            val = r_ref[...].astype(F32) + val
        o_ref[...] = val.astype(o_ref.dtype)

    def body(*refs):
        a_ref, b_ref = refs[0], refs[1]
        r_ref = refs[2] if has_res else None
        part = lax.dot_general(a_ref[...].astype(BF16), b_ref[...].astype(BF16), (dims, ((), ())),
                               preferred_element_type=F32)
        if nk == 1:
            finish(part, r_ref, refs[-1])
            return
        o_ref, acc_ref = refs[-2], refs[-1]
        kk = pl.program_id(2)

        @pl.when(kk == 0)
        def _():
            acc_ref[...] = part

        @pl.when(jnp.logical_and(kk > 0, kk < nk - 1))
        def _():
            acc_ref[...] += part

        @pl.when(kk == nk - 1)
        def _():
            finish(acc_ref[...] + part, r_ref, o_ref)

    return pl.pallas_call(
        body, name=name, grid=(m // tm, n // tn, nk),
        in_specs=[a_spec, b_spec] + ([o_spec] if has_res else []), out_specs=o_spec,
        out_shape=jax.ShapeDtypeStruct((m, n), out_dtype),
        scratch_shapes=[pltpu.VMEM((tm, tn), F32)] if nk > 1 else [],
        compiler_params=_cparams(("parallel", "parallel", "arbitrary")),
    )(*([a, b] + ([res] if has_res else [])))


def _full_spec(p):
    nd = p.ndim
    return pl.BlockSpec(p.shape, lambda i, _nd=nd: (0,) * _nd)


def _rows(f, rows, params, outs, *, tile, name):
    t = rows[0].shape[0]
    tile = min(tile, t)
    nr, npar = len(rows), len(params)

    def body(*refs):
        vals = f(*[r[...] for r in refs[:nr + npar]])
        for o_ref, v in zip(refs[nr + npar:], vals, strict=True):
            o_ref[...] = v.astype(o_ref.dtype)

    res = pl.pallas_call(
        body, name=name, grid=(t // tile,),
        in_specs=[pl.BlockSpec((tile, r.shape[1]), lambda i: (i, 0)) for r in rows] + [_full_spec(p) for p in params],
        out_specs=[pl.BlockSpec((tile, w), lambda i: (i, 0)) for w, _ in outs],
        out_shape=[jax.ShapeDtypeStruct((t, w), d) for w, d in outs],
        compiler_params=_cparams(("parallel",)),
    )(*rows, *params)
    return tuple(res)


def _rows_vjp(f, rows, params, cts, *, diff, grad_dtypes, tile, name, add=None):
    t = rows[0].shape[0]
    tile = min(tile, t)
    nr, npar, nct = len(rows), len(params), len(cts)
    didx = [i for i, d in enumerate(diff) if d]
    add = [None] * len(didx) if add is None else add
    adds = [a for a in add if a is not None]

    def body(*refs):
        row_refs, par_refs = refs[:nr], refs[nr:nr + npar]
        ct_refs = refs[nr + npar:nr + npar + nct]
        add_refs = list(refs[nr + npar + nct:nr + npar + nct + len(adds)])
        out_refs = refs[nr + npar + nct + len(adds):]
        rv = [r[...] for r in row_refs]
        pv = [p[...].astype(F32) for p in par_refs]

        def g(*args):
            full = list(rv)
            for k, i in enumerate(didx):
                full[i] = args[k]
            return f(*full, *args[len(didx):])

        outs, pull = jax.vjp(g, *[rv[i] for i in didx], *pv)
        grads = pull(tuple(c[...].astype(o.dtype) for c, o in zip(ct_refs, outs, strict=True)))
        for k in range(len(didx)):
            val = grads[k].astype(F32)
            if add[k] is not None:
                val = val + add_refs.pop(0)[...].astype(F32)
            out_refs[k][...] = val.astype(out_refs[k].dtype)

        @pl.when(pl.program_id(0) == 0)
        def _():
            for o_ref in out_refs[len(didx):]:
                o_ref[...] = jnp.zeros_like(o_ref)

        for o_ref, gp in zip(out_refs[len(didx):], grads[len(didx):], strict=True):
            o_ref[...] += gp.astype(F32)

    row_spec = lambda a: pl.BlockSpec((tile, a.shape[1]), lambda i: (i, 0))
    res = pl.pallas_call(
        body, name=name, grid=(t // tile,),
        in_specs=[row_spec(r) for r in rows] + [_full_spec(p) for p in params] + [row_spec(c) for c in cts]
        + [row_spec(a) for a in adds],
        out_specs=[row_spec(rows[i]) for i in didx] + [_full_spec(p) for p in params],
        out_shape=[jax.ShapeDtypeStruct(rows[i].shape, d) for i, d in zip(didx, grad_dtypes, strict=True)]
        + [jax.ShapeDtypeStruct(p.shape, F32) for p in params],
        compiler_params=_cparams(("arbitrary",)),
    )(*rows, *params, *cts, *adds)
    return tuple(res)


def _sum_list(arrs, out_dtype, name):
    shape = arrs[0].shape
    views = [a.reshape(-1, shape[-1]) for a in arrs]
    r, c = views[0].shape
    tile = _tile(r, max(16, (1 << 18) // c), 16)

    def body(*refs):
        acc = refs[0][...].astype(F32)
        for ref in refs[1:-1]:
            acc = acc + ref[...].astype(F32)
        refs[-1][...] = acc.astype(refs[-1].dtype)

    spec = pl.BlockSpec((tile, c), lambda i: (i, 0))
    return pl.pallas_call(
        body, name=name, grid=(r // tile,), in_specs=[spec] * len(views), out_specs=spec,
        out_shape=jax.ShapeDtypeStruct((r, c), out_dtype), compiler_params=_cparams(("parallel",)),
    )(*views).reshape(shape)


def _sum_slots(x, out_dtype, name):
    n, r, c = x.shape
    tile = _tile(r, max(16, (1 << 18) // c), 16)

    def body(*refs):
        acc = refs[0][...].astype(F32)
        for ref in refs[1:-1]:
            acc = acc + ref[...].astype(F32)
        refs[-1][...] = acc.astype(refs[-1].dtype)

    return pl.pallas_call(
        body, name=name, grid=(r // tile,),
        in_specs=[pl.BlockSpec((None, tile, c), lambda i, _s=s_: (_s, i, 0)) for s_ in range(n)],
        out_specs=pl.BlockSpec((tile, c), lambda i: (i, 0)),
        out_shape=jax.ShapeDtypeStruct((r, c), out_dtype), compiler_params=_cparams(("parallel",)),
    )(*([x] * n))


def _adamw(w, g, m, v, name):
    r, c = w.shape
    tile = _tile(r, max(8, (1 << 18) // c), 8)

    def body(w_ref, g_ref, m_ref, v_ref, d_out, m_out, v_out):
        g = g_ref[...]
        mn = ADAM_B1 * m_ref[...] + (1.0 - ADAM_B1) * g
        vn = ADAM_B2 * v_ref[...] + (1.0 - ADAM_B2) * (g * g)
        m_hat = mn / (1.0 - ADAM_B1 ** ADAM_STEP)
        v_hat = vn / (1.0 - ADAM_B2 ** ADAM_STEP)
        d_out[...] = -ADAM_LR * (m_hat / (jnp.sqrt(v_hat) + ADAM_EPS) + ADAM_WD * w_ref[...])
        m_out[...] = mn
        v_out[...] = vn

    spec = pl.BlockSpec((tile, c), lambda i: (i, 0))
    return pl.pallas_call(
        body, name=name, grid=(r // tile,), in_specs=[spec] * 4, out_specs=[spec] * 3,
        out_shape=[jax.ShapeDtypeStruct((r, c), F32)] * 3, compiler_params=_cparams(("parallel",)),
    )(w, g, m, v)


def _place():
    return lax.axis_index("x"), lax.axis_index("y"), lax.axis_index("c")


def _flip(v, bit):
    return 1 - v if bit else v


_ANY = pl.BlockSpec(memory_space=pl.ANY)


def _quarter(ref, j, shape, kind):
    if kind == "row":
        return ref.at[pl.ds(j * shape[0], shape[0])]
    if kind == "col":
        return ref.at[:, pl.ds(j * shape[1], shape[1])]
    return ref.at[j]


def _whole_shape(shape, kind):
    if kind == "row":
        return (N_CHIPS * shape[0],) + tuple(shape[1:])
    if kind == "col":
        return (shape[0], N_CHIPS * shape[1]) + tuple(shape[2:])
    return (N_CHIPS,) + tuple(shape)


def _dma_sems(*counts):
    return [pltpu.SemaphoreType.DMA((n,)) for n in counts]


def _gather_weights(shards, kinds, name):
    n = len(shards)
    shapes = [s.shape[1:] for s in shards]

    def body(*refs):
        ins, outs = refs[:n], refs[n:2 * n]
        send_sems, recv_sems, pair_send, pair_recv, local_sems = refs[2 * n:]
        mx, my, mc = _place()
        me = 2 * mx + my
        started, local = [], []
        for t in range(n):
            src = ins[t].at[mc]
            local.append(pltpu.make_async_copy(src, _quarter(outs[t].at[mc], me, shapes[t], kinds[t]), local_sems.at[t]))
            local[-1].start()
            for k in range(1, N_CHIPS):
                started.append(pltpu.make_async_remote_copy(
                    src_ref=src, dst_ref=_quarter(outs[t].at[mc], me, shapes[t], kinds[t]),
                    send_sem=send_sems.at[3 * t + k - 1], recv_sem=recv_sems.at[3 * t + k - 1],
                    device_id=(_flip(mx, k >> 1), _flip(my, k & 1), mc), device_id_type=MESH_ID))
                started[-1].start()
        for t in range(n):
            for k in range(1, N_CHIPS):
                px, py = _flip(mx, k >> 1), _flip(my, k & 1)
                pltpu.make_async_remote_copy(
                    src_ref=ins[t].at[mc], dst_ref=_quarter(outs[t].at[mc], 2 * px + py, shapes[t], kinds[t]),
                    send_sem=send_sems.at[3 * t + k - 1], recv_sem=recv_sems.at[3 * t + k - 1],
                    device_id=(px, py, mc), device_id_type=MESH_ID).wait_recv()
        for cp in local:
            cp.wait()
        for t in range(n):
            started.append(pltpu.make_async_remote_copy(
                src_ref=outs[t].at[mc], dst_ref=outs[t].at[mc], send_sem=pair_send.at[t], recv_sem=pair_recv.at[t],
                device_id=(mx, my, 1 - mc), device_id_type=MESH_ID))
            started[-1].start()
        for t in range(n):
            pltpu.make_async_remote_copy(
                src_ref=outs[t].at[1 - mc], dst_ref=outs[t].at[1 - mc], send_sem=pair_send.at[t], recv_sem=pair_recv.at[t],
                device_id=(mx, my, 1 - mc), device_id_type=MESH_ID).wait_recv()
        for cp in started:
            cp.wait_send()

    return pl.pallas_call(
        body, name=name, in_specs=[_ANY] * n, out_specs=[_ANY] * n,
        out_shape=[jax.ShapeDtypeStruct((2,) + _whole_shape(sh, kd), s.dtype) for s, sh, kd in zip(shards, shapes, kinds)],
        scratch_shapes=_dma_sems(3 * n, 3 * n, n, n, n),
    )(*shards)


def _swap_layers(g0, g1, name):
    n = len(g0)

    def body(*refs):
        a, b = refs[:n], refs[n:2 * n]
        own, got = refs[2 * n:3 * n], refs[3 * n:4 * n]
        send_sems, recv_sems, local_sems = refs[4 * n:]
        mx, my, mc = _place()
        peer = (mx, my, 1 - mc)
        for keep, give, core in ((a, b, 0), (b, a, 1)):
            @pl.when(mc == core)
            def _():
                for t in range(n):
                    pltpu.make_async_copy(keep[t], own[t], local_sems.at[t]).start()
                    pltpu.make_async_remote_copy(src_ref=give[t], dst_ref=got[t], send_sem=send_sems.at[t],
                                                 recv_sem=recv_sems.at[t], device_id=peer, device_id_type=MESH_ID).start()
        for t in range(n):
            cp = pltpu.make_async_remote_copy(src_ref=a[t], dst_ref=got[t], send_sem=send_sems.at[t],
                                              recv_sem=recv_sems.at[t], device_id=peer, device_id_type=MESH_ID)
            cp.wait_recv()
            cp.wait_send()
            pltpu.make_async_copy(a[t], own[t], local_sems.at[t]).wait()

    shp = [jax.ShapeDtypeStruct(g.shape, g.dtype) for g in g0]
    res = pl.pallas_call(
        body, name=name, in_specs=[_ANY] * (2 * n), out_specs=[_ANY] * (2 * n), out_shape=shp + shp,
        scratch_shapes=_dma_sems(n, n, n),
    )(*g0, *g1)
    return res[:n], res[n:]


def _scatter_quarters(gs, shapes, kinds, name):
    n = len(gs)

    def body(*refs):
        ins, outs = refs[:n], refs[n:2 * n]
        send_sems, recv_sems, local_sems = refs[2 * n:]
        mx, my, mc = _place()
        started = []
        for t in range(n):
            started.append(pltpu.make_async_copy(_quarter(ins[t], 2 * mx + my, shapes[t], kinds[t]), outs[t].at[0], local_sems.at[t]))
            started[-1].start()
        sends = []
        for t in range(n):
            for k in range(1, N_CHIPS):
                px, py = _flip(mx, k >> 1), _flip(my, k & 1)
                sends.append(pltpu.make_async_remote_copy(
                    src_ref=_quarter(ins[t], 2 * px + py, shapes[t], kinds[t]), dst_ref=outs[t].at[k],
                    send_sem=send_sems.at[3 * t + k - 1], recv_sem=recv_sems.at[3 * t + k - 1],
                    device_id=(px, py, mc), device_id_type=MESH_ID))
                sends[-1].start()
        for cp in sends:
            cp.wait_recv()
        for cp in sends:
            cp.wait_send()
        for cp in started:
            cp.wait()

    return pl.pallas_call(
        body, name=name, in_specs=[_ANY] * n, out_specs=[_ANY] * n,
        out_shape=[jax.ShapeDtypeStruct((N_CHIPS,) + tuple(sh), g.dtype) for g, sh in zip(gs, shapes)],
        scratch_shapes=_dma_sems(3 * n, 3 * n, n),
    )(*gs)


def _pair_layers(parts, name):
    n = len(parts)

    def body(*refs):
        ins, outs = refs[:n], refs[n:2 * n]
        send_sems, recv_sems, local_sems = refs[2 * n:]
        mx, my, mc = _place()
        local, sends = [], []
        for t in range(n):
            local.append(pltpu.make_async_copy(ins[t], outs[t].at[mc], local_sems.at[t]))
            local[-1].start()
            sends.append(pltpu.make_async_remote_copy(src_ref=ins[t], dst_ref=outs[t].at[mc], send_sem=send_sems.at[t],
                                                      recv_sem=recv_sems.at[t], device_id=(mx, my, 1 - mc),
                                                      device_id_type=MESH_ID))
            sends[-1].start()
        for t in range(n):
            pltpu.make_async_remote_copy(src_ref=ins[t], dst_ref=outs[t].at[1 - mc], send_sem=send_sems.at[t],
                                         recv_sem=recv_sems.at[t], device_id=(mx, my, 1 - mc),
                                         device_id_type=MESH_ID).wait_recv()
        for cp in sends:
            cp.wait_send()
        for cp in local:
            cp.wait()

    return pl.pallas_call(
        body, name=name, in_specs=[_ANY] * n, out_specs=[_ANY] * n,
        out_shape=[jax.ShapeDtypeStruct((2,) + tuple(p.shape), p.dtype) for p in parts],
        scratch_shapes=_dma_sems(n, n, n),
    )(*parts)


def _gather_all(x, name):
    r, w = x.shape

    def body(x_ref, o_ref, send_sems, recv_sems, local_sem):
        mx, my, mc = _place()
        mine = 4 * mx + 2 * my + mc
        local = pltpu.make_async_copy(x_ref, o_ref.at[mine], local_sem)
        local.start()
        copies = []
        for k in range(1, N_DEV):
            peer = (_flip(mx, k >> 2), _flip(my, (k >> 1) & 1), _flip(mc, k & 1))
            copies.append(pltpu.make_async_remote_copy(
                src_ref=x_ref, dst_ref=o_ref.at[mine], send_sem=send_sems.at[k - 1], recv_sem=recv_sems.at[k - 1],
                device_id=peer, device_id_type=MESH_ID))
            copies[-1].start()
        for k in range(1, N_DEV):
            peer = (_flip(mx, k >> 2), _flip(my, (k >> 1) & 1), _flip(mc, k & 1))
            pltpu.make_async_remote_copy(
                src_ref=x_ref, dst_ref=o_ref.at[4 * peer[0] + 2 * peer[1] + peer[2]], send_sem=send_sems.at[k - 1],
                recv_sem=recv_sems.at[k - 1], device_id=peer, device_id_type=MESH_ID).wait_recv()
        for cp in copies:
            cp.wait_send()
        local.wait()

    return pl.pallas_call(
        body, name=name, in_specs=[_ANY], out_specs=_ANY, out_shape=jax.ShapeDtypeStruct((N_DEV, r, w), x.dtype),
        scratch_shapes=[pltpu.SemaphoreType.DMA((N_DEV - 1,)), pltpu.SemaphoreType.DMA((N_DEV - 1,)),
                        pltpu.SemaphoreType.DMA],
    )(x)


def _pack(parts, dtype, row_unit):
    flat = jnp.concatenate([p.astype(dtype).reshape(-1) for p in parts])
    unit = row_unit * LANES
    pad = (-flat.shape[0]) % unit
    if pad:
        flat = jnp.concatenate([flat, jnp.zeros((pad,), dtype)])
    return flat.reshape(-1, LANES)


def _unpack(packed, shapes):
    flat = packed.reshape(-1)
    out, off = [], 0
    for s in shapes:
        n = math.prod(s)
        out.append(flat[off:off + n].reshape(s))
        off += n
    return out


def _f_rms(x, gain):
    return (_rms(x, gain),)


def _f_swiglu(gu):
    f = gu.shape[1] // 2
    return (_silu(gu[:, :f].astype(F32)) * gu[:, f:].astype(F32),)


def _f_xattn(q, kv):
    d = q.shape[1]
    hd = d // XA_HEADS
    outs = []
    for h in range(XA_HEADS):
        qh, kh, vh = q[:, h * hd:(h + 1) * hd], kv[:, h * hd:(h + 1) * hd], kv[:, d + h * hd:d + (h + 1) * hd]
        s = _dlo(qh, kh, NT) * (hd ** -0.5)
        s = s - jnp.max(s, axis=-1, keepdims=True)
        p = jnp.exp(s)
        p = p / jnp.sum(p, axis=-1, keepdims=True)
        outs.append(_dlo(p, vh, NN))
    return (jnp.concatenate(outs, axis=1),)


def _f_gmlp(uv, gain, bias, w_sp, b_sp):
    r = uv.shape[0]
    act = jax.nn.gelu(uv.astype(F32))
    u, v = act[:, :SG_WIDTH], act[:, SG_WIDTH:]
    mu = jnp.mean(v, axis=-1, keepdims=True)
    var = jnp.mean(jnp.square(v - mu), axis=-1, keepdims=True)
    v = (v - mu) * lax.rsqrt(var + NORM_EPS) * gain + bias
    row = lax.broadcasted_iota(jnp.int32, (SG_CHUNK, SG_CHUNK), 0)
    col = lax.broadcasted_iota(jnp.int32, (SG_CHUNK, SG_CHUNK), 1)
    lane_grp = lax.broadcasted_iota(jnp.int32, (b_sp.shape[0], SG_WIDTH), 1) // SG_DIM
    grp_row = lax.broadcasted_iota(jnp.int32, (b_sp.shape[0], SG_WIDTH), 0)
    spread = jnp.where(lane_grp == grp_row, 1.0, 0.0).astype(F32)
    bias_t = _dhi(b_sp, spread, TN)
    chunks = []
    for c in range(r // SG_CHUNK):
        vc = v[c * SG_CHUNK:(c + 1) * SG_CHUNK]
        parts = []
        for g in range(SG_GROUPS):
            wg = jnp.where(row >= col, w_sp[g], 0.0)
            parts.append(_dlo(wg, vc[:, g * SG_DIM:(g + 1) * SG_DIM], NN))
        chunks.append(jnp.concatenate(parts, axis=1) + bias_t)
    mixed = jnp.concatenate(chunks, axis=0) if len(chunks) > 1 else chunks[0]
    return (u * mixed,)


def _f_swa_mix(o1, o2, o3, l1, l2, l3):
    outs = []
    for h in range(SWA_HEADS):
        ls = [l[:, h:h + 1] for l in (l1, l2, l3)]
        mx = jnp.maximum(jnp.maximum(ls[0], ls[1]), ls[2])
        es = [jnp.exp(l - mx) for l in ls]
        den = es[0] + es[1] + es[2]
        sl = slice(h * SWA_DIM, (h + 1) * SWA_DIM)
        outs.append((es[0] * o1[:, sl] + es[1] * o2[:, sl] + es[2] * o3[:, sl]) / den)
    return (jnp.concatenate(outs, axis=1),)


def _swa_block(q, kp, kc, vp, vc, first, window, dilation):
    span = window // dilation
    qi = lax.broadcasted_iota(jnp.int32, (SWA_BLOCK, 2 * SWA_BLOCK), 0)
    kj = lax.broadcasted_iota(jnp.int32, (SWA_BLOCK, 2 * SWA_BLOCK), 1)
    rel = SWA_BLOCK + qi - kj
    valid = (rel >= 0) & (rel <= span) & jnp.logical_not(jnp.logical_and(first, kj < SWA_BLOCK))
    relf = (rel * dilation).astype(F32)
    kw = jnp.concatenate([kp, kc], axis=0)
    vw = jnp.concatenate([vp, vc], axis=0)
    lane = lax.broadcasted_iota(jnp.int32, (SWA_BLOCK, LANES), 1)
    outs, lse = [], jnp.zeros((SWA_BLOCK, LANES), F32)
    for h in range(SWA_HEADS):
        sl = slice(h * SWA_DIM, (h + 1) * SWA_DIM)
        slope = 2.0 ** (-8.0 * (h + 1) / SWA_HEADS)
        s = _dlo(q[:, sl], kw[:, sl], NT) * (SWA_DIM ** -0.5) - slope * relf
        s = jnp.where(valid, s, -1e30)
        m = jnp.max(s, axis=-1, keepdims=True)
        p = jnp.exp(s - m)
        den = jnp.sum(p, axis=-1, keepdims=True)
        outs.append(_dlo(p, vw[:, sl], NN) / den)
        lse = lse + jnp.where(lane == h, m + jnp.log(den), 0.0)
    return jnp.concatenate(outs, axis=1), lse


@jax.custom_vjp
def _unit_lower_inv(lower):
    c = lower.shape[0]
    row = lax.broadcasted_iota(jnp.int32, (c, c), 0)
    col = lax.broadcasted_iota(jnp.int32, (c, c), 1)
    pw = -lower
    t_inv = jnp.where(row == col, 1.0, 0.0).astype(F32) + pw
    for _ in range(int(math.log2(c)) - 1):
        pw = _dhi(pw, pw, NN)
        t_inv = t_inv + _dhi(t_inv, pw, NN)
    return t_inv


def _unit_lower_inv_fwd(lower):
    t_inv = _unit_lower_inv(lower)
    return t_inv, t_inv


def _unit_lower_inv_bwd(t_inv, g):
    return (-_dhi(_dhi(t_inv, g, TN), t_inv, NT),)


_unit_lower_inv.defvjp(_unit_lower_inv_fwd, _unit_lower_inv_bwd)


def _dn_group(xx, z, ba, state, conv_w, a_log, dt_bias, gain):
    rows, c = z.shape[0], DN_CHUNK
    acc = conv_w[0:1] * xx[HALO - 3:HALO - 3 + rows]
    for j in range(1, DN_CONV):
        acc = acc + conv_w[j:j + 1] * xx[HALO - 3 + j:HALO - 3 + j + rows]
    qkv = _silu(acc)
    beta_all = _sigmoid(ba)
    g_all = -jnp.exp(a_log) * _softplus(ba + dt_bias)
    row = lax.broadcasted_iota(jnp.int32, (c, c), 0)
    col = lax.broadcasted_iota(jnp.int32, (c, c), 1)
    incl, strict = row >= col, row > col
    tri = jnp.where(incl, 1.0, 0.0).astype(F32)
    local = []
    for ci in range(rows // c):
        r0 = ci * c
        gc_all = _dhi(tri, g_all[r0:r0 + c], NN)
        gc_t = gc_all.T
        heads = []
        for h in range(DN_HEADS):
            q = qkv[r0:r0 + c, h * DN_DIM:(h + 1) * DN_DIM]
            k = qkv[r0:r0 + c, DN_WIDTH + h * DN_DIM:DN_WIDTH + (h + 1) * DN_DIM]
            v = qkv[r0:r0 + c, 2 * DN_WIDTH + h * DN_DIM:2 * DN_WIDTH + (h + 1) * DN_DIM]
            q = q * lax.rsqrt(jnp.sum(q * q, axis=-1, keepdims=True) + NORM_EPS) * (DN_DIM ** -0.5)
            k = k * lax.rsqrt(jnp.sum(k * k, axis=-1, keepdims=True) + NORM_EPS)
            beta = beta_all[r0:r0 + c, h:h + 1]
            gc = gc_all[:, DN_HEADS + h:DN_HEADS + h + 1]
            g_last = gc[c - 1:c]
            diff = gc - gc_t[DN_HEADS + h:DN_HEADS + h + 1, :]
            decay = jnp.where(incl, jnp.exp(jnp.where(incl, diff, 0.0)), 0.0)
            kb = k * beta
            t_inv = _unit_lower_inv(jnp.where(strict, _dlo(kb, k, NT) * decay, 0.0))
            e_gc = jnp.exp(gc)
            u = _dlo(t_inv, v * beta, NN)
            w = _dlo(t_inv, kb * e_gc, NN)
            a_qk = jnp.where(incl, _dlo(q, k, NT) * decay, 0.0)
            heads.append((q * e_gc, k * jnp.exp(g_last - gc), u, w, a_qk, jnp.exp(g_last)))
        local.append(heads)
    s = list(state)
    out_rows = []
    for ci in range(rows // c):
        r0 = ci * c
        outs = []
        for h in range(DN_HEADS):
            q_dec, k_tail, u, w, a_qk, e_last = local[ci][h]
            v_new = u - _dlo(w, s[h], NN)
            o = _dlo(q_dec, s[h], NN) + _dlo(a_qk, v_new, NN)
            s[h] = s[h] * e_last + _dlo(k_tail, v_new, TN)
            o = o * lax.rsqrt(jnp.mean(o * o, axis=-1, keepdims=True) + NORM_EPS) * gain
            outs.append(o * _silu(z[r0:r0 + c, h * DN_DIM:(h + 1) * DN_DIM]))
        out_rows.append(jnp.concatenate(outs, axis=1))
    out = jnp.concatenate(out_rows, axis=0) if len(out_rows) > 1 else out_rows[0]
    return out, tuple(s)


def _dn_specs(n_of, rows):
    return [pl.BlockSpec((rows, 3 * DN_WIDTH), lambda i: (n_of(i), 0)),
            pl.BlockSpec((HALO, 3 * DN_WIDTH), lambda i: (jnp.maximum(n_of(i) * (rows // HALO) - 1, 0), 0)),
            pl.BlockSpec((rows, DN_WIDTH), lambda i: (n_of(i), 0)),
            pl.BlockSpec((rows, LANES), lambda i: (n_of(i), 0))]


def _dn_forward(xq, xz, xba, params, name):
    t = xq.shape[0]
    rows = min(DN_GROUP * DN_CHUNK, t)
    n_groups = t // rows

    def body(x_ref, halo_ref, z_ref, ba_ref, cw_ref, al_ref, dt_ref, gn_ref, o_ref, s_all_ref, s_ref):
        n = pl.program_id(0)

        @pl.when(n == 0)
        def _():
            s_ref[...] = jnp.zeros_like(s_ref)

        halo = jnp.where(n > 0, halo_ref[...], 0.0)
        xx = jnp.concatenate([halo, x_ref[...]], axis=0)
        s_all_ref[0] = s_ref[...]
        o, s_new = _dn_group(xx, z_ref[...], ba_ref[...], tuple(s_ref[h] for h in range(DN_HEADS)), cw_ref[...],
                             al_ref[...], dt_ref[...], gn_ref[...])
        o_ref[...] = o.astype(o_ref.dtype)
        for h in range(DN_HEADS):
            s_ref[h] = s_new[h]

    return pl.pallas_call(
        body, name=name, grid=(n_groups,),
        in_specs=_dn_specs(lambda i: i, rows) + [_full_spec(p) for p in params],
        out_specs=[pl.BlockSpec((rows, DN_WIDTH), lambda i: (i, 0)),
                   pl.BlockSpec((1, DN_HEADS, DN_DIM, DN_DIM), lambda i: (i, 0, 0, 0))],
        out_shape=[jax.ShapeDtypeStruct((t, DN_WIDTH), BF16),
                   jax.ShapeDtypeStruct((n_groups, DN_HEADS, DN_DIM, DN_DIM), F32)],
        scratch_shapes=[pltpu.VMEM((DN_HEADS, DN_DIM, DN_DIM), F32)],
        compiler_params=_cparams(("arbitrary",)),
    )(xq, xq, xz, xba, *params)


def _dn_backward(xq, xz, xba, params, s_all, d_out, name):
    t = xq.shape[0]
    rows = min(DN_GROUP * DN_CHUNK, t)
    n_groups = t // rows
    rev = lambda i: n_groups - 1 - i

    def body(x_ref, halo_ref, z_ref, ba_ref, cw_ref, al_ref, dt_ref, gn_ref, s_ref, do_ref,
             dx_ref, dz_ref, dba_ref, dcw_ref, dal_ref, ddt_ref, dgn_ref, ds_ref, dhalo_ref):
        i = pl.program_id(0)
        n = n_groups - 1 - i

        @pl.when(i == 0)
        def _():
            ds_ref[...] = jnp.zeros_like(ds_ref)
            dhalo_ref[...] = jnp.zeros_like(dhalo_ref)
            for r in (dcw_ref, dal_ref, ddt_ref, dgn_ref):
                r[...] = jnp.zeros_like(r)

        halo = jnp.where(n > 0, halo_ref[...], 0.0)
        xx = jnp.concatenate([halo, x_ref[...]], axis=0)
        _, pull = jax.vjp(_dn_group, xx, z_ref[...], ba_ref[...], tuple(s_ref[0, h] for h in range(DN_HEADS)),
                          cw_ref[...], al_ref[...], dt_ref[...], gn_ref[...])
        dxx, dz, dba, ds, dcw, dal, ddt, dgn = pull((do_ref[...].astype(F32), tuple(ds_ref[h] for h in range(DN_HEADS))))
        dx_ref[...] = jnp.concatenate([dxx[HALO:rows], dxx[rows:] + dhalo_ref[...]], axis=0).astype(dx_ref.dtype)
        dhalo_ref[...] = dxx[:HALO]
        dz_ref[...] = dz.astype(dz_ref.dtype)
        dba_ref[...] = dba.astype(dba_ref.dtype)
        for h in range(DN_HEADS):
            ds_ref[h] = ds[h]
        dcw_ref[...] += dcw
        dal_ref[...] += dal
        ddt_ref[...] += ddt
        dgn_ref[...] += dgn

    return pl.pallas_call(
        body, name=name, grid=(n_groups,),
        in_specs=_dn_specs(rev, rows) + [_full_spec(p) for p in params]
        + [pl.BlockSpec((1, DN_HEADS, DN_DIM, DN_DIM), lambda i: (rev(i), 0, 0, 0)),
           pl.BlockSpec((rows, DN_WIDTH), lambda i: (rev(i), 0))],
        out_specs=[pl.BlockSpec((rows, 3 * DN_WIDTH), lambda i: (rev(i), 0)),
                   pl.BlockSpec((rows, DN_WIDTH), lambda i: (rev(i), 0)),
                   pl.BlockSpec((rows, LANES), lambda i: (rev(i), 0))] + [_full_spec(p) for p in params],
        out_shape=[jax.ShapeDtypeStruct(xq.shape, BF16), jax.ShapeDtypeStruct(xz.shape, BF16),
                   jax.ShapeDtypeStruct(xba.shape, BF16)] + [jax.ShapeDtypeStruct(p.shape, F32) for p in params],
        scratch_shapes=[pltpu.VMEM((DN_HEADS, DN_DIM, DN_DIM), F32), pltpu.VMEM((HALO, 3 * DN_WIDTH), F32)],
        compiler_params=_cparams(("arbitrary",)),
    )(xq, xq, xz, xba, *params, s_all, d_out)


def _swa_forward(xs, window, dilation, name):
    t = xs.shape[0]
    d, l = dilation, t // dilation
    nb = l // SWA_BLOCK
    view = xs.reshape(l, d * 3 * SWA_WIDTH)
    blk = (SWA_BLOCK, SWA_WIDTH)

    def body(q_ref, kp_ref, kc_ref, vp_ref, vc_ref, o_ref, l_ref):
        blocks = [r[...].astype(F32) for r in (q_ref, kp_ref, kc_ref, vp_ref, vc_ref)]
        o, lse = _swa_block(*blocks, pl.program_id(1) == 0, window, dilation)
        o_ref[...] = o
        l_ref[...] = lse

    prev = lambda n: jnp.maximum(n - 1, 0)
    o, lse = pl.pallas_call(
        body, name=name, grid=(d, nb),
        in_specs=[pl.BlockSpec(blk, lambda r, n: (n, 3 * r)), pl.BlockSpec(blk, lambda r, n: (prev(n), 3 * r + 1)),
                  pl.BlockSpec(blk, lambda r, n: (n, 3 * r + 1)), pl.BlockSpec(blk, lambda r, n: (prev(n), 3 * r + 2)),
                  pl.BlockSpec(blk, lambda r, n: (n, 3 * r + 2))],
        out_specs=[pl.BlockSpec(blk, lambda r, n: (n, r)), pl.BlockSpec((SWA_BLOCK, LANES), lambda r, n: (n, r))],
        out_shape=[jax.ShapeDtypeStruct((l, d * SWA_WIDTH), F32), jax.ShapeDtypeStruct((l, d * LANES), F32)],
        compiler_params=_cparams(("parallel", "parallel")),
    )(view, view, view, view, view)
    return o.reshape(t, SWA_WIDTH), lse.reshape(t, LANES)


def _swa_backward(xs, d_o, d_lse, acc, window, dilation, name):
    t = xs.shape[0]
    d, l = dilation, t // dilation
    nb = l // SWA_BLOCK
    view = xs.reshape(l, d * 3 * SWA_WIDTH)
    blk = (SWA_BLOCK, SWA_WIDTH)
    has_acc = acc is not None

    def body(*refs):
        q_ref, kp_ref, kc_ref, vp_ref, vc_ref, do_ref, dl_ref = refs[:7]
        acc_refs = refs[7:10] if has_acc else None
        dq_ref, dk_ref, dv_ref, ck_ref, cv_ref = refs[-5:]
        i = pl.program_id(1)
        n = nb - 1 - i

        @pl.when(i == 0)
        def _():
            ck_ref[...] = jnp.zeros_like(ck_ref)
            cv_ref[...] = jnp.zeros_like(cv_ref)

        f = functools.partial(_swa_block, first=n == 0, window=window, dilation=dilation)
        _, pull = jax.vjp(f, *[r[...].astype(F32) for r in (q_ref, kp_ref, kc_ref, vp_ref, vc_ref)])
        dq, dkp, dkc, dvp, dvc = pull((do_ref[...], dl_ref[...]))
        dk = dkc + ck_ref[...]
        dv = dvc + cv_ref[...]
        if has_acc:
            dq, dk, dv = dq + acc_refs[0][...], dk + acc_refs[1][...], dv + acc_refs[2][...]
        dq_ref[...] = dq
        dk_ref[...] = dk
        dv_ref[...] = dv
        ck_ref[...] = dkp
        cv_ref[...] = dvp

    cur = lambda i: nb - 1 - i
    prev = lambda i: jnp.maximum(nb - 2 - i, 0)
    own = pl.BlockSpec(blk, lambda r, i: (cur(i), r))
    accs = [a.reshape(l, d * SWA_WIDTH) for a in acc] if has_acc else []
    outs = pl.pallas_call(
        body, name=name, grid=(d, nb),
        in_specs=[pl.BlockSpec(blk, lambda r, i: (cur(i), 3 * r)), pl.BlockSpec(blk, lambda r, i: (prev(i), 3 * r + 1)),
                  pl.BlockSpec(blk, lambda r, i: (cur(i), 3 * r + 1)), pl.BlockSpec(blk, lambda r, i: (prev(i), 3 * r + 2)),
                  pl.BlockSpec(blk, lambda r, i: (cur(i), 3 * r + 2)), own,
                  pl.BlockSpec((SWA_BLOCK, LANES), lambda r, i: (cur(i), r))] + [own] * len(accs),
        out_specs=[own] * 3, out_shape=[jax.ShapeDtypeStruct((l, d * SWA_WIDTH), F32)] * 3,
        scratch_shapes=[pltpu.VMEM(blk, F32), pltpu.VMEM(blk, F32)],
        compiler_params=_cparams(("parallel", "arbitrary")),
    )(view, view, view, view, view, d_o.reshape(l, d * SWA_WIDTH), d_lse.reshape(l, d * LANES), *accs)
    return tuple(o.reshape(t, SWA_WIDTH) for o in outs)


def _loss_head(h, target, gain, name, tile=256):
    t, d = h.shape
    tile = min(tile, t)

    def body(h_ref, t_ref, g_ref, loss_ref, dh_ref, dg_ref):
        def f(hv, gv):
            err = _rms(hv, gv) - t_ref[...]
            return 0.5 * jnp.sum(jnp.mean(err * err, axis=-1, keepdims=True), axis=0, keepdims=True)

        val, pull = jax.vjp(f, h_ref[...], g_ref[...])
        dh, dg = pull(jnp.ones((1, 1), F32))
        dh_ref[...] = dh

        @pl.when(pl.program_id(0) == 0)
        def _():
            loss_ref[...] = jnp.zeros_like(loss_ref)
            dg_ref[...] = jnp.zeros_like(dg_ref)

        loss_ref[...] += jnp.broadcast_to(val, loss_ref.shape)
        dg_ref[...] += dg

    return pl.pallas_call(
        body, name=name, grid=(t // tile,),
        in_specs=[pl.BlockSpec((tile, d), lambda i: (i, 0)), pl.BlockSpec((tile, d), lambda i: (i, 0)), _full_spec(gain)],
        out_specs=[pl.BlockSpec((1, LANES), lambda i: (0, 0)), pl.BlockSpec((tile, d), lambda i: (i, 0)), _full_spec(gain)],
        out_shape=[jax.ShapeDtypeStruct((1, LANES), F32), jax.ShapeDtypeStruct((t, d), F32),
                   jax.ShapeDtypeStruct(gain.shape, F32)],
        compiler_params=_cparams(("arbitrary",)),
    )(h, target, gain)


def _split_w_in(w_in):
    cuts = [0]
    for s in IN_SIZES:
        cuts.append(cuts[-1] + s)
    qkv, z = w_in[:, cuts[0]:cuts[1]], w_in[:, cuts[1]:cuts[2]]
    ba = jnp.pad(w_in[:, cuts[2]:cuts[4]], ((0, 0), (0, LANES - 2 * DN_HEADS)))
    return qkv, z, ba, w_in[:, cuts[4]:cuts[5]], w_in[:, cuts[5]:cuts[6]]


def _lane_pad(v, offset):
    return jnp.pad(v.reshape(1, -1), ((0, 0), (offset, LANES - offset - v.shape[0])))


def _layer_params(sm, i):
    return dict(
        ffn1_norm=sm["ffn1_norm"][i][None], mix_norm=sm["mix_norm"][i][None], xa_norm=sm["xa_norm"][i][None],
        xa_mem_norm=sm["xa_mem_norm"][i][None], ffn2_norm=sm["ffn2_norm"][i][None],
        dn=(sm["dn_conv_w"][i], _lane_pad(sm["dn_a_log"][i], DN_HEADS), _lane_pad(sm["dn_dt_bias"][i], DN_HEADS),
            sm["dn_out_norm"][i][None]),
        sg=(sm["sg_norm_gain"][i][None], sm["sg_norm_bias"][i][None], sm["sg_w_spatial"][i],
            jnp.pad(sm["sg_b_spatial"][i], ((0, 8 - SG_GROUPS), (0, 0)))),
    )


def _ffn_fwd(h, gain, w_gu, w_d, tag):
    n = _rows(_f_rms, [h], [gain], [(h.shape[1], BF16)], tile=512, name=f"{tag}_norm")[0]
    gu = _mm(n, w_gu, NN, out_dtype=BF16, name=f"{tag}_gate_up")
    a = _rows(_f_swiglu, [gu], [], [(gu.shape[1] // 2, BF16)], tile=256, name=f"{tag}_act")[0]
    out = _mm(a, w_d, NN, out_dtype=F32, res=h, scale=0.5, name=f"{tag}_down")
    return out, (h, n, gu, a)


def _ffn_bwd(dh, saved, gain, w_gu, w_d, tag):
    h, n, gu, a = saved
    da = _mm(dh, w_d, NT, out_dtype=F32, scale=0.5, name=f"{tag}_down_dx")
    dw_d = _mm(a, dh, TN, out_dtype=BF16, scale=0.5, name=f"{tag}_down_dw")
    dgu = _rows_vjp(_f_swiglu, [gu], [], [da], diff=[True], grad_dtypes=[BF16], tile=256, name=f"{tag}_act_bwd")[0]
    dn = _mm(dgu, w_gu, NT, out_dtype=F32, name=f"{tag}_gate_up_dx")
    dw_gu = _mm(n, dgu, TN, out_dtype=BF16, name=f"{tag}_gate_up_dw")
    dh_in, dgain = _rows_vjp(_f_rms, [h], [gain], [dn], diff=[True], grad_dtypes=[F32], tile=512, add=[dh],
                             name=f"{tag}_norm_bwd")
    return dh_in, dgain, dw_gu, dw_d


def _mixer_fwd(h, p, w_in, w_out, tag):
    d = h.shape[1]
    n = _rows(_f_rms, [h], [p["mix_norm"]], [(d, BF16)], tile=512, name=f"{tag}_norm")[0]
    w_parts = _split_w_in(w_in)
    xq, xz, xba, xs, xg = (_mm(n, w, NN, out_dtype=BF16 if j == 3 else F32, name=f"{tag}_in{j}") for j, w in enumerate(w_parts))
    oa, s_all = _dn_forward(xq, xz, xba, p["dn"], name=f"{tag}_dn")
    swa = [_swa_forward(xs, wnd, dil, name=f"{tag}_swa{j}") for j, (wnd, dil) in enumerate(SWA_PATTERNS)]
    ob = _rows(_f_swa_mix, [o for o, _ in swa] + [l for _, l in swa], [], [(SWA_WIDTH, BF16)], tile=512,
               name=f"{tag}_swa_mix")[0]
    oc = _rows(_f_gmlp, [xg], list(p["sg"]), [(SG_WIDTH, BF16)], tile=256, name=f"{tag}_gmlp")[0]
    merged = jnp.concatenate([oa, ob, oc], axis=1)
    out = _mm(merged, w_out, NN, out_dtype=F32, res=h, name=f"{tag}_out")
    return out, (h, n, xq, xz, xba, xs, xg, s_all, swa, merged)


def _mixer_bwd(dh, saved, p, w_in, w_out, tag):
    h, n, xq, xz, xba, xs, xg, s_all, swa, merged = saved
    dw_out = _mm(merged, dh, TN, out_dtype=BF16, name=f"{tag}_out_dw")
    doa = _mm(dh, w_out[:DN_WIDTH], NT, out_dtype=F32, name=f"{tag}_out_dxa")
    dob = _mm(dh, w_out[DN_WIDTH:DN_WIDTH + SWA_WIDTH], NT, out_dtype=F32, name=f"{tag}_out_dxb")
    doc = _mm(dh, w_out[DN_WIDTH + SWA_WIDTH:], NT, out_dtype=F32, name=f"{tag}_out_dxc")
    res = _rows_vjp(_f_gmlp, [xg], list(p["sg"]), [doc], diff=[True], grad_dtypes=[BF16], tile=256, name=f"{tag}_gmlp_bwd")
    dxg, d_sg = res[0], res[1:]
    mix_in = [o for o, _ in swa] + [l for _, l in swa]
    d_mix = _rows_vjp(_f_swa_mix, mix_in, [], [dob], diff=[True] * 6, grad_dtypes=[F32] * 6, tile=512,
                      name=f"{tag}_swa_mix_bwd")
    acc = None
    for j, (wnd, dil) in enumerate(SWA_PATTERNS):
        acc = _swa_backward(xs, d_mix[j], d_mix[3 + j], acc, wnd, dil, name=f"{tag}_swa{j}_bwd")
    dxs = jnp.concatenate([a.astype(BF16) for a in acc], axis=1)
    res = _dn_backward(xq, xz, xba, p["dn"], s_all, doa, name=f"{tag}_dn_bwd")
    (dxq, dxz, dxba), d_dn = res[:3], res[3:]
    w_parts = _split_w_in(w_in)
    dn = None
    dws = []
    for j, (dx, w) in enumerate(zip((dxq, dxz, dxba, dxs, dxg), w_parts, strict=True)):
        dn = _mm(dx, w, NT, out_dtype=F32, res=dn, name=f"{tag}_in{j}_dx")
        dws.append(_mm(n, dx, TN, out_dtype=BF16, name=f"{tag}_in{j}_dw"))
    dws[2] = dws[2][:, :2 * DN_HEADS]
    dw_in = jnp.concatenate(dws, axis=1)
    dh_in, dgain = _rows_vjp(_f_rms, [h], [p["mix_norm"]], [dn], diff=[True], grad_dtypes=[F32], tile=512, add=[dh],
                             name=f"{tag}_norm_bwd")
    return dh_in, dgain, dw_in, dw_out, d_dn, d_sg


def _xattn_fwd(h, mem, p, w_q, w_kv, w_o, tag):
    d = h.shape[1]
    n = _rows(_f_rms, [h], [p["xa_norm"]], [(d, BF16)], tile=512, name=f"{tag}_norm")[0]
    mn = _rows(_f_rms, [mem], [p["xa_mem_norm"]], [(d, BF16)], tile=512, name=f"{tag}_mem_norm")[0]
    q = _mm(n, w_q, NN, out_dtype=BF16, name=f"{tag}_q")
    kv = _mm(mn, w_kv, NN, out_dtype=BF16, name=f"{tag}_kv")
    o = _rows(_f_xattn, [q], [kv], [(d, BF16)], tile=256, name=f"{tag}_core")[0]
    out = _mm(o, w_o, NN, out_dtype=F32, res=h, name=f"{tag}_o")
    return out, (h, n, mn, q, kv, o)


def _xattn_bwd(dh, saved, mem, p, w_q, w_kv, w_o, tag):
    h, n, mn, q, kv, o = saved
    do = _mm(dh, w_o, NT, out_dtype=BF16, name=f"{tag}_o_dx")
    dw_o = _mm(o, dh, TN, out_dtype=BF16, name=f"{tag}_o_dw")
    dq, dkv = _rows_vjp(_f_xattn, [q], [kv], [do], diff=[True], grad_dtypes=[BF16], tile=256, name=f"{tag}_core_bwd")
    dn = _mm(dq, w_q, NT, out_dtype=F32, name=f"{tag}_q_dx")
    dw_q = _mm(n, dq, TN, out_dtype=BF16, name=f"{tag}_q_dw")
    dmn = _mm(dkv, w_kv, NT, out_dtype=F32, name=f"{tag}_kv_dx")
    dw_kv = _mm(mn, dkv, TN, out_dtype=BF16, name=f"{tag}_kv_dw")
    dmem_gain = _rows_vjp(_f_rms, [mem], [p["xa_mem_norm"]], [dmn], diff=[False], grad_dtypes=[], tile=512,
                          name=f"{tag}_mem_norm_bwd")[0]
    dh_in, dgain = _rows_vjp(_f_rms, [h], [p["xa_norm"]], [dn], diff=[True], grad_dtypes=[F32], tile=512, add=[dh],
                             name=f"{tag}_norm_bwd")
    return dh_in, dgain, dmem_gain, dw_q, dw_kv, dw_o


def kernel(x, mem, ffn1_norm, ffn1_w_gate_up, ffn1_w_down, mix_norm, mix_w_in, dn_conv_w, dn_a_log, dn_dt_bias, dn_out_norm, sg_norm_gain, sg_norm_bias, sg_w_spatial, sg_b_spatial, mix_w_out, xa_norm, xa_mem_norm, xa_w_q, xa_w_kv, xa_w_o, ffn2_norm, ffn2_w_gate_up, ffn2_w_down, final_norm, loss_target, m_ffn1_norm, m_ffn1_w_gate_up, m_ffn1_w_down, m_mix_norm, m_mix_w_in, m_dn_conv_w, m_dn_a_log, m_dn_dt_bias, m_dn_out_norm, m_sg_norm_gain, m_sg_norm_bias, m_sg_w_spatial, m_sg_b_spatial, m_mix_w_out, m_xa_norm, m_xa_mem_norm, m_xa_w_q, m_xa_w_kv, m_xa_w_o, m_ffn2_norm, m_ffn2_w_gate_up, m_ffn2_w_down, m_final_norm, v_ffn1_norm, v_ffn1_w_gate_up, v_ffn1_w_down, v_mix_norm, v_mix_w_in, v_dn_conv_w, v_dn_a_log, v_dn_dt_bias, v_dn_out_norm, v_sg_norm_gain, v_sg_norm_bias, v_sg_w_spatial, v_sg_b_spatial, v_mix_w_out, v_xa_norm, v_xa_mem_norm, v_xa_w_q, v_xa_w_kv, v_xa_w_o, v_ffn2_norm, v_ffn2_w_gate_up, v_ffn2_w_down, v_final_norm):
    args = dict(locals())
    wts = {k: args[k] for k in WEIGHTS}
    mom_m = {k: args["m_" + k] for k in WEIGHTS}
    mom_v = {k: args["v_" + k] for k in WEIGHTS}
    depth = ffn1_norm.shape[0]
    h = x[0]
    mem2 = mem[0]
    target = loss_target[0]

    assert depth == 2, "core c of a chip is responsible for layer c in the weight and gradient exchanges"
    kinds = {k: "stack" if k == "mix_w_in" else ("row" if BIG_AXIS[k] == 1 else "col") for k in BIG}
    whole = _gather_weights([wts[k].astype(BF16) for k in BIG] + [dn_conv_w], [kinds[k] for k in BIG] + ["stack"],
                            name="gather_weights")
    full = dict(zip(BIG, whole[:-1], strict=True))
    full["mix_w_in"] = jnp.concatenate([full["mix_w_in"][:, j] for j in range(N_CHIPS)], axis=2)
    small = {k: wts[k] for k in SMALL}
    small["dn_conv_w"] = jnp.concatenate([whole[-1][:, j] for j in range(N_CHIPS)], axis=2)

    saved = []
    for i in range(depth):
        p = _layer_params(small, i)
        h, s1 = _ffn_fwd(h, p["ffn1_norm"], full["ffn1_w_gate_up"][i], full["ffn1_w_down"][i], f"l{i}_ffn1")
        h, s2 = _mixer_fwd(h, p, full["mix_w_in"][i], full["mix_w_out"][i], f"l{i}_mix")
        h, s3 = _xattn_fwd(h, mem2, p, full["xa_w_q"][i], full["xa_w_kv"][i], full["xa_w_o"][i], f"l{i}_xa")
        h, s4 = _ffn_fwd(h, p["ffn2_norm"], full["ffn2_w_gate_up"][i], full["ffn2_w_down"][i], f"l{i}_ffn2")
        saved.append((p, s1, s2, s3, s4))
    loss_part, dh, d_final = _loss_head(h, target, final_norm[None], name="loss_head")
    loss = lax.psum(loss_part[0, 0], ("x", "y", "c"))

    g_big = {k: [None] * depth for k in BIG}
    g_small = {k: [None] * depth for k in SMALL if k != "final_norm"}
    g_small["dn_conv_w"] = [None] * depth
    for i in reversed(range(depth)):
        p, s1, s2, s3, s4 = saved[i]
        dh, dg, dw_gu, dw_d = _ffn_bwd(dh, s4, p["ffn2_norm"], full["ffn2_w_gate_up"][i], full["ffn2_w_down"][i], f"l{i}_ffn2")
        g_small["ffn2_norm"][i], g_big["ffn2_w_gate_up"][i], g_big["ffn2_w_down"][i] = dg[0], dw_gu, dw_d
        dh, dg, dmg, dw_q, dw_kv, dw_o = _xattn_bwd(dh, s3, mem2, p, full["xa_w_q"][i], full["xa_w_kv"][i],
                                                    full["xa_w_o"][i], f"l{i}_xa")
        g_small["xa_norm"][i], g_small["xa_mem_norm"][i] = dg[0], dmg[0]
        g_big["xa_w_q"][i], g_big["xa_w_kv"][i], g_big["xa_w_o"][i] = dw_q, dw_kv, dw_o
        dh, dg, dw_in, dw_out, d_dn, d_sg = _mixer_bwd(dh, s2, p, full["mix_w_in"][i], full["mix_w_out"][i], f"l{i}_mix")
        g_small["mix_norm"][i], g_big["mix_w_in"][i], g_big["mix_w_out"][i] = dg[0], dw_in, dw_out
        g_small["dn_conv_w"][i] = d_dn[0]
        g_small["dn_a_log"][i] = d_dn[1][0, DN_HEADS:2 * DN_HEADS]
        g_small["dn_dt_bias"][i] = d_dn[2][0, DN_HEADS:2 * DN_HEADS]
        g_small["dn_out_norm"][i] = d_dn[3][0]
        g_small["sg_norm_gain"][i], g_small["sg_norm_bias"][i] = d_sg[0][0], d_sg[1][0]
        g_small["sg_w_spatial"][i], g_small["sg_b_spatial"][i] = d_sg[2], d_sg[3][:SG_GROUPS]
        dh, dg, dw_gu, dw_d = _ffn_bwd(dh, s1, p["ffn1_norm"], full["ffn1_w_gate_up"][i], full["ffn1_w_down"][i], f"l{i}_ffn1")
        g_small["ffn1_norm"][i], g_big["ffn1_w_gate_up"][i], g_big["ffn1_w_down"][i] = dg[0], dw_gu, dw_d
    grad_x = dh[None]
    g_small = {k: jnp.stack(v) for k, v in g_small.items()}
    g_small["final_norm"] = d_final[0]

    def layer_grad(k, i):
        g = g_big[k][i]
        if kinds[k] == "stack":
            n = wts[k].shape[2]
            g = jnp.stack([g[:, j * n:(j + 1) * n] for j in range(N_CHIPS)])
        return g

    own, got = _swap_layers([layer_grad(k, 0) for k in BIG], [layer_grad(k, 1) for k in BIG], name="swap_layers")
    chip_sum = [_sum_list([o, g], BF16, name=f"sum_cores_{k}") for k, o, g in zip(BIG, own, got, strict=True)]
    recv = _scatter_quarters(chip_sum, [wts[k].shape[1:] for k in BIG], [kinds[k] for k in BIG], name="scatter_grads")
    parts = [_sum_slots(r, F32, name=f"sum_chips_{k}") for k, r in zip(BIG, recv, strict=True)]
    g_fin = dict(zip(BIG, _pair_layers(parts, name="pair_layers"), strict=True))
    small_names = list(SMALL) + ["dn_conv_w"]
    small_shapes = [g_small[k].shape for k in small_names]
    small_sum = _sum_slots(_gather_all(_pack([g_small[k] for k in small_names], F32, 64), name="gather_small"), F32,
                           name="sum_small")
    gs = dict(zip(small_names, _unpack(small_sum, small_shapes), strict=True))
    n_conv = dn_conv_w.shape[2]
    gs["dn_conv_w"] = lax.dynamic_slice_in_dim(gs["dn_conv_w"], (2 * lax.axis_index("x") + lax.axis_index("y")) * n_conv,
                                               n_conv, axis=2)

    results = {}
    for k in BIG:
        shp = wts[k].shape
        two_d = lambda a, _s=shp: a.reshape(-1, _s[-1])
        res = _adamw(two_d(wts[k]), two_d(g_fin[k]), two_d(mom_m[k]), two_d(mom_v[k]), name=f"adamw_{k}")
        results[k] = [g_fin[k]] + [r.reshape(shp) for r in res]
    sm_shapes = [wts[k].shape for k in small_names]
    pk = lambda d: _pack([d[k] for k in small_names], F32, 64)
    res = [_unpack(r, sm_shapes) for r in _adamw(pk(wts), pk(gs), pk(mom_m), pk(mom_v), name="adamw_small")]
    for i, k in enumerate(small_names):
        results[k] = [gs[k]] + [res[j][i] for j in range(3)]

    out = [loss, grad_x]
    for j in range(4):
        out += [results[k][j] for k in WEIGHTS]
    return tuple(out)
```

```python
import functools
import math

import jax
import jax.numpy as jnp
from jax import lax
from jax.experimental import pallas as pl
from jax.experimental.pallas import tpu as pltpu

F32, BF16 = jnp.float32, jnp.bfloat16
HI = lax.Precision.HIGHEST
NN, NT, TN = ((1,), (0,)), ((1,), (1,)), ((0,), (0,))

NORM_EPS = 1e-6
LANES = 128
V7X_VMEM_BYTES = 64 * 2**20
VMEM_LIMIT = V7X_VMEM_BYTES * 3 // 4

DN_HEADS, DN_DIM, DN_CHUNK, DN_CONV, HALO = 4, 128, 64, 4, 8
DN_GROUP = 4
INV_BLOCK = 16
DN_WIDTH = DN_HEADS * DN_DIM
SWA_HEADS, SWA_DIM, SWA_BLOCK = 4, 64, 128
SWA_WIDTH = SWA_HEADS * SWA_DIM
SWA_PATTERNS = ((128, 1), (512, 4), (2048, 16))
SG_GROUPS, SG_DIM, SG_CHUNK = 4, 64, 128
SG_WIDTH = SG_GROUPS * SG_DIM
XA_HEADS = 4
IN_SIZES = (3 * DN_WIDTH, DN_WIDTH, DN_HEADS, DN_HEADS, 3 * SWA_WIDTH, 2 * SG_WIDTH)
ADAM_LR, ADAM_B1, ADAM_B2, ADAM_EPS, ADAM_WD, ADAM_STEP = 0.001, 0.9, 0.999, 1e-08, 0.01, 10
N_CHIPS, N_DEV = 4, 8
MESH_ID = pl.DeviceIdType.MESH

BIG = ("ffn1_w_gate_up", "ffn1_w_down", "mix_w_in", "mix_w_out", "xa_w_q", "xa_w_kv", "xa_w_o",
       "ffn2_w_gate_up", "ffn2_w_down")
BIG_AXIS = {"ffn1_w_gate_up": 2, "ffn1_w_down": 1, "mix_w_in": 2, "mix_w_out": 1, "xa_w_q": 1, "xa_w_kv": 2,
            "xa_w_o": 1, "ffn2_w_gate_up": 2, "ffn2_w_down": 1}
SMALL = ("ffn1_norm", "mix_norm", "dn_a_log", "dn_dt_bias", "dn_out_norm", "sg_norm_gain", "sg_norm_bias",
         "sg_w_spatial", "sg_b_spatial", "xa_norm", "xa_mem_norm", "ffn2_norm", "final_norm")
WEIGHTS = ("ffn1_norm", "ffn1_w_gate_up", "ffn1_w_down", "mix_norm", "mix_w_in", "dn_conv_w", "dn_a_log",
           "dn_dt_bias", "dn_out_norm", "sg_norm_gain", "sg_norm_bias", "sg_w_spatial", "sg_b_spatial",
           "mix_w_out", "xa_norm", "xa_mem_norm", "xa_w_q", "xa_w_kv", "xa_w_o", "ffn2_norm", "ffn2_w_gate_up",
           "ffn2_w_down", "final_norm")


@functools.partial(jax.custom_vjp, nondiff_argnums=(2,))
def _dlo(a, b, dims):
    return lax.dot_general(a.astype(BF16), b.astype(BF16), (dims, ((), ())), preferred_element_type=F32)


def _dlo_fwd(a, b, dims):
    return _dlo(a, b, dims), (a, b)


def _dlo_bwd(dims, saved, g):
    a, b = saved
    if dims == NN:
        da, db = _dlo(g, b, NT), _dlo(a, g, TN)
    elif dims == NT:
        da, db = _dlo(g, b, NN), _dlo(g, a, TN)
    else:
        da, db = _dlo(b, g, NT), _dlo(a, g, NN)
    return da.astype(a.dtype), db.astype(b.dtype)


_dlo.defvjp(_dlo_fwd, _dlo_bwd)


def _dhi(a, b, dims):
    return lax.dot_general(a, b, (dims, ((), ())), preferred_element_type=F32, precision=HI)


def _sigmoid(x):
    return 1.0 / (1.0 + jnp.exp(-x))


def _silu(x):
    return x * _sigmoid(x)


def _softplus(x):
    return jnp.maximum(x, 0.0) + jnp.log(1.0 + jnp.exp(-jnp.abs(x)))


def _rms(x, gain):
    x = x.astype(F32)
    return x * lax.rsqrt(jnp.mean(x * x, axis=-1, keepdims=True) + NORM_EPS) * gain


def _tile(n, target, unit=LANES):
    best = None
    for t in range(unit, min(n, target) + 1, unit):
        if n % t == 0:
            best = t
    return best if best is not None else n


def _cparams(sem):
    return pltpu.CompilerParams(dimension_semantics=sem, vmem_limit_bytes=VMEM_LIMIT)


def _mm(a, b, dims, *, out_dtype, name, res=None, scale=1.0, tm=512, tn=1408, tk=2816):
    if dims == NN:
        (m, k), n = a.shape, b.shape[1]
    elif dims == NT:
        (m, k), n = a.shape, b.shape[0]
    else:
        (k, m), n = a.shape, b.shape[1]
    tm, tn, tk = _tile(m, tm, LANES if dims == TN else 8), _tile(n, tn), _tile(k, tk)
    nk = k // tk
    a_spec = pl.BlockSpec((tk, tm), lambda i, j, kk: (kk, i)) if dims == TN else pl.BlockSpec((tm, tk), lambda i, j, kk: (i, kk))
    b_spec = pl.BlockSpec((tn, tk), lambda i, j, kk: (j, kk)) if dims == NT else pl.BlockSpec((tk, tn), lambda i, j, kk: (kk, j))
    o_spec = pl.BlockSpec((tm, tn), lambda i, j, kk: (i, j))
    has_res = res is not None

    def finish(acc, r_ref, o_ref):
        val = acc * scale if scale != 1.0 else acc
        if has_res:
            val = r_ref[...].astype(F32) + val
        o_ref[...] = val.astype(o_ref.dtype)

    def body(*refs):
        a_ref, b_ref = refs[0], refs[1]
        r_ref = refs[2] if has_res else None
        part = lax.dot_general(a_ref[...].astype(BF16), b_ref[...].astype(BF16), (dims, ((), ())),
                               preferred_element_type=F32)
        if nk == 1:
            finish(part, r_ref, refs[-1])
            return
        o_ref, acc_ref = refs[-2], refs[-1]
        kk = pl.program_id(2)

        @pl.when(kk == 0)
        def _():
            acc_ref[...] = part

        @pl.when(jnp.logical_and(kk > 0, kk < nk - 1))
        def _():
            acc_ref[...] += part

        @pl.when(kk == nk - 1)
        def _():
            finish(acc_ref[...] + part, r_ref, o_ref)

    return pl.pallas_call(
        body, name=name, grid=(m // tm, n // tn, nk),
        in_specs=[a_spec, b_spec] + ([o_spec] if has_res else []), out_specs=o_spec,
        out_shape=jax.ShapeDtypeStruct((m, n), out_dtype),
        scratch_shapes=[pltpu.VMEM((tm, tn), F32)] if nk > 1 else [],
        compiler_params=_cparams(("parallel", "parallel", "arbitrary")),
    )(*([a, b] + ([res] if has_res else [])))


def _full_spec(p):
    nd = p.ndim
    return pl.BlockSpec(p.shape, lambda i, _nd=nd: (0,) * _nd)


def _rows(f, rows, params, outs, *, tile, name):
    t = rows[0].shape[0]
    tile = min(tile, t)
    nr, npar = len(rows), len(params)

    def body(*refs):
        vals = f(*[r[...] for r in refs[:nr + npar]])
        for o_ref, v in zip(refs[nr + npar:], vals, strict=True):
            o_ref[...] = v.astype(o_ref.dtype)

    res = pl.pallas_call(
        body, name=name, grid=(t // tile,),
        in_specs=[pl.BlockSpec((tile, r.shape[1]), lambda i: (i, 0)) for r in rows] + [_full_spec(p) for p in params],
        out_specs=[pl.BlockSpec((tile, w), lambda i: (i, 0)) for w, _ in outs],
        out_shape=[jax.ShapeDtypeStruct((t, w), d) for w, d in outs],
        compiler_params=_cparams(("parallel",)),
    )(*rows, *params)
    return tuple(res)


def _rows_vjp(f, rows, params, cts, *, diff, grad_dtypes, tile, name, add=None):
    t = rows[0].shape[0]
    tile = min(tile, t)
    nr, npar, nct = len(rows), len(params), len(cts)
    didx = [i for i, d in enumerate(diff) if d]
    add = [None] * len(didx) if add is None else add
    adds = [a for a in add if a is not None]

    def body(*refs):
        row_refs, par_refs = refs[:nr], refs[nr:nr + npar]
        ct_refs = refs[nr + npar:nr + npar + nct]
        add_refs = list(refs[nr + npar + nct:nr + npar + nct + len(adds)])
        out_refs = refs[nr + npar + nct + len(adds):]
        rv = [r[...] for r in row_refs]
        pv = [p[...].astype(F32) for p in par_refs]

        def g(*args):
            full = list(rv)
            for k, i in enumerate(didx):
                full[i] = args[k]
            return f(*full, *args[len(didx):])

        outs, pull = jax.vjp(g, *[rv[i] for i in didx], *pv)
        grads = pull(tuple(c[...].astype(o.dtype) for c, o in zip(ct_refs, outs, strict=True)))
        for k in range(len(didx)):
            val = grads[k].astype(F32)
            if add[k] is not None:
                val = val + add_refs.pop(0)[...].astype(F32)
            out_refs[k][...] = val.astype(out_refs[k].dtype)

        @pl.when(pl.program_id(0) == 0)
        def _():
            for o_ref in out_refs[len(didx):]:
                o_ref[...] = jnp.zeros_like(o_ref)

        for o_ref, gp in zip(out_refs[len(didx):], grads[len(didx):], strict=True):
            o_ref[...] += gp.astype(F32)

    row_spec = lambda a: pl.BlockSpec((tile, a.shape[1]), lambda i: (i, 0))
    res = pl.pallas_call(
        body, name=name, grid=(t // tile,),
        in_specs=[row_spec(r) for r in rows] + [_full_spec(p) for p in params] + [row_spec(c) for c in cts]
        + [row_spec(a) for a in adds],
        out_specs=[row_spec(rows[i]) for i in didx] + [_full_spec(p) for p in params],
        out_shape=[jax.ShapeDtypeStruct(rows[i].shape, d) for i, d in zip(didx, grad_dtypes, strict=True)]
        + [jax.ShapeDtypeStruct(p.shape, F32) for p in params],
        compiler_params=_cparams(("arbitrary",)),
    )(*rows, *params, *cts, *adds)
    return tuple(res)


def _sum_picked(a0, a1, pick, other, out_dtype, name):
    shape = a0.shape
    views = [a.reshape(-1, shape[-1]) for a in (a0, a1, other)]
    r, c = views[0].shape
    tile = _tile(r, max(16, (1 << 18) // c), 16)

    def body(a0_ref, a1_ref, other_ref, pick_ref, o_ref):
        mine = jnp.where(pick_ref[...] == 0, a0_ref[...].astype(F32), a1_ref[...].astype(F32))
        o_ref[...] = (mine + other_ref[...].astype(F32)).astype(o_ref.dtype)

    spec = pl.BlockSpec((tile, c), lambda i: (i, 0))
    return pl.pallas_call(
        body, name=name, grid=(r // tile,), in_specs=[spec] * 3 + [_full_spec(pick)], out_specs=spec,
        out_shape=jax.ShapeDtypeStruct((r, c), out_dtype), compiler_params=_cparams(("parallel",)),
    )(*views, pick).reshape(shape)


def _sum_slots(x, out_dtype, name, first=None):
    n, r, c = x.shape
    tile = _tile(r, max(16, (1 << 18) // c), 16)

    def body(*refs):
        acc = refs[0][...].astype(F32)
        for ref in refs[1:-1]:
            acc = acc + ref[...].astype(F32)
        refs[-1][...] = acc.astype(refs[-1].dtype)

    spec = pl.BlockSpec((tile, c), lambda i: (i, 0))
    return pl.pallas_call(
        body, name=name, grid=(r // tile,),
        in_specs=([spec] if first is not None else [])
        + [pl.BlockSpec((None, tile, c), lambda i, _s=s_: (_s, i, 0)) for s_ in range(n)],
        out_specs=spec, out_shape=jax.ShapeDtypeStruct((r, c), out_dtype), compiler_params=_cparams(("parallel",)),
    )(*(([first] if first is not None else []) + [x] * n))


def _adamw(w, g, m, v, name):
    r, c = w.shape
    tile = _tile(r, max(8, (1 << 18) // c), 8)

    def body(w_ref, g_ref, m_ref, v_ref, d_out, m_out, v_out):
        g = g_ref[...]
        mn = ADAM_B1 * m_ref[...] + (1.0 - ADAM_B1) * g
        vn = ADAM_B2 * v_ref[...] + (1.0 - ADAM_B2) * (g * g)
        m_hat = mn / (1.0 - ADAM_B1 ** ADAM_STEP)
        v_hat = vn / (1.0 - ADAM_B2 ** ADAM_STEP)
        d_out[...] = -ADAM_LR * (m_hat / (jnp.sqrt(v_hat) + ADAM_EPS) + ADAM_WD * w_ref[...])
        m_out[...] = mn
        v_out[...] = vn

    spec = pl.BlockSpec((tile, c), lambda i: (i, 0))
    return pl.pallas_call(
        body, name=name, grid=(r // tile,), in_specs=[spec] * 4, out_specs=[spec] * 3,
        out_shape=[jax.ShapeDtypeStruct((r, c), F32)] * 3, compiler_params=_cparams(("parallel",)),
    )(w, g, m, v)


def _place():
    return lax.axis_index("x"), lax.axis_index("y"), lax.axis_index("c")


def _flip(v, bit):
    return 1 - v if bit else v


_ANY = pl.BlockSpec(memory_space=pl.ANY)


def _quarter(ref, j, shape, kind):
    if kind == "row":
        return ref.at[pl.ds(j * shape[0], shape[0])]
    if kind == "col":
        return ref.at[:, pl.ds(j * shape[1], shape[1])]
    return ref.at[j]


def _whole_shape(shape, kind):
    if kind == "row":
        return (N_CHIPS * shape[0],) + tuple(shape[1:])
    if kind == "col":
        return (shape[0], N_CHIPS * shape[1]) + tuple(shape[2:])
    return (N_CHIPS,) + tuple(shape)


def _dma_sems(*counts):
    return [pltpu.SemaphoreType.DMA((n,)) for n in counts]


def _gather_weights(shards, kinds, name):
    n = len(shards)
    shapes = [s.shape[1:] for s in shards]

    def body(*refs):
        ins, outs = refs[:n], refs[n:2 * n]
        send_sems, recv_sems, pair_send, pair_recv, own_send, own_recv = refs[2 * n:]
        mx, my, mc = _place()
        me = 2 * mx + my
        started = []
        for t in range(n):
            started.append(pltpu.make_async_remote_copy(
                src_ref=ins[t].at[1 - mc], dst_ref=_quarter(outs[t].at[1 - mc], me, shapes[t], kinds[t]),
                send_sem=own_send.at[t], recv_sem=own_recv.at[t], device_id=(mx, my, 1 - mc), device_id_type=MESH_ID))
            started[-1].start()
            for k in range(1, N_CHIPS):
                started.append(pltpu.make_async_remote_copy(
                    src_ref=ins[t].at[mc], dst_ref=_quarter(outs[t].at[mc], me, shapes[t], kinds[t]),
                    send_sem=send_sems.at[3 * t + k - 1], recv_sem=recv_sems.at[3 * t + k - 1],
                    device_id=(_flip(mx, k >> 1), _flip(my, k & 1), mc), device_id_type=MESH_ID))
                started[-1].start()
        for t in range(n):
            pltpu.make_async_remote_copy(
                src_ref=ins[t].at[mc], dst_ref=_quarter(outs[t].at[mc], me, shapes[t], kinds[t]),
                send_sem=own_send.at[t], recv_sem=own_recv.at[t], device_id=(mx, my, 1 - mc),
                device_id_type=MESH_ID).wait_recv()
            for k in range(1, N_CHIPS):
                px, py = _flip(mx, k >> 1), _flip(my, k & 1)
                pltpu.make_async_remote_copy(
                    src_ref=ins[t].at[mc], dst_ref=_quarter(outs[t].at[mc], 2 * px + py, shapes[t], kinds[t]),
                    send_sem=send_sems.at[3 * t + k - 1], recv_sem=recv_sems.at[3 * t + k - 1],
                    device_id=(px, py, mc), device_id_type=MESH_ID).wait_recv()
        for t in range(n):
            started.append(pltpu.make_async_remote_copy(
                src_ref=outs[t].at[mc], dst_ref=outs[t].at[mc], send_sem=pair_send.at[t], recv_sem=pair_recv.at[t],
                device_id=(mx, my, 1 - mc), device_id_type=MESH_ID))
            started[-1].start()
        for t in range(n):
            pltpu.make_async_remote_copy(
                src_ref=outs[t].at[1 - mc], dst_ref=outs[t].at[1 - mc], send_sem=pair_send.at[t], recv_sem=pair_recv.at[t],
                device_id=(mx, my, 1 - mc), device_id_type=MESH_ID).wait_recv()
        for cp in started:
            cp.wait_send()

    return pl.pallas_call(
        body, name=name, in_specs=[_ANY] * n, out_specs=[_ANY] * n,
        out_shape=[jax.ShapeDtypeStruct((2,) + _whole_shape(sh, kd), s.dtype) for s, sh, kd in zip(shards, shapes, kinds)],
        scratch_shapes=_dma_sems(3 * n, 3 * n, n, n, n, n),
    )(*shards)


def _give_other_layer(g0, g1, name):
    n = len(g0)

    def body(*refs):
        a, b, got = refs[:n], refs[n:2 * n], refs[2 * n:3 * n]
        send_sems, recv_sems = refs[3 * n:]
        mx, my, mc = _place()
        peer = (mx, my, 1 - mc)
        for give, core in ((b, 0), (a, 1)):
            @pl.when(mc == core)
            def _():
                for t in range(n):
                    pltpu.make_async_remote_copy(src_ref=give[t], dst_ref=got[t], send_sem=send_sems.at[t],
                                                 recv_sem=recv_sems.at[t], device_id=peer, device_id_type=MESH_ID).start()
        for t in range(n):
            cp = pltpu.make_async_remote_copy(src_ref=a[t], dst_ref=got[t], send_sem=send_sems.at[t],
                                              recv_sem=recv_sems.at[t], device_id=peer, device_id_type=MESH_ID)
            cp.wait_recv()
            cp.wait_send()

    return pl.pallas_call(
        body, name=name, in_specs=[_ANY] * (2 * n), out_specs=[_ANY] * n,
        out_shape=[jax.ShapeDtypeStruct(g.shape, g.dtype) for g in g0], scratch_shapes=_dma_sems(n, n),
    )(*g0, *g1)


def _scatter_quarters(gs, shapes, kinds, name):
    n = len(gs)

    def body(*refs):
        ins, outs = refs[:n], refs[n:2 * n]
        send_sems, recv_sems = refs[2 * n:]
        mx, my, mc = _place()
        sends = []
        for t in range(n):
            for k in range(1, N_CHIPS):
                px, py = _flip(mx, k >> 1), _flip(my, k & 1)
                sends.append(pltpu.make_async_remote_copy(
                    src_ref=_quarter(ins[t], 2 * px + py, shapes[t], kinds[t]), dst_ref=outs[t].at[k - 1],
                    send_sem=send_sems.at[3 * t + k - 1], recv_sem=recv_sems.at[3 * t + k - 1],
                    device_id=(px, py, mc), device_id_type=MESH_ID))
                sends[-1].start()
        for cp in sends:
            cp.wait_recv()
        for cp in sends:
            cp.wait_send()

    return pl.pallas_call(
        body, name=name, in_specs=[_ANY] * n, out_specs=[_ANY] * n,
        out_shape=[jax.ShapeDtypeStruct((N_CHIPS - 1,) + tuple(sh), g.dtype) for g, sh in zip(gs, shapes)],
        scratch_shapes=_dma_sems(3 * n, 3 * n),
    )(*gs)


def _swap_cores(parts, name):
    n = len(parts)

    def body(*refs):
        ins, outs = refs[:n], refs[n:2 * n]
        send_sems, recv_sems = refs[2 * n:]
        mx, my, mc = _place()
        copies = [pltpu.make_async_remote_copy(src_ref=ins[t], dst_ref=outs[t], send_sem=send_sems.at[t],
                                               recv_sem=recv_sems.at[t], device_id=(mx, my, 1 - mc),
                                               device_id_type=MESH_ID) for t in range(n)]
        for cp in copies:
            cp.start()
        for cp in copies:
            cp.wait_recv()
        for cp in copies:
            cp.wait_send()

    return pl.pallas_call(
        body, name=name, in_specs=[_ANY] * n, out_specs=[_ANY] * n,
        out_shape=[jax.ShapeDtypeStruct(p.shape, p.dtype) for p in parts], scratch_shapes=_dma_sems(n, n),
    )(*parts)


def _gather_all(x, name):
    r, w = x.shape

    def body(x_ref, o_ref, send_sems, recv_sems, local_sem):
        mx, my, mc = _place()
        mine = 4 * mx + 2 * my + mc
        local = pltpu.make_async_copy(x_ref, o_ref.at[mine], local_sem)
        local.start()
        copies = []
        for k in range(1, N_DEV):
            peer = (_flip(mx, k >> 2), _flip(my, (k >> 1) & 1), _flip(mc, k & 1))
            copies.append(pltpu.make_async_remote_copy(
                src_ref=x_ref, dst_ref=o_ref.at[mine], send_sem=send_sems.at[k - 1], recv_sem=recv_sems.at[k - 1],
                device_id=peer, device_id_type=MESH_ID))
            copies[-1].start()
        for k in range(1, N_DEV):
            peer = (_flip(mx, k >> 2), _flip(my, (k >> 1) & 1), _flip(mc, k & 1))
            pltpu.make_async_remote_copy(
                src_ref=x_ref, dst_ref=o_ref.at[4 * peer[0] + 2 * peer[1] + peer[2]], send_sem=send_sems.at[k - 1],
                recv_sem=recv_sems.at[k - 1], device_id=peer, device_id_type=MESH_ID).wait_recv()
        for cp in copies:
            cp.wait_send()
        local.wait()

    return pl.pallas_call(
        body, name=name, in_specs=[_ANY], out_specs=_ANY, out_shape=jax.ShapeDtypeStruct((N_DEV, r, w), x.dtype),
        scratch_shapes=[pltpu.SemaphoreType.DMA((N_DEV - 1,)), pltpu.SemaphoreType.DMA((N_DEV - 1,)),
                        pltpu.SemaphoreType.DMA],
    )(x)


def _pack(parts, dtype, row_unit):
    flat = jnp.concatenate([p.astype(dtype).reshape(-1) for p in parts])
    unit = row_unit * LANES
    pad = (-flat.shape[0]) % unit
    if pad:
        flat = jnp.concatenate([flat, jnp.zeros((pad,), dtype)])
    return flat.reshape(-1, LANES)


def _unpack(packed, shapes):
    flat = packed.reshape(-1)
    out, off = [], 0
    for s in shapes:
        n = math.prod(s)
        out.append(flat[off:off + n].reshape(s))
        off += n
    return out


def _f_rms(x, gain):
    return (_rms(x, gain),)


def _f_swiglu(gu):
    f = gu.shape[1] // 2
    return (_silu(gu[:, :f].astype(F32)) * gu[:, f:].astype(F32),)


def _f_xattn(q, kv):
    d = q.shape[1]
    hd = d // XA_HEADS
    outs = []
    for h in range(XA_HEADS):
        qh, kh, vh = q[:, h * hd:(h + 1) * hd], kv[:, h * hd:(h + 1) * hd], kv[:, d + h * hd:d + (h + 1) * hd]
        s = _dlo(qh, kh, NT) * (hd ** -0.5)
        s = s - jnp.max(s, axis=-1, keepdims=True)
        p = jnp.exp(s)
        p = p / jnp.sum(p, axis=-1, keepdims=True)
        outs.append(_dlo(p, vh, NN))
    return (jnp.concatenate(outs, axis=1),)


def _f_gmlp(uv, gain, bias, w_sp, b_sp):
    r = uv.shape[0]
    act = jax.nn.gelu(uv.astype(F32))
    u, v = act[:, :SG_WIDTH], act[:, SG_WIDTH:]
    mu = jnp.mean(v, axis=-1, keepdims=True)
    var = jnp.mean(jnp.square(v - mu), axis=-1, keepdims=True)
    v = (v - mu) * lax.rsqrt(var + NORM_EPS) * gain + bias
    row = lax.broadcasted_iota(jnp.int32, (SG_CHUNK, SG_CHUNK), 0)
    col = lax.broadcasted_iota(jnp.int32, (SG_CHUNK, SG_CHUNK), 1)
    lane_grp = lax.broadcasted_iota(jnp.int32, (b_sp.shape[0], SG_WIDTH), 1) // SG_DIM
    grp_row = lax.broadcasted_iota(jnp.int32, (b_sp.shape[0], SG_WIDTH), 0)
    spread = jnp.where(lane_grp == grp_row, 1.0, 0.0).astype(F32)
    bias_t = _dhi(b_sp, spread, TN)
    chunks = []
    for c in range(r // SG_CHUNK):
        vc = v[c * SG_CHUNK:(c + 1) * SG_CHUNK]
        parts = []
        for g in range(SG_GROUPS):
            wg = jnp.where(row >= col, w_sp[g], 0.0)
            parts.append(_dlo(wg, vc[:, g * SG_DIM:(g + 1) * SG_DIM], NN))
        chunks.append(jnp.concatenate(parts, axis=1) + bias_t)
    mixed = jnp.concatenate(chunks, axis=0) if len(chunks) > 1 else chunks[0]
    return (u * mixed,)


def _f_swa_mix(o1, o2, o3, l1, l2, l3):
    outs = []
    for h in range(SWA_HEADS):
        ls = [l[:, h:h + 1] for l in (l1, l2, l3)]
        mx = jnp.maximum(jnp.maximum(ls[0], ls[1]), ls[2])
        es = [jnp.exp(l - mx) for l in ls]
        den = es[0] + es[1] + es[2]
        sl = slice(h * SWA_DIM, (h + 1) * SWA_DIM)
        outs.append((es[0] * o1[:, sl] + es[1] * o2[:, sl] + es[2] * o3[:, sl]) / den)
    return (jnp.concatenate(outs, axis=1),)


def _swa_block(q, kp, kc, vp, vc, first, window, dilation):
    span = window // dilation
    qi = lax.broadcasted_iota(jnp.int32, (SWA_BLOCK, 2 * SWA_BLOCK), 0)
    kj = lax.broadcasted_iota(jnp.int32, (SWA_BLOCK, 2 * SWA_BLOCK), 1)
    rel = SWA_BLOCK + qi - kj
    valid = (rel >= 0) & (rel <= span) & jnp.logical_not(jnp.logical_and(first, kj < SWA_BLOCK))
    relf = (rel * dilation).astype(F32)
    kw = jnp.concatenate([kp, kc], axis=0)
    vw = jnp.concatenate([vp, vc], axis=0)
    lane = lax.broadcasted_iota(jnp.int32, (SWA_BLOCK, LANES), 1)
    outs, lse = [], jnp.zeros((SWA_BLOCK, LANES), F32)
    for h in range(SWA_HEADS):
        sl = slice(h * SWA_DIM, (h + 1) * SWA_DIM)
        slope = 2.0 ** (-8.0 * (h + 1) / SWA_HEADS)
        s = _dlo(q[:, sl], kw[:, sl], NT) * (SWA_DIM ** -0.5) - slope * relf
        s = jnp.where(valid, s, -1e30)
        m = jnp.max(s, axis=-1, keepdims=True)
        p = jnp.exp(s - m)
        den = jnp.sum(p, axis=-1, keepdims=True)
        outs.append(_dlo(p, vw[:, sl], NN) / den)
        lse = lse + jnp.where(lane == h, m + jnp.log(den), 0.0)
    return jnp.concatenate(outs, axis=1), lse


@jax.custom_vjp
def _unit_lower_inv(lower):
    c = lower.shape[0]
    assert c == 4 * INV_BLOCK
    row = lax.broadcasted_iota(jnp.int32, (c, c), 0)
    col = lax.broadcasted_iota(jnp.int32, (c, c), 1)
    eye = jnp.where(row == col, 1.0, 0.0).astype(F32)
    same = (row // INV_BLOCK) == (col // INV_BLOCK)
    pw = -jnp.where(same, lower, 0.0)
    d_inv = eye + pw
    for _ in range(int(math.log2(INV_BLOCK)) - 1):
        pw = _dlo(pw, pw, NN)
        d_inv = d_inv + _dlo(d_inv, pw, NN)
    n1 = _dlo(d_inv, jnp.where(same, 0.0, lower), NN)
    n2 = _dlo(n1, n1, NN)
    rough = _dlo(eye - n1 + n2 - _dlo(n1, n2, NN), d_inv, NN)
    residual = eye - _dhi(eye + lower, rough, NN)
    return rough + _dlo(rough, residual, NN)


def _unit_lower_inv_fwd(lower):
    t_inv = _unit_lower_inv(lower)
    return t_inv, t_inv


def _unit_lower_inv_bwd(t_inv, g):
    return (-_dlo(_dlo(t_inv, g, TN), t_inv, NT),)


_unit_lower_inv.defvjp(_unit_lower_inv_fwd, _unit_lower_inv_bwd)


def _dn_group(xx, z, ba, state, conv_w, a_log, dt_bias, gain):
    rows, c = z.shape[0], DN_CHUNK
    acc = conv_w[0:1] * xx[HALO - 3:HALO - 3 + rows]
    for j in range(1, DN_CONV):
        acc = acc + conv_w[j:j + 1] * xx[HALO - 3 + j:HALO - 3 + j + rows]
    qkv = _silu(acc)
    beta_all = _sigmoid(ba)
    g_all = -jnp.exp(a_log) * _softplus(ba + dt_bias)
    row = lax.broadcasted_iota(jnp.int32, (c, c), 0)
    col = lax.broadcasted_iota(jnp.int32, (c, c), 1)
    incl, strict = row >= col, row > col
    tri = jnp.where(incl, 1.0, 0.0).astype(F32)
    local = []
    for ci in range(rows // c):
        r0 = ci * c
        gc_all = _dhi(tri, g_all[r0:r0 + c], NN)
        gc_t = gc_all.T
        heads = []
        for h in range(DN_HEADS):
            q = qkv[r0:r0 + c, h * DN_DIM:(h + 1) * DN_DIM]
            k = qkv[r0:r0 + c, DN_WIDTH + h * DN_DIM:DN_WIDTH + (h + 1) * DN_DIM]
            v = qkv[r0:r0 + c, 2 * DN_WIDTH + h * DN_DIM:2 * DN_WIDTH + (h + 1) * DN_DIM]
            q = q * lax.rsqrt(jnp.sum(q * q, axis=-1, keepdims=True) + NORM_EPS) * (DN_DIM ** -0.5)
            k = k * lax.rsqrt(jnp.sum(k * k, axis=-1, keepdims=True) + NORM_EPS)
            beta = beta_all[r0:r0 + c, h:h + 1]
            gc = gc_all[:, DN_HEADS + h:DN_HEADS + h + 1]
            g_last = gc[c - 1:c]
            diff = gc - gc_t[DN_HEADS + h:DN_HEADS + h + 1, :]
            decay = jnp.where(incl, jnp.exp(jnp.where(incl, diff, 0.0)), 0.0)
            kb = k * beta
            t_inv = _unit_lower_inv(jnp.where(strict, _dlo(kb, k, NT) * decay, 0.0))
            e_gc = jnp.exp(gc)
            u = _dlo(t_inv, v * beta, NN)
            w = _dlo(t_inv, kb * e_gc, NN)
            a_qk = jnp.where(incl, _dlo(q, k, NT) * decay, 0.0)
            heads.append((q * e_gc, k * jnp.exp(g_last - gc), u, w, a_qk, jnp.exp(g_last)))
        local.append(heads)
    s = list(state)
    out_rows = []
    for ci in range(rows // c):
        r0 = ci * c
        outs = []
        for h in range(DN_HEADS):
            q_dec, k_tail, u, w, a_qk, e_last = local[ci][h]
            v_new = u - _dlo(w, s[h], NN)
            o = _dlo(q_dec, s[h], NN) + _dlo(a_qk, v_new, NN)
            s[h] = s[h] * e_last + _dlo(k_tail, v_new, TN)
            o = o * lax.rsqrt(jnp.mean(o * o, axis=-1, keepdims=True) + NORM_EPS) * gain
            outs.append(o * _silu(z[r0:r0 + c, h * DN_DIM:(h + 1) * DN_DIM]))
        out_rows.append(jnp.concatenate(outs, axis=1))
    out = jnp.concatenate(out_rows, axis=0) if len(out_rows) > 1 else out_rows[0]
    return out, tuple(s)


def _dn_specs(n_of, rows):
    return [pl.BlockSpec((rows, 3 * DN_WIDTH), lambda i: (n_of(i), 0)),
            pl.BlockSpec((HALO, 3 * DN_WIDTH), lambda i: (jnp.maximum(n_of(i) * (rows // HALO) - 1, 0), 0)),
            pl.BlockSpec((rows, DN_WIDTH), lambda i: (n_of(i), 0)),
            pl.BlockSpec((rows, LANES), lambda i: (n_of(i), 0))]


def _dn_forward(xq, xz, xba, params, name):
    t = xq.shape[0]
    rows = min(DN_GROUP * DN_CHUNK, t)
    n_groups = t // rows

    def body(x_ref, halo_ref, z_ref, ba_ref, cw_ref, al_ref, dt_ref, gn_ref, o_ref, s_all_ref, s_ref):
        n = pl.program_id(0)

        @pl.when(n == 0)
        def _():
            s_ref[...] = jnp.zeros_like(s_ref)

        halo = jnp.where(n > 0, halo_ref[...], 0.0)
        xx = jnp.concatenate([halo, x_ref[...]], axis=0)
        s_all_ref[0] = s_ref[...]
        o, s_new = _dn_group(xx, z_ref[...], ba_ref[...], tuple(s_ref[h] for h in range(DN_HEADS)), cw_ref[...],
                             al_ref[...], dt_ref[...], gn_ref[...])
        o_ref[...] = o.astype(o_ref.dtype)
        for h in range(DN_HEADS):
            s_ref[h] = s_new[h]

    return pl.pallas_call(
        body, name=name, grid=(n_groups,),
        in_specs=_dn_specs(lambda i: i, rows) + [_full_spec(p) for p in params],
        out_specs=[pl.BlockSpec((rows, DN_WIDTH), lambda i: (i, 0)),
                   pl.BlockSpec((1, DN_HEADS, DN_DIM, DN_DIM), lambda i: (i, 0, 0, 0))],
        out_shape=[jax.ShapeDtypeStruct((t, DN_WIDTH), BF16),
                   jax.ShapeDtypeStruct((n_groups, DN_HEADS, DN_DIM, DN_DIM), F32)],
        scratch_shapes=[pltpu.VMEM((DN_HEADS, DN_DIM, DN_DIM), F32)],
        compiler_params=_cparams(("arbitrary",)),
    )(xq, xq, xz, xba, *params)


def _dn_backward(xq, xz, xba, params, s_all, d_out, name):
    t = xq.shape[0]
    rows = min(DN_GROUP * DN_CHUNK, t)
    n_groups = t // rows
    rev = lambda i: n_groups - 1 - i

    def body(x_ref, halo_ref, z_ref, ba_ref, cw_ref, al_ref, dt_ref, gn_ref, s_ref, do_ref,
             dx_ref, dz_ref, dba_ref, dcw_ref, dal_ref, ddt_ref, dgn_ref, ds_ref, dhalo_ref):
        i = pl.program_id(0)
        n = n_groups - 1 - i

        @pl.when(i == 0)
        def _():
            ds_ref[...] = jnp.zeros_like(ds_ref)
            dhalo_ref[...] = jnp.zeros_like(dhalo_ref)
            for r in (dcw_ref, dal_ref, ddt_ref, dgn_ref):
                r[...] = jnp.zeros_like(r)

        halo = jnp.where(n > 0, halo_ref[...], 0.0)
        xx = jnp.concatenate([halo, x_ref[...]], axis=0)
        _, pull = jax.vjp(_dn_group, xx, z_ref[...], ba_ref[...], tuple(s_ref[0, h] for h in range(DN_HEADS)),
                          cw_ref[...], al_ref[...], dt_ref[...], gn_ref[...])
        dxx, dz, dba, ds, dcw, dal, ddt, dgn = pull((do_ref[...].astype(F32), tuple(ds_ref[h] for h in range(DN_HEADS))))
        dx_ref[...] = jnp.concatenate([dxx[HALO:rows], dxx[rows:] + dhalo_ref[...]], axis=0).astype(dx_ref.dtype)
        dhalo_ref[...] = dxx[:HALO]
        dz_ref[...] = dz.astype(dz_ref.dtype)
        dba_ref[...] = dba.astype(dba_ref.dtype)
        for h in range(DN_HEADS):
            ds_ref[h] = ds[h]
        dcw_ref[...] += dcw
        dal_ref[...] += dal
        ddt_ref[...] += ddt
        dgn_ref[...] += dgn

    return pl.pallas_call(
        body, name=name, grid=(n_groups,),
        in_specs=_dn_specs(rev, rows) + [_full_spec(p) for p in params]
        + [pl.BlockSpec((1, DN_HEADS, DN_DIM, DN_DIM), lambda i: (rev(i), 0, 0, 0)),
           pl.BlockSpec((rows, DN_WIDTH), lambda i: (rev(i), 0))],
        out_specs=[pl.BlockSpec((rows, 3 * DN_WIDTH), lambda i: (rev(i), 0)),
                   pl.BlockSpec((rows, DN_WIDTH), lambda i: (rev(i), 0)),
                   pl.BlockSpec((rows, LANES), lambda i: (rev(i), 0))] + [_full_spec(p) for p in params],
        out_shape=[jax.ShapeDtypeStruct(xq.shape, BF16), jax.ShapeDtypeStruct(xz.shape, BF16),
                   jax.ShapeDtypeStruct(xba.shape, BF16)] + [jax.ShapeDtypeStruct(p.shape, F32) for p in params],
        scratch_shapes=[pltpu.VMEM((DN_HEADS, DN_DIM, DN_DIM), F32), pltpu.VMEM((HALO, 3 * DN_WIDTH), F32)],
        compiler_params=_cparams(("arbitrary",)),
    )(xq, xq, xz, xba, *params, s_all, d_out)


def _swa_forward(xs, window, dilation, name):
    t = xs.shape[0]
    d, l = dilation, t // dilation
    nb = l // SWA_BLOCK
    view = xs.reshape(l, d * 3 * SWA_WIDTH)
    blk = (SWA_BLOCK, SWA_WIDTH)

    def body(q_ref, kp_ref, kc_ref, vp_ref, vc_ref, o_ref, l_ref):
        blocks = [r[...].astype(F32) for r in (q_ref, kp_ref, kc_ref, vp_ref, vc_ref)]
        o, lse = _swa_block(*blocks, pl.program_id(1) == 0, window, dilation)
        o_ref[...] = o
        l_ref[...] = lse

    prev = lambda n: jnp.maximum(n - 1, 0)
    o, lse = pl.pallas_call(
        body, name=name, grid=(d, nb),
        in_specs=[pl.BlockSpec(blk, lambda r, n: (n, 3 * r)), pl.BlockSpec(blk, lambda r, n: (prev(n), 3 * r + 1)),
                  pl.BlockSpec(blk, lambda r, n: (n, 3 * r + 1)), pl.BlockSpec(blk, lambda r, n: (prev(n), 3 * r + 2)),
                  pl.BlockSpec(blk, lambda r, n: (n, 3 * r + 2))],
        out_specs=[pl.BlockSpec(blk, lambda r, n: (n, r)), pl.BlockSpec((SWA_BLOCK, LANES), lambda r, n: (n, r))],
        out_shape=[jax.ShapeDtypeStruct((l, d * SWA_WIDTH), F32), jax.ShapeDtypeStruct((l, d * LANES), F32)],
        compiler_params=_cparams(("parallel", "parallel")),
    )(view, view, view, view, view)
    return o.reshape(t, SWA_WIDTH), lse.reshape(t, LANES)


def _swa_backward(xs, d_o, d_lse, acc, window, dilation, name):
    t = xs.shape[0]
    d, l = dilation, t // dilation
    nb = l // SWA_BLOCK
    view = xs.reshape(l, d * 3 * SWA_WIDTH)
    blk = (SWA_BLOCK, SWA_WIDTH)
    has_acc = acc is not None

    def body(*refs):
        q_ref, kp_ref, kc_ref, vp_ref, vc_ref, do_ref, dl_ref = refs[:7]
        acc_refs = refs[7:10] if has_acc else None
        dq_ref, dk_ref, dv_ref, ck_ref, cv_ref = refs[-5:]
        i = pl.program_id(1)
        n = nb - 1 - i

        @pl.when(i == 0)
        def _():
            ck_ref[...] = jnp.zeros_like(ck_ref)
            cv_ref[...] = jnp.zeros_like(cv_ref)

        f = functools.partial(_swa_block, first=n == 0, window=window, dilation=dilation)
        _, pull = jax.vjp(f, *[r[...].astype(F32) for r in (q_ref, kp_ref, kc_ref, vp_ref, vc_ref)])
        dq, dkp, dkc, dvp, dvc = pull((do_ref[...], dl_ref[...]))
        dk = dkc + ck_ref[...]
        dv = dvc + cv_ref[...]
        if has_acc:
            dq, dk, dv = dq + acc_refs[0][...], dk + acc_refs[1][...], dv + acc_refs[2][...]
        dq_ref[...] = dq
        dk_ref[...] = dk
        dv_ref[...] = dv
        ck_ref[...] = dkp
        cv_ref[...] = dvp

    cur = lambda i: nb - 1 - i
    prev = lambda i: jnp.maximum(nb - 2 - i, 0)
    own = pl.BlockSpec(blk, lambda r, i: (cur(i), r))
    accs = [a.reshape(l, d * SWA_WIDTH) for a in acc] if has_acc else []
    outs = pl.pallas_call(
        body, name=name, grid=(d, nb),
        in_specs=[pl.BlockSpec(blk, lambda r, i: (cur(i), 3 * r)), pl.BlockSpec(blk, lambda r, i: (prev(i), 3 * r + 1)),
                  pl.BlockSpec(blk, lambda r, i: (cur(i), 3 * r + 1)), pl.BlockSpec(blk, lambda r, i: (prev(i), 3 * r + 2)),
                  pl.BlockSpec(blk, lambda r, i: (cur(i), 3 * r + 2)), own,
                  pl.BlockSpec((SWA_BLOCK, LANES), lambda r, i: (cur(i), r))] + [own] * len(accs),
        out_specs=[own] * 3, out_shape=[jax.ShapeDtypeStruct((l, d * SWA_WIDTH), F32)] * 3,
        scratch_shapes=[pltpu.VMEM(blk, F32), pltpu.VMEM(blk, F32)],
        compiler_params=_cparams(("parallel", "arbitrary")),
    )(view, view, view, view, view, d_o.reshape(l, d * SWA_WIDTH), d_lse.reshape(l, d * LANES), *accs)
    return tuple(o.reshape(t, SWA_WIDTH) for o in outs)


def _loss_head(h, target, gain, name, tile=256):
    t, d = h.shape
    tile = min(tile, t)

    def body(h_ref, t_ref, g_ref, loss_ref, dh_ref, dg_ref):
        def f(hv, gv):
            err = _rms(hv, gv) - t_ref[...]
            return 0.5 * jnp.sum(jnp.mean(err * err, axis=-1, keepdims=True), axis=0, keepdims=True)

        val, pull = jax.vjp(f, h_ref[...], g_ref[...])
        dh, dg = pull(jnp.ones((1, 1), F32))
        dh_ref[...] = dh

        @pl.when(pl.program_id(0) == 0)
        def _():
            loss_ref[...] = jnp.zeros_like(loss_ref)
            dg_ref[...] = jnp.zeros_like(dg_ref)

        loss_ref[...] += jnp.broadcast_to(val, loss_ref.shape)
        dg_ref[...] += dg

    return pl.pallas_call(
        body, name=name, grid=(t // tile,),
        in_specs=[pl.BlockSpec((tile, d), lambda i: (i, 0)), pl.BlockSpec((tile, d), lambda i: (i, 0)), _full_spec(gain)],
        out_specs=[pl.BlockSpec((1, LANES), lambda i: (0, 0)), pl.BlockSpec((tile, d), lambda i: (i, 0)), _full_spec(gain)],
        out_shape=[jax.ShapeDtypeStruct((1, LANES), F32), jax.ShapeDtypeStruct((t, d), F32),
                   jax.ShapeDtypeStruct(gain.shape, F32)],
        compiler_params=_cparams(("arbitrary",)),
    )(h, target, gain)


def _split_w_in(w_in):
    cuts = [0]
    for s in IN_SIZES:
        cuts.append(cuts[-1] + s)
    qkv, z = w_in[:, cuts[0]:cuts[1]], w_in[:, cuts[1]:cuts[2]]
    ba = jnp.pad(w_in[:, cuts[2]:cuts[4]], ((0, 0), (0, LANES - 2 * DN_HEADS)))
    return qkv, z, ba, w_in[:, cuts[4]:cuts[5]], w_in[:, cuts[5]:cuts[6]]


def _lane_pad(v, offset):
    return jnp.pad(v.reshape(1, -1), ((0, 0), (offset, LANES - offset - v.shape[0])))


def _layer_params(sm, i):
    return dict(
        ffn1_norm=sm["ffn1_norm"][i][None], mix_norm=sm["mix_norm"][i][None], xa_norm=sm["xa_norm"][i][None],
        xa_mem_norm=sm["xa_mem_norm"][i][None], ffn2_norm=sm["ffn2_norm"][i][None],
        dn=(sm["dn_conv_w"][i], _lane_pad(sm["dn_a_log"][i], DN_HEADS), _lane_pad(sm["dn_dt_bias"][i], DN_HEADS),
            sm["dn_out_norm"][i][None]),
        sg=(sm["sg_norm_gain"][i][None], sm["sg_norm_bias"][i][None], sm["sg_w_spatial"][i],
            jnp.pad(sm["sg_b_spatial"][i], ((0, 8 - SG_GROUPS), (0, 0)))),
    )


def _ffn_fwd(h, gain, w_gu, w_d, tag):
    n = _rows(_f_rms, [h], [gain], [(h.shape[1], BF16)], tile=512, name=f"{tag}_norm")[0]
    gu = _mm(n, w_gu, NN, out_dtype=BF16, name=f"{tag}_gate_up")
    a = _rows(_f_swiglu, [gu], [], [(gu.shape[1] // 2, BF16)], tile=256, name=f"{tag}_act")[0]
    out = _mm(a, w_d, NN, out_dtype=F32, res=h, scale=0.5, name=f"{tag}_down")
    return out, (h, n, gu, a)


def _ffn_bwd(dh, saved, gain, w_gu, w_d, tag):
    h, n, gu, a = saved
    da = _mm(dh, w_d, NT, out_dtype=F32, scale=0.5, name=f"{tag}_down_dx")
    dw_d = _mm(a, dh, TN, out_dtype=BF16, scale=0.5, name=f"{tag}_down_dw")
    dgu = _rows_vjp(_f_swiglu, [gu], [], [da], diff=[True], grad_dtypes=[BF16], tile=256, name=f"{tag}_act_bwd")[0]
    dn = _mm(dgu, w_gu, NT, out_dtype=F32, name=f"{tag}_gate_up_dx")
    dw_gu = _mm(n, dgu, TN, out_dtype=BF16, name=f"{tag}_gate_up_dw")
    dh_in, dgain = _rows_vjp(_f_rms, [h], [gain], [dn], diff=[True], grad_dtypes=[F32], tile=512, add=[dh],
                             name=f"{tag}_norm_bwd")
    return dh_in, dgain, dw_gu, dw_d


def _mixer_fwd(h, p, w_in, w_out, tag):
    d = h.shape[1]
    n = _rows(_f_rms, [h], [p["mix_norm"]], [(d, BF16)], tile=512, name=f"{tag}_norm")[0]
    w_parts = _split_w_in(w_in)
    xq, xz, xba, xs, xg = (_mm(n, w, NN, out_dtype=BF16 if j == 3 else F32, name=f"{tag}_in{j}") for j, w in enumerate(w_parts))
    oa, s_all = _dn_forward(xq, xz, xba, p["dn"], name=f"{tag}_dn")
    swa = [_swa_forward(xs, wnd, dil, name=f"{tag}_swa{j}") for j, (wnd, dil) in enumerate(SWA_PATTERNS)]
    ob = _rows(_f_swa_mix, [o for o, _ in swa] + [l for _, l in swa], [], [(SWA_WIDTH, BF16)], tile=512,
               name=f"{tag}_swa_mix")[0]
    oc = _rows(_f_gmlp, [xg], list(p["sg"]), [(SG_WIDTH, BF16)], tile=256, name=f"{tag}_gmlp")[0]
    merged = jnp.concatenate([oa, ob, oc], axis=1)
    out = _mm(merged, w_out, NN, out_dtype=F32, res=h, name=f"{tag}_out")
    return out, (h, n, xq, xz, xba, xs, xg, s_all, swa, merged)


def _mixer_bwd(dh, saved, p, w_in, w_out, tag):
    h, n, xq, xz, xba, xs, xg, s_all, swa, merged = saved
    dw_out = _mm(merged, dh, TN, out_dtype=BF16, name=f"{tag}_out_dw")
    doa = _mm(dh, w_out[:DN_WIDTH], NT, out_dtype=F32, name=f"{tag}_out_dxa")
    dob = _mm(dh, w_out[DN_WIDTH:DN_WIDTH + SWA_WIDTH], NT, out_dtype=F32, name=f"{tag}_out_dxb")
    doc = _mm(dh, w_out[DN_WIDTH + SWA_WIDTH:], NT, out_dtype=F32, name=f"{tag}_out_dxc")
    res = _rows_vjp(_f_gmlp, [xg], list(p["sg"]), [doc], diff=[True], grad_dtypes=[BF16], tile=256, name=f"{tag}_gmlp_bwd")
    dxg, d_sg = res[0], res[1:]
    mix_in = [o for o, _ in swa] + [l for _, l in swa]
    d_mix = _rows_vjp(_f_swa_mix, mix_in, [], [dob], diff=[True] * 6, grad_dtypes=[F32] * 6, tile=512,
                      name=f"{tag}_swa_mix_bwd")
    acc = None
    for j, (wnd, dil) in enumerate(SWA_PATTERNS):
        acc = _swa_backward(xs, d_mix[j], d_mix[3 + j], acc, wnd, dil, name=f"{tag}_swa{j}_bwd")
    dxs = jnp.concatenate([a.astype(BF16) for a in acc], axis=1)
    res = _dn_backward(xq, xz, xba, p["dn"], s_all, doa, name=f"{tag}_dn_bwd")
    (dxq, dxz, dxba), d_dn = res[:3], res[3:]
    w_parts = _split_w_in(w_in)
    dn = None
    dws = []
    for j, (dx, w) in enumerate(zip((dxq, dxz, dxba, dxs, dxg), w_parts, strict=True)):
        dn = _mm(dx, w, NT, out_dtype=F32, res=dn, name=f"{tag}_in{j}_dx")
        dws.append(_mm(n, dx, TN, out_dtype=BF16, name=f"{tag}_in{j}_dw"))
    dws[2] = dws[2][:, :2 * DN_HEADS]
    dw_in = jnp.concatenate(dws, axis=1)
    dh_in, dgain = _rows_vjp(_f_rms, [h], [p["mix_norm"]], [dn], diff=[True], grad_dtypes=[F32], tile=512, add=[dh],
                             name=f"{tag}_norm_bwd")
    return dh_in, dgain, dw_in, dw_out, d_dn, d_sg


def _xattn_fwd(h, mem, p, w_q, w_kv, w_o, tag):
    d = h.shape[1]
    n = _rows(_f_rms, [h], [p["xa_norm"]], [(d, BF16)], tile=512, name=f"{tag}_norm")[0]
    mn = _rows(_f_rms, [mem], [p["xa_mem_norm"]], [(d, BF16)], tile=512, name=f"{tag}_mem_norm")[0]
    q = _mm(n, w_q, NN, out_dtype=BF16, name=f"{tag}_q")
    kv = _mm(mn, w_kv, NN, out_dtype=BF16, name=f"{tag}_kv")
    o = _rows(_f_xattn, [q], [kv], [(d, BF16)], tile=256, name=f"{tag}_core")[0]
    out = _mm(o, w_o, NN, out_dtype=F32, res=h, name=f"{tag}_o")
    return out, (h, n, mn, q, kv, o)


def _xattn_bwd(dh, saved, mem, p, w_q, w_kv, w_o, tag):
    h, n, mn, q, kv, o = saved
    do = _mm(dh, w_o, NT, out_dtype=BF16, name=f"{tag}_o_dx")
    dw_o = _mm(o, dh, TN, out_dtype=BF16, name=f"{tag}_o_dw")
    dq, dkv = _rows_vjp(_f_xattn, [q], [kv], [do], diff=[True], grad_dtypes=[BF16], tile=256, name=f"{tag}_core_bwd")
    dn = _mm(dq, w_q, NT, out_dtype=F32, name=f"{tag}_q_dx")
    dw_q = _mm(n, dq, TN, out_dtype=BF16, name=f"{tag}_q_dw")
    dmn = _mm(dkv, w_kv, NT, out_dtype=F32, name=f"{tag}_kv_dx")
    dw_kv = _mm(mn, dkv, TN, out_dtype=BF16, name=f"{tag}_kv_dw")
    dmem_gain = _rows_vjp(_f_rms, [mem], [p["xa_mem_norm"]], [dmn], diff=[False], grad_dtypes=[], tile=512,
                          name=f"{tag}_mem_norm_bwd")[0]
    dh_in, dgain = _rows_vjp(_f_rms, [h], [p["xa_norm"]], [dn], diff=[True], grad_dtypes=[F32], tile=512, add=[dh],
                             name=f"{tag}_norm_bwd")
    return dh_in, dgain, dmem_gain, dw_q, dw_kv, dw_o


def kernel(x, mem, ffn1_norm, ffn1_w_gate_up, ffn1_w_down, mix_norm, mix_w_in, dn_conv_w, dn_a_log, dn_dt_bias, dn_out_norm, sg_norm_gain, sg_norm_bias, sg_w_spatial, sg_b_spatial, mix_w_out, xa_norm, xa_mem_norm, xa_w_q, xa_w_kv, xa_w_o, ffn2_norm, ffn2_w_gate_up, ffn2_w_down, final_norm, loss_target, m_ffn1_norm, m_ffn1_w_gate_up, m_ffn1_w_down, m_mix_norm, m_mix_w_in, m_dn_conv_w, m_dn_a_log, m_dn_dt_bias, m_dn_out_norm, m_sg_norm_gain, m_sg_norm_bias, m_sg_w_spatial, m_sg_b_spatial, m_mix_w_out, m_xa_norm, m_xa_mem_norm, m_xa_w_q, m_xa_w_kv, m_xa_w_o, m_ffn2_norm, m_ffn2_w_gate_up, m_ffn2_w_down, m_final_norm, v_ffn1_norm, v_ffn1_w_gate_up, v_ffn1_w_down, v_mix_norm, v_mix_w_in, v_dn_conv_w, v_dn_a_log, v_dn_dt_bias, v_dn_out_norm, v_sg_norm_gain, v_sg_norm_bias, v_sg_w_spatial, v_sg_b_spatial, v_mix_w_out, v_xa_norm, v_xa_mem_norm, v_xa_w_q, v_xa_w_kv, v_xa_w_o, v_ffn2_norm, v_ffn2_w_gate_up, v_ffn2_w_down, v_final_norm):
    args = dict(locals())
    wts = {k: args[k] for k in WEIGHTS}
    mom_m = {k: args["m_" + k] for k in WEIGHTS}
    mom_v = {k: args["v_" + k] for k in WEIGHTS}
    depth = ffn1_norm.shape[0]
    h = x[0]
    mem2 = mem[0]
    target = loss_target[0]

    assert depth == 2, "core c of a chip is responsible for layer c in the weight and gradient exchanges"
    kinds = {k: "stack" if k == "mix_w_in" else ("row" if BIG_AXIS[k] == 1 else "col") for k in BIG}
    whole = _gather_weights([wts[k].astype(BF16) for k in BIG] + [dn_conv_w], [kinds[k] for k in BIG] + ["stack"],
                            name="gather_weights")
    full = dict(zip(BIG, whole[:-1], strict=True))
    full["mix_w_in"] = jnp.concatenate([full["mix_w_in"][:, j] for j in range(N_CHIPS)], axis=2)
    small = {k: wts[k] for k in SMALL}
    small["dn_conv_w"] = jnp.concatenate([whole[-1][:, j] for j in range(N_CHIPS)], axis=2)

    saved = []
    for i in range(depth):
        p = _layer_params(small, i)
        h, s1 = _ffn_fwd(h, p["ffn1_norm"], full["ffn1_w_gate_up"][i], full["ffn1_w_down"][i], f"l{i}_ffn1")
        h, s2 = _mixer_fwd(h, p, full["mix_w_in"][i], full["mix_w_out"][i], f"l{i}_mix")
        h, s3 = _xattn_fwd(h, mem2, p, full["xa_w_q"][i], full["xa_w_kv"][i], full["xa_w_o"][i], f"l{i}_xa")
        h, s4 = _ffn_fwd(h, p["ffn2_norm"], full["ffn2_w_gate_up"][i], full["ffn2_w_down"][i], f"l{i}_ffn2")
        saved.append((p, s1, s2, s3, s4))
    loss_part, dh, d_final = _loss_head(h, target, final_norm[None], name="loss_head")
    loss = lax.psum(loss_part[0, 0], ("x", "y", "c"))

    g_big = {k: [None] * depth for k in BIG}
    g_small = {k: [None] * depth for k in SMALL if k != "final_norm"}
    g_small["dn_conv_w"] = [None] * depth
    for i in reversed(range(depth)):
        p, s1, s2, s3, s4 = saved[i]
        dh, dg, dw_gu, dw_d = _ffn_bwd(dh, s4, p["ffn2_norm"], full["ffn2_w_gate_up"][i], full["ffn2_w_down"][i], f"l{i}_ffn2")
        g_small["ffn2_norm"][i], g_big["ffn2_w_gate_up"][i], g_big["ffn2_w_down"][i] = dg[0], dw_gu, dw_d
        dh, dg, dmg, dw_q, dw_kv, dw_o = _xattn_bwd(dh, s3, mem2, p, full["xa_w_q"][i], full["xa_w_kv"][i],
                                                    full["xa_w_o"][i], f"l{i}_xa")
        g_small["xa_norm"][i], g_small["xa_mem_norm"][i] = dg[0], dmg[0]
        g_big["xa_w_q"][i], g_big["xa_w_kv"][i], g_big["xa_w_o"][i] = dw_q, dw_kv, dw_o
        dh, dg, dw_in, dw_out, d_dn, d_sg = _mixer_bwd(dh, s2, p, full["mix_w_in"][i], full["mix_w_out"][i], f"l{i}_mix")
        g_small["mix_norm"][i], g_big["mix_w_in"][i], g_big["mix_w_out"][i] = dg[0], dw_in, dw_out
        g_small["dn_conv_w"][i] = d_dn[0]
        g_small["dn_a_log"][i] = d_dn[1][0, DN_HEADS:2 * DN_HEADS]
        g_small["dn_dt_bias"][i] = d_dn[2][0, DN_HEADS:2 * DN_HEADS]
        g_small["dn_out_norm"][i] = d_dn[3][0]
        g_small["sg_norm_gain"][i], g_small["sg_norm_bias"][i] = d_sg[0][0], d_sg[1][0]
        g_small["sg_w_spatial"][i], g_small["sg_b_spatial"][i] = d_sg[2], d_sg[3][:SG_GROUPS]
        dh, dg, dw_gu, dw_d = _ffn_bwd(dh, s1, p["ffn1_norm"], full["ffn1_w_gate_up"][i], full["ffn1_w_down"][i], f"l{i}_ffn1")
        g_small["ffn1_norm"][i], g_big["ffn1_w_gate_up"][i], g_big["ffn1_w_down"][i] = dg[0], dw_gu, dw_d
    grad_x = dh[None]
    g_small = {k: jnp.stack(v) for k, v in g_small.items()}
    g_small["final_norm"] = d_final[0]

    def layer_grad(k, i):
        g = g_big[k][i]
        if kinds[k] == "stack":
            n = wts[k].shape[2]
            g = jnp.stack([g[:, j * n:(j + 1) * n] for j in range(N_CHIPS)])
        return g

    core = lax.axis_index("c")
    chip = 2 * lax.axis_index("x") + lax.axis_index("y")
    pick = jnp.full((1, 1), core, jnp.int32)
    g0, g1 = [layer_grad(k, 0) for k in BIG], [layer_grad(k, 1) for k in BIG]
    got = _give_other_layer(g0, g1, name="give_other_layer")
    chip_sum = [_sum_picked(a, b, pick, g, BF16, name=f"sum_cores_{k}") for k, a, b, g in zip(BIG, g0, g1, got, strict=True)]
    recv = _scatter_quarters(chip_sum, [wts[k].shape[1:] for k in BIG], [kinds[k] for k in BIG], name="scatter_grads")

    def own_quarter(k, g):
        if kinds[k] == "stack":
            return lax.dynamic_index_in_dim(g, chip, axis=0, keepdims=False)
        axis = BIG_AXIS[k] - 1
        n = wts[k].shape[BIG_AXIS[k]]
        return lax.dynamic_slice_in_dim(g, chip * n, n, axis=axis)

    parts = [_sum_slots(r, F32, name=f"sum_chips_{k}", first=own_quarter(k, g))
             for k, r, g in zip(BIG, recv, chip_sum, strict=True)]
    others = _swap_cores(parts, name="swap_cores")
    g_fin = {k: jnp.where(core == 0, jnp.stack([p, o]), jnp.stack([o, p])) for k, p, o in zip(BIG, parts, others, strict=True)}
    small_names = list(SMALL) + ["dn_conv_w"]
    small_shapes = [g_small[k].shape for k in small_names]
    small_sum = _sum_slots(_gather_all(_pack([g_small[k] for k in small_names], F32, 64), name="gather_small"), F32,
                           name="sum_small")
    gs = dict(zip(small_names, _unpack(small_sum, small_shapes), strict=True))
    n_conv = dn_conv_w.shape[2]
    gs["dn_conv_w"] = lax.dynamic_slice_in_dim(gs["dn_conv_w"], (2 * lax.axis_index("x") + lax.axis_index("y")) * n_conv,
                                               n_conv, axis=2)

    results = {}
    for k in BIG:
        shp = wts[k].shape
        two_d = lambda a, _s=shp: a.reshape(-1, _s[-1])
        res = _adamw(two_d(wts[k]), two_d(g_fin[k]), two_d(mom_m[k]), two_d(mom_v[k]), name=f"adamw_{k}")
        results[k] = [g_fin[k]] + [r.reshape(shp) for r in res]
    sm_shapes = [wts[k].shape for k in small_names]
    pk = lambda d: _pack([d[k] for k in small_names], F32, 64)
    res = [_unpack(r, sm_shapes) for r in _adamw(pk(wts), pk(gs), pk(mom_m), pk(mom_v), name="adamw_small")]
    for i, k in enumerate(small_names):
        results[k] = [gs[k]] + [res[j][i] for j in range(3)]

    out = [loss, grad_x]
    for j in range(4):
        out += [results[k][j] for k in WEIGHTS]
    return tuple(out)
```

```python
import functools
import math

import jax
import jax.numpy as jnp
from jax import lax
from jax.experimental import pallas as pl
from jax.experimental.pallas import tpu as pltpu

F32, BF16 = jnp.float32, jnp.bfloat16
HI = lax.Precision.HIGHEST
NN, NT, TN = ((1,), (0,)), ((1,), (1,)), ((0,), (0,))

NORM_EPS = 1e-6
LANES = 128
V7X_VMEM_BYTES = 64 * 2**20
VMEM_LIMIT = V7X_VMEM_BYTES * 3 // 4

DN_HEADS, DN_DIM, DN_CHUNK, DN_CONV, HALO = 4, 128, 64, 4, 8
DN_GROUP = 4
INV_BLOCK = 16
DN_WIDTH = DN_HEADS * DN_DIM
SWA_HEADS, SWA_DIM, SWA_BLOCK = 4, 64, 128
SWA_WIDTH = SWA_HEADS * SWA_DIM
SWA_PATTERNS = ((128, 1), (512, 4), (2048, 16))
SG_GROUPS, SG_DIM, SG_CHUNK = 4, 64, 128
SG_WIDTH = SG_GROUPS * SG_DIM
XA_HEADS = 4
IN_SIZES = (3 * DN_WIDTH, DN_WIDTH, DN_HEADS, DN_HEADS, 3 * SWA_WIDTH, 2 * SG_WIDTH)
ADAM_LR, ADAM_B1, ADAM_B2, ADAM_EPS, ADAM_WD, ADAM_STEP = 0.001, 0.9, 0.999, 1e-08, 0.01, 10
N_CHIPS, N_DEV = 4, 8
MESH_ID = pl.DeviceIdType.MESH

BIG = ("ffn1_w_gate_up", "ffn1_w_down", "mix_w_in", "mix_w_out", "xa_w_q", "xa_w_kv", "xa_w_o",
       "ffn2_w_gate_up", "ffn2_w_down")
BIG_AXIS = {"ffn1_w_gate_up": 2, "ffn1_w_down": 1, "mix_w_in": 2, "mix_w_out": 1, "xa_w_q": 1, "xa_w_kv": 2,
            "xa_w_o": 1, "ffn2_w_gate_up": 2, "ffn2_w_down": 1}
SMALL = ("ffn1_norm", "mix_norm", "dn_a_log", "dn_dt_bias", "dn_out_norm", "sg_norm_gain", "sg_norm_bias",
         "sg_w_spatial", "sg_b_spatial", "xa_norm", "xa_mem_norm", "ffn2_norm", "final_norm")
WEIGHTS = ("ffn1_norm", "ffn1_w_gate_up", "ffn1_w_down", "mix_norm", "mix_w_in", "dn_conv_w", "dn_a_log",
           "dn_dt_bias", "dn_out_norm", "sg_norm_gain", "sg_norm_bias", "sg_w_spatial", "sg_b_spatial",
           "mix_w_out", "xa_norm", "xa_mem_norm", "xa_w_q", "xa_w_kv", "xa_w_o", "ffn2_norm", "ffn2_w_gate_up",
           "ffn2_w_down", "final_norm")


@functools.partial(jax.custom_vjp, nondiff_argnums=(2,))
def _dlo(a, b, dims):
    return lax.dot_general(a.astype(BF16), b.astype(BF16), (dims, ((), ())), preferred_element_type=F32)


def _dlo_fwd(a, b, dims):
    return _dlo(a, b, dims), (a, b)


def _dlo_bwd(dims, saved, g):
    a, b = saved
    if dims == NN:
        da, db = _dlo(g, b, NT), _dlo(a, g, TN)
    elif dims == NT:
        da, db = _dlo(g, b, NN), _dlo(g, a, TN)
    else:
        da, db = _dlo(b, g, NT), _dlo(a, g, NN)
    return da.astype(a.dtype), db.astype(b.dtype)


_dlo.defvjp(_dlo_fwd, _dlo_bwd)


def _dhi(a, b, dims):
    return lax.dot_general(a, b, (dims, ((), ())), preferred_element_type=F32, precision=HI)


def _sigmoid(x):
    return 1.0 / (1.0 + jnp.exp(-x))


def _silu(x):
    return x * _sigmoid(x)


def _softplus(x):
    return jnp.maximum(x, 0.0) + jnp.log(1.0 + jnp.exp(-jnp.abs(x)))


def _rms(x, gain):
    x = x.astype(F32)
    return x * lax.rsqrt(jnp.mean(x * x, axis=-1, keepdims=True) + NORM_EPS) * gain


def _tile(n, target, unit=LANES):
    best = None
    for t in range(unit, min(n, target) + 1, unit):
        if n % t == 0:
            best = t
    return best if best is not None else n


def _cparams(sem):
    return pltpu.CompilerParams(dimension_semantics=sem, vmem_limit_bytes=VMEM_LIMIT)


def _mm(a, b, dims, *, out_dtype, name, res=None, scale=1.0, tm=512, tn=1408, tk=2816):
    if dims == NN:
        (m, k), n = a.shape, b.shape[1]
    elif dims == NT:
        (m, k), n = a.shape, b.shape[0]
    else:
        (k, m), n = a.shape, b.shape[1]
    tm, tn, tk = _tile(m, tm, LANES if dims == TN else 8), _tile(n, tn), _tile(k, tk)
    nk = k // tk
    a_spec = pl.BlockSpec((tk, tm), lambda i, j, kk: (kk, i)) if dims == TN else pl.BlockSpec((tm, tk), lambda i, j, kk: (i, kk))
    b_spec = pl.BlockSpec((tn, tk), lambda i, j, kk: (j, kk)) if dims == NT else pl.BlockSpec((tk, tn), lambda i, j, kk: (kk, j))
    o_spec = pl.BlockSpec((tm, tn), lambda i, j, kk: (i, j))
    has_res = res is not None

    def finish(acc, r_ref, o_ref):
        val = acc * scale if scale != 1.0 else acc
        if has_res:
            val = r_ref[...].astype(F32) + val
        o_ref[...] = val.astype(o_ref.dtype)

    def body(*refs):
        a_ref, b_ref = refs[0], refs[1]
        r_ref = refs[2] if has_res else None
        part = lax.dot_general(a_ref[...].astype(BF16), b_ref[...].astype(BF16), (dims, ((), ())),
                               preferred_element_type=F32)
        if nk == 1:
            finish(part, r_ref, refs[-1])
            return
        o_ref, acc_ref = refs[-2], refs[-1]
        kk = pl.program_id(2)

        @pl.when(kk == 0)
        def _():
            acc_ref[...] = part

        @pl.when(jnp.logical_and(kk > 0, kk < nk - 1))
        def _():
            acc_ref[...] += part

        @pl.when(kk == nk - 1)
        def _():
            finish(acc_ref[...] + part, r_ref, o_ref)

    return pl.pallas_call(
        body, name=name, grid=(m // tm, n // tn, nk),
        in_specs=[a_spec, b_spec] + ([o_spec] if has_res else []), out_specs=o_spec,
        out_shape=jax.ShapeDtypeStruct((m, n), out_dtype),
        scratch_shapes=[pltpu.VMEM((tm, tn), F32)] if nk > 1 else [],
        compiler_params=_cparams(("parallel", "parallel", "arbitrary")),
    )(*([a, b] + ([res] if has_res else [])))


def _full_spec(p):
    nd = p.ndim
    return pl.BlockSpec(p.shape, lambda i, _nd=nd: (0,) * _nd)


def _rows(f, rows, params, outs, *, tile, name):
    t = rows[0].shape[0]
    tile = min(tile, t)
    nr, npar = len(rows), len(params)

    def body(*refs):
        vals = f(*[r[...] for r in refs[:nr + npar]])
        for o_ref, v in zip(refs[nr + npar:], vals, strict=True):
            o_ref[...] = v.astype(o_ref.dtype)

    res = pl.pallas_call(
        body, name=name, grid=(t // tile,),
        in_specs=[pl.BlockSpec((tile, r.shape[1]), lambda i: (i, 0)) for r in rows] + [_full_spec(p) for p in params],
        out_specs=[pl.BlockSpec((tile, w), lambda i: (i, 0)) for w, _ in outs],
        out_shape=[jax.ShapeDtypeStruct((t, w), d) for w, d in outs],
        compiler_params=_cparams(("parallel",)),
    )(*rows, *params)
    return tuple(res)


def _rows_vjp(f, rows, params, cts, *, diff, grad_dtypes, tile, name, add=None):
    t = rows[0].shape[0]
    tile = min(tile, t)
    nr, npar, nct = len(rows), len(params), len(cts)
    didx = [i for i, d in enumerate(diff) if d]
    add = [None] * len(didx) if add is None else add
    adds = [a for a in add if a is not None]

    def body(*refs):
        row_refs, par_refs = refs[:nr], refs[nr:nr + npar]
        ct_refs = refs[nr + npar:nr + npar + nct]
        add_refs = list(refs[nr + npar + nct:nr + npar + nct + len(adds)])
        out_refs = refs[nr + npar + nct + len(adds):]
        rv = [r[...] for r in row_refs]
        pv = [p[...].astype(F32) for p in par_refs]

        def g(*args):
            full = list(rv)
            for k, i in enumerate(didx):
                full[i] = args[k]
            return f(*full, *args[len(didx):])

        outs, pull = jax.vjp(g, *[rv[i] for i in didx], *pv)
        grads = pull(tuple(c[...].astype(o.dtype) for c, o in zip(ct_refs, outs, strict=True)))
        for k in range(len(didx)):
            val = grads[k].astype(F32)
            if add[k] is not None:
                val = val + add_refs.pop(0)[...].astype(F32)
            out_refs[k][...] = val.astype(out_refs[k].dtype)

        @pl.when(pl.program_id(0) == 0)
        def _():
            for o_ref in out_refs[len(didx):]:
                o_ref[...] = jnp.zeros_like(o_ref)

        for o_ref, gp in zip(out_refs[len(didx):], grads[len(didx):], strict=True):
            o_ref[...] += gp.astype(F32)

    row_spec = lambda a: pl.BlockSpec((tile, a.shape[1]), lambda i: (i, 0))
    res = pl.pallas_call(
        body, name=name, grid=(t // tile,),
        in_specs=[row_spec(r) for r in rows] + [_full_spec(p) for p in params] + [row_spec(c) for c in cts]
        + [row_spec(a) for a in adds],
        out_specs=[row_spec(rows[i]) for i in didx] + [_full_spec(p) for p in params],
        out_shape=[jax.ShapeDtypeStruct(rows[i].shape, d) for i, d in zip(didx, grad_dtypes, strict=True)]
        + [jax.ShapeDtypeStruct(p.shape, F32) for p in params],
        compiler_params=_cparams(("arbitrary",)),
    )(*rows, *params, *cts, *adds)
    return tuple(res)


def _sum_picked(a0, a1, pick, other, out_dtype, name):
    shape = a0.shape
    views = [a.reshape(-1, shape[-1]) for a in (a0, a1, other)]
    r, c = views[0].shape
    tile = _tile(r, max(16, (1 << 18) // c), 16)

    def body(a0_ref, a1_ref, other_ref, pick_ref, o_ref):
        mine = jnp.where(pick_ref[...] == 0, a0_ref[...].astype(F32), a1_ref[...].astype(F32))
        o_ref[...] = (mine + other_ref[...].astype(F32)).astype(o_ref.dtype)

    spec = pl.BlockSpec((tile, c), lambda i: (i, 0))
    return pl.pallas_call(
        body, name=name, grid=(r // tile,), in_specs=[spec] * 3 + [_full_spec(pick)], out_specs=spec,
        out_shape=jax.ShapeDtypeStruct((r, c), out_dtype), compiler_params=_cparams(("parallel",)),
    )(*views, pick).reshape(shape)


def _sum_slots(x, out_dtype, name, first=None):
    n, r, c = x.shape
    tile = _tile(r, max(16, (1 << 18) // c), 16)

    def body(*refs):
        acc = refs[0][...].astype(F32)
        for ref in refs[1:-1]:
            acc = acc + ref[...].astype(F32)
        refs[-1][...] = acc.astype(refs[-1].dtype)

    spec = pl.BlockSpec((tile, c), lambda i: (i, 0))
    return pl.pallas_call(
        body, name=name, grid=(r // tile,),
        in_specs=([spec] if first is not None else [])
        + [pl.BlockSpec((None, tile, c), lambda i, _s=s_: (_s, i, 0)) for s_ in range(n)],
        out_specs=spec, out_shape=jax.ShapeDtypeStruct((r, c), out_dtype), compiler_params=_cparams(("parallel",)),
    )(*(([first] if first is not None else []) + [x] * n))


def _adamw(w, g, m, v, name):
    r, c = w.shape
    tile = _tile(r, max(8, (1 << 18) // c), 8)

    def body(w_ref, g_ref, m_ref, v_ref, d_out, m_out, v_out):
        g = g_ref[...]
        mn = ADAM_B1 * m_ref[...] + (1.0 - ADAM_B1) * g
        vn = ADAM_B2 * v_ref[...] + (1.0 - ADAM_B2) * (g * g)
        m_hat = mn / (1.0 - ADAM_B1 ** ADAM_STEP)
        v_hat = vn / (1.0 - ADAM_B2 ** ADAM_STEP)
        d_out[...] = -ADAM_LR * (m_hat / (jnp.sqrt(v_hat) + ADAM_EPS) + ADAM_WD * w_ref[...])
        m_out[...] = mn
        v_out[...] = vn

    spec = pl.BlockSpec((tile, c), lambda i: (i, 0))
    return pl.pallas_call(
        body, name=name, grid=(r // tile,), in_specs=[spec] * 4, out_specs=[spec] * 3,
        out_shape=[jax.ShapeDtypeStruct((r, c), F32)] * 3, compiler_params=_cparams(("parallel",)),
    )(w, g, m, v)


def _place():
    return lax.axis_index("x"), lax.axis_index("y"), lax.axis_index("c")


def _flip(v, bit):
    return 1 - v if bit else v


_ANY = pl.BlockSpec(memory_space=pl.ANY)


def _quarter(ref, j, shape, kind):
    if kind == "row":
        return ref.at[pl.ds(j * shape[0], shape[0])]
    if kind == "col":
        return ref.at[:, pl.ds(j * shape[1], shape[1])]
    return ref.at[j]


def _whole_shape(shape, kind):
    if kind == "row":
        return (N_CHIPS * shape[0],) + tuple(shape[1:])
    if kind == "col":
        return (shape[0], N_CHIPS * shape[1]) + tuple(shape[2:])
    return (N_CHIPS,) + tuple(shape)


def _dma_sems(*counts):
    return [pltpu.SemaphoreType.DMA((n,)) for n in counts]


def _gather_weights(shards, kinds, name):
    n = len(shards)
    shapes = [s.shape[1:] for s in shards]

    def body(*refs):
        ins, outs = refs[:n], refs[n:2 * n]
        send_sems, recv_sems, pair_send, pair_recv, own_send, own_recv = refs[2 * n:]
        mx, my, mc = _place()
        me = 2 * mx + my
        started = []
        for t in range(n):
            started.append(pltpu.make_async_remote_copy(
                src_ref=ins[t].at[1 - mc], dst_ref=_quarter(outs[t].at[1 - mc], me, shapes[t], kinds[t]),
                send_sem=own_send.at[t], recv_sem=own_recv.at[t], device_id=(mx, my, 1 - mc), device_id_type=MESH_ID))
            started[-1].start()
            for k in range(1, N_CHIPS):
                started.append(pltpu.make_async_remote_copy(
                    src_ref=ins[t].at[mc], dst_ref=_quarter(outs[t].at[mc], me, shapes[t], kinds[t]),
                    send_sem=send_sems.at[3 * t + k - 1], recv_sem=recv_sems.at[3 * t + k - 1],
                    device_id=(_flip(mx, k >> 1), _flip(my, k & 1), mc), device_id_type=MESH_ID))
                started[-1].start()
        for t in range(n):
            pltpu.make_async_remote_copy(
                src_ref=ins[t].at[mc], dst_ref=_quarter(outs[t].at[mc], me, shapes[t], kinds[t]),
                send_sem=own_send.at[t], recv_sem=own_recv.at[t], device_id=(mx, my, 1 - mc),
                device_id_type=MESH_ID).wait_recv()
            for k in range(1, N_CHIPS):
                px, py = _flip(mx, k >> 1), _flip(my, k & 1)
                pltpu.make_async_remote_copy(
                    src_ref=ins[t].at[mc], dst_ref=_quarter(outs[t].at[mc], 2 * px + py, shapes[t], kinds[t]),
                    send_sem=send_sems.at[3 * t + k - 1], recv_sem=recv_sems.at[3 * t + k - 1],
                    device_id=(px, py, mc), device_id_type=MESH_ID).wait_recv()
        for t in range(n):
            started.append(pltpu.make_async_remote_copy(
                src_ref=outs[t].at[mc], dst_ref=outs[t].at[mc], send_sem=pair_send.at[t], recv_sem=pair_recv.at[t],
                device_id=(mx, my, 1 - mc), device_id_type=MESH_ID))
            started[-1].start()
        for t in range(n):
            pltpu.make_async_remote_copy(
                src_ref=outs[t].at[1 - mc], dst_ref=outs[t].at[1 - mc], send_sem=pair_send.at[t], recv_sem=pair_recv.at[t],
                device_id=(mx, my, 1 - mc), device_id_type=MESH_ID).wait_recv()
        for cp in started:
            cp.wait_send()

    return pl.pallas_call(
        body, name=name, in_specs=[_ANY] * n, out_specs=[_ANY] * n,
        out_shape=[jax.ShapeDtypeStruct((2,) + _whole_shape(sh, kd), s.dtype) for s, sh, kd in zip(shards, shapes, kinds)],
        scratch_shapes=_dma_sems(3 * n, 3 * n, n, n, n, n),
    )(*shards)


def _give_other_layer(g0, g1, name):
    n = len(g0)

    def body(*refs):
        a, b, got = refs[:n], refs[n:2 * n], refs[2 * n:3 * n]
        send_sems, recv_sems = refs[3 * n:]
        mx, my, mc = _place()
        peer = (mx, my, 1 - mc)
        for give, core in ((b, 0), (a, 1)):
            @pl.when(mc == core)
            def _():
                for t in range(n):
                    pltpu.make_async_remote_copy(src_ref=give[t], dst_ref=got[t], send_sem=send_sems.at[t],
                                                 recv_sem=recv_sems.at[t], device_id=peer, device_id_type=MESH_ID).start()
        for t in range(n):
            cp = pltpu.make_async_remote_copy(src_ref=a[t], dst_ref=got[t], send_sem=send_sems.at[t],
                                              recv_sem=recv_sems.at[t], device_id=peer, device_id_type=MESH_ID)
            cp.wait_recv()
            cp.wait_send()

    return pl.pallas_call(
        body, name=name, in_specs=[_ANY] * (2 * n), out_specs=[_ANY] * n,
        out_shape=[jax.ShapeDtypeStruct(g.shape, g.dtype) for g in g0], scratch_shapes=_dma_sems(n, n),
    )(*g0, *g1)


def _scatter_quarters(gs, shapes, kinds, name):
    n = len(gs)

    def body(*refs):
        ins, outs = refs[:n], refs[n:2 * n]
        send_sems, recv_sems = refs[2 * n:]
        mx, my, mc = _place()
        sends = []
        for t in range(n):
            for k in range(1, N_CHIPS):
                px, py = _flip(mx, k >> 1), _flip(my, k & 1)
                sends.append(pltpu.make_async_remote_copy(
                    src_ref=_quarter(ins[t], 2 * px + py, shapes[t], kinds[t]), dst_ref=outs[t].at[k - 1],
                    send_sem=send_sems.at[3 * t + k - 1], recv_sem=recv_sems.at[3 * t + k - 1],
                    device_id=(px, py, mc), device_id_type=MESH_ID))
                sends[-1].start()
        for cp in sends:
            cp.wait_recv()
        for cp in sends:
            cp.wait_send()

    return pl.pallas_call(
        body, name=name, in_specs=[_ANY] * n, out_specs=[_ANY] * n,
        out_shape=[jax.ShapeDtypeStruct((N_CHIPS - 1,) + tuple(sh), g.dtype) for g, sh in zip(gs, shapes)],
        scratch_shapes=_dma_sems(3 * n, 3 * n),
    )(*gs)


def _swap_cores(parts, name):
    n = len(parts)

    def body(*refs):
        ins, outs = refs[:n], refs[n:2 * n]
        send_sems, recv_sems = refs[2 * n:]
        mx, my, mc = _place()
        copies = [pltpu.make_async_remote_copy(src_ref=ins[t], dst_ref=outs[t], send_sem=send_sems.at[t],
                                               recv_sem=recv_sems.at[t], device_id=(mx, my, 1 - mc),
                                               device_id_type=MESH_ID) for t in range(n)]
        for cp in copies:
            cp.start()
        for cp in copies:
            cp.wait_recv()
        for cp in copies:
            cp.wait_send()

    return pl.pallas_call(
        body, name=name, in_specs=[_ANY] * n, out_specs=[_ANY] * n,
        out_shape=[jax.ShapeDtypeStruct(p.shape, p.dtype) for p in parts], scratch_shapes=_dma_sems(n, n),
    )(*parts)


def _gather_all(x, name):
    r, w = x.shape

    def body(x_ref, o_ref, send_sems, recv_sems, local_sem):
        mx, my, mc = _place()
        mine = 4 * mx + 2 * my + mc
        local = pltpu.make_async_copy(x_ref, o_ref.at[mine], local_sem)
        local.start()
        copies = []
        for k in range(1, N_DEV):
            peer = (_flip(mx, k >> 2), _flip(my, (k >> 1) & 1), _flip(mc, k & 1))
            copies.append(pltpu.make_async_remote_copy(
                src_ref=x_ref, dst_ref=o_ref.at[mine], send_sem=send_sems.at[k - 1], recv_sem=recv_sems.at[k - 1],
                device_id=peer, device_id_type=MESH_ID))
            copies[-1].start()
        for k in range(1, N_DEV):
            peer = (_flip(mx, k >> 2), _flip(my, (k >> 1) & 1), _flip(mc, k & 1))
            pltpu.make_async_remote_copy(
                src_ref=x_ref, dst_ref=o_ref.at[4 * peer[0] + 2 * peer[1] + peer[2]], send_sem=send_sems.at[k - 1],
                recv_sem=recv_sems.at[k - 1], device_id=peer, device_id_type=MESH_ID).wait_recv()
        for cp in copies:
            cp.wait_send()
        local.wait()

    return pl.pallas_call(
        body, name=name, in_specs=[_ANY], out_specs=_ANY, out_shape=jax.ShapeDtypeStruct((N_DEV, r, w), x.dtype),
        scratch_shapes=[pltpu.SemaphoreType.DMA((N_DEV - 1,)), pltpu.SemaphoreType.DMA((N_DEV - 1,)),
                        pltpu.SemaphoreType.DMA],
    )(x)


def _pack(parts, dtype, row_unit):
    flat = jnp.concatenate([p.astype(dtype).reshape(-1) for p in parts])
    unit = row_unit * LANES
    pad = (-flat.shape[0]) % unit
    if pad:
        flat = jnp.concatenate([flat, jnp.zeros((pad,), dtype)])
    return flat.reshape(-1, LANES)


def _unpack(packed, shapes):
    flat = packed.reshape(-1)
    out, off = [], 0
    for s in shapes:
        n = math.prod(s)
        out.append(flat[off:off + n].reshape(s))
        off += n
    return out


def _f_rms(x, gain):
    return (_rms(x, gain),)


def _f_swiglu(gu):
    f = gu.shape[1] // 2
    return (_silu(gu[:, :f].astype(F32)) * gu[:, f:].astype(F32),)


def _f_xattn(q, kv):
    d = q.shape[1]
    hd = d // XA_HEADS
    outs = []
    for h in range(XA_HEADS):
        qh, kh, vh = q[:, h * hd:(h + 1) * hd], kv[:, h * hd:(h + 1) * hd], kv[:, d + h * hd:d + (h + 1) * hd]
        s = _dlo(qh, kh, NT) * (hd ** -0.5)
        s = s - jnp.max(s, axis=-1, keepdims=True)
        p = jnp.exp(s)
        p = p / jnp.sum(p, axis=-1, keepdims=True)
        outs.append(_dlo(p, vh, NN))
    return (jnp.concatenate(outs, axis=1),)


def _f_gmlp(uv, gain, bias, w_sp, b_sp):
    r = uv.shape[0]
    act = jax.nn.gelu(uv.astype(F32))
    u, v = act[:, :SG_WIDTH], act[:, SG_WIDTH:]
    mu = jnp.mean(v, axis=-1, keepdims=True)
    var = jnp.mean(jnp.square(v - mu), axis=-1, keepdims=True)
    v = (v - mu) * lax.rsqrt(var + NORM_EPS) * gain + bias
    row = lax.broadcasted_iota(jnp.int32, (SG_CHUNK, SG_CHUNK), 0)
    col = lax.broadcasted_iota(jnp.int32, (SG_CHUNK, SG_CHUNK), 1)
    lane_grp = lax.broadcasted_iota(jnp.int32, (b_sp.shape[0], SG_WIDTH), 1) // SG_DIM
    grp_row = lax.broadcasted_iota(jnp.int32, (b_sp.shape[0], SG_WIDTH), 0)
    spread = jnp.where(lane_grp == grp_row, 1.0, 0.0).astype(F32)
    bias_t = _dhi(b_sp, spread, TN)
    chunks = []
    for c in range(r // SG_CHUNK):
        vc = v[c * SG_CHUNK:(c + 1) * SG_CHUNK]
        parts = []
        for g in range(SG_GROUPS):
            wg = jnp.where(row >= col, w_sp[g], 0.0)
            parts.append(_dlo(wg, vc[:, g * SG_DIM:(g + 1) * SG_DIM], NN))
        chunks.append(jnp.concatenate(parts, axis=1) + bias_t)
    mixed = jnp.concatenate(chunks, axis=0) if len(chunks) > 1 else chunks[0]
    return (u * mixed,)


def _f_swa_mix(o1, o2, o3, l1, l2, l3):
    outs = []
    for h in range(SWA_HEADS):
        ls = [l[:, h:h + 1] for l in (l1, l2, l3)]
        mx = jnp.maximum(jnp.maximum(ls[0], ls[1]), ls[2])
        es = [jnp.exp(l - mx) for l in ls]
        den = es[0] + es[1] + es[2]
        sl = slice(h * SWA_DIM, (h + 1) * SWA_DIM)
        outs.append((es[0] * o1[:, sl] + es[1] * o2[:, sl] + es[2] * o3[:, sl]) / den)
    return (jnp.concatenate(outs, axis=1),)


def _swa_block(q, kp, kc, vp, vc, first, window, dilation):
    span = window // dilation
    qi = lax.broadcasted_iota(jnp.int32, (SWA_BLOCK, 2 * SWA_BLOCK), 0)
    kj = lax.broadcasted_iota(jnp.int32, (SWA_BLOCK, 2 * SWA_BLOCK), 1)
    rel = SWA_BLOCK + qi - kj
    valid = (rel >= 0) & (rel <= span) & jnp.logical_not(jnp.logical_and(first, kj < SWA_BLOCK))
    relf = (rel * dilation).astype(F32)
    kw = jnp.concatenate([kp, kc], axis=0)
    vw = jnp.concatenate([vp, vc], axis=0)
    lane = lax.broadcasted_iota(jnp.int32, (SWA_BLOCK, LANES), 1)
    outs, lse = [], jnp.zeros((SWA_BLOCK, LANES), F32)
    for h in range(SWA_HEADS):
        sl = slice(h * SWA_DIM, (h + 1) * SWA_DIM)
        slope = 2.0 ** (-8.0 * (h + 1) / SWA_HEADS)
        s = _dlo(q[:, sl], kw[:, sl], NT) * (SWA_DIM ** -0.5) - slope * relf
        s = jnp.where(valid, s, -1e30)
        m = jnp.max(s, axis=-1, keepdims=True)
        p = jnp.exp(s - m)
        den = jnp.sum(p, axis=-1, keepdims=True)
        outs.append(_dlo(p, vw[:, sl], NN) / den)
        lse = lse + jnp.where(lane == h, m + jnp.log(den), 0.0)
    return jnp.concatenate(outs, axis=1), lse


@jax.custom_vjp
def _unit_lower_inv(lower):
    c = lower.shape[0]
    assert DN_CHUNK == 4 * INV_BLOCK and c % DN_CHUNK == 0
    row = lax.broadcasted_iota(jnp.int32, (c, c), 0)
    col = lax.broadcasted_iota(jnp.int32, (c, c), 1)
    eye = jnp.where(row == col, 1.0, 0.0).astype(F32)
    same = (row // INV_BLOCK) == (col // INV_BLOCK)
    pw = -jnp.where(same, lower, 0.0)
    d_inv = eye + pw
    for _ in range(int(math.log2(INV_BLOCK)) - 1):
        pw = _dlo(pw, pw, NN)
        d_inv = d_inv + _dlo(d_inv, pw, NN)
    n1 = _dlo(d_inv, jnp.where(same, 0.0, lower), NN)
    n2 = _dlo(n1, n1, NN)
    rough = _dlo(eye - n1 + n2 - _dlo(n1, n2, NN), d_inv, NN)
    residual = eye - _dhi(eye + lower, rough, NN)
    return rough + _dlo(rough, residual, NN)


def _unit_lower_inv_fwd(lower):
    t_inv = _unit_lower_inv(lower)
    return t_inv, t_inv


def _unit_lower_inv_bwd(t_inv, g):
    return (-_dlo(_dlo(t_inv, g, TN), t_inv, NT),)


_unit_lower_inv.defvjp(_unit_lower_inv_fwd, _unit_lower_inv_bwd)


def _dn_group(xx, z, ba, state, conv_w, a_log, dt_bias, gain):
    rows, c = z.shape[0], DN_CHUNK
    hc = DN_HEADS * c
    acc = conv_w[0:1] * xx[HALO - 3:HALO - 3 + rows]
    for j in range(1, DN_CONV):
        acc = acc + conv_w[j:j + 1] * xx[HALO - 3 + j:HALO - 3 + j + rows]
    qkv = _silu(acc)
    beta_all = _sigmoid(ba)
    g_all = -jnp.exp(a_log) * _softplus(ba + dt_bias)
    row = lax.broadcasted_iota(jnp.int32, (hc, hc), 0)
    col = lax.broadcasted_iota(jnp.int32, (hc, hc), 1)
    same_head = (row // c) == (col // c)
    incl, strict = same_head & (row >= col), same_head & (row > col)
    tri = jnp.where(incl[:c, :c], 1.0, 0.0).astype(F32)

    def stack(piece):
        return jnp.concatenate([piece(h) for h in range(DN_HEADS)], axis=0)

    local = []
    for ci in range(rows // c):
        r0 = ci * c
        gc_all = _dhi(tri, g_all[r0:r0 + c], NN)
        gc_t = gc_all.T
        q = stack(lambda h: qkv[r0:r0 + c, h * DN_DIM:(h + 1) * DN_DIM])
        k = stack(lambda h: qkv[r0:r0 + c, DN_WIDTH + h * DN_DIM:DN_WIDTH + (h + 1) * DN_DIM])
        v = stack(lambda h: qkv[r0:r0 + c, 2 * DN_WIDTH + h * DN_DIM:2 * DN_WIDTH + (h + 1) * DN_DIM])
        q = q * lax.rsqrt(jnp.sum(q * q, axis=-1, keepdims=True) + NORM_EPS) * (DN_DIM ** -0.5)
        k = k * lax.rsqrt(jnp.sum(k * k, axis=-1, keepdims=True) + NORM_EPS)
        beta = stack(lambda h: beta_all[r0:r0 + c, h:h + 1])
        gc = stack(lambda h: gc_all[:, DN_HEADS + h:DN_HEADS + h + 1])
        g_last = stack(lambda h: jnp.broadcast_to(gc_all[c - 1:c, DN_HEADS + h:DN_HEADS + h + 1], (c, 1)))
        gc_row = jnp.concatenate([gc_t[DN_HEADS + h:DN_HEADS + h + 1, :] for h in range(DN_HEADS)], axis=1)
        decay = jnp.where(incl, jnp.exp(jnp.where(incl, gc - gc_row, 0.0)), 0.0)
        kb = k * beta
        t_inv = _unit_lower_inv(jnp.where(strict, _dlo(kb, k, NT) * decay, 0.0))
        e_gc = jnp.exp(gc)
        u = _dlo(t_inv, v * beta, NN)
        w = _dlo(t_inv, kb * e_gc, NN)
        a_qk = jnp.where(incl, _dlo(q, k, NT) * decay, 0.0)
        e_last = jnp.concatenate([jnp.broadcast_to(jnp.exp(gc_all[c - 1:c, DN_HEADS + h:DN_HEADS + h + 1]), (1, DN_DIM))
                                  for h in range(DN_HEADS)], axis=1)
        local.append((jnp.concatenate([w, q * e_gc], axis=0), k * jnp.exp(g_last - gc), u, a_qk, e_last))
    own = (lax.broadcasted_iota(jnp.int32, (hc, DN_WIDTH), 0) // c) == (lax.broadcasted_iota(jnp.int32, (hc, DN_WIDTH), 1) // DN_DIM)

    def own_blocks(m):
        return stack(lambda h: m[h * c:(h + 1) * c, h * DN_DIM:(h + 1) * DN_DIM])

    s = state
    out_rows = []
    for ci in range(rows // c):
        r0 = ci * c
        wq, k_tail, u, a_qk, e_last = local[ci]
        through = _dlo(wq, s, NN)
        v_new = u - own_blocks(through[:hc])
        o = own_blocks(through[hc:]) + _dlo(a_qk, v_new, NN)
        v_wide = jnp.where(own, jnp.concatenate([v_new] * DN_HEADS, axis=1), 0.0)
        s = s * e_last + _dlo(k_tail, v_wide, TN)
        o = o * lax.rsqrt(jnp.mean(o * o, axis=-1, keepdims=True) + NORM_EPS) * gain
        o = jnp.concatenate([o[h * c:(h + 1) * c] for h in range(DN_HEADS)], axis=1)
        out_rows.append(o * _silu(z[r0:r0 + c]))
    out = jnp.concatenate(out_rows, axis=0) if len(out_rows) > 1 else out_rows[0]
    return out, s


def _dn_specs(n_of, rows):
    return [pl.BlockSpec((rows, 3 * DN_WIDTH), lambda i: (n_of(i), 0)),
            pl.BlockSpec((HALO, 3 * DN_WIDTH), lambda i: (jnp.maximum(n_of(i) * (rows // HALO) - 1, 0), 0)),
            pl.BlockSpec((rows, DN_WIDTH), lambda i: (n_of(i), 0)),
            pl.BlockSpec((rows, LANES), lambda i: (n_of(i), 0))]


def _dn_forward(xq, xz, xba, params, name):
    t = xq.shape[0]
    rows = min(DN_GROUP * DN_CHUNK, t)
    n_groups = t // rows

    def body(x_ref, halo_ref, z_ref, ba_ref, cw_ref, al_ref, dt_ref, gn_ref, o_ref, s_all_ref, s_ref):
        n = pl.program_id(0)

        @pl.when(n == 0)
        def _():
            s_ref[...] = jnp.zeros_like(s_ref)

        halo = jnp.where(n > 0, halo_ref[...], 0.0)
        xx = jnp.concatenate([halo, x_ref[...]], axis=0)
        s_all_ref[0] = s_ref[...]
        o, s_new = _dn_group(xx, z_ref[...], ba_ref[...], s_ref[...], cw_ref[...], al_ref[...], dt_ref[...], gn_ref[...])
        o_ref[...] = o.astype(o_ref.dtype)
        s_ref[...] = s_new

    return pl.pallas_call(
        body, name=name, grid=(n_groups,),
        in_specs=_dn_specs(lambda i: i, rows) + [_full_spec(p) for p in params],
        out_specs=[pl.BlockSpec((rows, DN_WIDTH), lambda i: (i, 0)),
                   pl.BlockSpec((1, DN_DIM, DN_WIDTH), lambda i: (i, 0, 0))],
        out_shape=[jax.ShapeDtypeStruct((t, DN_WIDTH), BF16),
                   jax.ShapeDtypeStruct((n_groups, DN_DIM, DN_WIDTH), F32)],
        scratch_shapes=[pltpu.VMEM((DN_DIM, DN_WIDTH), F32)],
        compiler_params=_cparams(("arbitrary",)),
    )(xq, xq, xz, xba, *params)


def _dn_backward(xq, xz, xba, params, s_all, d_out, name):
    t = xq.shape[0]
    rows = min(DN_GROUP * DN_CHUNK, t)
    n_groups = t // rows
    rev = lambda i: n_groups - 1 - i

    def body(x_ref, halo_ref, z_ref, ba_ref, cw_ref, al_ref, dt_ref, gn_ref, s_ref, do_ref,
             dx_ref, dz_ref, dba_ref, dcw_ref, dal_ref, ddt_ref, dgn_ref, ds_ref, dhalo_ref):
        i = pl.program_id(0)
        n = n_groups - 1 - i

        @pl.when(i == 0)
        def _():
            ds_ref[...] = jnp.zeros_like(ds_ref)
            dhalo_ref[...] = jnp.zeros_like(dhalo_ref)
            for r in (dcw_ref, dal_ref, ddt_ref, dgn_ref):
                r[...] = jnp.zeros_like(r)

        halo = jnp.where(n > 0, halo_ref[...], 0.0)
        xx = jnp.concatenate([halo, x_ref[...]], axis=0)
        _, pull = jax.vjp(_dn_group, xx, z_ref[...], ba_ref[...], s_ref[0], cw_ref[...], al_ref[...], dt_ref[...],
                          gn_ref[...])
        dxx, dz, dba, ds, dcw, dal, ddt, dgn = pull((do_ref[...].astype(F32), ds_ref[...]))
        dx_ref[...] = jnp.concatenate([dxx[HALO:rows], dxx[rows:] + dhalo_ref[...]], axis=0).astype(dx_ref.dtype)
        dhalo_ref[...] = dxx[:HALO]
        dz_ref[...] = dz.astype(dz_ref.dtype)
        dba_ref[...] = dba.astype(dba_ref.dtype)
        ds_ref[...] = ds
        dcw_ref[...] += dcw
        dal_ref[...] += dal
        ddt_ref[...] += ddt
        dgn_ref[...] += dgn

    return pl.pallas_call(
        body, name=name, grid=(n_groups,),
        in_specs=_dn_specs(rev, rows) + [_full_spec(p) for p in params]
        + [pl.BlockSpec((1, DN_DIM, DN_WIDTH), lambda i: (rev(i), 0, 0)),
           pl.BlockSpec((rows, DN_WIDTH), lambda i: (rev(i), 0))],
        out_specs=[pl.BlockSpec((rows, 3 * DN_WIDTH), lambda i: (rev(i), 0)),
                   pl.BlockSpec((rows, DN_WIDTH), lambda i: (rev(i), 0)),
                   pl.BlockSpec((rows, LANES), lambda i: (rev(i), 0))] + [_full_spec(p) for p in params],
        out_shape=[jax.ShapeDtypeStruct(xq.shape, BF16), jax.ShapeDtypeStruct(xz.shape, BF16),
                   jax.ShapeDtypeStruct(xba.shape, BF16)] + [jax.ShapeDtypeStruct(p.shape, F32) for p in params],
        scratch_shapes=[pltpu.VMEM((DN_DIM, DN_WIDTH), F32), pltpu.VMEM((HALO, 3 * DN_WIDTH), F32)],
        compiler_params=_cparams(("arbitrary",)),
    )(xq, xq, xz, xba, *params, s_all, d_out)


def _swa_forward(xs, window, dilation, name):
    t = xs.shape[0]
    d, l = dilation, t // dilation
    nb = l // SWA_BLOCK
    view = xs.reshape(l, d * 3 * SWA_WIDTH)
    blk = (SWA_BLOCK, SWA_WIDTH)

    def body(q_ref, kp_ref, kc_ref, vp_ref, vc_ref, o_ref, l_ref):
        blocks = [r[...].astype(F32) for r in (q_ref, kp_ref, kc_ref, vp_ref, vc_ref)]
        o, lse = _swa_block(*blocks, pl.program_id(1) == 0, window, dilation)
        o_ref[...] = o
        l_ref[...] = lse

    prev = lambda n: jnp.maximum(n - 1, 0)
    o, lse = pl.pallas_call(
        body, name=name, grid=(d, nb),
        in_specs=[pl.BlockSpec(blk, lambda r, n: (n, 3 * r)), pl.BlockSpec(blk, lambda r, n: (prev(n), 3 * r + 1)),
                  pl.BlockSpec(blk, lambda r, n: (n, 3 * r + 1)), pl.BlockSpec(blk, lambda r, n: (prev(n), 3 * r + 2)),
                  pl.BlockSpec(blk, lambda r, n: (n, 3 * r + 2))],
        out_specs=[pl.BlockSpec(blk, lambda r, n: (n, r)), pl.BlockSpec((SWA_BLOCK, LANES), lambda r, n: (n, r))],
        out_shape=[jax.ShapeDtypeStruct((l, d * SWA_WIDTH), F32), jax.ShapeDtypeStruct((l, d * LANES), F32)],
        compiler_params=_cparams(("parallel", "parallel")),
    )(view, view, view, view, view)
    return o.reshape(t, SWA_WIDTH), lse.reshape(t, LANES)


def _swa_backward(xs, d_o, d_lse, acc, window, dilation, name):
    t = xs.shape[0]
    d, l = dilation, t // dilation
    nb = l // SWA_BLOCK
    view = xs.reshape(l, d * 3 * SWA_WIDTH)
    blk = (SWA_BLOCK, SWA_WIDTH)
    has_acc = acc is not None

    def body(*refs):
        q_ref, kp_ref, kc_ref, vp_ref, vc_ref, do_ref, dl_ref = refs[:7]
        acc_refs = refs[7:10] if has_acc else None
        dq_ref, dk_ref, dv_ref, ck_ref, cv_ref = refs[-5:]
        i = pl.program_id(1)
        n = nb - 1 - i

        @pl.when(i == 0)
        def _():
            ck_ref[...] = jnp.zeros_like(ck_ref)
            cv_ref[...] = jnp.zeros_like(cv_ref)

        f = functools.partial(_swa_block, first=n == 0, window=window, dilation=dilation)
        _, pull = jax.vjp(f, *[r[...].astype(F32) for r in (q_ref, kp_ref, kc_ref, vp_ref, vc_ref)])
        dq, dkp, dkc, dvp, dvc = pull((do_ref[...], dl_ref[...]))
        dk = dkc + ck_ref[...]
        dv = dvc + cv_ref[...]
        if has_acc:
            dq, dk, dv = dq + acc_refs[0][...], dk + acc_refs[1][...], dv + acc_refs[2][...]
        dq_ref[...] = dq
        dk_ref[...] = dk
        dv_ref[...] = dv
        ck_ref[...] = dkp
        cv_ref[...] = dvp

    cur = lambda i: nb - 1 - i
    prev = lambda i: jnp.maximum(nb - 2 - i, 0)
    own = pl.BlockSpec(blk, lambda r, i: (cur(i), r))
    accs = [a.reshape(l, d * SWA_WIDTH) for a in acc] if has_acc else []
    outs = pl.pallas_call(
        body, name=name, grid=(d, nb),
        in_specs=[pl.BlockSpec(blk, lambda r, i: (cur(i), 3 * r)), pl.BlockSpec(blk, lambda r, i: (prev(i), 3 * r + 1)),
                  pl.BlockSpec(blk, lambda r, i: (cur(i), 3 * r + 1)), pl.BlockSpec(blk, lambda r, i: (prev(i), 3 * r + 2)),
                  pl.BlockSpec(blk, lambda r, i: (cur(i), 3 * r + 2)), own,
                  pl.BlockSpec((SWA_BLOCK, LANES), lambda r, i: (cur(i), r))] + [own] * len(accs),
        out_specs=[own] * 3, out_shape=[jax.ShapeDtypeStruct((l, d * SWA_WIDTH), F32)] * 3,
        scratch_shapes=[pltpu.VMEM(blk, F32), pltpu.VMEM(blk, F32)],
        compiler_params=_cparams(("parallel", "arbitrary")),
    )(view, view, view, view, view, d_o.reshape(l, d * SWA_WIDTH), d_lse.reshape(l, d * LANES), *accs)
    return tuple(o.reshape(t, SWA_WIDTH) for o in outs)


def _loss_head(h, target, gain, name, tile=256):
    t, d = h.shape
    tile = min(tile, t)

    def body(h_ref, t_ref, g_ref, loss_ref, dh_ref, dg_ref):
        def f(hv, gv):
            err = _rms(hv, gv) - t_ref[...]
            return 0.5 * jnp.sum(jnp.mean(err * err, axis=-1, keepdims=True), axis=0, keepdims=True)

        val, pull = jax.vjp(f, h_ref[...], g_ref[...])
        dh, dg = pull(jnp.ones((1, 1), F32))
        dh_ref[...] = dh

        @pl.when(pl.program_id(0) == 0)
        def _():
            loss_ref[...] = jnp.zeros_like(loss_ref)
            dg_ref[...] = jnp.zeros_like(dg_ref)

        loss_ref[...] += jnp.broadcast_to(val, loss_ref.shape)
        dg_ref[...] += dg

    return pl.pallas_call(
        body, name=name, grid=(t // tile,),
        in_specs=[pl.BlockSpec((tile, d), lambda i: (i, 0)), pl.BlockSpec((tile, d), lambda i: (i, 0)), _full_spec(gain)],
        out_specs=[pl.BlockSpec((1, LANES), lambda i: (0, 0)), pl.BlockSpec((tile, d), lambda i: (i, 0)), _full_spec(gain)],
        out_shape=[jax.ShapeDtypeStruct((1, LANES), F32), jax.ShapeDtypeStruct((t, d), F32),
                   jax.ShapeDtypeStruct(gain.shape, F32)],
        compiler_params=_cparams(("arbitrary",)),
    )(h, target, gain)


def _split_w_in(w_in):
    cuts = [0]
    for s in IN_SIZES:
        cuts.append(cuts[-1] + s)
    qkv, z = w_in[:, cuts[0]:cuts[1]], w_in[:, cuts[1]:cuts[2]]
    ba = jnp.pad(w_in[:, cuts[2]:cuts[4]], ((0, 0), (0, LANES - 2 * DN_HEADS)))
    return qkv, z, ba, w_in[:, cuts[4]:cuts[5]], w_in[:, cuts[5]:cuts[6]]


def _lane_pad(v, offset):
    return jnp.pad(v.reshape(1, -1), ((0, 0), (offset, LANES - offset - v.shape[0])))


def _layer_params(sm, i):
    return dict(
        ffn1_norm=sm["ffn1_norm"][i][None], mix_norm=sm["mix_norm"][i][None], xa_norm=sm["xa_norm"][i][None],
        xa_mem_norm=sm["xa_mem_norm"][i][None], ffn2_norm=sm["ffn2_norm"][i][None],
        dn=(sm["dn_conv_w"][i], _lane_pad(sm["dn_a_log"][i], DN_HEADS), _lane_pad(sm["dn_dt_bias"][i], DN_HEADS),
            sm["dn_out_norm"][i][None]),
        sg=(sm["sg_norm_gain"][i][None], sm["sg_norm_bias"][i][None], sm["sg_w_spatial"][i],
            jnp.pad(sm["sg_b_spatial"][i], ((0, 8 - SG_GROUPS), (0, 0)))),
    )


def _ffn_fwd(h, gain, w_gu, w_d, tag):
    n = _rows(_f_rms, [h], [gain], [(h.shape[1], BF16)], tile=512, name=f"{tag}_norm")[0]
    gu = _mm(n, w_gu, NN, out_dtype=BF16, name=f"{tag}_gate_up")
    a = _rows(_f_swiglu, [gu], [], [(gu.shape[1] // 2, BF16)], tile=256, name=f"{tag}_act")[0]
    out = _mm(a, w_d, NN, out_dtype=F32, res=h, scale=0.5, name=f"{tag}_down")
    return out, (h, n, gu, a)


def _ffn_bwd(dh, saved, gain, w_gu, w_d, tag):
    h, n, gu, a = saved
    da = _mm(dh, w_d, NT, out_dtype=F32, scale=0.5, name=f"{tag}_down_dx")
    dw_d = _mm(a, dh, TN, out_dtype=BF16, scale=0.5, name=f"{tag}_down_dw")
    dgu = _rows_vjp(_f_swiglu, [gu], [], [da], diff=[True], grad_dtypes=[BF16], tile=256, name=f"{tag}_act_bwd")[0]
    dn = _mm(dgu, w_gu, NT, out_dtype=F32, name=f"{tag}_gate_up_dx")
    dw_gu = _mm(n, dgu, TN, out_dtype=BF16, name=f"{tag}_gate_up_dw")
    dh_in, dgain = _rows_vjp(_f_rms, [h], [gain], [dn], diff=[True], grad_dtypes=[F32], tile=512, add=[dh],
                             name=f"{tag}_norm_bwd")
    return dh_in, dgain, dw_gu, dw_d


def _mixer_fwd(h, p, w_in, w_out, tag):
    d = h.shape[1]
    n = _rows(_f_rms, [h], [p["mix_norm"]], [(d, BF16)], tile=512, name=f"{tag}_norm")[0]
    w_parts = _split_w_in(w_in)
    xq, xz, xba, xs, xg = (_mm(n, w, NN, out_dtype=BF16 if j == 3 else F32, name=f"{tag}_in{j}") for j, w in enumerate(w_parts))
    oa, s_all = _dn_forward(xq, xz, xba, p["dn"], name=f"{tag}_dn")
    swa = [_swa_forward(xs, wnd, dil, name=f"{tag}_swa{j}") for j, (wnd, dil) in enumerate(SWA_PATTERNS)]
    ob = _rows(_f_swa_mix, [o for o, _ in swa] + [l for _, l in swa], [], [(SWA_WIDTH, BF16)], tile=512,
               name=f"{tag}_swa_mix")[0]
    oc = _rows(_f_gmlp, [xg], list(p["sg"]), [(SG_WIDTH, BF16)], tile=256, name=f"{tag}_gmlp")[0]
    merged = jnp.concatenate([oa, ob, oc], axis=1)
    out = _mm(merged, w_out, NN, out_dtype=F32, res=h, name=f"{tag}_out")
    return out, (h, n, xq, xz, xba, xs, xg, s_all, swa, merged)


def _mixer_bwd(dh, saved, p, w_in, w_out, tag):
    h, n, xq, xz, xba, xs, xg, s_all, swa, merged = saved
    dw_out = _mm(merged, dh, TN, out_dtype=BF16, name=f"{tag}_out_dw")
    doa = _mm(dh, w_out[:DN_WIDTH], NT, out_dtype=F32, name=f"{tag}_out_dxa")
    dob = _mm(dh, w_out[DN_WIDTH:DN_WIDTH + SWA_WIDTH], NT, out_dtype=F32, name=f"{tag}_out_dxb")
    doc = _mm(dh, w_out[DN_WIDTH + SWA_WIDTH:], NT, out_dtype=F32, name=f"{tag}_out_dxc")
    res = _rows_vjp(_f_gmlp, [xg], list(p["sg"]), [doc], diff=[True], grad_dtypes=[BF16], tile=256, name=f"{tag}_gmlp_bwd")
    dxg, d_sg = res[0], res[1:]
    mix_in = [o for o, _ in swa] + [l for _, l in swa]
    d_mix = _rows_vjp(_f_swa_mix, mix_in, [], [dob], diff=[True] * 6, grad_dtypes=[F32] * 6, tile=512,
                      name=f"{tag}_swa_mix_bwd")
    acc = None
    for j, (wnd, dil) in enumerate(SWA_PATTERNS):
        acc = _swa_backward(xs, d_mix[j], d_mix[3 + j], acc, wnd, dil, name=f"{tag}_swa{j}_bwd")
    dxs = jnp.concatenate([a.astype(BF16) for a in acc], axis=1)
    res = _dn_backward(xq, xz, xba, p["dn"], s_all, doa, name=f"{tag}_dn_bwd")
    (dxq, dxz, dxba), d_dn = res[:3], res[3:]
    w_parts = _split_w_in(w_in)
    dn = None
    dws = []
    for j, (dx, w) in enumerate(zip((dxq, dxz, dxba, dxs, dxg), w_parts, strict=True)):
        dn = _mm(dx, w, NT, out_dtype=F32, res=dn, name=f"{tag}_in{j}_dx")
        dws.append(_mm(n, dx, TN, out_dtype=BF16, name=f"{tag}_in{j}_dw"))
    dws[2] = dws[2][:, :2 * DN_HEADS]
    dw_in = jnp.concatenate(dws, axis=1)
    dh_in, dgain = _rows_vjp(_f_rms, [h], [p["mix_norm"]], [dn], diff=[True], grad_dtypes=[F32], tile=512, add=[dh],
                             name=f"{tag}_norm_bwd")
    return dh_in, dgain, dw_in, dw_out, d_dn, d_sg


def _xattn_fwd(h, mem, p, w_q, w_kv, w_o, tag):
    d = h.shape[1]
    n = _rows(_f_rms, [h], [p["xa_norm"]], [(d, BF16)], tile=512, name=f"{tag}_norm")[0]
    mn = _rows(_f_rms, [mem], [p["xa_mem_norm"]], [(d, BF16)], tile=512, name=f"{tag}_mem_norm")[0]
    q = _mm(n, w_q, NN, out_dtype=BF16, name=f"{tag}_q")
    kv = _mm(mn, w_kv, NN, out_dtype=BF16, name=f"{tag}_kv")
    o = _rows(_f_xattn, [q], [kv], [(d, BF16)], tile=256, name=f"{tag}_core")[0]
    out = _mm(o, w_o, NN, out_dtype=F32, res=h, name=f"{tag}_o")
    return out, (h, n, mn, q, kv, o)


def _xattn_bwd(dh, saved, mem, p, w_q, w_kv, w_o, tag):
    h, n, mn, q, kv, o = saved
    do = _mm(dh, w_o, NT, out_dtype=BF16, name=f"{tag}_o_dx")
    dw_o = _mm(o, dh, TN, out_dtype=BF16, name=f"{tag}_o_dw")
    dq, dkv = _rows_vjp(_f_xattn, [q], [kv], [do], diff=[True], grad_dtypes=[BF16], tile=256, name=f"{tag}_core_bwd")
    dn = _mm(dq, w_q, NT, out_dtype=F32, name=f"{tag}_q_dx")
    dw_q = _mm(n, dq, TN, out_dtype=BF16, name=f"{tag}_q_dw")
    dmn = _mm(dkv, w_kv, NT, out_dtype=F32, name=f"{tag}_kv_dx")
    dw_kv = _mm(mn, dkv, TN, out_dtype=BF16, name=f"{tag}_kv_dw")
    dmem_gain = _rows_vjp(_f_rms, [mem], [p["xa_mem_norm"]], [dmn], diff=[False], grad_dtypes=[], tile=512,
                          name=f"{tag}_mem_norm_bwd")[0]
    dh_in, dgain = _rows_vjp(_f_rms, [h], [p["xa_norm"]], [dn], diff=[True], grad_dtypes=[F32], tile=512, add=[dh],
                             name=f"{tag}_norm_bwd")
    return dh_in, dgain, dmem_gain, dw_q, dw_kv, dw_o


def kernel(x, mem, ffn1_norm, ffn1_w_gate_up, ffn1_w_down, mix_norm, mix_w_in, dn_conv_w, dn_a_log, dn_dt_bias, dn_out_norm, sg_norm_gain, sg_norm_bias, sg_w_spatial, sg_b_spatial, mix_w_out, xa_norm, xa_mem_norm, xa_w_q, xa_w_kv, xa_w_o, ffn2_norm, ffn2_w_gate_up, ffn2_w_down, final_norm, loss_target, m_ffn1_norm, m_ffn1_w_gate_up, m_ffn1_w_down, m_mix_norm, m_mix_w_in, m_dn_conv_w, m_dn_a_log, m_dn_dt_bias, m_dn_out_norm, m_sg_norm_gain, m_sg_norm_bias, m_sg_w_spatial, m_sg_b_spatial, m_mix_w_out, m_xa_norm, m_xa_mem_norm, m_xa_w_q, m_xa_w_kv, m_xa_w_o, m_ffn2_norm, m_ffn2_w_gate_up, m_ffn2_w_down, m_final_norm, v_ffn1_norm, v_ffn1_w_gate_up, v_ffn1_w_down, v_mix_norm, v_mix_w_in, v_dn_conv_w, v_dn_a_log, v_dn_dt_bias, v_dn_out_norm, v_sg_norm_gain, v_sg_norm_bias, v_sg_w_spatial, v_sg_b_spatial, v_mix_w_out, v_xa_norm, v_xa_mem_norm, v_xa_w_q, v_xa_w_kv, v_xa_w_o, v_ffn2_norm, v_ffn2_w_gate_up, v_ffn2_w_down, v_final_norm):
    args = dict(locals())
    wts = {k: args[k] for k in WEIGHTS}
    mom_m = {k: args["m_" + k] for k in WEIGHTS}
    mom_v = {k: args["v_" + k] for k in WEIGHTS}
    depth = ffn1_norm.shape[0]
    h = x[0]
    mem2 = mem[0]
    target = loss_target[0]

    assert depth == 2, "core c of a chip is responsible for layer c in the weight and gradient exchanges"
    kinds = {k: "stack" if k == "mix_w_in" else ("row" if BIG_AXIS[k] == 1 else "col") for k in BIG}
    whole = _gather_weights([wts[k].astype(BF16) for k in BIG] + [dn_conv_w], [kinds[k] for k in BIG] + ["stack"],
                            name="gather_weights")
    full = dict(zip(BIG, whole[:-1], strict=True))
    full["mix_w_in"] = jnp.concatenate([full["mix_w_in"][:, j] for j in range(N_CHIPS)], axis=2)
    small = {k: wts[k] for k in SMALL}
    small["dn_conv_w"] = jnp.concatenate([whole[-1][:, j] for j in range(N_CHIPS)], axis=2)

    saved = []
    for i in range(depth):
        p = _layer_params(small, i)
        h, s1 = _ffn_fwd(h, p["ffn1_norm"], full["ffn1_w_gate_up"][i], full["ffn1_w_down"][i], f"l{i}_ffn1")
        h, s2 = _mixer_fwd(h, p, full["mix_w_in"][i], full["mix_w_out"][i], f"l{i}_mix")
        h, s3 = _xattn_fwd(h, mem2, p, full["xa_w_q"][i], full["xa_w_kv"][i], full["xa_w_o"][i], f"l{i}_xa")
        h, s4 = _ffn_fwd(h, p["ffn2_norm"], full["ffn2_w_gate_up"][i], full["ffn2_w_down"][i], f"l{i}_ffn2")
        saved.append((p, s1, s2, s3, s4))
    loss_part, dh, d_final = _loss_head(h, target, final_norm[None], name="loss_head")
    loss = lax.psum(loss_part[0, 0], ("x", "y", "c"))

    g_big = {k: [None] * depth for k in BIG}
    g_small = {k: [None] * depth for k in SMALL if k != "final_norm"}
    g_small["dn_conv_w"] = [None] * depth
    for i in reversed(range(depth)):
        p, s1, s2, s3, s4 = saved[i]
        dh, dg, dw_gu, dw_d = _ffn_bwd(dh, s4, p["ffn2_norm"], full["ffn2_w_gate_up"][i], full["ffn2_w_down"][i], f"l{i}_ffn2")
        g_small["ffn2_norm"][i], g_big["ffn2_w_gate_up"][i], g_big["ffn2_w_down"][i] = dg[0], dw_gu, dw_d
        dh, dg, dmg, dw_q, dw_kv, dw_o = _xattn_bwd(dh, s3, mem2, p, full["xa_w_q"][i], full["xa_w_kv"][i],
                                                    full["xa_w_o"][i], f"l{i}_xa")
        g_small["xa_norm"][i], g_small["xa_mem_norm"][i] = dg[0], dmg[0]
        g_big["xa_w_q"][i], g_big["xa_w_kv"][i], g_big["xa_w_o"][i] = dw_q, dw_kv, dw_o
        dh, dg, dw_in, dw_out, d_dn, d_sg = _mixer_bwd(dh, s2, p, full["mix_w_in"][i], full["mix_w_out"][i], f"l{i}_mix")
        g_small["mix_norm"][i], g_big["mix_w_in"][i], g_big["mix_w_out"][i] = dg[0], dw_in, dw_out
        g_small["dn_conv_w"][i] = d_dn[0]
        g_small["dn_a_log"][i] = d_dn[1][0, DN_HEADS:2 * DN_HEADS]
        g_small["dn_dt_bias"][i] = d_dn[2][0, DN_HEADS:2 * DN_HEADS]
        g_small["dn_out_norm"][i] = d_dn[3][0]
        g_small["sg_norm_gain"][i], g_small["sg_norm_bias"][i] = d_sg[0][0], d_sg[1][0]
        g_small["sg_w_spatial"][i], g_small["sg_b_spatial"][i] = d_sg[2], d_sg[3][:SG_GROUPS]
        dh, dg, dw_gu, dw_d = _ffn_bwd(dh, s1, p["ffn1_norm"], full["ffn1_w_gate_up"][i], full["ffn1_w_down"][i], f"l{i}_ffn1")
        g_small["ffn1_norm"][i], g_big["ffn1_w_gate_up"][i], g_big["ffn1_w_down"][i] = dg[0], dw_gu, dw_d
    grad_x = dh[None]
    g_small = {k: jnp.stack(v) for k, v in g_small.items()}
    g_small["final_norm"] = d_final[0]

    def layer_grad(k, i):
        g = g_big[k][i]
        if kinds[k] == "stack":
            n = wts[k].shape[2]
            g = jnp.stack([g[:, j * n:(j + 1) * n] for j in range(N_CHIPS)])
        return g

    core = lax.axis_index("c")
    chip = 2 * lax.axis_index("x") + lax.axis_index("y")
    pick = jnp.full((1, 1), core, jnp.int32)
    g0, g1 = [layer_grad(k, 0) for k in BIG], [layer_grad(k, 1) for k in BIG]
    got = _give_other_layer(g0, g1, name="give_other_layer")
    chip_sum = [_sum_picked(a, b, pick, g, BF16, name=f"sum_cores_{k}") for k, a, b, g in zip(BIG, g0, g1, got, strict=True)]
    recv = _scatter_quarters(chip_sum, [wts[k].shape[1:] for k in BIG], [kinds[k] for k in BIG], name="scatter_grads")

    def own_quarter(k, g):
        if kinds[k] == "stack":
            return lax.dynamic_index_in_dim(g, chip, axis=0, keepdims=False)
        axis = BIG_AXIS[k] - 1
        n = wts[k].shape[BIG_AXIS[k]]
        return lax.dynamic_slice_in_dim(g, chip * n, n, axis=axis)

    parts = [_sum_slots(r, F32, name=f"sum_chips_{k}", first=own_quarter(k, g))
             for k, r, g in zip(BIG, recv, chip_sum, strict=True)]
    others = _swap_cores(parts, name="swap_cores")
    g_fin = {k: jnp.where(core == 0, jnp.stack([p, o]), jnp.stack([o, p])) for k, p, o in zip(BIG, parts, others, strict=True)}
    small_names = list(SMALL) + ["dn_conv_w"]
    small_shapes = [g_small[k].shape for k in small_names]
    small_sum = _sum_slots(_gather_all(_pack([g_small[k] for k in small_names], F32, 64), name="gather_small"), F32,
                           name="sum_small")
    gs = dict(zip(small_names, _unpack(small_sum, small_shapes), strict=True))
    n_conv = dn_conv_w.shape[2]
    gs["dn_conv_w"] = lax.dynamic_slice_in_dim(gs["dn_conv_w"], (2 * lax.axis_index("x") + lax.axis_index("y")) * n_conv,
                                               n_conv, axis=2)

    results = {}
    for k in BIG:
        shp = wts[k].shape
        two_d = lambda a, _s=shp: a.reshape(-1, _s[-1])
        res = _adamw(two_d(wts[k]), two_d(g_fin[k]), two_d(mom_m[k]), two_d(mom_v[k]), name=f"adamw_{k}")
        results[k] = [g_fin[k]] + [r.reshape(shp) for r in res]
    sm_shapes = [wts[k].shape for k in small_names]
    pk = lambda d: _pack([d[k] for k in small_names], F32, 64)
    res = [_unpack(r, sm_shapes) for r in _adamw(pk(wts), pk(gs), pk(mom_m), pk(mom_v), name="adamw_small")]
    for i, k in enumerate(small_names):
        results[k] = [gs[k]] + [res[j][i] for j in range(3)]

    out = [loss, grad_x]
    for j in range(4):
        out += [results[k][j] for k in WEIGHTS]
    return tuple(out)
```

```python
import functools
import math

import jax
import jax.numpy as jnp
from jax import lax
from jax.experimental import pallas as pl
from jax.experimental.pallas import tpu as pltpu

F32, BF16 = jnp.float32, jnp.bfloat16
HI = lax.Precision.HIGHEST
NN, NT, TN = ((1,), (0,)), ((1,), (1,)), ((0,), (0,))

NORM_EPS = 1e-6
LANES = 128
V7X_VMEM_BYTES = 64 * 2**20
VMEM_LIMIT = V7X_VMEM_BYTES * 3 // 4
MM_TM, MM_TN, MM_TK = 1024, 1408, 2816

DN_HEADS, DN_DIM, DN_CHUNK, DN_CONV, HALO = 4, 128, 64, 4, 8
DN_GROUP = 4
INV_BLOCK = 16
DN_WIDTH = DN_HEADS * DN_DIM
SWA_HEADS, SWA_DIM, SWA_BLOCK = 4, 64, 128
SWA_WIDTH = SWA_HEADS * SWA_DIM
SWA_PATTERNS = ((128, 1), (512, 4), (2048, 16))
SG_GROUPS, SG_DIM, SG_CHUNK = 4, 64, 128
SG_WIDTH = SG_GROUPS * SG_DIM
XA_HEADS = 4
IN_SIZES = (3 * DN_WIDTH, DN_WIDTH, DN_HEADS, DN_HEADS, 3 * SWA_WIDTH, 2 * SG_WIDTH)
ADAM_LR, ADAM_B1, ADAM_B2, ADAM_EPS, ADAM_WD, ADAM_STEP = 0.001, 0.9, 0.999, 1e-08, 0.01, 10
N_CHIPS, N_DEV = 4, 8
MESH_ID = pl.DeviceIdType.MESH

BIG = ("ffn1_w_gate_up", "ffn1_w_down", "mix_w_in", "mix_w_out", "xa_w_q", "xa_w_kv", "xa_w_o",
       "ffn2_w_gate_up", "ffn2_w_down")
BIG_AXIS = {"ffn1_w_gate_up": 2, "ffn1_w_down": 1, "mix_w_in": 2, "mix_w_out": 1, "xa_w_q": 1, "xa_w_kv": 2,
            "xa_w_o": 1, "ffn2_w_gate_up": 2, "ffn2_w_down": 1}
SMALL = ("ffn1_norm", "mix_norm", "dn_a_log", "dn_dt_bias", "dn_out_norm", "sg_norm_gain", "sg_norm_bias",
         "sg_w_spatial", "sg_b_spatial", "xa_norm", "xa_mem_norm", "ffn2_norm", "final_norm")
WEIGHTS = ("ffn1_norm", "ffn1_w_gate_up", "ffn1_w_down", "mix_norm", "mix_w_in", "dn_conv_w", "dn_a_log",
           "dn_dt_bias", "dn_out_norm", "sg_norm_gain", "sg_norm_bias", "sg_w_spatial", "sg_b_spatial",
           "mix_w_out", "xa_norm", "xa_mem_norm", "xa_w_q", "xa_w_kv", "xa_w_o", "ffn2_norm", "ffn2_w_gate_up",
           "ffn2_w_down", "final_norm")


@functools.partial(jax.custom_vjp, nondiff_argnums=(2,))
def _dlo(a, b, dims):
    return lax.dot_general(a.astype(BF16), b.astype(BF16), (dims, ((), ())), preferred_element_type=F32)


def _dlo_fwd(a, b, dims):
    return _dlo(a, b, dims), (a, b)


def _dlo_bwd(dims, saved, g):
    a, b = saved
    if dims == NN:
        da, db = _dlo(g, b, NT), _dlo(a, g, TN)
    elif dims == NT:
        da, db = _dlo(g, b, NN), _dlo(g, a, TN)
    else:
        da, db = _dlo(b, g, NT), _dlo(a, g, NN)
    return da.astype(a.dtype), db.astype(b.dtype)


_dlo.defvjp(_dlo_fwd, _dlo_bwd)


def _dhi(a, b, dims):
    return lax.dot_general(a, b, (dims, ((), ())), preferred_element_type=F32, precision=HI)


def _sigmoid(x):
    return 1.0 / (1.0 + jnp.exp(-x))


def _silu(x):
    return x * _sigmoid(x)


def _softplus(x):
    return jnp.maximum(x, 0.0) + jnp.log(1.0 + jnp.exp(-jnp.abs(x)))


def _rms(x, gain):
    x = x.astype(F32)
    return x * lax.rsqrt(jnp.mean(x * x, axis=-1, keepdims=True) + NORM_EPS) * gain


def _tile(n, target, unit=LANES):
    best = None
    for t in range(unit, min(n, target) + 1, unit):
        if n % t == 0:
            best = t
    return best if best is not None else n


def _cparams(sem):
    return pltpu.CompilerParams(dimension_semantics=sem, vmem_limit_bytes=VMEM_LIMIT)


def _mm(a, b, dims, *, out_dtype, name, res=None, scale=1.0):
    if dims == NN:
        (m, k), n = a.shape, b.shape[1]
    elif dims == NT:
        (m, k), n = a.shape, b.shape[0]
    else:
        (k, m), n = a.shape, b.shape[1]
    if dims == TN:
        tm, tn, tk = _tile(m, MM_TN), _tile(n, MM_TN), _tile(k, MM_TM)
    else:
        tm, tn, tk = _tile(m, MM_TM, 8), _tile(n, MM_TN), _tile(k, MM_TK)
    nk = k // tk
    a_spec = pl.BlockSpec((tk, tm), lambda i, j, kk: (kk, i)) if dims == TN else pl.BlockSpec((tm, tk), lambda i, j, kk: (i, kk))
    b_spec = pl.BlockSpec((tn, tk), lambda i, j, kk: (j, kk)) if dims == NT else pl.BlockSpec((tk, tn), lambda i, j, kk: (kk, j))
    o_spec = pl.BlockSpec((tm, tn), lambda i, j, kk: (i, j))
    has_res = res is not None

    def finish(acc, r_ref, o_ref):
        val = acc * scale if scale != 1.0 else acc
        if has_res:
            val = r_ref[...].astype(F32) + val
        o_ref[...] = val.astype(o_ref.dtype)

    def body(*refs):
        a_ref, b_ref = refs[0], refs[1]
        r_ref = refs[2] if has_res else None
        part = lax.dot_general(a_ref[...].astype(BF16), b_ref[...].astype(BF16), (dims, ((), ())),
                               preferred_element_type=F32)
        if nk == 1:
            finish(part, r_ref, refs[-1])
            return
        o_ref, acc_ref = refs[-2], refs[-1]
        kk = pl.program_id(2)

        @pl.when(kk == 0)
        def _():
            acc_ref[...] = part

        @pl.when(jnp.logical_and(kk > 0, kk < nk - 1))
        def _():
            acc_ref[...] += part

        @pl.when(kk == nk - 1)
        def _():
            finish(acc_ref[...] + part, r_ref, o_ref)

    return pl.pallas_call(
        body, name=name, grid=(m // tm, n // tn, nk),
        in_specs=[a_spec, b_spec] + ([o_spec] if has_res else []), out_specs=o_spec,
        out_shape=jax.ShapeDtypeStruct((m, n), out_dtype),
        scratch_shapes=[pltpu.VMEM((tm, tn), F32)] if nk > 1 else [],
        compiler_params=_cparams(("parallel", "parallel", "arbitrary")),
    )(*([a, b] + ([res] if has_res else [])))


def _full_spec(p):
    nd = p.ndim
    return pl.BlockSpec(p.shape, lambda i, _nd=nd: (0,) * _nd)


def _rows(f, rows, params, outs, *, tile, name):
    t = rows[0].shape[0]
    tile = min(tile, t)
    nr, npar = len(rows), len(params)

    def body(*refs):
        vals = f(*[r[...] for r in refs[:nr + npar]])
        for o_ref, v in zip(refs[nr + npar:], vals, strict=True):
            o_ref[...] = v.astype(o_ref.dtype)

    res = pl.pallas_call(
        body, name=name, grid=(t // tile,),
        in_specs=[pl.BlockSpec((tile, r.shape[1]), lambda i: (i, 0)) for r in rows] + [_full_spec(p) for p in params],
        out_specs=[pl.BlockSpec((tile, w), lambda i: (i, 0)) for w, _ in outs],
        out_shape=[jax.ShapeDtypeStruct((t, w), d) for w, d in outs],
        compiler_params=_cparams(("parallel",)),
    )(*rows, *params)
    return tuple(res)


def _rows_vjp(f, rows, params, cts, *, diff, grad_dtypes, tile, name, add=None):
    t = rows[0].shape[0]
    tile = min(tile, t)
    nr, npar, nct = len(rows), len(params), len(cts)
    didx = [i for i, d in enumerate(diff) if d]
    add = [None] * len(didx) if add is None else add
    adds = [a for a in add if a is not None]

    def body(*refs):
        row_refs, par_refs = refs[:nr], refs[nr:nr + npar]
        ct_refs = refs[nr + npar:nr + npar + nct]
        add_refs = list(refs[nr + npar + nct:nr + npar + nct + len(adds)])
        out_refs = refs[nr + npar + nct + len(adds):]
        rv = [r[...] for r in row_refs]
        pv = [p[...].astype(F32) for p in par_refs]

        def g(*args):
            full = list(rv)
            for k, i in enumerate(didx):
                full[i] = args[k]
            return f(*full, *args[len(didx):])

        outs, pull = jax.vjp(g, *[rv[i] for i in didx], *pv)
        grads = pull(tuple(c[...].astype(o.dtype) for c, o in zip(ct_refs, outs, strict=True)))
        for k in range(len(didx)):
            val = grads[k].astype(F32)
            if add[k] is not None:
                val = val + add_refs.pop(0)[...].astype(F32)
            out_refs[k][...] = val.astype(out_refs[k].dtype)

        @pl.when(pl.program_id(0) == 0)
        def _():
            for o_ref in out_refs[len(didx):]:
                o_ref[...] = jnp.zeros_like(o_ref)

        for o_ref, gp in zip(out_refs[len(didx):], grads[len(didx):], strict=True):
            o_ref[...] += gp.astype(F32)

    row_spec = lambda a: pl.BlockSpec((tile, a.shape[1]), lambda i: (i, 0))
    res = pl.pallas_call(
        body, name=name, grid=(t // tile,),
        in_specs=[row_spec(r) for r in rows] + [_full_spec(p) for p in params] + [row_spec(c) for c in cts]
        + [row_spec(a) for a in adds],
        out_specs=[row_spec(rows[i]) for i in didx] + [_full_spec(p) for p in params],
        out_shape=[jax.ShapeDtypeStruct(rows[i].shape, d) for i, d in zip(didx, grad_dtypes, strict=True)]
        + [jax.ShapeDtypeStruct(p.shape, F32) for p in params],
        compiler_params=_cparams(("arbitrary",)),
    )(*rows, *params, *cts, *adds)
    return tuple(res)


def _sum_picked(a0, a1, pick, other, out_dtype, name):
    shape = a0.shape
    views = [a.reshape(-1, shape[-1]) for a in (a0, a1, other)]
    r, c = views[0].shape
    tile = _tile(r, max(16, (1 << 18) // c), 16)

    def body(a0_ref, a1_ref, other_ref, pick_ref, o_ref):
        mine = jnp.where(pick_ref[...] == 0, a0_ref[...].astype(F32), a1_ref[...].astype(F32))
        o_ref[...] = (mine + other_ref[...].astype(F32)).astype(o_ref.dtype)

    spec = pl.BlockSpec((tile, c), lambda i: (i, 0))
    return pl.pallas_call(
        body, name=name, grid=(r // tile,), in_specs=[spec] * 3 + [_full_spec(pick)], out_specs=spec,
        out_shape=jax.ShapeDtypeStruct((r, c), out_dtype), compiler_params=_cparams(("parallel",)),
    )(*views, pick).reshape(shape)


def _sum_slots(x, out_dtype, name, first=None):
    n, r, c = x.shape
    tile = _tile(r, max(16, (1 << 18) // c), 16)

    def body(*refs):
        acc = refs[0][...].astype(F32)
        for ref in refs[1:-1]:
            acc = acc + ref[...].astype(F32)
        refs[-1][...] = acc.astype(refs[-1].dtype)

    spec = pl.BlockSpec((tile, c), lambda i: (i, 0))
    return pl.pallas_call(
        body, name=name, grid=(r // tile,),
        in_specs=([spec] if first is not None else [])
        + [pl.BlockSpec((None, tile, c), lambda i, _s=s_: (_s, i, 0)) for s_ in range(n)],
        out_specs=spec, out_shape=jax.ShapeDtypeStruct((r, c), out_dtype), compiler_params=_cparams(("parallel",)),
    )(*(([first] if first is not None else []) + [x] * n))


def _adamw(w, g, m, v, name):
    r, c = w.shape
    tile = _tile(r, max(8, (1 << 18) // c), 8)

    def body(w_ref, g_ref, m_ref, v_ref, d_out, m_out, v_out):
        g = g_ref[...]
        mn = ADAM_B1 * m_ref[...] + (1.0 - ADAM_B1) * g
        vn = ADAM_B2 * v_ref[...] + (1.0 - ADAM_B2) * (g * g)
        m_hat = mn / (1.0 - ADAM_B1 ** ADAM_STEP)
        v_hat = vn / (1.0 - ADAM_B2 ** ADAM_STEP)
        d_out[...] = -ADAM_LR * (m_hat / (jnp.sqrt(v_hat) + ADAM_EPS) + ADAM_WD * w_ref[...])
        m_out[...] = mn
        v_out[...] = vn

    spec = pl.BlockSpec((tile, c), lambda i: (i, 0))
    return pl.pallas_call(
        body, name=name, grid=(r // tile,), in_specs=[spec] * 4, out_specs=[spec] * 3,
        out_shape=[jax.ShapeDtypeStruct((r, c), F32)] * 3, compiler_params=_cparams(("parallel",)),
    )(w, g, m, v)


def _place():
    return lax.axis_index("x"), lax.axis_index("y"), lax.axis_index("c")


def _flip(v, bit):
    return 1 - v if bit else v


_ANY = pl.BlockSpec(memory_space=pl.ANY)


def _quarter(ref, j, shape, kind):
    if kind == "row":
        return ref.at[pl.ds(j * shape[0], shape[0])]
    if kind == "col":
        return ref.at[:, pl.ds(j * shape[1], shape[1])]
    return ref.at[j]


def _whole_shape(shape, kind):
    if kind == "row":
        return (N_CHIPS * shape[0],) + tuple(shape[1:])
    if kind == "col":
        return (shape[0], N_CHIPS * shape[1]) + tuple(shape[2:])
    return (N_CHIPS,) + tuple(shape)


def _dma_sems(*counts):
    return [pltpu.SemaphoreType.DMA((n,)) for n in counts]


def _gather_weights(shards, kinds, name):
    n = len(shards)
    shapes = [s.shape[1:] for s in shards]

    def body(*refs):
        ins, outs = refs[:n], refs[n:2 * n]
        send_sems, recv_sems, pair_send, pair_recv, own_send, own_recv = refs[2 * n:]
        mx, my, mc = _place()
        me = 2 * mx + my
        started = []
        for t in range(n):
            started.append(pltpu.make_async_remote_copy(
                src_ref=ins[t].at[1 - mc], dst_ref=_quarter(outs[t].at[1 - mc], me, shapes[t], kinds[t]),
                send_sem=own_send.at[t], recv_sem=own_recv.at[t], device_id=(mx, my, 1 - mc), device_id_type=MESH_ID))
            started[-1].start()
            for k in range(1, N_CHIPS):
                started.append(pltpu.make_async_remote_copy(
                    src_ref=ins[t].at[mc], dst_ref=_quarter(outs[t].at[mc], me, shapes[t], kinds[t]),
                    send_sem=send_sems.at[3 * t + k - 1], recv_sem=recv_sems.at[3 * t + k - 1],
                    device_id=(_flip(mx, k >> 1), _flip(my, k & 1), mc), device_id_type=MESH_ID))
                started[-1].start()
        for t in range(n):
            pltpu.make_async_remote_copy(
                src_ref=ins[t].at[mc], dst_ref=_quarter(outs[t].at[mc], me, shapes[t], kinds[t]),
                send_sem=own_send.at[t], recv_sem=own_recv.at[t], device_id=(mx, my, 1 - mc),
                device_id_type=MESH_ID).wait_recv()
            for k in range(1, N_CHIPS):
                px, py = _flip(mx, k >> 1), _flip(my, k & 1)
                pltpu.make_async_remote_copy(
                    src_ref=ins[t].at[mc], dst_ref=_quarter(outs[t].at[mc], 2 * px + py, shapes[t], kinds[t]),
                    send_sem=send_sems.at[3 * t + k - 1], recv_sem=recv_sems.at[3 * t + k - 1],
                    device_id=(px, py, mc), device_id_type=MESH_ID).wait_recv()
        for t in range(n):
            started.append(pltpu.make_async_remote_copy(
                src_ref=outs[t].at[mc], dst_ref=outs[t].at[mc], send_sem=pair_send.at[t], recv_sem=pair_recv.at[t],
                device_id=(mx, my, 1 - mc), device_id_type=MESH_ID))
            started[-1].start()
        for t in range(n):
            pltpu.make_async_remote_copy(
                src_ref=outs[t].at[1 - mc], dst_ref=outs[t].at[1 - mc], send_sem=pair_send.at[t], recv_sem=pair_recv.at[t],
                device_id=(mx, my, 1 - mc), device_id_type=MESH_ID).wait_recv()
        for cp in started:
            cp.wait_send()

    return pl.pallas_call(
        body, name=name, in_specs=[_ANY] * n, out_specs=[_ANY] * n,
        out_shape=[jax.ShapeDtypeStruct((2,) + _whole_shape(sh, kd), s.dtype) for s, sh, kd in zip(shards, shapes, kinds)],
        scratch_shapes=_dma_sems(3 * n, 3 * n, n, n, n, n),
    )(*shards)


def _give_other_layer(g0, g1, name):
    n = len(g0)

    def body(*refs):
        a, b, got = refs[:n], refs[n:2 * n], refs[2 * n:3 * n]
        send_sems, recv_sems = refs[3 * n:]
        mx, my, mc = _place()
        peer = (mx, my, 1 - mc)
        for give, core in ((b, 0), (a, 1)):
            @pl.when(mc == core)
            def _():
                for t in range(n):
                    pltpu.make_async_remote_copy(src_ref=give[t], dst_ref=got[t], send_sem=send_sems.at[t],
                                                 recv_sem=recv_sems.at[t], device_id=peer, device_id_type=MESH_ID).start()
        for t in range(n):
            cp = pltpu.make_async_remote_copy(src_ref=a[t], dst_ref=got[t], send_sem=send_sems.at[t],
                                              recv_sem=recv_sems.at[t], device_id=peer, device_id_type=MESH_ID)
            cp.wait_recv()
            cp.wait_send()

    return pl.pallas_call(
        body, name=name, in_specs=[_ANY] * (2 * n), out_specs=[_ANY] * n,
        out_shape=[jax.ShapeDtypeStruct(g.shape, g.dtype) for g in g0], scratch_shapes=_dma_sems(n, n),
    )(*g0, *g1)


def _scatter_quarters(gs, shapes, kinds, name):
    n = len(gs)

    def body(*refs):
        ins, outs = refs[:n], refs[n:2 * n]
        send_sems, recv_sems = refs[2 * n:]
        mx, my, mc = _place()
        sends = []
        for t in range(n):
            for k in range(1, N_CHIPS):
                px, py = _flip(mx, k >> 1), _flip(my, k & 1)
                sends.append(pltpu.make_async_remote_copy(
                    src_ref=_quarter(ins[t], 2 * px + py, shapes[t], kinds[t]), dst_ref=outs[t].at[k - 1],
                    send_sem=send_sems.at[3 * t + k - 1], recv_sem=recv_sems.at[3 * t + k - 1],
                    device_id=(px, py, mc), device_id_type=MESH_ID))
                sends[-1].start()
        for cp in sends:
            cp.wait_recv()
        for cp in sends:
            cp.wait_send()

    return pl.pallas_call(
        body, name=name, in_specs=[_ANY] * n, out_specs=[_ANY] * n,
        out_shape=[jax.ShapeDtypeStruct((N_CHIPS - 1,) + tuple(sh), g.dtype) for g, sh in zip(gs, shapes)],
        scratch_shapes=_dma_sems(3 * n, 3 * n),
    )(*gs)


def _swap_cores(parts, name):
    n = len(parts)

    def body(*refs):
        ins, outs = refs[:n], refs[n:2 * n]
        send_sems, recv_sems = refs[2 * n:]
        mx, my, mc = _place()
        copies = [pltpu.make_async_remote_copy(src_ref=ins[t], dst_ref=outs[t], send_sem=send_sems.at[t],
                                               recv_sem=recv_sems.at[t], device_id=(mx, my, 1 - mc),
                                               device_id_type=MESH_ID) for t in range(n)]
        for cp in copies:
            cp.start()
        for cp in copies:
            cp.wait_recv()
        for cp in copies:
            cp.wait_send()

    return pl.pallas_call(
        body, name=name, in_specs=[_ANY] * n, out_specs=[_ANY] * n,
        out_shape=[jax.ShapeDtypeStruct(p.shape, p.dtype) for p in parts], scratch_shapes=_dma_sems(n, n),
    )(*parts)


def _gather_all(x, name):
    r, w = x.shape

    def body(x_ref, o_ref, send_sems, recv_sems, local_sem):
        mx, my, mc = _place()
        mine = 4 * mx + 2 * my + mc
        local = pltpu.make_async_copy(x_ref, o_ref.at[mine], local_sem)
        local.start()
        copies = []
        for k in range(1, N_DEV):
            peer = (_flip(mx, k >> 2), _flip(my, (k >> 1) & 1), _flip(mc, k & 1))
            copies.append(pltpu.make_async_remote_copy(
                src_ref=x_ref, dst_ref=o_ref.at[mine], send_sem=send_sems.at[k - 1], recv_sem=recv_sems.at[k - 1],
                device_id=peer, device_id_type=MESH_ID))
            copies[-1].start()
        for k in range(1, N_DEV):
            peer = (_flip(mx, k >> 2), _flip(my, (k >> 1) & 1), _flip(mc, k & 1))
            pltpu.make_async_remote_copy(
                src_ref=x_ref, dst_ref=o_ref.at[4 * peer[0] + 2 * peer[1] + peer[2]], send_sem=send_sems.at[k - 1],
                recv_sem=recv_sems.at[k - 1], device_id=peer, device_id_type=MESH_ID).wait_recv()
        for cp in copies:
            cp.wait_send()
        local.wait()

    return pl.pallas_call(
        body, name=name, in_specs=[_ANY], out_specs=_ANY, out_shape=jax.ShapeDtypeStruct((N_DEV, r, w), x.dtype),
        scratch_shapes=[pltpu.SemaphoreType.DMA((N_DEV - 1,)), pltpu.SemaphoreType.DMA((N_DEV - 1,)),
                        pltpu.SemaphoreType.DMA],
    )(x)


def _pack(parts, dtype, row_unit):
    flat = jnp.concatenate([p.astype(dtype).reshape(-1) for p in parts])
    unit = row_unit * LANES
    pad = (-flat.shape[0]) % unit
    if pad:
        flat = jnp.concatenate([flat, jnp.zeros((pad,), dtype)])
    return flat.reshape(-1, LANES)


def _unpack(packed, shapes):
    flat = packed.reshape(-1)
    out, off = [], 0
    for s in shapes:
        n = math.prod(s)
        out.append(flat[off:off + n].reshape(s))
        off += n
    return out


def _f_rms(x, gain):
    return (_rms(x, gain),)


def _f_swiglu(gu):
    f = gu.shape[1] // 2
    return (_silu(gu[:, :f].astype(F32)) * gu[:, f:].astype(F32),)


def _f_xattn(q, kv):
    d = q.shape[1]
    hd = d // XA_HEADS
    outs = []
    for h in range(XA_HEADS):
        qh, kh, vh = q[:, h * hd:(h + 1) * hd], kv[:, h * hd:(h + 1) * hd], kv[:, d + h * hd:d + (h + 1) * hd]
        s = _dlo(qh, kh, NT) * (hd ** -0.5)
        s = s - jnp.max(s, axis=-1, keepdims=True)
        p = jnp.exp(s)
        p = p / jnp.sum(p, axis=-1, keepdims=True)
        outs.append(_dlo(p, vh, NN))
    return (jnp.concatenate(outs, axis=1),)


def _f_gmlp(uv, gain, bias, w_sp, b_sp):
    r = uv.shape[0]
    act = jax.nn.gelu(uv.astype(F32))
    u, v = act[:, :SG_WIDTH], act[:, SG_WIDTH:]
    mu = jnp.mean(v, axis=-1, keepdims=True)
    var = jnp.mean(jnp.square(v - mu), axis=-1, keepdims=True)
    v = (v - mu) * lax.rsqrt(var + NORM_EPS) * gain + bias
    row = lax.broadcasted_iota(jnp.int32, (SG_CHUNK, SG_CHUNK), 0)
    col = lax.broadcasted_iota(jnp.int32, (SG_CHUNK, SG_CHUNK), 1)
    lane_grp = lax.broadcasted_iota(jnp.int32, (b_sp.shape[0], SG_WIDTH), 1) // SG_DIM
    grp_row = lax.broadcasted_iota(jnp.int32, (b_sp.shape[0], SG_WIDTH), 0)
    spread = jnp.where(lane_grp == grp_row, 1.0, 0.0).astype(F32)
    bias_t = _dhi(b_sp, spread, TN)
    chunks = []
    for c in range(r // SG_CHUNK):
        vc = v[c * SG_CHUNK:(c + 1) * SG_CHUNK]
        parts = []
        for g in range(SG_GROUPS):
            wg = jnp.where(row >= col, w_sp[g], 0.0)
            parts.append(_dlo(wg, vc[:, g * SG_DIM:(g + 1) * SG_DIM], NN))
        chunks.append(jnp.concatenate(parts, axis=1) + bias_t)
    mixed = jnp.concatenate(chunks, axis=0) if len(chunks) > 1 else chunks[0]
    return (u * mixed,)


def _f_swa_mix(o1, o2, o3, l1, l2, l3):
    outs = []
    for h in range(SWA_HEADS):
        ls = [l[:, h:h + 1] for l in (l1, l2, l3)]
        mx = jnp.maximum(jnp.maximum(ls[0], ls[1]), ls[2])
        es = [jnp.exp(l - mx) for l in ls]
        den = es[0] + es[1] + es[2]
        sl = slice(h * SWA_DIM, (h + 1) * SWA_DIM)
        outs.append((es[0] * o1[:, sl] + es[1] * o2[:, sl] + es[2] * o3[:, sl]) / den)
    return (jnp.concatenate(outs, axis=1),)


def _swa_block(q, kp, kc, vp, vc, first, window, dilation):
    span = window // dilation
    rows = SWA_HEADS * SWA_BLOCK
    ri = lax.broadcasted_iota(jnp.int32, (rows, 2 * SWA_BLOCK), 0)
    kj = lax.broadcasted_iota(jnp.int32, (rows, 2 * SWA_BLOCK), 1)
    head = ri // SWA_BLOCK
    rel = SWA_BLOCK + ri % SWA_BLOCK - kj
    valid = (rel >= 0) & (rel <= span) & jnp.logical_not(jnp.logical_and(first, kj < SWA_BLOCK))
    slope = jnp.exp((head + 1).astype(F32) * (-8.0 / SWA_HEADS * math.log(2.0)))
    bias = slope * (rel * dilation).astype(F32)
    kw = jnp.concatenate([kp, kc], axis=0)
    vw = jnp.concatenate([vp, vc], axis=0)
    lane_head = lax.broadcasted_iota(jnp.int32, (rows, SWA_WIDTH), 1) // SWA_DIM
    own_head = lane_head == lax.broadcasted_iota(jnp.int32, (rows, SWA_WIDTH), 0) // SWA_BLOCK
    q_rows = jnp.where(own_head, jnp.concatenate([q] * SWA_HEADS, axis=0), 0.0)
    s = _dlo(q_rows, kw, NT) * (SWA_DIM ** -0.5) - bias
    s = jnp.where(valid, s, -1e30)
    m = jnp.max(s, axis=-1, keepdims=True)
    p = jnp.exp(s - m)
    den = jnp.sum(p, axis=-1, keepdims=True)
    wide = _dlo(p, vw, NN) / den
    lse_rows = m + jnp.log(den)
    lane = lax.broadcasted_iota(jnp.int32, (SWA_BLOCK, LANES), 1)
    outs, lse = [], jnp.zeros((SWA_BLOCK, LANES), F32)
    for h in range(SWA_HEADS):
        outs.append(wide[h * SWA_BLOCK:(h + 1) * SWA_BLOCK, h * SWA_DIM:(h + 1) * SWA_DIM])
        lse = lse + jnp.where(lane == h, lse_rows[h * SWA_BLOCK:(h + 1) * SWA_BLOCK], 0.0)
    return jnp.concatenate(outs, axis=1), lse


@jax.custom_vjp
def _unit_lower_inv(lower):
    c = lower.shape[0]
    assert DN_CHUNK == 4 * INV_BLOCK and c % DN_CHUNK == 0
    row = lax.broadcasted_iota(jnp.int32, (c, c), 0)
    col = lax.broadcasted_iota(jnp.int32, (c, c), 1)
    eye = jnp.where(row == col, 1.0, 0.0).astype(F32)
    same = (row // INV_BLOCK) == (col // INV_BLOCK)
    pw = -jnp.where(same, lower, 0.0)
    d_inv = eye + pw
    for _ in range(int(math.log2(INV_BLOCK)) - 1):
        pw = _dlo(pw, pw, NN)
        d_inv = d_inv + _dlo(d_inv, pw, NN)
    n1 = _dlo(d_inv, jnp.where(same, 0.0, lower), NN)
    n2 = _dlo(n1, n1, NN)
    rough = _dlo(eye - n1 + n2 - _dlo(n1, n2, NN), d_inv, NN)
    residual = eye - _dhi(eye + lower, rough, NN)
    return rough + _dlo(rough, residual, NN)


def _unit_lower_inv_fwd(lower):
    t_inv = _unit_lower_inv(lower)
    return t_inv, t_inv


def _unit_lower_inv_bwd(t_inv, g):
    return (-_dlo(_dlo(t_inv, g, TN), t_inv, NT),)


_unit_lower_inv.defvjp(_unit_lower_inv_fwd, _unit_lower_inv_bwd)


def _dn_group(xx, z, ba, state, conv_w, a_log, dt_bias, gain):
    rows, c = z.shape[0], DN_CHUNK
    hc = DN_HEADS * c
    acc = conv_w[0:1] * xx[HALO - 3:HALO - 3 + rows]
    for j in range(1, DN_CONV):
        acc = acc + conv_w[j:j + 1] * xx[HALO - 3 + j:HALO - 3 + j + rows]
    qkv = _silu(acc)
    beta_all = _sigmoid(ba)
    g_all = -jnp.exp(a_log) * _softplus(ba + dt_bias)
    row = lax.broadcasted_iota(jnp.int32, (hc, hc), 0)
    col = lax.broadcasted_iota(jnp.int32, (hc, hc), 1)
    same_head = (row // c) == (col // c)
    incl, strict = same_head & (row >= col), same_head & (row > col)
    tri = jnp.where(incl[:c, :c], 1.0, 0.0).astype(F32)

    def stack(piece):
        return jnp.concatenate([piece(h) for h in range(DN_HEADS)], axis=0)

    local = []
    for ci in range(rows // c):
        r0 = ci * c
        gc_all = _dhi(tri, g_all[r0:r0 + c], NN)
        gc_t = gc_all.T
        q = stack(lambda h: qkv[r0:r0 + c, h * DN_DIM:(h + 1) * DN_DIM])
        k = stack(lambda h: qkv[r0:r0 + c, DN_WIDTH + h * DN_DIM:DN_WIDTH + (h + 1) * DN_DIM])
        v = stack(lambda h: qkv[r0:r0 + c, 2 * DN_WIDTH + h * DN_DIM:2 * DN_WIDTH + (h + 1) * DN_DIM])
        q = q * lax.rsqrt(jnp.sum(q * q, axis=-1, keepdims=True) + NORM_EPS) * (DN_DIM ** -0.5)
        k = k * lax.rsqrt(jnp.sum(k * k, axis=-1, keepdims=True) + NORM_EPS)
        beta = stack(lambda h: beta_all[r0:r0 + c, h:h + 1])
        gc = stack(lambda h: gc_all[:, DN_HEADS + h:DN_HEADS + h + 1])
        g_last = stack(lambda h: jnp.broadcast_to(gc_all[c - 1:c, DN_HEADS + h:DN_HEADS + h + 1], (c, 1)))
        gc_row = jnp.concatenate([gc_t[DN_HEADS + h:DN_HEADS + h + 1, :] for h in range(DN_HEADS)], axis=1)
        decay = jnp.where(incl, jnp.exp(jnp.where(incl, gc - gc_row, 0.0)), 0.0)
        kb = k * beta
        t_inv = _unit_lower_inv(jnp.where(strict, _dlo(kb, k, NT) * decay, 0.0))
        e_gc = jnp.exp(gc)
        u = _dlo(t_inv, v * beta, NN)
        w = _dlo(t_inv, kb * e_gc, NN)
        a_qk = jnp.where(incl, _dlo(q, k, NT) * decay, 0.0)
        e_last = jnp.concatenate([jnp.broadcast_to(jnp.exp(gc_all[c - 1:c, DN_HEADS + h:DN_HEADS + h + 1]), (1, DN_DIM))
                                  for h in range(DN_HEADS)], axis=1)
        local.append((jnp.concatenate([w, q * e_gc], axis=0), k * jnp.exp(g_last - gc), u, a_qk, e_last))
    own = (lax.broadcasted_iota(jnp.int32, (hc, DN_WIDTH), 0) // c) == (lax.broadcasted_iota(jnp.int32, (hc, DN_WIDTH), 1) // DN_DIM)

    def own_blocks(m):
        return stack(lambda h: m[h * c:(h + 1) * c, h * DN_DIM:(h + 1) * DN_DIM])

    s = state
    out_rows = []
    for ci in range(rows // c):
        r0 = ci * c
        wq, k_tail, u, a_qk, e_last = local[ci]
        through = _dlo(wq, s, NN)
        v_new = u - own_blocks(through[:hc])
        o = own_blocks(through[hc:]) + _dlo(a_qk, v_new, NN)
        v_wide = jnp.where(own, jnp.concatenate([v_new] * DN_HEADS, axis=1), 0.0)
        s = s * e_last + _dlo(k_tail, v_wide, TN)
        o = o * lax.rsqrt(jnp.mean(o * o, axis=-1, keepdims=True) + NORM_EPS) * gain
        o = jnp.concatenate([o[h * c:(h + 1) * c] for h in range(DN_HEADS)], axis=1)
        out_rows.append(o * _silu(z[r0:r0 + c]))
    out = jnp.concatenate(out_rows, axis=0) if len(out_rows) > 1 else out_rows[0]
    return out, s


def _dn_specs(n_of, rows):
    return [pl.BlockSpec((rows, 3 * DN_WIDTH), lambda i: (n_of(i), 0)),
            pl.BlockSpec((HALO, 3 * DN_WIDTH), lambda i: (jnp.maximum(n_of(i) * (rows // HALO) - 1, 0), 0)),
            pl.BlockSpec((rows, DN_WIDTH), lambda i: (n_of(i), 0)),
            pl.BlockSpec((rows, LANES), lambda i: (n_of(i), 0))]


def _dn_forward(xq, xz, xba, params, name):
    t = xq.shape[0]
    rows = min(DN_GROUP * DN_CHUNK, t)
    n_groups = t // rows

    def body(x_ref, halo_ref, z_ref, ba_ref, cw_ref, al_ref, dt_ref, gn_ref, o_ref, s_all_ref, s_ref):
        n = pl.program_id(0)

        @pl.when(n == 0)
        def _():
            s_ref[...] = jnp.zeros_like(s_ref)

        halo = jnp.where(n > 0, halo_ref[...], 0.0)
        xx = jnp.concatenate([halo, x_ref[...]], axis=0)
        s_all_ref[0] = s_ref[...]
        o, s_new = _dn_group(xx, z_ref[...], ba_ref[...], s_ref[...], cw_ref[...], al_ref[...], dt_ref[...], gn_ref[...])
        o_ref[...] = o.astype(o_ref.dtype)
        s_ref[...] = s_new

    return pl.pallas_call(
        body, name=name, grid=(n_groups,),
        in_specs=_dn_specs(lambda i: i, rows) + [_full_spec(p) for p in params],
        out_specs=[pl.BlockSpec((rows, DN_WIDTH), lambda i: (i, 0)),
                   pl.BlockSpec((1, DN_DIM, DN_WIDTH), lambda i: (i, 0, 0))],
        out_shape=[jax.ShapeDtypeStruct((t, DN_WIDTH), BF16),
                   jax.ShapeDtypeStruct((n_groups, DN_DIM, DN_WIDTH), F32)],
        scratch_shapes=[pltpu.VMEM((DN_DIM, DN_WIDTH), F32)],
        compiler_params=_cparams(("arbitrary",)),
    )(xq, xq, xz, xba, *params)


def _dn_backward(xq, xz, xba, params, s_all, d_out, name):
    t = xq.shape[0]
    rows = min(DN_GROUP * DN_CHUNK, t)
    n_groups = t // rows
    rev = lambda i: n_groups - 1 - i

    def body(x_ref, halo_ref, z_ref, ba_ref, cw_ref, al_ref, dt_ref, gn_ref, s_ref, do_ref,
             dx_ref, dz_ref, dba_ref, dcw_ref, dal_ref, ddt_ref, dgn_ref, ds_ref, dhalo_ref):
        i = pl.program_id(0)
        n = n_groups - 1 - i

        @pl.when(i == 0)
        def _():
            ds_ref[...] = jnp.zeros_like(ds_ref)
            dhalo_ref[...] = jnp.zeros_like(dhalo_ref)
            for r in (dcw_ref, dal_ref, ddt_ref, dgn_ref):
                r[...] = jnp.zeros_like(r)

        halo = jnp.where(n > 0, halo_ref[...], 0.0)
        xx = jnp.concatenate([halo, x_ref[...]], axis=0)
        _, pull = jax.vjp(_dn_group, xx, z_ref[...], ba_ref[...], s_ref[0], cw_ref[...], al_ref[...], dt_ref[...],
                          gn_ref[...])
        dxx, dz, dba, ds, dcw, dal, ddt, dgn = pull((do_ref[...].astype(F32), ds_ref[...]))
        dx_ref[...] = jnp.concatenate([dxx[HALO:rows], dxx[rows:] + dhalo_ref[...]], axis=0).astype(dx_ref.dtype)
        dhalo_ref[...] = dxx[:HALO]
        dz_ref[...] = dz.astype(dz_ref.dtype)
        dba_ref[...] = dba.astype(dba_ref.dtype)
        ds_ref[...] = ds
        dcw_ref[...] += dcw
        dal_ref[...] += dal
        ddt_ref[...] += ddt
        dgn_ref[...] += dgn

    return pl.pallas_call(
        body, name=name, grid=(n_groups,),
        in_specs=_dn_specs(rev, rows) + [_full_spec(p) for p in params]
        + [pl.BlockSpec((1, DN_DIM, DN_WIDTH), lambda i: (rev(i), 0, 0)),
           pl.BlockSpec((rows, DN_WIDTH), lambda i: (rev(i), 0))],
        out_specs=[pl.BlockSpec((rows, 3 * DN_WIDTH), lambda i: (rev(i), 0)),
                   pl.BlockSpec((rows, DN_WIDTH), lambda i: (rev(i), 0)),
                   pl.BlockSpec((rows, LANES), lambda i: (rev(i), 0))] + [_full_spec(p) for p in params],
        out_shape=[jax.ShapeDtypeStruct(xq.shape, BF16), jax.ShapeDtypeStruct(xz.shape, BF16),
                   jax.ShapeDtypeStruct(xba.shape, BF16)] + [jax.ShapeDtypeStruct(p.shape, F32) for p in params],
        scratch_shapes=[pltpu.VMEM((DN_DIM, DN_WIDTH), F32), pltpu.VMEM((HALO, 3 * DN_WIDTH), F32)],
        compiler_params=_cparams(("arbitrary",)),
    )(xq, xq, xz, xba, *params, s_all, d_out)


def _swa_forward(xs, window, dilation, name):
    t = xs.shape[0]
    d, l = dilation, t // dilation
    nb = l // SWA_BLOCK
    view = xs.reshape(l, d * 3 * SWA_WIDTH)
    blk = (SWA_BLOCK, SWA_WIDTH)

    def body(q_ref, kp_ref, kc_ref, vp_ref, vc_ref, o_ref, l_ref):
        blocks = [r[...].astype(F32) for r in (q_ref, kp_ref, kc_ref, vp_ref, vc_ref)]
        o, lse = _swa_block(*blocks, pl.program_id(1) == 0, window, dilation)
        o_ref[...] = o
        l_ref[...] = lse

    prev = lambda n: jnp.maximum(n - 1, 0)
    o, lse = pl.pallas_call(
        body, name=name, grid=(d, nb),
        in_specs=[pl.BlockSpec(blk, lambda r, n: (n, 3 * r)), pl.BlockSpec(blk, lambda r, n: (prev(n), 3 * r + 1)),
                  pl.BlockSpec(blk, lambda r, n: (n, 3 * r + 1)), pl.BlockSpec(blk, lambda r, n: (prev(n), 3 * r + 2)),
                  pl.BlockSpec(blk, lambda r, n: (n, 3 * r + 2))],
        out_specs=[pl.BlockSpec(blk, lambda r, n: (n, r)), pl.BlockSpec((SWA_BLOCK, LANES), lambda r, n: (n, r))],
        out_shape=[jax.ShapeDtypeStruct((l, d * SWA_WIDTH), F32), jax.ShapeDtypeStruct((l, d * LANES), F32)],
        compiler_params=_cparams(("parallel", "parallel")),
    )(view, view, view, view, view)
    return o.reshape(t, SWA_WIDTH), lse.reshape(t, LANES)


def _swa_backward(xs, d_o, d_lse, acc, window, dilation, name):
    t = xs.shape[0]
    d, l = dilation, t // dilation
    nb = l // SWA_BLOCK
    view = xs.reshape(l, d * 3 * SWA_WIDTH)
    blk = (SWA_BLOCK, SWA_WIDTH)
    has_acc = acc is not None

    def body(*refs):
        q_ref, kp_ref, kc_ref, vp_ref, vc_ref, do_ref, dl_ref = refs[:7]
        acc_refs = refs[7:10] if has_acc else None
        dq_ref, dk_ref, dv_ref, ck_ref, cv_ref = refs[-5:]
        i = pl.program_id(1)
        n = nb - 1 - i

        @pl.when(i == 0)
        def _():
            ck_ref[...] = jnp.zeros_like(ck_ref)
            cv_ref[...] = jnp.zeros_like(cv_ref)

        f = functools.partial(_swa_block, first=n == 0, window=window, dilation=dilation)
        _, pull = jax.vjp(f, *[r[...].astype(F32) for r in (q_ref, kp_ref, kc_ref, vp_ref, vc_ref)])
        dq, dkp, dkc, dvp, dvc = pull((do_ref[...], dl_ref[...]))
        dk = dkc + ck_ref[...]
        dv = dvc + cv_ref[...]
        if has_acc:
            dq, dk, dv = dq + acc_refs[0][...], dk + acc_refs[1][...], dv + acc_refs[2][...]
        dq_ref[...] = dq
        dk_ref[...] = dk
        dv_ref[...] = dv
        ck_ref[...] = dkp
        cv_ref[...] = dvp

    cur = lambda i: nb - 1 - i
    prev = lambda i: jnp.maximum(nb - 2 - i, 0)
    own = pl.BlockSpec(blk, lambda r, i: (cur(i), r))
    accs = [a.reshape(l, d * SWA_WIDTH) for a in acc] if has_acc else []
    outs = pl.pallas_call(
        body, name=name, grid=(d, nb),
        in_specs=[pl.BlockSpec(blk, lambda r, i: (cur(i), 3 * r)), pl.BlockSpec(blk, lambda r, i: (prev(i), 3 * r + 1)),
                  pl.BlockSpec(blk, lambda r, i: (cur(i), 3 * r + 1)), pl.BlockSpec(blk, lambda r, i: (prev(i), 3 * r + 2)),
                  pl.BlockSpec(blk, lambda r, i: (cur(i), 3 * r + 2)), own,
                  pl.BlockSpec((SWA_BLOCK, LANES), lambda r, i: (cur(i), r))] + [own] * len(accs),
        out_specs=[own] * 3, out_shape=[jax.ShapeDtypeStruct((l, d * SWA_WIDTH), F32)] * 3,
        scratch_shapes=[pltpu.VMEM(blk, F32), pltpu.VMEM(blk, F32)],
        compiler_params=_cparams(("parallel", "arbitrary")),
    )(view, view, view, view, view, d_o.reshape(l, d * SWA_WIDTH), d_lse.reshape(l, d * LANES), *accs)
    return tuple(o.reshape(t, SWA_WIDTH) for o in outs)


def _loss_head(h, target, gain, name, tile=256):
    t, d = h.shape
    tile = min(tile, t)

    def body(h_ref, t_ref, g_ref, loss_ref, dh_ref, dg_ref):
        def f(hv, gv):
            err = _rms(hv, gv) - t_ref[...]
            return 0.5 * jnp.sum(jnp.mean(err * err, axis=-1, keepdims=True), axis=0, keepdims=True)

        val, pull = jax.vjp(f, h_ref[...], g_ref[...])
        dh, dg = pull(jnp.ones((1, 1), F32))
        dh_ref[...] = dh

        @pl.when(pl.program_id(0) == 0)
        def _():
            loss_ref[...] = jnp.zeros_like(loss_ref)
            dg_ref[...] = jnp.zeros_like(dg_ref)

        loss_ref[...] += jnp.broadcast_to(val, loss_ref.shape)
        dg_ref[...] += dg

    return pl.pallas_call(
        body, name=name, grid=(t // tile,),
        in_specs=[pl.BlockSpec((tile, d), lambda i: (i, 0)), pl.BlockSpec((tile, d), lambda i: (i, 0)), _full_spec(gain)],
        out_specs=[pl.BlockSpec((1, LANES), lambda i: (0, 0)), pl.BlockSpec((tile, d), lambda i: (i, 0)), _full_spec(gain)],
        out_shape=[jax.ShapeDtypeStruct((1, LANES), F32), jax.ShapeDtypeStruct((t, d), F32),
                   jax.ShapeDtypeStruct(gain.shape, F32)],
        compiler_params=_cparams(("arbitrary",)),
    )(h, target, gain)


def _split_w_in(w_in):
    cuts = [0]
    for s in IN_SIZES:
        cuts.append(cuts[-1] + s)
    qkv, z = w_in[:, cuts[0]:cuts[1]], w_in[:, cuts[1]:cuts[2]]
    ba = jnp.pad(w_in[:, cuts[2]:cuts[4]], ((0, 0), (0, LANES - 2 * DN_HEADS)))
    return qkv, z, ba, w_in[:, cuts[4]:cuts[5]], w_in[:, cuts[5]:cuts[6]]


def _lane_pad(v, offset):
    return jnp.pad(v.reshape(1, -1), ((0, 0), (offset, LANES - offset - v.shape[0])))


def _layer_params(sm, i):
    return dict(
        ffn1_norm=sm["ffn1_norm"][i][None], mix_norm=sm["mix_norm"][i][None], xa_norm=sm["xa_norm"][i][None],
        xa_mem_norm=sm["xa_mem_norm"][i][None], ffn2_norm=sm["ffn2_norm"][i][None],
        dn=(sm["dn_conv_w"][i], _lane_pad(sm["dn_a_log"][i], DN_HEADS), _lane_pad(sm["dn_dt_bias"][i], DN_HEADS),
            sm["dn_out_norm"][i][None]),
        sg=(sm["sg_norm_gain"][i][None], sm["sg_norm_bias"][i][None], sm["sg_w_spatial"][i],
            jnp.pad(sm["sg_b_spatial"][i], ((0, 8 - SG_GROUPS), (0, 0)))),
    )


def _ffn_fwd(h, gain, w_gu, w_d, tag):
    n = _rows(_f_rms, [h], [gain], [(h.shape[1], BF16)], tile=512, name=f"{tag}_norm")[0]
    gu = _mm(n, w_gu, NN, out_dtype=BF16, name=f"{tag}_gate_up")
    a = _rows(_f_swiglu, [gu], [], [(gu.shape[1] // 2, BF16)], tile=256, name=f"{tag}_act")[0]
    out = _mm(a, w_d, NN, out_dtype=F32, res=h, scale=0.5, name=f"{tag}_down")
    return out, (h, n, gu, a)


def _ffn_bwd(dh, saved, gain, w_gu, w_d, tag):
    h, n, gu, a = saved
    da = _mm(dh, w_d, NT, out_dtype=F32, scale=0.5, name=f"{tag}_down_dx")
    dw_d = _mm(a, dh, TN, out_dtype=BF16, scale=0.5, name=f"{tag}_down_dw")
    dgu = _rows_vjp(_f_swiglu, [gu], [], [da], diff=[True], grad_dtypes=[BF16], tile=256, name=f"{tag}_act_bwd")[0]
    dn = _mm(dgu, w_gu, NT, out_dtype=F32, name=f"{tag}_gate_up_dx")
    dw_gu = _mm(n, dgu, TN, out_dtype=BF16, name=f"{tag}_gate_up_dw")
    dh_in, dgain = _rows_vjp(_f_rms, [h], [gain], [dn], diff=[True], grad_dtypes=[F32], tile=512, add=[dh],
                             name=f"{tag}_norm_bwd")
    return dh_in, dgain, dw_gu, dw_d


def _mixer_fwd(h, p, w_in, w_out, tag):
    d = h.shape[1]
    n = _rows(_f_rms, [h], [p["mix_norm"]], [(d, BF16)], tile=512, name=f"{tag}_norm")[0]
    w_parts = _split_w_in(w_in)
    xq, xz, xba, xs, xg = (_mm(n, w, NN, out_dtype=BF16 if j == 3 else F32, name=f"{tag}_in{j}") for j, w in enumerate(w_parts))
    oa, s_all = _dn_forward(xq, xz, xba, p["dn"], name=f"{tag}_dn")
    swa = [_swa_forward(xs, wnd, dil, name=f"{tag}_swa{j}") for j, (wnd, dil) in enumerate(SWA_PATTERNS)]
    ob = _rows(_f_swa_mix, [o for o, _ in swa] + [l for _, l in swa], [], [(SWA_WIDTH, BF16)], tile=512,
               name=f"{tag}_swa_mix")[0]
    oc = _rows(_f_gmlp, [xg], list(p["sg"]), [(SG_WIDTH, BF16)], tile=256, name=f"{tag}_gmlp")[0]
    merged = jnp.concatenate([oa, ob, oc], axis=1)
    out = _mm(merged, w_out, NN, out_dtype=F32, res=h, name=f"{tag}_out")
    return out, (h, n, xq, xz, xba, xs, xg, s_all, swa, merged)


def _mixer_bwd(dh, saved, p, w_in, w_out, tag):
    h, n, xq, xz, xba, xs, xg, s_all, swa, merged = saved
    dw_out = _mm(merged, dh, TN, out_dtype=BF16, name=f"{tag}_out_dw")
    doa = _mm(dh, w_out[:DN_WIDTH], NT, out_dtype=F32, name=f"{tag}_out_dxa")
    dob = _mm(dh, w_out[DN_WIDTH:DN_WIDTH + SWA_WIDTH], NT, out_dtype=F32, name=f"{tag}_out_dxb")
    doc = _mm(dh, w_out[DN_WIDTH + SWA_WIDTH:], NT, out_dtype=F32, name=f"{tag}_out_dxc")
    res = _rows_vjp(_f_gmlp, [xg], list(p["sg"]), [doc], diff=[True], grad_dtypes=[BF16], tile=256, name=f"{tag}_gmlp_bwd")
    dxg, d_sg = res[0], res[1:]
    mix_in = [o for o, _ in swa] + [l for _, l in swa]
    d_mix = _rows_vjp(_f_swa_mix, mix_in, [], [dob], diff=[True] * 6, grad_dtypes=[F32] * 6, tile=512,
                      name=f"{tag}_swa_mix_bwd")
    acc = None
    for j, (wnd, dil) in enumerate(SWA_PATTERNS):
        acc = _swa_backward(xs, d_mix[j], d_mix[3 + j], acc, wnd, dil, name=f"{tag}_swa{j}_bwd")
    dxs = jnp.concatenate([a.astype(BF16) for a in acc], axis=1)
    res = _dn_backward(xq, xz, xba, p["dn"], s_all, doa, name=f"{tag}_dn_bwd")
    (dxq, dxz, dxba), d_dn = res[:3], res[3:]
    w_parts = _split_w_in(w_in)
    dn = None
    dws = []
    for j, (dx, w) in enumerate(zip((dxq, dxz, dxba, dxs, dxg), w_parts, strict=True)):
        dn = _mm(dx, w, NT, out_dtype=F32, res=dn, name=f"{tag}_in{j}_dx")
        dws.append(_mm(n, dx, TN, out_dtype=BF16, name=f"{tag}_in{j}_dw"))
    dws[2] = dws[2][:, :2 * DN_HEADS]
    dw_in = jnp.concatenate(dws, axis=1)
    dh_in, dgain = _rows_vjp(_f_rms, [h], [p["mix_norm"]], [dn], diff=[True], grad_dtypes=[F32], tile=512, add=[dh],
                             name=f"{tag}_norm_bwd")
    return dh_in, dgain, dw_in, dw_out, d_dn, d_sg


def _xattn_fwd(h, mem, p, w_q, w_kv, w_o, tag):
    d = h.shape[1]
    n = _rows(_f_rms, [h], [p["xa_norm"]], [(d, BF16)], tile=512, name=f"{tag}_norm")[0]
    mn = _rows(_f_rms, [mem], [p["xa_mem_norm"]], [(d, BF16)], tile=512, name=f"{tag}_mem_norm")[0]
    q = _mm(n, w_q, NN, out_dtype=BF16, name=f"{tag}_q")
    kv = _mm(mn, w_kv, NN, out_dtype=BF16, name=f"{tag}_kv")
    o = _rows(_f_xattn, [q], [kv], [(d, BF16)], tile=256, name=f"{tag}_core")[0]
    out = _mm(o, w_o, NN, out_dtype=F32, res=h, name=f"{tag}_o")
    return out, (h, n, mn, q, kv, o)


def _xattn_bwd(dh, saved, mem, p, w_q, w_kv, w_o, tag):
    h, n, mn, q, kv, o = saved
    do = _mm(dh, w_o, NT, out_dtype=BF16, name=f"{tag}_o_dx")
    dw_o = _mm(o, dh, TN, out_dtype=BF16, name=f"{tag}_o_dw")
    dq, dkv = _rows_vjp(_f_xattn, [q], [kv], [do], diff=[True], grad_dtypes=[BF16], tile=256, name=f"{tag}_core_bwd")
    dn = _mm(dq, w_q, NT, out_dtype=F32, name=f"{tag}_q_dx")
    dw_q = _mm(n, dq, TN, out_dtype=BF16, name=f"{tag}_q_dw")
    dmn = _mm(dkv, w_kv, NT, out_dtype=F32, name=f"{tag}_kv_dx")
    dw_kv = _mm(mn, dkv, TN, out_dtype=BF16, name=f"{tag}_kv_dw")
    dmem_gain = _rows_vjp(_f_rms, [mem], [p["xa_mem_norm"]], [dmn], diff=[False], grad_dtypes=[], tile=512,
                          name=f"{tag}_mem_norm_bwd")[0]
    dh_in, dgain = _rows_vjp(_f_rms, [h], [p["xa_norm"]], [dn], diff=[True], grad_dtypes=[F32], tile=512, add=[dh],
                             name=f"{tag}_norm_bwd")
    return dh_in, dgain, dmem_gain, dw_q, dw_kv, dw_o


def kernel(x, mem, ffn1_norm, ffn1_w_gate_up, ffn1_w_down, mix_norm, mix_w_in, dn_conv_w, dn_a_log, dn_dt_bias, dn_out_norm, sg_norm_gain, sg_norm_bias, sg_w_spatial, sg_b_spatial, mix_w_out, xa_norm, xa_mem_norm, xa_w_q, xa_w_kv, xa_w_o, ffn2_norm, ffn2_w_gate_up, ffn2_w_down, final_norm, loss_target, m_ffn1_norm, m_ffn1_w_gate_up, m_ffn1_w_down, m_mix_norm, m_mix_w_in, m_dn_conv_w, m_dn_a_log, m_dn_dt_bias, m_dn_out_norm, m_sg_norm_gain, m_sg_norm_bias, m_sg_w_spatial, m_sg_b_spatial, m_mix_w_out, m_xa_norm, m_xa_mem_norm, m_xa_w_q, m_xa_w_kv, m_xa_w_o, m_ffn2_norm, m_ffn2_w_gate_up, m_ffn2_w_down, m_final_norm, v_ffn1_norm, v_ffn1_w_gate_up, v_ffn1_w_down, v_mix_norm, v_mix_w_in, v_dn_conv_w, v_dn_a_log, v_dn_dt_bias, v_dn_out_norm, v_sg_norm_gain, v_sg_norm_bias, v_sg_w_spatial, v_sg_b_spatial, v_mix_w_out, v_xa_norm, v_xa_mem_norm, v_xa_w_q, v_xa_w_kv, v_xa_w_o, v_ffn2_norm, v_ffn2_w_gate_up, v_ffn2_w_down, v_final_norm):
    args = dict(locals())
    wts = {k: args[k] for k in WEIGHTS}
    mom_m = {k: args["m_" + k] for k in WEIGHTS}
    mom_v = {k: args["v_" + k] for k in WEIGHTS}
    depth = ffn1_norm.shape[0]
    h = x[0]
    mem2 = mem[0]
    target = loss_target[0]

    assert depth == 2, "core c of a chip is responsible for layer c in the weight and gradient exchanges"
    kinds = {k: "stack" if k == "mix_w_in" else ("row" if BIG_AXIS[k] == 1 else "col") for k in BIG}
    whole = _gather_weights([wts[k].astype(BF16) for k in BIG] + [dn_conv_w], [kinds[k] for k in BIG] + ["stack"],
                            name="gather_weights")
    full = dict(zip(BIG, whole[:-1], strict=True))
    full["mix_w_in"] = jnp.concatenate([full["mix_w_in"][:, j] for j in range(N_CHIPS)], axis=2)
    small = {k: wts[k] for k in SMALL}
    small["dn_conv_w"] = jnp.concatenate([whole[-1][:, j] for j in range(N_CHIPS)], axis=2)

    saved = []
    for i in range(depth):
        p = _layer_params(small, i)
        h, s1 = _ffn_fwd(h, p["ffn1_norm"], full["ffn1_w_gate_up"][i], full["ffn1_w_down"][i], f"l{i}_ffn1")
        h, s2 = _mixer_fwd(h, p, full["mix_w_in"][i], full["mix_w_out"][i], f"l{i}_mix")
        h, s3 = _xattn_fwd(h, mem2, p, full["xa_w_q"][i], full["xa_w_kv"][i], full["xa_w_o"][i], f"l{i}_xa")
        h, s4 = _ffn_fwd(h, p["ffn2_norm"], full["ffn2_w_gate_up"][i], full["ffn2_w_down"][i], f"l{i}_ffn2")
        saved.append((p, s1, s2, s3, s4))
    loss_part, dh, d_final = _loss_head(h, target, final_norm[None], name="loss_head")
    loss = lax.psum(loss_part[0, 0], ("x", "y", "c"))

    g_big = {k: [None] * depth for k in BIG}
    g_small = {k: [None] * depth for k in SMALL if k != "final_norm"}
    g_small["dn_conv_w"] = [None] * depth
    for i in reversed(range(depth)):
        p, s1, s2, s3, s4 = saved[i]
        dh, dg, dw_gu, dw_d = _ffn_bwd(dh, s4, p["ffn2_norm"], full["ffn2_w_gate_up"][i], full["ffn2_w_down"][i], f"l{i}_ffn2")
        g_small["ffn2_norm"][i], g_big["ffn2_w_gate_up"][i], g_big["ffn2_w_down"][i] = dg[0], dw_gu, dw_d
        dh, dg, dmg, dw_q, dw_kv, dw_o = _xattn_bwd(dh, s3, mem2, p, full["xa_w_q"][i], full["xa_w_kv"][i],
                                                    full["xa_w_o"][i], f"l{i}_xa")
        g_small["xa_norm"][i], g_small["xa_mem_norm"][i] = dg[0], dmg[0]
        g_big["xa_w_q"][i], g_big["xa_w_kv"][i], g_big["xa_w_o"][i] = dw_q, dw_kv, dw_o
        dh, dg, dw_in, dw_out, d_dn, d_sg = _mixer_bwd(dh, s2, p, full["mix_w_in"][i], full["mix_w_out"][i], f"l{i}_mix")
        g_small["mix_norm"][i], g_big["mix_w_in"][i], g_big["mix_w_out"][i] = dg[0], dw_in, dw_out
        g_small["dn_conv_w"][i] = d_dn[0]
        g_small["dn_a_log"][i] = d_dn[1][0, DN_HEADS:2 * DN_HEADS]
        g_small["dn_dt_bias"][i] = d_dn[2][0, DN_HEADS:2 * DN_HEADS]
        g_small["dn_out_norm"][i] = d_dn[3][0]
        g_small["sg_norm_gain"][i], g_small["sg_norm_bias"][i] = d_sg[0][0], d_sg[1][0]
        g_small["sg_w_spatial"][i], g_small["sg_b_spatial"][i] = d_sg[2], d_sg[3][:SG_GROUPS]
        dh, dg, dw_gu, dw_d = _ffn_bwd(dh, s1, p["ffn1_norm"], full["ffn1_w_gate_up"][i], full["ffn1_w_down"][i], f"l{i}_ffn1")
        g_small["ffn1_norm"][i], g_big["ffn1_w_gate_up"][i], g_big["ffn1_w_down"][i] = dg[0], dw_gu, dw_d
    grad_x = dh[None]
    g_small = {k: jnp.stack(v) for k, v in g_small.items()}
    g_small["final_norm"] = d_final[0]

    def layer_grad(k, i):
        g = g_big[k][i]
        if kinds[k] == "stack":
            n = wts[k].shape[2]
            g = jnp.stack([g[:, j * n:(j + 1) * n] for j in range(N_CHIPS)])
        return g

    core = lax.axis_index("c")
    chip = 2 * lax.axis_index("x") + lax.axis_index("y")
    pick = jnp.full((1, 1), core, jnp.int32)
    g0, g1 = [layer_grad(k, 0) for k in BIG], [layer_grad(k, 1) for k in BIG]
    got = _give_other_layer(g0, g1, name="give_other_layer")
    chip_sum = [_sum_picked(a, b, pick, g, BF16, name=f"sum_cores_{k}") for k, a, b, g in zip(BIG, g0, g1, got, strict=True)]
    recv = _scatter_quarters(chip_sum, [wts[k].shape[1:] for k in BIG], [kinds[k] for k in BIG], name="scatter_grads")

    def own_quarter(k, g):
        if kinds[k] == "stack":
            return lax.dynamic_index_in_dim(g, chip, axis=0, keepdims=False)
        axis = BIG_AXIS[k] - 1
        n = wts[k].shape[BIG_AXIS[k]]
        return lax.dynamic_slice_in_dim(g, chip * n, n, axis=axis)

    parts = [_sum_slots(r, F32, name=f"sum_chips_{k}", first=own_quarter(k, g))
             for k, r, g in zip(BIG, recv, chip_sum, strict=True)]
    others = _swap_cores(parts, name="swap_cores")
    g_fin = {k: jnp.where(core == 0, jnp.stack([p, o]), jnp.stack([o, p])) for k, p, o in zip(BIG, parts, others, strict=True)}
    small_names = list(SMALL) + ["dn_conv_w"]
    small_shapes = [g_small[k].shape for k in small_names]
    small_sum = _sum_slots(_gather_all(_pack([g_small[k] for k in small_names], F32, 64), name="gather_small"), F32,
                           name="sum_small")
    gs = dict(zip(small_names, _unpack(small_sum, small_shapes), strict=True))
    n_conv = dn_conv_w.shape[2]
    gs["dn_conv_w"] = lax.dynamic_slice_in_dim(gs["dn_conv_w"], (2 * lax.axis_index("x") + lax.axis_index("y")) * n_conv,
                                               n_conv, axis=2)

    results = {}
    for k in BIG:
        shp = wts[k].shape
        two_d = lambda a, _s=shp: a.reshape(-1, _s[-1])
        res = _adamw(two_d(wts[k]), two_d(g_fin[k]), two_d(mom_m[k]), two_d(mom_v[k]), name=f"adamw_{k}")
        results[k] = [g_fin[k]] + [r.reshape(shp) for r in res]
    sm_shapes = [wts[k].shape for k in small_names]
    pk = lambda d: _pack([d[k] for k in small_names], F32, 64)
    res = [_unpack(r, sm_shapes) for r in _adamw(pk(wts), pk(gs), pk(mom_m), pk(mom_v), name="adamw_small")]
    for i, k in enumerate(small_names):
        results[k] = [gs[k]] + [res[j][i] for j in range(3)]

    out = [loss, grad_x]
    for j in range(4):
        out += [results[k][j] for k in WEIGHTS]
    return tuple(out)
```

```python
import functools
import math

import jax
import jax.numpy as jnp
from jax import lax
from jax.experimental import pallas as pl
from jax.experimental.pallas import tpu as pltpu

F32, BF16 = jnp.float32, jnp.bfloat16
HI = lax.Precision.HIGHEST
NN, NT, TN = ((1,), (0,)), ((1,), (1,)), ((0,), (0,))

NORM_EPS = 1e-6
LANES = 128
V7X_VMEM_BYTES = 64 * 2**20
VMEM_LIMIT = V7X_VMEM_BYTES * 3 // 4
MM_TM, MM_TN, MM_TK = 1024, 1408, 2816

DN_HEADS, DN_DIM, DN_CHUNK, DN_CONV, HALO = 4, 128, 64, 4, 8
DN_GROUP = 4
INV_BLOCK = 16
DN_WIDTH = DN_HEADS * DN_DIM
SWA_HEADS, SWA_DIM, SWA_BLOCK = 4, 64, 128
SWA_WIDTH = SWA_HEADS * SWA_DIM
SWA_PATTERNS = ((128, 1), (512, 4), (2048, 16))
SG_GROUPS, SG_DIM, SG_CHUNK = 4, 64, 128
SG_WIDTH = SG_GROUPS * SG_DIM
XA_HEADS = 4
IN_SIZES = (3 * DN_WIDTH, DN_WIDTH, DN_HEADS, DN_HEADS, 3 * SWA_WIDTH, 2 * SG_WIDTH)
ADAM_LR, ADAM_B1, ADAM_B2, ADAM_EPS, ADAM_WD, ADAM_STEP = 0.001, 0.9, 0.999, 1e-08, 0.01, 10
N_CHIPS, N_DEV = 4, 8
MESH_ID = pl.DeviceIdType.MESH

BIG = ("ffn1_w_gate_up", "ffn1_w_down", "mix_w_in", "mix_w_out", "xa_w_q", "xa_w_kv", "xa_w_o",
       "ffn2_w_gate_up", "ffn2_w_down")
BIG_AXIS = {"ffn1_w_gate_up": 2, "ffn1_w_down": 1, "mix_w_in": 2, "mix_w_out": 1, "xa_w_q": 1, "xa_w_kv": 2,
            "xa_w_o": 1, "ffn2_w_gate_up": 2, "ffn2_w_down": 1}
SMALL = ("ffn1_norm", "mix_norm", "dn_a_log", "dn_dt_bias", "dn_out_norm", "sg_norm_gain", "sg_norm_bias",
         "sg_w_spatial", "sg_b_spatial", "xa_norm", "xa_mem_norm", "ffn2_norm", "final_norm")
WEIGHTS = ("ffn1_norm", "ffn1_w_gate_up", "ffn1_w_down", "mix_norm", "mix_w_in", "dn_conv_w", "dn_a_log",
           "dn_dt_bias", "dn_out_norm", "sg_norm_gain", "sg_norm_bias", "sg_w_spatial", "sg_b_spatial",
           "mix_w_out", "xa_norm", "xa_mem_norm", "xa_w_q", "xa_w_kv", "xa_w_o", "ffn2_norm", "ffn2_w_gate_up",
           "ffn2_w_down", "final_norm")


@functools.partial(jax.custom_vjp, nondiff_argnums=(2,))
def _dlo(a, b, dims):
    return lax.dot_general(a.astype(BF16), b.astype(BF16), (dims, ((), ())), preferred_element_type=F32)


def _dlo_fwd(a, b, dims):
    return _dlo(a, b, dims), (a, b)


def _dlo_bwd(dims, saved, g):
    a, b = saved
    if dims == NN:
        da, db = _dlo(g, b, NT), _dlo(a, g, TN)
    elif dims == NT:
        da, db = _dlo(g, b, NN), _dlo(g, a, TN)
    else:
        da, db = _dlo(b, g, NT), _dlo(a, g, NN)
    return da.astype(a.dtype), db.astype(b.dtype)


_dlo.defvjp(_dlo_fwd, _dlo_bwd)


def _dhi(a, b, dims):
    return lax.dot_general(a, b, (dims, ((), ())), preferred_element_type=F32, precision=HI)


def _sigmoid(x):
    return 1.0 / (1.0 + jnp.exp(-x))


def _silu(x):
    return x * _sigmoid(x)


def _softplus(x):
    return jnp.maximum(x, 0.0) + jnp.log(1.0 + jnp.exp(-jnp.abs(x)))


def _rms(x, gain):
    x = x.astype(F32)
    return x * lax.rsqrt(jnp.mean(x * x, axis=-1, keepdims=True) + NORM_EPS) * gain


def _tile(n, target, unit=LANES):
    best = None
    for t in range(unit, min(n, target) + 1, unit):
        if n % t == 0:
            best = t
    return best if best is not None else n


def _cparams(sem):
    return pltpu.CompilerParams(dimension_semantics=sem, vmem_limit_bytes=VMEM_LIMIT)


def _mm(a, b, dims, *, out_dtype, name, res=None, scale=1.0, swiglu_a=False, b_k0=0):
    feat = 2 if swiglu_a else 1
    if dims == NN:
        (m, k), n = (a.shape[0], a.shape[1] // feat), b.shape[1]
    elif dims == NT:
        (m, k), n = a.shape, b.shape[0]
    else:
        (k, m), n = (a.shape[0], a.shape[1] // feat), b.shape[1]
    if dims == TN:
        tm, tn, tk = _tile(m, MM_TN), _tile(n, MM_TN), _tile(k, MM_TM // feat)
    else:
        tm, tn, tk = _tile(m, MM_TM // feat, 8), _tile(n, MM_TN), _tile(k, MM_TK // feat)
    nk = k // tk
    assert b_k0 % tk == 0 and (b_k0 == 0 or dims == NT)
    k0 = b_k0 // tk
    if dims == TN:
        a_specs = [pl.BlockSpec((tk, tm), lambda i, j, kk, _o=o * (m // tm): (kk, i + _o)) for o in range(feat)]
    else:
        a_specs = [pl.BlockSpec((tm, tk), lambda i, j, kk, _o=o * nk: (i, kk + _o)) for o in range(feat)]
    b_spec = pl.BlockSpec((tn, tk), lambda i, j, kk: (j, kk + k0)) if dims == NT else pl.BlockSpec((tk, tn), lambda i, j, kk: (kk, j))
    o_spec = pl.BlockSpec((tm, tn), lambda i, j, kk: (i, j))
    has_res = res is not None

    def finish(acc, r_ref, o_ref):
        val = acc * scale if scale != 1.0 else acc
        if has_res:
            val = r_ref[...].astype(F32) + val
        o_ref[...] = val.astype(o_ref.dtype)

    def body(*refs):
        b_ref = refs[feat]
        r_ref = refs[feat + 1] if has_res else None
        a_val = _silu(refs[0][...].astype(F32)) * refs[1][...].astype(F32) if swiglu_a else refs[0][...]
        part = lax.dot_general(a_val.astype(BF16), b_ref[...].astype(BF16), (dims, ((), ())),
                               preferred_element_type=F32)
        if nk == 1:
            finish(part, r_ref, refs[-1])
            return
        o_ref, acc_ref = refs[-2], refs[-1]
        kk = pl.program_id(2)

        @pl.when(kk == 0)
        def _():
            acc_ref[...] = part

        @pl.when(jnp.logical_and(kk > 0, kk < nk - 1))
        def _():
            acc_ref[...] += part

        @pl.when(kk == nk - 1)
        def _():
            finish(acc_ref[...] + part, r_ref, o_ref)

    return pl.pallas_call(
        body, name=name, grid=(m // tm, n // tn, nk),
        in_specs=a_specs + [b_spec] + ([o_spec] if has_res else []), out_specs=o_spec,
        out_shape=jax.ShapeDtypeStruct((m, n), out_dtype),
        scratch_shapes=[pltpu.VMEM((tm, tn), F32)] if nk > 1 else [],
        compiler_params=_cparams(("parallel", "parallel", "arbitrary")),
    )(*([a] * feat + [b] + ([res] if has_res else [])))


def _mm_swiglu_bwd(dh, w_d, gu, scale, name):
    m, k = dh.shape
    f = w_d.shape[0]
    tm, tn = _tile(m, MM_TM // 2, 8), _tile(f, MM_TN)

    def body(dh_ref, w_ref, g_ref, u_ref, dg_ref, du_ref):
        d_act = lax.dot_general(dh_ref[...].astype(BF16), w_ref[...].astype(BF16), (NT, ((), ())),
                                preferred_element_type=F32) * scale
        _, pull = jax.vjp(lambda g, u: _silu(g) * u, g_ref[...].astype(F32), u_ref[...].astype(F32))
        dg, du = pull(d_act)
        dg_ref[...] = dg.astype(dg_ref.dtype)
        du_ref[...] = du.astype(du_ref.dtype)

    tile = pl.BlockSpec((tm, tn), lambda i, j: (i, j))
    return pl.pallas_call(
        body, name=name, grid=(m // tm, f // tn),
        in_specs=[pl.BlockSpec((tm, k), lambda i, j: (i, 0)), pl.BlockSpec((tn, k), lambda i, j: (j, 0)), tile,
                  pl.BlockSpec((tm, tn), lambda i, j: (i, j + f // tn))],
        out_specs=[tile, tile], out_shape=[jax.ShapeDtypeStruct((m, f), BF16)] * 2,
        compiler_params=_cparams(("parallel", "parallel")),
    )(dh, w_d, gu, gu)


def _full_spec(p):
    nd = p.ndim
    return pl.BlockSpec(p.shape, lambda i, _nd=nd: (0,) * _nd)


def _rows(f, rows, params, outs, *, tile, name):
    t = rows[0].shape[0]
    tile = min(tile, t)
    nr, npar = len(rows), len(params)

    def body(*refs):
        vals = f(*[r[...] for r in refs[:nr + npar]])
        for o_ref, v in zip(refs[nr + npar:], vals, strict=True):
            o_ref[...] = v.astype(o_ref.dtype)

    res = pl.pallas_call(
        body, name=name, grid=(t // tile,),
        in_specs=[pl.BlockSpec((tile, r.shape[1]), lambda i: (i, 0)) for r in rows] + [_full_spec(p) for p in params],
        out_specs=[pl.BlockSpec((tile, w), lambda i: (i, 0)) for w, _ in outs],
        out_shape=[jax.ShapeDtypeStruct((t, w), d) for w, d in outs],
        compiler_params=_cparams(("parallel",)),
    )(*rows, *params)
    return tuple(res)


def _rows_vjp(f, rows, params, cts, *, diff, grad_dtypes, tile, name, add=None):
    t = rows[0].shape[0]
    tile = min(tile, t)
    nr, npar, nct = len(rows), len(params), len(cts)
    didx = [i for i, d in enumerate(diff) if d]
    add = [None] * len(didx) if add is None else add
    adds = [a for a in add if a is not None]

    def body(*refs):
        row_refs, par_refs = refs[:nr], refs[nr:nr + npar]
        ct_refs = refs[nr + npar:nr + npar + nct]
        add_refs = list(refs[nr + npar + nct:nr + npar + nct + len(adds)])
        out_refs = refs[nr + npar + nct + len(adds):]
        rv = [r[...] for r in row_refs]
        pv = [p[...].astype(F32) for p in par_refs]

        def g(*args):
            full = list(rv)
            for k, i in enumerate(didx):
                full[i] = args[k]
            return f(*full, *args[len(didx):])

        outs, pull = jax.vjp(g, *[rv[i] for i in didx], *pv)
        grads = pull(tuple(c[...].astype(o.dtype) for c, o in zip(ct_refs, outs, strict=True)))
        for k in range(len(didx)):
            val = grads[k].astype(F32)
            if add[k] is not None:
                val = val + add_refs.pop(0)[...].astype(F32)
            out_refs[k][...] = val.astype(out_refs[k].dtype)

        @pl.when(pl.program_id(0) == 0)
        def _():
            for o_ref in out_refs[len(didx):]:
                o_ref[...] = jnp.zeros_like(o_ref)

        for o_ref, gp in zip(out_refs[len(didx):], grads[len(didx):], strict=True):
            o_ref[...] += gp.astype(F32)

    row_spec = lambda a: pl.BlockSpec((tile, a.shape[1]), lambda i: (i, 0))
    res = pl.pallas_call(
        body, name=name, grid=(t // tile,),
        in_specs=[row_spec(r) for r in rows] + [_full_spec(p) for p in params] + [row_spec(c) for c in cts]
        + [row_spec(a) for a in adds],
        out_specs=[row_spec(rows[i]) for i in didx] + [_full_spec(p) for p in params],
        out_shape=[jax.ShapeDtypeStruct(rows[i].shape, d) for i, d in zip(didx, grad_dtypes, strict=True)]
        + [jax.ShapeDtypeStruct(p.shape, F32) for p in params],
        compiler_params=_cparams(("arbitrary",)),
    )(*rows, *params, *cts, *adds)
    return tuple(res)


def _sum_picked(a0, a1, pick, other, out_dtype, name):
    shape = a0.shape
    views = [a.reshape(-1, shape[-1]) for a in (a0, a1, other)]
    r, c = views[0].shape
    tile = _tile(r, max(16, (1 << 18) // c), 16)

    def body(a0_ref, a1_ref, other_ref, pick_ref, o_ref):
        mine = jnp.where(pick_ref[...] == 0, a0_ref[...].astype(F32), a1_ref[...].astype(F32))
        o_ref[...] = (mine + other_ref[...].astype(F32)).astype(o_ref.dtype)

    spec = pl.BlockSpec((tile, c), lambda i: (i, 0))
    return pl.pallas_call(
        body, name=name, grid=(r // tile,), in_specs=[spec] * 3 + [_full_spec(pick)], out_specs=spec,
        out_shape=jax.ShapeDtypeStruct((r, c), out_dtype), compiler_params=_cparams(("parallel",)),
    )(*views, pick).reshape(shape)


def _sum_slots(x, out_dtype, name, first=None):
    n, r, c = x.shape
    tile = _tile(r, max(16, (1 << 18) // c), 16)

    def body(*refs):
        acc = refs[0][...].astype(F32)
        for ref in refs[1:-1]:
            acc = acc + ref[...].astype(F32)
        refs[-1][...] = acc.astype(refs[-1].dtype)

    spec = pl.BlockSpec((tile, c), lambda i: (i, 0))
    return pl.pallas_call(
        body, name=name, grid=(r // tile,),
        in_specs=([spec] if first is not None else [])
        + [pl.BlockSpec((None, tile, c), lambda i, _s=s_: (_s, i, 0)) for s_ in range(n)],
        out_specs=spec, out_shape=jax.ShapeDtypeStruct((r, c), out_dtype), compiler_params=_cparams(("parallel",)),
    )(*(([first] if first is not None else []) + [x] * n))


def _adamw(w, g, m, v, name):
    r, c = w.shape
    tile = _tile(r, max(8, (1 << 18) // c), 8)

    def body(w_ref, g_ref, m_ref, v_ref, d_out, m_out, v_out):
        g = g_ref[...]
        mn = ADAM_B1 * m_ref[...] + (1.0 - ADAM_B1) * g
        vn = ADAM_B2 * v_ref[...] + (1.0 - ADAM_B2) * (g * g)
        m_hat = mn / (1.0 - ADAM_B1 ** ADAM_STEP)
        v_hat = vn / (1.0 - ADAM_B2 ** ADAM_STEP)
        d_out[...] = -ADAM_LR * (m_hat / (jnp.sqrt(v_hat) + ADAM_EPS) + ADAM_WD * w_ref[...])
        m_out[...] = mn
        v_out[...] = vn

    spec = pl.BlockSpec((tile, c), lambda i: (i, 0))
    return pl.pallas_call(
        body, name=name, grid=(r // tile,), in_specs=[spec] * 4, out_specs=[spec] * 3,
        out_shape=[jax.ShapeDtypeStruct((r, c), F32)] * 3, compiler_params=_cparams(("parallel",)),
    )(w, g, m, v)


def _place():
    return lax.axis_index("x"), lax.axis_index("y"), lax.axis_index("c")


def _flip(v, bit):
    return 1 - v if bit else v


_ANY = pl.BlockSpec(memory_space=pl.ANY)


def _quarter(ref, j, shape, kind):
    if kind == "row":
        return ref.at[pl.ds(j * shape[0], shape[0])]
    if kind == "col":
        return ref.at[:, pl.ds(j * shape[1], shape[1])]
    return ref.at[j]


def _whole_shape(shape, kind):
    if kind == "row":
        return (N_CHIPS * shape[0],) + tuple(shape[1:])
    if kind == "col":
        return (shape[0], N_CHIPS * shape[1]) + tuple(shape[2:])
    return (N_CHIPS,) + tuple(shape)


def _dma_sems(*counts):
    return [pltpu.SemaphoreType.DMA((n,)) for n in counts]


def _gather_weights(shards, kinds, name):
    n = len(shards)
    shapes = [s.shape[1:] for s in shards]

    def body(*refs):
        ins, outs = refs[:n], refs[n:2 * n]
        send_sems, recv_sems, pair_send, pair_recv, own_send, own_recv = refs[2 * n:]
        mx, my, mc = _place()
        me = 2 * mx + my
        started = []
        for t in range(n):
            started.append(pltpu.make_async_remote_copy(
                src_ref=ins[t].at[1 - mc], dst_ref=_quarter(outs[t].at[1 - mc], me, shapes[t], kinds[t]),
                send_sem=own_send.at[t], recv_sem=own_recv.at[t], device_id=(mx, my, 1 - mc), device_id_type=MESH_ID))
            started[-1].start()
            for k in range(1, N_CHIPS):
                started.append(pltpu.make_async_remote_copy(
                    src_ref=ins[t].at[mc], dst_ref=_quarter(outs[t].at[mc], me, shapes[t], kinds[t]),
                    send_sem=send_sems.at[3 * t + k - 1], recv_sem=recv_sems.at[3 * t + k - 1],
                    device_id=(_flip(mx, k >> 1), _flip(my, k & 1), mc), device_id_type=MESH_ID))
                started[-1].start()
        for t in range(n):
            pltpu.make_async_remote_copy(
                src_ref=ins[t].at[mc], dst_ref=_quarter(outs[t].at[mc], me, shapes[t], kinds[t]),
                send_sem=own_send.at[t], recv_sem=own_recv.at[t], device_id=(mx, my, 1 - mc),
                device_id_type=MESH_ID).wait_recv()
            for k in range(1, N_CHIPS):
                px, py = _flip(mx, k >> 1), _flip(my, k & 1)
                pltpu.make_async_remote_copy(
                    src_ref=ins[t].at[mc], dst_ref=_quarter(outs[t].at[mc], 2 * px + py, shapes[t], kinds[t]),
                    send_sem=send_sems.at[3 * t + k - 1], recv_sem=recv_sems.at[3 * t + k - 1],
                    device_id=(px, py, mc), device_id_type=MESH_ID).wait_recv()
        for t in range(n):
            started.append(pltpu.make_async_remote_copy(
                src_ref=outs[t].at[mc], dst_ref=outs[t].at[mc], send_sem=pair_send.at[t], recv_sem=pair_recv.at[t],
                device_id=(mx, my, 1 - mc), device_id_type=MESH_ID))
            started[-1].start()
        for t in range(n):
            pltpu.make_async_remote_copy(
                src_ref=outs[t].at[1 - mc], dst_ref=outs[t].at[1 - mc], send_sem=pair_send.at[t], recv_sem=pair_recv.at[t],
                device_id=(mx, my, 1 - mc), device_id_type=MESH_ID).wait_recv()
        for cp in started:
            cp.wait_send()

    return pl.pallas_call(
        body, name=name, in_specs=[_ANY] * n, out_specs=[_ANY] * n,
        out_shape=[jax.ShapeDtypeStruct((2,) + _whole_shape(sh, kd), s.dtype) for s, sh, kd in zip(shards, shapes, kinds)],
        scratch_shapes=_dma_sems(3 * n, 3 * n, n, n, n, n),
    )(*shards)


def _give_other_layer(g0, g1, name):
    n = len(g0)

    def body(*refs):
        a, b, got = refs[:n], refs[n:2 * n], refs[2 * n:3 * n]
        send_sems, recv_sems = refs[3 * n:]
        mx, my, mc = _place()
        peer = (mx, my, 1 - mc)
        for give, core in ((b, 0), (a, 1)):
            @pl.when(mc == core)
            def _():
                for t in range(n):
                    pltpu.make_async_remote_copy(src_ref=give[t], dst_ref=got[t], send_sem=send_sems.at[t],
                                                 recv_sem=recv_sems.at[t], device_id=peer, device_id_type=MESH_ID).start()
        for t in range(n):
            cp = pltpu.make_async_remote_copy(src_ref=a[t], dst_ref=got[t], send_sem=send_sems.at[t],
                                              recv_sem=recv_sems.at[t], device_id=peer, device_id_type=MESH_ID)
            cp.wait_recv()
            cp.wait_send()

    return pl.pallas_call(
        body, name=name, in_specs=[_ANY] * (2 * n), out_specs=[_ANY] * n,
        out_shape=[jax.ShapeDtypeStruct(g.shape, g.dtype) for g in g0], scratch_shapes=_dma_sems(n, n),
    )(*g0, *g1)


def _scatter_quarters(gs, shapes, kinds, name):
    n = len(gs)

    def body(*refs):
        ins, outs = refs[:n], refs[n:2 * n]
        send_sems, recv_sems = refs[2 * n:]
        mx, my, mc = _place()
        sends = []
        for t in range(n):
            for k in range(1, N_CHIPS):
                px, py = _flip(mx, k >> 1), _flip(my, k & 1)
                sends.append(pltpu.make_async_remote_copy(
                    src_ref=_quarter(ins[t], 2 * px + py, shapes[t], kinds[t]), dst_ref=outs[t].at[k - 1],
                    send_sem=send_sems.at[3 * t + k - 1], recv_sem=recv_sems.at[3 * t + k - 1],
                    device_id=(px, py, mc), device_id_type=MESH_ID))
                sends[-1].start()
        for cp in sends:
            cp.wait_recv()
        for cp in sends:
            cp.wait_send()

    return pl.pallas_call(
        body, name=name, in_specs=[_ANY] * n, out_specs=[_ANY] * n,
        out_shape=[jax.ShapeDtypeStruct((N_CHIPS - 1,) + tuple(sh), g.dtype) for g, sh in zip(gs, shapes)],
        scratch_shapes=_dma_sems(3 * n, 3 * n),
    )(*gs)


def _swap_cores(parts, name):
    n = len(parts)

    def body(*refs):
        ins, outs = refs[:n], refs[n:2 * n]
        send_sems, recv_sems = refs[2 * n:]
        mx, my, mc = _place()
        copies = [pltpu.make_async_remote_copy(src_ref=ins[t], dst_ref=outs[t], send_sem=send_sems.at[t],
                                               recv_sem=recv_sems.at[t], device_id=(mx, my, 1 - mc),
                                               device_id_type=MESH_ID) for t in range(n)]
        for cp in copies:
            cp.start()
        for cp in copies:
            cp.wait_recv()
        for cp in copies:
            cp.wait_send()

    return pl.pallas_call(
        body, name=name, in_specs=[_ANY] * n, out_specs=[_ANY] * n,
        out_shape=[jax.ShapeDtypeStruct(p.shape, p.dtype) for p in parts], scratch_shapes=_dma_sems(n, n),
    )(*parts)


def _gather_all(x, name):
    r, w = x.shape

    def body(x_ref, o_ref, send_sems, recv_sems, local_sem):
        mx, my, mc = _place()
        mine = 4 * mx + 2 * my + mc
        local = pltpu.make_async_copy(x_ref, o_ref.at[mine], local_sem)
        local.start()
        copies = []
        for k in range(1, N_DEV):
            peer = (_flip(mx, k >> 2), _flip(my, (k >> 1) & 1), _flip(mc, k & 1))
            copies.append(pltpu.make_async_remote_copy(
                src_ref=x_ref, dst_ref=o_ref.at[mine], send_sem=send_sems.at[k - 1], recv_sem=recv_sems.at[k - 1],
                device_id=peer, device_id_type=MESH_ID))
            copies[-1].start()
        for k in range(1, N_DEV):
            peer = (_flip(mx, k >> 2), _flip(my, (k >> 1) & 1), _flip(mc, k & 1))
            pltpu.make_async_remote_copy(
                src_ref=x_ref, dst_ref=o_ref.at[4 * peer[0] + 2 * peer[1] + peer[2]], send_sem=send_sems.at[k - 1],
                recv_sem=recv_sems.at[k - 1], device_id=peer, device_id_type=MESH_ID).wait_recv()
        for cp in copies:
            cp.wait_send()
        local.wait()

    return pl.pallas_call(
        body, name=name, in_specs=[_ANY], out_specs=_ANY, out_shape=jax.ShapeDtypeStruct((N_DEV, r, w), x.dtype),
        scratch_shapes=[pltpu.SemaphoreType.DMA((N_DEV - 1,)), pltpu.SemaphoreType.DMA((N_DEV - 1,)),
                        pltpu.SemaphoreType.DMA],
    )(x)


def _pack(parts, dtype, row_unit):
    flat = jnp.concatenate([p.astype(dtype).reshape(-1) for p in parts])
    unit = row_unit * LANES
    pad = (-flat.shape[0]) % unit
    if pad:
        flat = jnp.concatenate([flat, jnp.zeros((pad,), dtype)])
    return flat.reshape(-1, LANES)


def _unpack(packed, shapes):
    flat = packed.reshape(-1)
    out, off = [], 0
    for s in shapes:
        n = math.prod(s)
        out.append(flat[off:off + n].reshape(s))
        off += n
    return out


def _f_rms(x, gain):
    return (_rms(x, gain),)


def _f_xattn(q, kv):
    d = q.shape[1]
    hd = d // XA_HEADS
    outs = []
    for h in range(XA_HEADS):
        qh, kh, vh = q[:, h * hd:(h + 1) * hd], kv[:, h * hd:(h + 1) * hd], kv[:, d + h * hd:d + (h + 1) * hd]
        s = _dlo(qh, kh, NT) * (hd ** -0.5)
        s = s - jnp.max(s, axis=-1, keepdims=True)
        p = jnp.exp(s)
        p = p / jnp.sum(p, axis=-1, keepdims=True)
        outs.append(_dlo(p, vh, NN))
    return (jnp.concatenate(outs, axis=1),)


def _f_gmlp(uv, gain, bias, w_sp, b_sp):
    r = uv.shape[0]
    act = jax.nn.gelu(uv.astype(F32))
    u, v = act[:, :SG_WIDTH], act[:, SG_WIDTH:]
    mu = jnp.mean(v, axis=-1, keepdims=True)
    var = jnp.mean(jnp.square(v - mu), axis=-1, keepdims=True)
    v = (v - mu) * lax.rsqrt(var + NORM_EPS) * gain + bias
    row = lax.broadcasted_iota(jnp.int32, (SG_CHUNK, SG_CHUNK), 0)
    col = lax.broadcasted_iota(jnp.int32, (SG_CHUNK, SG_CHUNK), 1)
    lane_grp = lax.broadcasted_iota(jnp.int32, (b_sp.shape[0], SG_WIDTH), 1) // SG_DIM
    grp_row = lax.broadcasted_iota(jnp.int32, (b_sp.shape[0], SG_WIDTH), 0)
    spread = jnp.where(lane_grp == grp_row, 1.0, 0.0).astype(F32)
    bias_t = _dhi(b_sp, spread, TN)
    chunks = []
    for c in range(r // SG_CHUNK):
        vc = v[c * SG_CHUNK:(c + 1) * SG_CHUNK]
        parts = []
        for g in range(SG_GROUPS):
            wg = jnp.where(row >= col, w_sp[g], 0.0)
            parts.append(_dlo(wg, vc[:, g * SG_DIM:(g + 1) * SG_DIM], NN))
        chunks.append(jnp.concatenate(parts, axis=1) + bias_t)
    mixed = jnp.concatenate(chunks, axis=0) if len(chunks) > 1 else chunks[0]
    return (u * mixed,)


def _f_swa_mix(o1, o2, o3, l1, l2, l3):
    outs = []
    for h in range(SWA_HEADS):
        ls = [l[:, h:h + 1] for l in (l1, l2, l3)]
        mx = jnp.maximum(jnp.maximum(ls[0], ls[1]), ls[2])
        es = [jnp.exp(l - mx) for l in ls]
        den = es[0] + es[1] + es[2]
        sl = slice(h * SWA_DIM, (h + 1) * SWA_DIM)
        outs.append((es[0] * o1[:, sl] + es[1] * o2[:, sl] + es[2] * o3[:, sl]) / den)
    return (jnp.concatenate(outs, axis=1),)


def _swa_block(q, kp, kc, vp, vc, first, window, dilation):
    span = window // dilation
    rows = SWA_HEADS * SWA_BLOCK
    ri = lax.broadcasted_iota(jnp.int32, (rows, 2 * SWA_BLOCK), 0)
    kj = lax.broadcasted_iota(jnp.int32, (rows, 2 * SWA_BLOCK), 1)
    head = ri // SWA_BLOCK
    rel = SWA_BLOCK + ri % SWA_BLOCK - kj
    valid = (rel >= 0) & (rel <= span) & jnp.logical_not(jnp.logical_and(first, kj < SWA_BLOCK))
    slope = jnp.exp((head + 1).astype(F32) * (-8.0 / SWA_HEADS * math.log(2.0)))
    bias = slope * (rel * dilation).astype(F32)
    kw = jnp.concatenate([kp, kc], axis=0)
    vw = jnp.concatenate([vp, vc], axis=0)
    lane_head = lax.broadcasted_iota(jnp.int32, (rows, SWA_WIDTH), 1) // SWA_DIM
    own_head = lane_head == lax.broadcasted_iota(jnp.int32, (rows, SWA_WIDTH), 0) // SWA_BLOCK
    q_rows = jnp.where(own_head, jnp.concatenate([q] * SWA_HEADS, axis=0), 0.0)
    s = _dlo(q_rows, kw, NT) * (SWA_DIM ** -0.5) - bias
    s = jnp.where(valid, s, -1e30)
    m = jnp.max(s, axis=-1, keepdims=True)
    p = jnp.exp(s - m)
    den = jnp.sum(p, axis=-1, keepdims=True)
    wide = _dlo(p, vw, NN) / den
    lse_rows = m + jnp.log(den)
    lane = lax.broadcasted_iota(jnp.int32, (SWA_BLOCK, LANES), 1)
    outs, lse = [], jnp.zeros((SWA_BLOCK, LANES), F32)
    for h in range(SWA_HEADS):
        outs.append(wide[h * SWA_BLOCK:(h + 1) * SWA_BLOCK, h * SWA_DIM:(h + 1) * SWA_DIM])
        lse = lse + jnp.where(lane == h, lse_rows[h * SWA_BLOCK:(h + 1) * SWA_BLOCK], 0.0)
    return jnp.concatenate(outs, axis=1), lse


@jax.custom_vjp
def _unit_lower_inv(lower):
    c = lower.shape[0]
    assert DN_CHUNK == 4 * INV_BLOCK and c % DN_CHUNK == 0
    row = lax.broadcasted_iota(jnp.int32, (c, c), 0)
    col = lax.broadcasted_iota(jnp.int32, (c, c), 1)
    eye = jnp.where(row == col, 1.0, 0.0).astype(F32)
    same = (row // INV_BLOCK) == (col // INV_BLOCK)
    pw = -jnp.where(same, lower, 0.0)
    d_inv = eye + pw
    for _ in range(int(math.log2(INV_BLOCK)) - 1):
        pw = _dlo(pw, pw, NN)
        d_inv = d_inv + _dlo(d_inv, pw, NN)
    n1 = _dlo(d_inv, jnp.where(same, 0.0, lower), NN)
    n2 = _dlo(n1, n1, NN)
    rough = _dlo(eye - n1 + n2 - _dlo(n1, n2, NN), d_inv, NN)
    residual = eye - _dhi(eye + lower, rough, NN)
    return rough + _dlo(rough, residual, NN)


def _unit_lower_inv_fwd(lower):
    t_inv = _unit_lower_inv(lower)
    return t_inv, t_inv


def _unit_lower_inv_bwd(t_inv, g):
    return (-_dlo(_dlo(t_inv, g, TN), t_inv, NT),)


_unit_lower_inv.defvjp(_unit_lower_inv_fwd, _unit_lower_inv_bwd)


def _dn_group(xx, z, ba, state, conv_w, a_log, dt_bias, gain):
    rows, c = z.shape[0], DN_CHUNK
    hc = DN_HEADS * c
    acc = conv_w[0:1] * xx[HALO - 3:HALO - 3 + rows]
    for j in range(1, DN_CONV):
        acc = acc + conv_w[j:j + 1] * xx[HALO - 3 + j:HALO - 3 + j + rows]
    qkv = _silu(acc)
    beta_all = _sigmoid(ba)
    g_all = -jnp.exp(a_log) * _softplus(ba + dt_bias)
    row = lax.broadcasted_iota(jnp.int32, (hc, hc), 0)
    col = lax.broadcasted_iota(jnp.int32, (hc, hc), 1)
    same_head = (row // c) == (col // c)
    incl, strict = same_head & (row >= col), same_head & (row > col)
    tri = jnp.where(incl[:c, :c], 1.0, 0.0).astype(F32)

    def stack(piece):
        return jnp.concatenate([piece(h) for h in range(DN_HEADS)], axis=0)

    local = []
    for ci in range(rows // c):
        r0 = ci * c
        gc_all = _dhi(tri, g_all[r0:r0 + c], NN)
        gc_t = gc_all.T
        q = stack(lambda h: qkv[r0:r0 + c, h * DN_DIM:(h + 1) * DN_DIM])
        k = stack(lambda h: qkv[r0:r0 + c, DN_WIDTH + h * DN_DIM:DN_WIDTH + (h + 1) * DN_DIM])
        v = stack(lambda h: qkv[r0:r0 + c, 2 * DN_WIDTH + h * DN_DIM:2 * DN_WIDTH + (h + 1) * DN_DIM])
        q = q * lax.rsqrt(jnp.sum(q * q, axis=-1, keepdims=True) + NORM_EPS) * (DN_DIM ** -0.5)
        k = k * lax.rsqrt(jnp.sum(k * k, axis=-1, keepdims=True) + NORM_EPS)
        beta = stack(lambda h: beta_all[r0:r0 + c, h:h + 1])
        gc = stack(lambda h: gc_all[:, DN_HEADS + h:DN_HEADS + h + 1])
        g_last = stack(lambda h: jnp.broadcast_to(gc_all[c - 1:c, DN_HEADS + h:DN_HEADS + h + 1], (c, 1)))
        gc_row = jnp.concatenate([gc_t[DN_HEADS + h:DN_HEADS + h + 1, :] for h in range(DN_HEADS)], axis=1)
        decay = jnp.where(incl, jnp.exp(jnp.where(incl, gc - gc_row, 0.0)), 0.0)
        kb = k * beta
        t_inv = _unit_lower_inv(jnp.where(strict, _dlo(kb, k, NT) * decay, 0.0))
        e_gc = jnp.exp(gc)
        u = _dlo(t_inv, v * beta, NN)
        w = _dlo(t_inv, kb * e_gc, NN)
        a_qk = jnp.where(incl, _dlo(q, k, NT) * decay, 0.0)
        e_last = jnp.concatenate([jnp.broadcast_to(jnp.exp(gc_all[c - 1:c, DN_HEADS + h:DN_HEADS + h + 1]), (1, DN_DIM))
                                  for h in range(DN_HEADS)], axis=1)
        local.append((jnp.concatenate([w, q * e_gc], axis=0), k * jnp.exp(g_last - gc), u, a_qk, e_last))
    own = (lax.broadcasted_iota(jnp.int32, (hc, DN_WIDTH), 0) // c) == (lax.broadcasted_iota(jnp.int32, (hc, DN_WIDTH), 1) // DN_DIM)

    def own_blocks(m):
        return stack(lambda h: m[h * c:(h + 1) * c, h * DN_DIM:(h + 1) * DN_DIM])

    s = state
    out_rows = []
    for ci in range(rows // c):
        r0 = ci * c
        wq, k_tail, u, a_qk, e_last = local[ci]
        through = _dlo(wq, s, NN)
        v_new = u - own_blocks(through[:hc])
        o = own_blocks(through[hc:]) + _dlo(a_qk, v_new, NN)
        v_wide = jnp.where(own, jnp.concatenate([v_new] * DN_HEADS, axis=1), 0.0)
        s = s * e_last + _dlo(k_tail, v_wide, TN)
        o = o * lax.rsqrt(jnp.mean(o * o, axis=-1, keepdims=True) + NORM_EPS) * gain
        o = jnp.concatenate([o[h * c:(h + 1) * c] for h in range(DN_HEADS)], axis=1)
        out_rows.append(o * _silu(z[r0:r0 + c]))
    out = jnp.concatenate(out_rows, axis=0) if len(out_rows) > 1 else out_rows[0]
    return out, s


def _dn_specs(n_of, rows):
    return [pl.BlockSpec((rows, 3 * DN_WIDTH), lambda i: (n_of(i), 0)),
            pl.BlockSpec((HALO, 3 * DN_WIDTH), lambda i: (jnp.maximum(n_of(i) * (rows // HALO) - 1, 0), 0)),
            pl.BlockSpec((rows, DN_WIDTH), lambda i: (n_of(i), 0)),
            pl.BlockSpec((rows, LANES), lambda i: (n_of(i), 0))]


def _dn_forward(xq, xz, xba, params, name):
    t = xq.shape[0]
    rows = min(DN_GROUP * DN_CHUNK, t)
    n_groups = t // rows

    def body(x_ref, halo_ref, z_ref, ba_ref, cw_ref, al_ref, dt_ref, gn_ref, o_ref, s_all_ref, s_ref):
        n = pl.program_id(0)

        @pl.when(n == 0)
        def _():
            s_ref[...] = jnp.zeros_like(s_ref)

        halo = jnp.where(n > 0, halo_ref[...], 0.0)
        xx = jnp.concatenate([halo, x_ref[...]], axis=0)
        s_all_ref[0] = s_ref[...]
        o, s_new = _dn_group(xx, z_ref[...], ba_ref[...], s_ref[...], cw_ref[...], al_ref[...], dt_ref[...], gn_ref[...])
        o_ref[...] = o.astype(o_ref.dtype)
        s_ref[...] = s_new

    return pl.pallas_call(
        body, name=name, grid=(n_groups,),
        in_specs=_dn_specs(lambda i: i, rows) + [_full_spec(p) for p in params],
        out_specs=[pl.BlockSpec((rows, DN_WIDTH), lambda i: (i, 0)),
                   pl.BlockSpec((1, DN_DIM, DN_WIDTH), lambda i: (i, 0, 0))],
        out_shape=[jax.ShapeDtypeStruct((t, DN_WIDTH), BF16),
                   jax.ShapeDtypeStruct((n_groups, DN_DIM, DN_WIDTH), F32)],
        scratch_shapes=[pltpu.VMEM((DN_DIM, DN_WIDTH), F32)],
        compiler_params=_cparams(("arbitrary",)),
    )(xq, xq, xz, xba, *params)


def _dn_backward(xq, xz, xba, params, s_all, d_out, name):
    t = xq.shape[0]
    rows = min(DN_GROUP * DN_CHUNK, t)
    n_groups = t // rows
    rev = lambda i: n_groups - 1 - i

    def body(x_ref, halo_ref, z_ref, ba_ref, cw_ref, al_ref, dt_ref, gn_ref, s_ref, do_ref,
             dx_ref, dz_ref, dba_ref, dcw_ref, dal_ref, ddt_ref, dgn_ref, ds_ref, dhalo_ref):
        i = pl.program_id(0)
        n = n_groups - 1 - i

        @pl.when(i == 0)
        def _():
            ds_ref[...] = jnp.zeros_like(ds_ref)
            dhalo_ref[...] = jnp.zeros_like(dhalo_ref)
            for r in (dcw_ref, dal_ref, ddt_ref, dgn_ref):
                r[...] = jnp.zeros_like(r)

        halo = jnp.where(n > 0, halo_ref[...], 0.0)
        xx = jnp.concatenate([halo, x_ref[...]], axis=0)
        _, pull = jax.vjp(_dn_group, xx, z_ref[...], ba_ref[...], s_ref[0], cw_ref[...], al_ref[...], dt_ref[...],
                          gn_ref[...])
        dxx, dz, dba, ds, dcw, dal, ddt, dgn = pull((do_ref[...].astype(F32), ds_ref[...]))
        dx_ref[...] = jnp.concatenate([dxx[HALO:rows], dxx[rows:] + dhalo_ref[...]], axis=0).astype(dx_ref.dtype)
        dhalo_ref[...] = dxx[:HALO]
        dz_ref[...] = dz.astype(dz_ref.dtype)
        dba_ref[...] = dba.astype(dba_ref.dtype)
        ds_ref[...] = ds
        dcw_ref[...] += dcw
        dal_ref[...] += dal
        ddt_ref[...] += ddt
        dgn_ref[...] += dgn

    return pl.pallas_call(
        body, name=name, grid=(n_groups,),
        in_specs=_dn_specs(rev, rows) + [_full_spec(p) for p in params]
        + [pl.BlockSpec((1, DN_DIM, DN_WIDTH), lambda i: (rev(i), 0, 0)),
           pl.BlockSpec((rows, DN_WIDTH), lambda i: (rev(i), 0))],
        out_specs=[pl.BlockSpec((rows, 3 * DN_WIDTH), lambda i: (rev(i), 0)),
                   pl.BlockSpec((rows, DN_WIDTH), lambda i: (rev(i), 0)),
                   pl.BlockSpec((rows, LANES), lambda i: (rev(i), 0))] + [_full_spec(p) for p in params],
        out_shape=[jax.ShapeDtypeStruct(xq.shape, BF16), jax.ShapeDtypeStruct(xz.shape, BF16),
                   jax.ShapeDtypeStruct(xba.shape, BF16)] + [jax.ShapeDtypeStruct(p.shape, F32) for p in params],
        scratch_shapes=[pltpu.VMEM((DN_DIM, DN_WIDTH), F32), pltpu.VMEM((HALO, 3 * DN_WIDTH), F32)],
        compiler_params=_cparams(("arbitrary",)),
    )(xq, xq, xz, xba, *params, s_all, d_out)


def _swa_forward(xs, window, dilation, name):
    t = xs.shape[0]
    d, l = dilation, t // dilation
    nb = l // SWA_BLOCK
    view = xs.reshape(l, d * 3 * SWA_WIDTH)
    blk = (SWA_BLOCK, SWA_WIDTH)

    def body(q_ref, kp_ref, kc_ref, vp_ref, vc_ref, o_ref, l_ref):
        blocks = [r[...].astype(F32) for r in (q_ref, kp_ref, kc_ref, vp_ref, vc_ref)]
        o, lse = _swa_block(*blocks, pl.program_id(1) == 0, window, dilation)
        o_ref[...] = o
        l_ref[...] = lse

    prev = lambda n: jnp.maximum(n - 1, 0)
    o, lse = pl.pallas_call(
        body, name=name, grid=(d, nb),
        in_specs=[pl.BlockSpec(blk, lambda r, n: (n, 3 * r)), pl.BlockSpec(blk, lambda r, n: (prev(n), 3 * r + 1)),
                  pl.BlockSpec(blk, lambda r, n: (n, 3 * r + 1)), pl.BlockSpec(blk, lambda r, n: (prev(n), 3 * r + 2)),
                  pl.BlockSpec(blk, lambda r, n: (n, 3 * r + 2))],
        out_specs=[pl.BlockSpec(blk, lambda r, n: (n, r)), pl.BlockSpec((SWA_BLOCK, LANES), lambda r, n: (n, r))],
        out_shape=[jax.ShapeDtypeStruct((l, d * SWA_WIDTH), F32), jax.ShapeDtypeStruct((l, d * LANES), F32)],
        compiler_params=_cparams(("parallel", "parallel")),
    )(view, view, view, view, view)
    return o.reshape(t, SWA_WIDTH), lse.reshape(t, LANES)


def _swa_backward(xs, d_o, d_lse, acc, window, dilation, name):
    t = xs.shape[0]
    d, l = dilation, t // dilation
    nb = l // SWA_BLOCK
    view = xs.reshape(l, d * 3 * SWA_WIDTH)
    blk = (SWA_BLOCK, SWA_WIDTH)
    has_acc = acc is not None

    def body(*refs):
        q_ref, kp_ref, kc_ref, vp_ref, vc_ref, do_ref, dl_ref = refs[:7]
        acc_refs = refs[7:10] if has_acc else None
        dq_ref, dk_ref, dv_ref, ck_ref, cv_ref = refs[-5:]
        i = pl.program_id(1)
        n = nb - 1 - i

        @pl.when(i == 0)
        def _():
            ck_ref[...] = jnp.zeros_like(ck_ref)
            cv_ref[...] = jnp.zeros_like(cv_ref)

        f = functools.partial(_swa_block, first=n == 0, window=window, dilation=dilation)
        _, pull = jax.vjp(f, *[r[...].astype(F32) for r in (q_ref, kp_ref, kc_ref, vp_ref, vc_ref)])
        dq, dkp, dkc, dvp, dvc = pull((do_ref[...], dl_ref[...]))
        dk = dkc + ck_ref[...]
        dv = dvc + cv_ref[...]
        if has_acc:
            dq, dk, dv = dq + acc_refs[0][...], dk + acc_refs[1][...], dv + acc_refs[2][...]
        dq_ref[...] = dq
        dk_ref[...] = dk
        dv_ref[...] = dv
        ck_ref[...] = dkp
        cv_ref[...] = dvp

    cur = lambda i: nb - 1 - i
    prev = lambda i: jnp.maximum(nb - 2 - i, 0)
    own = pl.BlockSpec(blk, lambda r, i: (cur(i), r))
    accs = [a.reshape(l, d * SWA_WIDTH) for a in acc] if has_acc else []
    outs = pl.pallas_call(
        body, name=name, grid=(d, nb),
        in_specs=[pl.BlockSpec(blk, lambda r, i: (cur(i), 3 * r)), pl.BlockSpec(blk, lambda r, i: (prev(i), 3 * r + 1)),
                  pl.BlockSpec(blk, lambda r, i: (cur(i), 3 * r + 1)), pl.BlockSpec(blk, lambda r, i: (prev(i), 3 * r + 2)),
                  pl.BlockSpec(blk, lambda r, i: (cur(i), 3 * r + 2)), own,
                  pl.BlockSpec((SWA_BLOCK, LANES), lambda r, i: (cur(i), r))] + [own] * len(accs),
        out_specs=[own] * 3, out_shape=[jax.ShapeDtypeStruct((l, d * SWA_WIDTH), F32)] * 3,
        scratch_shapes=[pltpu.VMEM(blk, F32), pltpu.VMEM(blk, F32)],
        compiler_params=_cparams(("parallel", "arbitrary")),
    )(view, view, view, view, view, d_o.reshape(l, d * SWA_WIDTH), d_lse.reshape(l, d * LANES), *accs)
    return tuple(o.reshape(t, SWA_WIDTH) for o in outs)


def _loss_head(h, target, gain, name, tile=256):
    t, d = h.shape
    tile = min(tile, t)

    def body(h_ref, t_ref, g_ref, loss_ref, dh_ref, dg_ref):
        def f(hv, gv):
            err = _rms(hv, gv) - t_ref[...]
            return 0.5 * jnp.sum(jnp.mean(err * err, axis=-1, keepdims=True), axis=0, keepdims=True)

        val, pull = jax.vjp(f, h_ref[...], g_ref[...])
        dh, dg = pull(jnp.ones((1, 1), F32))
        dh_ref[...] = dh

        @pl.when(pl.program_id(0) == 0)
        def _():
            loss_ref[...] = jnp.zeros_like(loss_ref)
            dg_ref[...] = jnp.zeros_like(dg_ref)

        loss_ref[...] += jnp.broadcast_to(val, loss_ref.shape)
        dg_ref[...] += dg

    return pl.pallas_call(
        body, name=name, grid=(t // tile,),
        in_specs=[pl.BlockSpec((tile, d), lambda i: (i, 0)), pl.BlockSpec((tile, d), lambda i: (i, 0)), _full_spec(gain)],
        out_specs=[pl.BlockSpec((1, LANES), lambda i: (0, 0)), pl.BlockSpec((tile, d), lambda i: (i, 0)), _full_spec(gain)],
        out_shape=[jax.ShapeDtypeStruct((1, LANES), F32), jax.ShapeDtypeStruct((t, d), F32),
                   jax.ShapeDtypeStruct(gain.shape, F32)],
        compiler_params=_cparams(("arbitrary",)),
    )(h, target, gain)


def _split_w_in(w_in):
    cuts = [0]
    for s in IN_SIZES:
        cuts.append(cuts[-1] + s)
    qkv, z = w_in[:, cuts[0]:cuts[1]], w_in[:, cuts[1]:cuts[2]]
    ba = jnp.pad(w_in[:, cuts[2]:cuts[4]], ((0, 0), (0, LANES - 2 * DN_HEADS)))
    return qkv, z, ba, w_in[:, cuts[4]:cuts[5]], w_in[:, cuts[5]:cuts[6]]


def _lane_pad(v, offset):
    return jnp.pad(v.reshape(1, -1), ((0, 0), (offset, LANES - offset - v.shape[0])))


def _layer_params(sm, i):
    return dict(
        ffn1_norm=sm["ffn1_norm"][i][None], mix_norm=sm["mix_norm"][i][None], xa_norm=sm["xa_norm"][i][None],
        xa_mem_norm=sm["xa_mem_norm"][i][None], ffn2_norm=sm["ffn2_norm"][i][None],
        dn=(sm["dn_conv_w"][i], _lane_pad(sm["dn_a_log"][i], DN_HEADS), _lane_pad(sm["dn_dt_bias"][i], DN_HEADS),
            sm["dn_out_norm"][i][None]),
        sg=(sm["sg_norm_gain"][i][None], sm["sg_norm_bias"][i][None], sm["sg_w_spatial"][i],
            jnp.pad(sm["sg_b_spatial"][i], ((0, 8 - SG_GROUPS), (0, 0)))),
    )


def _ffn_fwd(h, gain, w_gu, w_d, tag):
    n = _rows(_f_rms, [h], [gain], [(h.shape[1], BF16)], tile=512, name=f"{tag}_norm")[0]
    gu = _mm(n, w_gu, NN, out_dtype=BF16, name=f"{tag}_gate_up")
    out = _mm(gu, w_d, NN, out_dtype=F32, res=h, scale=0.5, swiglu_a=True, name=f"{tag}_down")
    return out, (h, n, gu)


def _ffn_bwd(dh, saved, gain, w_gu, w_d, tag):
    h, n, gu = saved
    f = w_d.shape[0]
    d_gate, d_up = _mm_swiglu_bwd(dh, w_d, gu, 0.5, name=f"{tag}_down_dx")
    dw_d = _mm(gu, dh, TN, out_dtype=BF16, scale=0.5, swiglu_a=True, name=f"{tag}_down_dw")
    dn = _mm(d_gate, w_gu, NT, out_dtype=F32, name=f"{tag}_gate_dx")
    dn = _mm(d_up, w_gu, NT, out_dtype=F32, res=dn, b_k0=f, name=f"{tag}_up_dx")
    dw_gu = jnp.concatenate([_mm(n, d_gate, TN, out_dtype=BF16, name=f"{tag}_gate_dw"),
                             _mm(n, d_up, TN, out_dtype=BF16, name=f"{tag}_up_dw")], axis=1)
    dh_in, dgain = _rows_vjp(_f_rms, [h], [gain], [dn], diff=[True], grad_dtypes=[F32], tile=512, add=[dh],
                             name=f"{tag}_norm_bwd")
    return dh_in, dgain, dw_gu, dw_d


def _mixer_fwd(h, p, w_in, w_out, tag):
    d = h.shape[1]
    n = _rows(_f_rms, [h], [p["mix_norm"]], [(d, BF16)], tile=512, name=f"{tag}_norm")[0]
    w_parts = _split_w_in(w_in)
    xq, xz, xba, xs, xg = (_mm(n, w, NN, out_dtype=BF16 if j == 3 else F32, name=f"{tag}_in{j}") for j, w in enumerate(w_parts))
    oa, s_all = _dn_forward(xq, xz, xba, p["dn"], name=f"{tag}_dn")
    swa = [_swa_forward(xs, wnd, dil, name=f"{tag}_swa{j}") for j, (wnd, dil) in enumerate(SWA_PATTERNS)]
    ob = _rows(_f_swa_mix, [o for o, _ in swa] + [l for _, l in swa], [], [(SWA_WIDTH, BF16)], tile=512,
               name=f"{tag}_swa_mix")[0]
    oc = _rows(_f_gmlp, [xg], list(p["sg"]), [(SG_WIDTH, BF16)], tile=256, name=f"{tag}_gmlp")[0]
    merged = jnp.concatenate([oa, ob, oc], axis=1)
    out = _mm(merged, w_out, NN, out_dtype=F32, res=h, name=f"{tag}_out")
    return out, (h, n, xq, xz, xba, xs, xg, s_all, swa, merged)


def _mixer_bwd(dh, saved, p, w_in, w_out, tag):
    h, n, xq, xz, xba, xs, xg, s_all, swa, merged = saved
    dw_out = _mm(merged, dh, TN, out_dtype=BF16, name=f"{tag}_out_dw")
    doa = _mm(dh, w_out[:DN_WIDTH], NT, out_dtype=F32, name=f"{tag}_out_dxa")
    dob = _mm(dh, w_out[DN_WIDTH:DN_WIDTH + SWA_WIDTH], NT, out_dtype=F32, name=f"{tag}_out_dxb")
    doc = _mm(dh, w_out[DN_WIDTH + SWA_WIDTH:], NT, out_dtype=F32, name=f"{tag}_out_dxc")
    res = _rows_vjp(_f_gmlp, [xg], list(p["sg"]), [doc], diff=[True], grad_dtypes=[BF16], tile=256, name=f"{tag}_gmlp_bwd")
    dxg, d_sg = res[0], res[1:]
    mix_in = [o for o, _ in swa] + [l for _, l in swa]
    d_mix = _rows_vjp(_f_swa_mix, mix_in, [], [dob], diff=[True] * 6, grad_dtypes=[F32] * 6, tile=512,
                      name=f"{tag}_swa_mix_bwd")
    acc = None
    for j, (wnd, dil) in enumerate(SWA_PATTERNS):
        acc = _swa_backward(xs, d_mix[j], d_mix[3 + j], acc, wnd, dil, name=f"{tag}_swa{j}_bwd")
    dxs = jnp.concatenate([a.astype(BF16) for a in acc], axis=1)
    res = _dn_backward(xq, xz, xba, p["dn"], s_all, doa, name=f"{tag}_dn_bwd")
    (dxq, dxz, dxba), d_dn = res[:3], res[3:]
    w_parts = _split_w_in(w_in)
    dn = None
    dws = []
    for j, (dx, w) in enumerate(zip((dxq, dxz, dxba, dxs, dxg), w_parts, strict=True)):
        dn = _mm(dx, w, NT, out_dtype=F32, res=dn, name=f"{tag}_in{j}_dx")
        dws.append(_mm(n, dx, TN, out_dtype=BF16, name=f"{tag}_in{j}_dw"))
    dws[2] = dws[2][:, :2 * DN_HEADS]
    dw_in = jnp.concatenate(dws, axis=1)
    dh_in, dgain = _rows_vjp(_f_rms, [h], [p["mix_norm"]], [dn], diff=[True], grad_dtypes=[F32], tile=512, add=[dh],
                             name=f"{tag}_norm_bwd")
    return dh_in, dgain, dw_in, dw_out, d_dn, d_sg


def _xattn_fwd(h, mem, p, w_q, w_kv, w_o, tag):
    d = h.shape[1]
    n = _rows(_f_rms, [h], [p["xa_norm"]], [(d, BF16)], tile=512, name=f"{tag}_norm")[0]
    mn = _rows(_f_rms, [mem], [p["xa_mem_norm"]], [(d, BF16)], tile=512, name=f"{tag}_mem_norm")[0]
    q = _mm(n, w_q, NN, out_dtype=BF16, name=f"{tag}_q")
    kv = _mm(mn, w_kv, NN, out_dtype=BF16, name=f"{tag}_kv")
    o = _rows(_f_xattn, [q], [kv], [(d, BF16)], tile=256, name=f"{tag}_core")[0]
    out = _mm(o, w_o, NN, out_dtype=F32, res=h, name=f"{tag}_o")
    return out, (h, n, mn, q, kv, o)


def _xattn_bwd(dh, saved, mem, p, w_q, w_kv, w_o, tag):
    h, n, mn, q, kv, o = saved
    do = _mm(dh, w_o, NT, out_dtype=BF16, name=f"{tag}_o_dx")
    dw_o = _mm(o, dh, TN, out_dtype=BF16, name=f"{tag}_o_dw")
    dq, dkv = _rows_vjp(_f_xattn, [q], [kv], [do], diff=[True], grad_dtypes=[BF16], tile=256, name=f"{tag}_core_bwd")
    dn = _mm(dq, w_q, NT, out_dtype=F32, name=f"{tag}_q_dx")
    dw_q = _mm(n, dq, TN, out_dtype=BF16, name=f"{tag}_q_dw")
    dmn = _mm(dkv, w_kv, NT, out_dtype=F32, name=f"{tag}_kv_dx")
    dw_kv = _mm(mn, dkv, TN, out_dtype=BF16, name=f"{tag}_kv_dw")
    dmem_gain = _rows_vjp(_f_rms, [mem], [p["xa_mem_norm"]], [dmn], diff=[False], grad_dtypes=[], tile=512,
                          name=f"{tag}_mem_norm_bwd")[0]
    dh_in, dgain = _rows_vjp(_f_rms, [h], [p["xa_norm"]], [dn], diff=[True], grad_dtypes=[F32], tile=512, add=[dh],
                             name=f"{tag}_norm_bwd")
    return dh_in, dgain, dmem_gain, dw_q, dw_kv, dw_o


def kernel(x, mem, ffn1_norm, ffn1_w_gate_up, ffn1_w_down, mix_norm, mix_w_in, dn_conv_w, dn_a_log, dn_dt_bias, dn_out_norm, sg_norm_gain, sg_norm_bias, sg_w_spatial, sg_b_spatial, mix_w_out, xa_norm, xa_mem_norm, xa_w_q, xa_w_kv, xa_w_o, ffn2_norm, ffn2_w_gate_up, ffn2_w_down, final_norm, loss_target, m_ffn1_norm, m_ffn1_w_gate_up, m_ffn1_w_down, m_mix_norm, m_mix_w_in, m_dn_conv_w, m_dn_a_log, m_dn_dt_bias, m_dn_out_norm, m_sg_norm_gain, m_sg_norm_bias, m_sg_w_spatial, m_sg_b_spatial, m_mix_w_out, m_xa_norm, m_xa_mem_norm, m_xa_w_q, m_xa_w_kv, m_xa_w_o, m_ffn2_norm, m_ffn2_w_gate_up, m_ffn2_w_down, m_final_norm, v_ffn1_norm, v_ffn1_w_gate_up, v_ffn1_w_down, v_mix_norm, v_mix_w_in, v_dn_conv_w, v_dn_a_log, v_dn_dt_bias, v_dn_out_norm, v_sg_norm_gain, v_sg_norm_bias, v_sg_w_spatial, v_sg_b_spatial, v_mix_w_out, v_xa_norm, v_xa_mem_norm, v_xa_w_q, v_xa_w_kv, v_xa_w_o, v_ffn2_norm, v_ffn2_w_gate_up, v_ffn2_w_down, v_final_norm):
    args = dict(locals())
    wts = {k: args[k] for k in WEIGHTS}
    mom_m = {k: args["m_" + k] for k in WEIGHTS}
    mom_v = {k: args["v_" + k] for k in WEIGHTS}
    depth = ffn1_norm.shape[0]
    h = x[0]
    mem2 = mem[0]
    target = loss_target[0]

    assert depth == 2, "core c of a chip is responsible for layer c in the weight and gradient exchanges"
    kinds = {k: "stack" if k == "mix_w_in" else ("row" if BIG_AXIS[k] == 1 else "col") for k in BIG}
    whole = _gather_weights([wts[k].astype(BF16) for k in BIG] + [dn_conv_w], [kinds[k] for k in BIG] + ["stack"],
                            name="gather_weights")
    full = dict(zip(BIG, whole[:-1], strict=True))
    full["mix_w_in"] = jnp.concatenate([full["mix_w_in"][:, j] for j in range(N_CHIPS)], axis=2)
    small = {k: wts[k] for k in SMALL}
    small["dn_conv_w"] = jnp.concatenate([whole[-1][:, j] for j in range(N_CHIPS)], axis=2)

    saved = []
    for i in range(depth):
        p = _layer_params(small, i)
        h, s1 = _ffn_fwd(h, p["ffn1_norm"], full["ffn1_w_gate_up"][i], full["ffn1_w_down"][i], f"l{i}_ffn1")
        h, s2 = _mixer_fwd(h, p, full["mix_w_in"][i], full["mix_w_out"][i], f"l{i}_mix")
        h, s3 = _xattn_fwd(h, mem2, p, full["xa_w_q"][i], full["xa_w_kv"][i], full["xa_w_o"][i], f"l{i}_xa")
        h, s4 = _ffn_fwd(h, p["ffn2_norm"], full["ffn2_w_gate_up"][i], full["ffn2_w_down"][i], f"l{i}_ffn2")
        saved.append((p, s1, s2, s3, s4))
    loss_part, dh, d_final = _loss_head(h, target, final_norm[None], name="loss_head")
    loss = lax.psum(loss_part[0, 0], ("x", "y", "c"))

    g_big = {k: [None] * depth for k in BIG}
    g_small = {k: [None] * depth for k in SMALL if k != "final_norm"}
    g_small["dn_conv_w"] = [None] * depth
    for i in reversed(range(depth)):
        p, s1, s2, s3, s4 = saved[i]
        dh, dg, dw_gu, dw_d = _ffn_bwd(dh, s4, p["ffn2_norm"], full["ffn2_w_gate_up"][i], full["ffn2_w_down"][i], f"l{i}_ffn2")
        g_small["ffn2_norm"][i], g_big["ffn2_w_gate_up"][i], g_big["ffn2_w_down"][i] = dg[0], dw_gu, dw_d
        dh, dg, dmg, dw_q, dw_kv, dw_o = _xattn_bwd(dh, s3, mem2, p, full["xa_w_q"][i], full["xa_w_kv"][i],
                                                    full["xa_w_o"][i], f"l{i}_xa")
        g_small["xa_norm"][i], g_small["xa_mem_norm"][i] = dg[0], dmg[0]
        g_big["xa_w_q"][i], g_big["xa_w_kv"][i], g_big["xa_w_o"][i] = dw_q, dw_kv, dw_o
        dh, dg, dw_in, dw_out, d_dn, d_sg = _mixer_bwd(dh, s2, p, full["mix_w_in"][i], full["mix_w_out"][i], f"l{i}_mix")
        g_small["mix_norm"][i], g_big["mix_w_in"][i], g_big["mix_w_out"][i] = dg[0], dw_in, dw_out
        g_small["dn_conv_w"][i] = d_dn[0]
        g_small["dn_a_log"][i] = d_dn[1][0, DN_HEADS:2 * DN_HEADS]
        g_small["dn_dt_bias"][i] = d_dn[2][0, DN_HEADS:2 * DN_HEADS]
        g_small["dn_out_norm"][i] = d_dn[3][0]
        g_small["sg_norm_gain"][i], g_small["sg_norm_bias"][i] = d_sg[0][0], d_sg[1][0]
        g_small["sg_w_spatial"][i], g_small["sg_b_spatial"][i] = d_sg[2], d_sg[3][:SG_GROUPS]
        dh, dg, dw_gu, dw_d = _ffn_bwd(dh, s1, p["ffn1_norm"], full["ffn1_w_gate_up"][i], full["ffn1_w_down"][i], f"l{i}_ffn1")
        g_small["ffn1_norm"][i], g_big["ffn1_w_gate_up"][i], g_big["ffn1_w_down"][i] = dg[0], dw_gu, dw_d
    grad_x = dh[None]
    g_small = {k: jnp.stack(v) for k, v in g_small.items()}
    g_small["final_norm"] = d_final[0]

    def layer_grad(k, i):
        g = g_big[k][i]
        if kinds[k] == "stack":
            n = wts[k].shape[2]
            g = jnp.stack([g[:, j * n:(j + 1) * n] for j in range(N_CHIPS)])
        return g

    core = lax.axis_index("c")
    chip = 2 * lax.axis_index("x") + lax.axis_index("y")
    pick = jnp.full((1, 1), core, jnp.int32)
    g0, g1 = [layer_grad(k, 0) for k in BIG], [layer_grad(k, 1) for k in BIG]
    got = _give_other_layer(g0, g1, name="give_other_layer")
    chip_sum = [_sum_picked(a, b, pick, g, BF16, name=f"sum_cores_{k}") for k, a, b, g in zip(BIG, g0, g1, got, strict=True)]
    recv = _scatter_quarters(chip_sum, [wts[k].shape[1:] for k in BIG], [kinds[k] for k in BIG], name="scatter_grads")

    def own_quarter(k, g):
        if kinds[k] == "stack":
            return lax.dynamic_index_in_dim(g, chip, axis=0, keepdims=False)
        axis = BIG_AXIS[k] - 1
        n = wts[k].shape[BIG_AXIS[k]]
        return lax.dynamic_slice_in_dim(g, chip * n, n, axis=axis)

    parts = [_sum_slots(r, F32, name=f"sum_chips_{k}", first=own_quarter(k, g))
             for k, r, g in zip(BIG, recv, chip_sum, strict=True)]
    others = _swap_cores(parts, name="swap_cores")
    g_fin = {k: jnp.where(core == 0, jnp.stack([p, o]), jnp.stack([o, p])) for k, p, o in zip(BIG, parts, others, strict=True)}
    small_names = list(SMALL) + ["dn_conv_w"]
    small_shapes = [g_small[k].shape for k in small_names]
    small_sum = _sum_slots(_gather_all(_pack([g_small[k] for k in small_names], F32, 64), name="gather_small"), F32,
                           name="sum_small")
    gs = dict(zip(small_names, _unpack(small_sum, small_shapes), strict=True))
    n_conv = dn_conv_w.shape[2]
    gs["dn_conv_w"] = lax.dynamic_slice_in_dim(gs["dn_conv_w"], (2 * lax.axis_index("x") + lax.axis_index("y")) * n_conv,
                                               n_conv, axis=2)

    results = {}
    for k in BIG:
        shp = wts[k].shape
        two_d = lambda a, _s=shp: a.reshape(-1, _s[-1])
        res = _adamw(two_d(wts[k]), two_d(g_fin[k]), two_d(mom_m[k]), two_d(mom_v[k]), name=f"adamw_{k}")
        results[k] = [g_fin[k]] + [r.reshape(shp) for r in res]
    sm_shapes = [wts[k].shape for k in small_names]
    pk = lambda d: _pack([d[k] for k in small_names], F32, 64)
    res = [_unpack(r, sm_shapes) for r in _adamw(pk(wts), pk(gs), pk(mom_m), pk(mom_v), name="adamw_small")]
    for i, k in enumerate(small_names):
        results[k] = [gs[k]] + [res[j][i] for j in range(3)]

    out = [loss, grad_x]
    for j in range(4):
        out += [results[k][j] for k in WEIGHTS]
    return tuple(out)
```

```python
import functools
import math

import jax
import jax.numpy as jnp
from jax import lax
from jax.experimental import pallas as pl
from jax.experimental.pallas import tpu as pltpu

F32, BF16 = jnp.float32, jnp.bfloat16
HI = lax.Precision.HIGHEST
NN, NT, TN = ((1,), (0,)), ((1,), (1,)), ((0,), (0,))

NORM_EPS = 1e-6
LANES = 128
V7X_VMEM_BYTES = 64 * 2**20
VMEM_LIMIT = V7X_VMEM_BYTES * 3 // 4
MM_TM, MM_TN, MM_TK = 1024, 1408, 2816

DN_HEADS, DN_DIM, DN_CHUNK, DN_CONV, HALO = 4, 128, 64, 4, 8
DN_GROUP = 4
INV_BLOCK = 16
DN_WIDTH = DN_HEADS * DN_DIM
SWA_HEADS, SWA_DIM, SWA_BLOCK = 4, 64, 128
SWA_WIDTH = SWA_HEADS * SWA_DIM
SWA_PATTERNS = ((128, 1), (512, 4), (2048, 16))
SG_GROUPS, SG_DIM, SG_CHUNK = 4, 64, 128
SG_WIDTH = SG_GROUPS * SG_DIM
XA_HEADS = 4
IN_SIZES = (3 * DN_WIDTH, DN_WIDTH, DN_HEADS, DN_HEADS, 3 * SWA_WIDTH, 2 * SG_WIDTH)
ADAM_LR, ADAM_B1, ADAM_B2, ADAM_EPS, ADAM_WD, ADAM_STEP = 0.001, 0.9, 0.999, 1e-08, 0.01, 10
N_CHIPS, N_DEV = 4, 8
MESH_ID = pl.DeviceIdType.MESH

BIG = ("ffn1_w_gate_up", "ffn1_w_down", "mix_w_in", "mix_w_out", "xa_w_q", "xa_w_kv", "xa_w_o",
       "ffn2_w_gate_up", "ffn2_w_down")
BIG_AXIS = {"ffn1_w_gate_up": 2, "ffn1_w_down": 1, "mix_w_in": 2, "mix_w_out": 1, "xa_w_q": 1, "xa_w_kv": 2,
            "xa_w_o": 1, "ffn2_w_gate_up": 2, "ffn2_w_down": 1}
SMALL = ("ffn1_norm", "mix_norm", "dn_a_log", "dn_dt_bias", "dn_out_norm", "sg_norm_gain", "sg_norm_bias",
         "sg_w_spatial", "sg_b_spatial", "xa_norm", "xa_mem_norm", "ffn2_norm", "final_norm")
WEIGHTS = ("ffn1_norm", "ffn1_w_gate_up", "ffn1_w_down", "mix_norm", "mix_w_in", "dn_conv_w", "dn_a_log",
           "dn_dt_bias", "dn_out_norm", "sg_norm_gain", "sg_norm_bias", "sg_w_spatial", "sg_b_spatial",
           "mix_w_out", "xa_norm", "xa_mem_norm", "xa_w_q", "xa_w_kv", "xa_w_o", "ffn2_norm", "ffn2_w_gate_up",
           "ffn2_w_down", "final_norm")


@functools.partial(jax.custom_vjp, nondiff_argnums=(2,))
def _dlo(a, b, dims):
    return lax.dot_general(a.astype(BF16), b.astype(BF16), (dims, ((), ())), preferred_element_type=F32)


def _dlo_fwd(a, b, dims):
    return _dlo(a, b, dims), (a, b)


def _dlo_bwd(dims, saved, g):
    a, b = saved
    if dims == NN:
        da, db = _dlo(g, b, NT), _dlo(a, g, TN)
    elif dims == NT:
        da, db = _dlo(g, b, NN), _dlo(g, a, TN)
    else:
        da, db = _dlo(b, g, NT), _dlo(a, g, NN)
    return da.astype(a.dtype), db.astype(b.dtype)


_dlo.defvjp(_dlo_fwd, _dlo_bwd)


def _dhi(a, b, dims):
    return lax.dot_general(a, b, (dims, ((), ())), preferred_element_type=F32, precision=HI)


def _sigmoid(x):
    return 1.0 / (1.0 + jnp.exp(-x))


def _silu(x):
    return x * _sigmoid(x)


def _softplus(x):
    return jnp.maximum(x, 0.0) + jnp.log(1.0 + jnp.exp(-jnp.abs(x)))


def _rms(x, gain):
    x = x.astype(F32)
    return x * lax.rsqrt(jnp.mean(x * x, axis=-1, keepdims=True) + NORM_EPS) * gain


def _tile(n, target, unit=LANES):
    best = None
    for t in range(unit, min(n, target) + 1, unit):
        if n % t == 0:
            best = t
    return best if best is not None else n


def _cparams(sem):
    return pltpu.CompilerParams(dimension_semantics=sem, vmem_limit_bytes=VMEM_LIMIT)


def _mm(a, b, dims, *, out_dtype, name, res=None, scale=1.0, swiglu_a=False, b_k0=0):
    feat = 2 if swiglu_a else 1
    if dims == NN:
        (m, k), n = (a.shape[0], a.shape[1] // feat), b.shape[1]
    elif dims == NT:
        (m, k), n = a.shape, b.shape[0]
    else:
        (k, m), n = (a.shape[0], a.shape[1] // feat), b.shape[1]
    if dims == TN:
        tm, tn, tk = _tile(m, MM_TN), _tile(n, MM_TN), _tile(k, MM_TM // feat)
    else:
        tm, tn, tk = _tile(m, MM_TM // feat, 8), _tile(n, MM_TN), _tile(k, MM_TK // feat)
    nk = k // tk
    assert b_k0 % tk == 0 and (b_k0 == 0 or dims == NT)
    k0 = b_k0 // tk
    if dims == TN:
        a_specs = [pl.BlockSpec((tk, tm), lambda i, j, kk, _o=o * (m // tm): (kk, i + _o)) for o in range(feat)]
    else:
        a_specs = [pl.BlockSpec((tm, tk), lambda i, j, kk, _o=o * nk: (i, kk + _o)) for o in range(feat)]
    b_spec = pl.BlockSpec((tn, tk), lambda i, j, kk: (j, kk + k0)) if dims == NT else pl.BlockSpec((tk, tn), lambda i, j, kk: (kk, j))
    o_spec = pl.BlockSpec((tm, tn), lambda i, j, kk: (i, j))
    has_res = res is not None

    def finish(acc, r_ref, o_ref):
        val = acc * scale if scale != 1.0 else acc
        if has_res:
            val = r_ref[...].astype(F32) + val
        o_ref[...] = val.astype(o_ref.dtype)

    def body(*refs):
        b_ref = refs[feat]
        r_ref = refs[feat + 1] if has_res else None
        a_val = _silu(refs[0][...].astype(F32)) * refs[1][...].astype(F32) if swiglu_a else refs[0][...]
        part = lax.dot_general(a_val.astype(BF16), b_ref[...].astype(BF16), (dims, ((), ())),
                               preferred_element_type=F32)
        if nk == 1:
            finish(part, r_ref, refs[-1])
            return
        o_ref, acc_ref = refs[-2], refs[-1]
        kk = pl.program_id(2)

        @pl.when(kk == 0)
        def _():
            acc_ref[...] = part

        @pl.when(jnp.logical_and(kk > 0, kk < nk - 1))
        def _():
            acc_ref[...] += part

        @pl.when(kk == nk - 1)
        def _():
            finish(acc_ref[...] + part, r_ref, o_ref)

    return pl.pallas_call(
        body, name=name, grid=(m // tm, n // tn, nk),
        in_specs=a_specs + [b_spec] + ([o_spec] if has_res else []), out_specs=o_spec,
        out_shape=jax.ShapeDtypeStruct((m, n), out_dtype),
        scratch_shapes=[pltpu.VMEM((tm, tn), F32)] if nk > 1 else [],
        compiler_params=_cparams(("parallel", "parallel", "arbitrary")),
    )(*([a] * feat + [b] + ([res] if has_res else [])))


def _mm_swiglu_bwd(dh, w_d, gu, scale, name):
    m, k = dh.shape
    f = w_d.shape[0]
    tm, tn = _tile(m, MM_TM // 2, 8), _tile(f, MM_TN)

    def body(dh_ref, w_ref, g_ref, u_ref, dg_ref, du_ref):
        d_act = lax.dot_general(dh_ref[...].astype(BF16), w_ref[...].astype(BF16), (NT, ((), ())),
                                preferred_element_type=F32) * scale
        _, pull = jax.vjp(lambda g, u: _silu(g) * u, g_ref[...].astype(F32), u_ref[...].astype(F32))
        dg, du = pull(d_act)
        dg_ref[...] = dg.astype(dg_ref.dtype)
        du_ref[...] = du.astype(du_ref.dtype)

    tile = pl.BlockSpec((tm, tn), lambda i, j: (i, j))
    return pl.pallas_call(
        body, name=name, grid=(m // tm, f // tn),
        in_specs=[pl.BlockSpec((tm, k), lambda i, j: (i, 0)), pl.BlockSpec((tn, k), lambda i, j: (j, 0)), tile,
                  pl.BlockSpec((tm, tn), lambda i, j: (i, j + f // tn))],
        out_specs=[tile, tile], out_shape=[jax.ShapeDtypeStruct((m, f), BF16)] * 2,
        compiler_params=_cparams(("parallel", "parallel")),
    )(dh, w_d, gu, gu)


def _full_spec(p):
    nd = p.ndim
    return pl.BlockSpec(p.shape, lambda i, _nd=nd: (0,) * _nd)


def _rows(f, rows, params, outs, *, tile, name):
    t = rows[0].shape[0]
    tile = min(tile, t)
    nr, npar = len(rows), len(params)

    def body(*refs):
        vals = f(*[r[...] for r in refs[:nr + npar]])
        for o_ref, v in zip(refs[nr + npar:], vals, strict=True):
            o_ref[...] = v.astype(o_ref.dtype)

    res = pl.pallas_call(
        body, name=name, grid=(t // tile,),
        in_specs=[pl.BlockSpec((tile, r.shape[1]), lambda i: (i, 0)) for r in rows] + [_full_spec(p) for p in params],
        out_specs=[pl.BlockSpec((tile, w), lambda i: (i, 0)) for w, _ in outs],
        out_shape=[jax.ShapeDtypeStruct((t, w), d) for w, d in outs],
        compiler_params=_cparams(("parallel",)),
    )(*rows, *params)
    return tuple(res)


def _rows_vjp(f, rows, params, cts, *, diff, grad_dtypes, tile, name, add=None):
    t = rows[0].shape[0]
    tile = min(tile, t)
    nr, npar, nct = len(rows), len(params), len(cts)
    didx = [i for i, d in enumerate(diff) if d]
    add = [None] * len(didx) if add is None else add
    adds = [a for a in add if a is not None]

    def body(*refs):
        row_refs, par_refs = refs[:nr], refs[nr:nr + npar]
        ct_refs = refs[nr + npar:nr + npar + nct]
        add_refs = list(refs[nr + npar + nct:nr + npar + nct + len(adds)])
        out_refs = refs[nr + npar + nct + len(adds):]
        rv = [r[...] for r in row_refs]
        pv = [p[...].astype(F32) for p in par_refs]

        def g(*args):
            full = list(rv)
            for k, i in enumerate(didx):
                full[i] = args[k]
            return f(*full, *args[len(didx):])

        outs, pull = jax.vjp(g, *[rv[i] for i in didx], *pv)
        grads = pull(tuple(c[...].astype(o.dtype) for c, o in zip(ct_refs, outs, strict=True)))
        for k in range(len(didx)):
            val = grads[k].astype(F32)
            if add[k] is not None:
                val = val + add_refs.pop(0)[...].astype(F32)
            out_refs[k][...] = val.astype(out_refs[k].dtype)

        @pl.when(pl.program_id(0) == 0)
        def _():
            for o_ref in out_refs[len(didx):]:
                o_ref[...] = jnp.zeros_like(o_ref)

        for o_ref, gp in zip(out_refs[len(didx):], grads[len(didx):], strict=True):
            o_ref[...] += gp.astype(F32)

    row_spec = lambda a: pl.BlockSpec((tile, a.shape[1]), lambda i: (i, 0))
    res = pl.pallas_call(
        body, name=name, grid=(t // tile,),
        in_specs=[row_spec(r) for r in rows] + [_full_spec(p) for p in params] + [row_spec(c) for c in cts]
        + [row_spec(a) for a in adds],
        out_specs=[row_spec(rows[i]) for i in didx] + [_full_spec(p) for p in params],
        out_shape=[jax.ShapeDtypeStruct(rows[i].shape, d) for i, d in zip(didx, grad_dtypes, strict=True)]
        + [jax.ShapeDtypeStruct(p.shape, F32) for p in params],
        compiler_params=_cparams(("arbitrary",)),
    )(*rows, *params, *cts, *adds)
    return tuple(res)


def _sum2(a, b, out_dtype, name):
    shape = a.shape
    views = [x.reshape(-1, shape[-1]) for x in (a, b)]
    r, c = views[0].shape
    tile = _tile(r, max(16, (1 << 18) // c), 16)

    def body(a_ref, b_ref, o_ref):
        o_ref[...] = (a_ref[...].astype(F32) + b_ref[...].astype(F32)).astype(o_ref.dtype)

    spec = pl.BlockSpec((tile, c), lambda i: (i, 0))
    return pl.pallas_call(
        body, name=name, grid=(r // tile,), in_specs=[spec] * 2, out_specs=spec,
        out_shape=jax.ShapeDtypeStruct((r, c), out_dtype), compiler_params=_cparams(("parallel",)),
    )(*views).reshape(shape)


def _sum_slots(x, out_dtype, name, first=None):
    n, r, c = x.shape
    tile = _tile(r, max(16, (1 << 18) // c), 16)

    def body(*refs):
        acc = refs[0][...].astype(F32)
        for ref in refs[1:-1]:
            acc = acc + ref[...].astype(F32)
        refs[-1][...] = acc.astype(refs[-1].dtype)

    spec = pl.BlockSpec((tile, c), lambda i: (i, 0))
    return pl.pallas_call(
        body, name=name, grid=(r // tile,),
        in_specs=([spec] if first is not None else [])
        + [pl.BlockSpec((None, tile, c), lambda i, _s=s_: (_s, i, 0)) for s_ in range(n)],
        out_specs=spec, out_shape=jax.ShapeDtypeStruct((r, c), out_dtype), compiler_params=_cparams(("parallel",)),
    )(*(([first] if first is not None else []) + [x] * n))


def _adamw(w, g, m, v, name):
    r, c = w.shape
    tile = _tile(r, max(8, (1 << 18) // c), 8)

    def body(w_ref, g_ref, m_ref, v_ref, d_out, m_out, v_out):
        g = g_ref[...]
        mn = ADAM_B1 * m_ref[...] + (1.0 - ADAM_B1) * g
        vn = ADAM_B2 * v_ref[...] + (1.0 - ADAM_B2) * (g * g)
        m_hat = mn / (1.0 - ADAM_B1 ** ADAM_STEP)
        v_hat = vn / (1.0 - ADAM_B2 ** ADAM_STEP)
        d_out[...] = -ADAM_LR * (m_hat / (jnp.sqrt(v_hat) + ADAM_EPS) + ADAM_WD * w_ref[...])
        m_out[...] = mn
        v_out[...] = vn

    spec = pl.BlockSpec((tile, c), lambda i: (i, 0))
    return pl.pallas_call(
        body, name=name, grid=(r // tile,), in_specs=[spec] * 4, out_specs=[spec] * 3,
        out_shape=[jax.ShapeDtypeStruct((r, c), F32)] * 3, compiler_params=_cparams(("parallel",)),
    )(w, g, m, v)


def _place():
    return lax.axis_index("x"), lax.axis_index("y"), lax.axis_index("c")


def _flip(v, bit):
    return 1 - v if bit else v


_ANY = pl.BlockSpec(memory_space=pl.ANY)


def _quarter(ref, j, shape, kind):
    if kind == "row":
        return ref.at[pl.ds(j * shape[0], shape[0])]
    if kind == "col":
        return ref.at[:, pl.ds(j * shape[1], shape[1])]
    return ref.at[j]


def _whole_shape(shape, kind):
    if kind == "row":
        return (N_CHIPS * shape[0],) + tuple(shape[1:])
    if kind == "col":
        return (shape[0], N_CHIPS * shape[1]) + tuple(shape[2:])
    return (N_CHIPS,) + tuple(shape)


def _dma_sems(*counts):
    return [pltpu.SemaphoreType.DMA((n,)) for n in counts]


class _Job:
    def __init__(self, ins, outs, sems, phases):
        self.ins, self.outs, self.sems, self.phases = list(ins), list(outs), list(sems), list(phases)

    def specs(self):
        return [_ANY] * len(self.ins), [_ANY] * len(self.outs), _dma_sems(*self.sems)

    def run_at(self, step, n_steps, in_refs, out_refs, sem_refs):
        last = len(self.phases) - 1
        for p, phase in enumerate(self.phases):
            @pl.when(step == (p * (n_steps - 1)) // max(last, 1))
            def _():
                phase(in_refs, out_refs, sem_refs)


def _run_job(job, name):
    def body(*refs):
        ni, no = len(job.ins), len(job.outs)
        for phase in job.phases:
            phase(refs[:ni], refs[ni:ni + no], refs[ni + no:])

    in_specs, out_specs, sems = job.specs()
    return pl.pallas_call(body, name=name, in_specs=in_specs, out_specs=out_specs, out_shape=job.outs,
                          scratch_shapes=sems)(*job.ins)


def _remote(src, dst, sems, i, j, peer):
    return pltpu.make_async_remote_copy(src_ref=src, dst_ref=dst, send_sem=sems[i].at[j], recv_sem=sems[i + 1].at[j],
                                        device_id=peer, device_id_type=MESH_ID)


def _gather_job(shards, kinds, layer):
    n = len(shards)
    shapes = [s.shape[1:] for s in shards]
    other = 1 - layer

    def window(outs, t, j):
        return _quarter(outs[t], j, shapes[t], kinds[t])

    def start(ins, outs, sems):
        mx, my, mc = _place()
        me = 2 * mx + my

        @pl.when(mc == layer)
        def _():
            for t in range(n):
                for k in range(1, N_CHIPS):
                    _remote(ins[t].at[layer], window(outs, t, me), sems, 0, 3 * t + k - 1,
                            (_flip(mx, k >> 1), _flip(my, k & 1), layer)).start()

        @pl.when(mc == other)
        def _():
            for t in range(n):
                _remote(ins[t].at[layer], window(outs, t, me), sems, 2, t, (mx, my, layer)).start()

    def relay(ins, outs, sems):
        mx, my, mc = _place()
        me = 2 * mx + my

        @pl.when(mc == layer)
        def _():
            for t in range(n):
                _remote(ins[t].at[layer], window(outs, t, me), sems, 2, t, (mx, my, other)).wait_recv()
                for k in range(1, N_CHIPS):
                    px, py = _flip(mx, k >> 1), _flip(my, k & 1)
                    cp = _remote(ins[t].at[layer], window(outs, t, 2 * px + py), sems, 0, 3 * t + k - 1, (px, py, layer))
                    cp.wait_recv()
                    cp.wait_send()
                _remote(outs[t], outs[t], sems, 4, t, (mx, my, other)).start()

        @pl.when(mc == other)
        def _():
            for t in range(n):
                _remote(ins[t].at[layer], window(outs, t, me), sems, 2, t, (mx, my, layer)).wait_send()

    def finish(ins, outs, sems):
        mx, my, mc = _place()

        @pl.when(mc == layer)
        def _():
            for t in range(n):
                _remote(outs[t], outs[t], sems, 4, t, (mx, my, other)).wait_send()

        @pl.when(mc == other)
        def _():
            for t in range(n):
                _remote(outs[t], outs[t], sems, 4, t, (mx, my, layer)).wait_recv()

    outs = [jax.ShapeDtypeStruct(_whole_shape(sh, kd), s.dtype) for s, sh, kd in zip(shards, shapes, kinds)]
    return _Job(shards, outs, [3 * n, 3 * n, n, n, n, n], [start, relay, finish])


def _scatter_job(gs, shapes, kinds, layer):
    n = len(gs)

    def copies(ins, outs, sems):
        mx, my, _ = _place()
        return [_remote(_quarter(ins[t], 2 * _flip(mx, k >> 1) + _flip(my, k & 1), shapes[t], kinds[t]), outs[t].at[k - 1],
                        sems, 0, 3 * t + k - 1, (_flip(mx, k >> 1), _flip(my, k & 1), layer))
                for t in range(n) for k in range(1, N_CHIPS)]

    def start(ins, outs, sems):
        @pl.when(lax.axis_index("c") == layer)
        def _():
            for cp in copies(ins, outs, sems):
                cp.start()

    def finish(ins, outs, sems):
        @pl.when(lax.axis_index("c") == layer)
        def _():
            for cp in copies(ins, outs, sems):
                cp.wait_recv()
                cp.wait_send()

    outs = [jax.ShapeDtypeStruct((N_CHIPS - 1,) + tuple(sh), g.dtype) for g, sh in zip(gs, shapes)]
    return _Job(gs, outs, [3 * n, 3 * n], [start, finish])


def _core_to_core(parts, sender, name):
    n = len(parts)

    def move(ins, outs, sems):
        mx, my, mc = _place()

        @pl.when(mc == sender)
        def _():
            copies = [_remote(ins[t], outs[t], sems, 0, t, (mx, my, 1 - sender)) for t in range(n)]
            for cp in copies:
                cp.start()
            for cp in copies:
                cp.wait_send()

        @pl.when(mc != sender)
        def _():
            for t in range(n):
                _remote(ins[t], outs[t], sems, 0, t, (mx, my, sender)).wait_recv()

    return _run_job(_Job(parts, [jax.ShapeDtypeStruct(p.shape, p.dtype) for p in parts], [n, n], [move]), name)


def _gather_all(x, name):
    r, w = x.shape

    def body(x_ref, o_ref, send_sems, recv_sems, local_sem):
        mx, my, mc = _place()
        mine = 4 * mx + 2 * my + mc
        local = pltpu.make_async_copy(x_ref, o_ref.at[mine], local_sem)
        local.start()
        copies = []
        for k in range(1, N_DEV):
            peer = (_flip(mx, k >> 2), _flip(my, (k >> 1) & 1), _flip(mc, k & 1))
            copies.append(pltpu.make_async_remote_copy(
                src_ref=x_ref, dst_ref=o_ref.at[mine], send_sem=send_sems.at[k - 1], recv_sem=recv_sems.at[k - 1],
                device_id=peer, device_id_type=MESH_ID))
            copies[-1].start()
        for k in range(1, N_DEV):
            peer = (_flip(mx, k >> 2), _flip(my, (k >> 1) & 1), _flip(mc, k & 1))
            pltpu.make_async_remote_copy(
                src_ref=x_ref, dst_ref=o_ref.at[4 * peer[0] + 2 * peer[1] + peer[2]], send_sem=send_sems.at[k - 1],
                recv_sem=recv_sems.at[k - 1], device_id=peer, device_id_type=MESH_ID).wait_recv()
        for cp in copies:
            cp.wait_send()
        local.wait()

    return pl.pallas_call(
        body, name=name, in_specs=[_ANY], out_specs=_ANY, out_shape=jax.ShapeDtypeStruct((N_DEV, r, w), x.dtype),
        scratch_shapes=[pltpu.SemaphoreType.DMA((N_DEV - 1,)), pltpu.SemaphoreType.DMA((N_DEV - 1,)),
                        pltpu.SemaphoreType.DMA],
    )(x)


def _pack(parts, dtype, row_unit):
    flat = jnp.concatenate([p.astype(dtype).reshape(-1) for p in parts])
    unit = row_unit * LANES
    pad = (-flat.shape[0]) % unit
    if pad:
        flat = jnp.concatenate([flat, jnp.zeros((pad,), dtype)])
    return flat.reshape(-1, LANES)


def _unpack(packed, shapes):
    flat = packed.reshape(-1)
    out, off = [], 0
    for s in shapes:
        n = math.prod(s)
        out.append(flat[off:off + n].reshape(s))
        off += n
    return out


def _f_rms(x, gain):
    return (_rms(x, gain),)


def _f_xattn(q, kv):
    d = q.shape[1]
    hd = d // XA_HEADS
    outs = []
    for h in range(XA_HEADS):
        qh, kh, vh = q[:, h * hd:(h + 1) * hd], kv[:, h * hd:(h + 1) * hd], kv[:, d + h * hd:d + (h + 1) * hd]
        s = _dlo(qh, kh, NT) * (hd ** -0.5)
        s = s - jnp.max(s, axis=-1, keepdims=True)
        p = jnp.exp(s)
        p = p / jnp.sum(p, axis=-1, keepdims=True)
        outs.append(_dlo(p, vh, NN))
    return (jnp.concatenate(outs, axis=1),)


def _f_gmlp(uv, gain, bias, w_sp, b_sp):
    r = uv.shape[0]
    act = jax.nn.gelu(uv.astype(F32))
    u, v = act[:, :SG_WIDTH], act[:, SG_WIDTH:]
    mu = jnp.mean(v, axis=-1, keepdims=True)
    var = jnp.mean(jnp.square(v - mu), axis=-1, keepdims=True)
    v = (v - mu) * lax.rsqrt(var + NORM_EPS) * gain + bias
    row = lax.broadcasted_iota(jnp.int32, (SG_CHUNK, SG_CHUNK), 0)
    col = lax.broadcasted_iota(jnp.int32, (SG_CHUNK, SG_CHUNK), 1)
    lane_grp = lax.broadcasted_iota(jnp.int32, (b_sp.shape[0], SG_WIDTH), 1) // SG_DIM
    grp_row = lax.broadcasted_iota(jnp.int32, (b_sp.shape[0], SG_WIDTH), 0)
    spread = jnp.where(lane_grp == grp_row, 1.0, 0.0).astype(F32)
    bias_t = _dhi(b_sp, spread, TN)
    chunks = []
    for c in range(r // SG_CHUNK):
        vc = v[c * SG_CHUNK:(c + 1) * SG_CHUNK]
        parts = []
        for g in range(SG_GROUPS):
            wg = jnp.where(row >= col, w_sp[g], 0.0)
            parts.append(_dlo(wg, vc[:, g * SG_DIM:(g + 1) * SG_DIM], NN))
        chunks.append(jnp.concatenate(parts, axis=1) + bias_t)
    mixed = jnp.concatenate(chunks, axis=0) if len(chunks) > 1 else chunks[0]
    return (u * mixed,)


def _f_swa_mix(o1, o2, o3, l1, l2, l3):
    outs = []
    for h in range(SWA_HEADS):
        ls = [l[:, h:h + 1] for l in (l1, l2, l3)]
        mx = jnp.maximum(jnp.maximum(ls[0], ls[1]), ls[2])
        es = [jnp.exp(l - mx) for l in ls]
        den = es[0] + es[1] + es[2]
        sl = slice(h * SWA_DIM, (h + 1) * SWA_DIM)
        outs.append((es[0] * o1[:, sl] + es[1] * o2[:, sl] + es[2] * o3[:, sl]) / den)
    return (jnp.concatenate(outs, axis=1),)


def _swa_block(q, kp, kc, vp, vc, first, window, dilation):
    span = window // dilation
    rows = SWA_HEADS * SWA_BLOCK
    ri = lax.broadcasted_iota(jnp.int32, (rows, 2 * SWA_BLOCK), 0)
    kj = lax.broadcasted_iota(jnp.int32, (rows, 2 * SWA_BLOCK), 1)
    head = ri // SWA_BLOCK
    rel = SWA_BLOCK + ri % SWA_BLOCK - kj
    valid = (rel >= 0) & (rel <= span) & jnp.logical_not(jnp.logical_and(first, kj < SWA_BLOCK))
    slope = jnp.exp((head + 1).astype(F32) * (-8.0 / SWA_HEADS * math.log(2.0)))
    bias = slope * (rel * dilation).astype(F32)
    kw = jnp.concatenate([kp, kc], axis=0)
    vw = jnp.concatenate([vp, vc], axis=0)
    lane_head = lax.broadcasted_iota(jnp.int32, (rows, SWA_WIDTH), 1) // SWA_DIM
    own_head = lane_head == lax.broadcasted_iota(jnp.int32, (rows, SWA_WIDTH), 0) // SWA_BLOCK
    q_rows = jnp.where(own_head, jnp.concatenate([q] * SWA_HEADS, axis=0), 0.0)
    s = _dlo(q_rows, kw, NT) * (SWA_DIM ** -0.5) - bias
    s = jnp.where(valid, s, -1e30)
    m = jnp.max(s, axis=-1, keepdims=True)
    p = jnp.exp(s - m)
    den = jnp.sum(p, axis=-1, keepdims=True)
    wide = _dlo(p, vw, NN) / den
    lse_rows = m + jnp.log(den)
    lane = lax.broadcasted_iota(jnp.int32, (SWA_BLOCK, LANES), 1)
    outs, lse = [], jnp.zeros((SWA_BLOCK, LANES), F32)
    for h in range(SWA_HEADS):
        outs.append(wide[h * SWA_BLOCK:(h + 1) * SWA_BLOCK, h * SWA_DIM:(h + 1) * SWA_DIM])
        lse = lse + jnp.where(lane == h, lse_rows[h * SWA_BLOCK:(h + 1) * SWA_BLOCK], 0.0)
    return jnp.concatenate(outs, axis=1), lse


@jax.custom_vjp
def _unit_lower_inv(lower):
    c = lower.shape[0]
    assert DN_CHUNK == 4 * INV_BLOCK and c % DN_CHUNK == 0
    row = lax.broadcasted_iota(jnp.int32, (c, c), 0)
    col = lax.broadcasted_iota(jnp.int32, (c, c), 1)
    eye = jnp.where(row == col, 1.0, 0.0).astype(F32)
    same = (row // INV_BLOCK) == (col // INV_BLOCK)
    pw = -jnp.where(same, lower, 0.0)
    d_inv = eye + pw
    for _ in range(int(math.log2(INV_BLOCK)) - 1):
        pw = _dlo(pw, pw, NN)
        d_inv = d_inv + _dlo(d_inv, pw, NN)
    n1 = _dlo(d_inv, jnp.where(same, 0.0, lower), NN)
    n2 = _dlo(n1, n1, NN)
    rough = _dlo(eye - n1 + n2 - _dlo(n1, n2, NN), d_inv, NN)
    residual = eye - _dhi(eye + lower, rough, NN)
    return rough + _dlo(rough, residual, NN)


def _unit_lower_inv_fwd(lower):
    t_inv = _unit_lower_inv(lower)
    return t_inv, t_inv


def _unit_lower_inv_bwd(t_inv, g):
    return (-_dlo(_dlo(t_inv, g, TN), t_inv, NT),)


_unit_lower_inv.defvjp(_unit_lower_inv_fwd, _unit_lower_inv_bwd)


def _dn_group(xx, z, ba, state, conv_w, a_log, dt_bias, gain):
    rows, c = z.shape[0], DN_CHUNK
    hc = DN_HEADS * c
    acc = conv_w[0:1] * xx[HALO - 3:HALO - 3 + rows]
    for j in range(1, DN_CONV):
        acc = acc + conv_w[j:j + 1] * xx[HALO - 3 + j:HALO - 3 + j + rows]
    qkv = _silu(acc)
    beta_all = _sigmoid(ba)
    g_all = -jnp.exp(a_log) * _softplus(ba + dt_bias)
    row = lax.broadcasted_iota(jnp.int32, (hc, hc), 0)
    col = lax.broadcasted_iota(jnp.int32, (hc, hc), 1)
    same_head = (row // c) == (col // c)
    incl, strict = same_head & (row >= col), same_head & (row > col)
    tri = jnp.where(incl[:c, :c], 1.0, 0.0).astype(F32)

    def stack(piece):
        return jnp.concatenate([piece(h) for h in range(DN_HEADS)], axis=0)

    local = []
    for ci in range(rows // c):
        r0 = ci * c
        gc_all = _dhi(tri, g_all[r0:r0 + c], NN)
        gc_t = gc_all.T
        q = stack(lambda h: qkv[r0:r0 + c, h * DN_DIM:(h + 1) * DN_DIM])
        k = stack(lambda h: qkv[r0:r0 + c, DN_WIDTH + h * DN_DIM:DN_WIDTH + (h + 1) * DN_DIM])
        v = stack(lambda h: qkv[r0:r0 + c, 2 * DN_WIDTH + h * DN_DIM:2 * DN_WIDTH + (h + 1) * DN_DIM])
        q = q * lax.rsqrt(jnp.sum(q * q, axis=-1, keepdims=True) + NORM_EPS) * (DN_DIM ** -0.5)
        k = k * lax.rsqrt(jnp.sum(k * k, axis=-1, keepdims=True) + NORM_EPS)
        beta = stack(lambda h: beta_all[r0:r0 + c, h:h + 1])
        gc = stack(lambda h: gc_all[:, DN_HEADS + h:DN_HEADS + h + 1])
        g_last = stack(lambda h: jnp.broadcast_to(gc_all[c - 1:c, DN_HEADS + h:DN_HEADS + h + 1], (c, 1)))
        gc_row = jnp.concatenate([gc_t[DN_HEADS + h:DN_HEADS + h + 1, :] for h in range(DN_HEADS)], axis=1)
        decay = jnp.where(incl, jnp.exp(jnp.where(incl, gc - gc_row, 0.0)), 0.0)
        kb = k * beta
        t_inv = _unit_lower_inv(jnp.where(strict, _dlo(kb, k, NT) * decay, 0.0))
        e_gc = jnp.exp(gc)
        u = _dlo(t_inv, v * beta, NN)
        w = _dlo(t_inv, kb * e_gc, NN)
        a_qk = jnp.where(incl, _dlo(q, k, NT) * decay, 0.0)
        e_last = jnp.concatenate([jnp.broadcast_to(jnp.exp(gc_all[c - 1:c, DN_HEADS + h:DN_HEADS + h + 1]), (1, DN_DIM))
                                  for h in range(DN_HEADS)], axis=1)
        local.append((jnp.concatenate([w, q * e_gc], axis=0), k * jnp.exp(g_last - gc), u, a_qk, e_last))
    own = (lax.broadcasted_iota(jnp.int32, (hc, DN_WIDTH), 0) // c) == (lax.broadcasted_iota(jnp.int32, (hc, DN_WIDTH), 1) // DN_DIM)

    def own_blocks(m):
        return stack(lambda h: m[h * c:(h + 1) * c, h * DN_DIM:(h + 1) * DN_DIM])

    s = state
    out_rows = []
    for ci in range(rows // c):
        r0 = ci * c
        wq, k_tail, u, a_qk, e_last = local[ci]
        through = _dlo(wq, s, NN)
        v_new = u - own_blocks(through[:hc])
        o = own_blocks(through[hc:]) + _dlo(a_qk, v_new, NN)
        v_wide = jnp.where(own, jnp.concatenate([v_new] * DN_HEADS, axis=1), 0.0)
        s = s * e_last + _dlo(k_tail, v_wide, TN)
        o = o * lax.rsqrt(jnp.mean(o * o, axis=-1, keepdims=True) + NORM_EPS) * gain
        o = jnp.concatenate([o[h * c:(h + 1) * c] for h in range(DN_HEADS)], axis=1)
        out_rows.append(o * _silu(z[r0:r0 + c]))
    out = jnp.concatenate(out_rows, axis=0) if len(out_rows) > 1 else out_rows[0]
    return out, s


def _dn_specs(n_of, rows):
    return [pl.BlockSpec((rows, 3 * DN_WIDTH), lambda i: (n_of(i), 0)),
            pl.BlockSpec((HALO, 3 * DN_WIDTH), lambda i: (jnp.maximum(n_of(i) * (rows // HALO) - 1, 0), 0)),
            pl.BlockSpec((rows, DN_WIDTH), lambda i: (n_of(i), 0)),
            pl.BlockSpec((rows, LANES), lambda i: (n_of(i), 0))]


def _no_job():
    return _Job([], [], [], [])


def _dn_forward(xq, xz, xba, params, name, job=None):
    t = xq.shape[0]
    rows = min(DN_GROUP * DN_CHUNK, t)
    n_groups = t // rows
    job = job or _no_job()
    job_in, job_out, job_sems = job.specs()

    def body(*refs):
        x_ref, halo_ref, z_ref, ba_ref, cw_ref, al_ref, dt_ref, gn_ref = refs[:8]
        ji, jo, js = 8 + len(job_in), 10 + len(job_in) + len(job_out), 11 + len(job_in) + len(job_out)
        o_ref, s_all_ref, s_ref = refs[ji], refs[ji + 1], refs[jo]
        n = pl.program_id(0)
        job.run_at(n, n_groups, refs[8:ji], refs[ji + 2:jo], refs[js:])

        @pl.when(n == 0)
        def _():
            s_ref[...] = jnp.zeros_like(s_ref)

        halo = jnp.where(n > 0, halo_ref[...], 0.0)
        xx = jnp.concatenate([halo, x_ref[...]], axis=0)
        s_all_ref[0] = s_ref[...]
        o, s_new = _dn_group(xx, z_ref[...], ba_ref[...], s_ref[...], cw_ref[...], al_ref[...], dt_ref[...], gn_ref[...])
        o_ref[...] = o.astype(o_ref.dtype)
        s_ref[...] = s_new

    assert len(params) == 4
    res = pl.pallas_call(
        body, name=name, grid=(n_groups,),
        in_specs=_dn_specs(lambda i: i, rows) + [_full_spec(p) for p in params] + job_in,
        out_specs=[pl.BlockSpec((rows, DN_WIDTH), lambda i: (i, 0)),
                   pl.BlockSpec((1, DN_DIM, DN_WIDTH), lambda i: (i, 0, 0))] + job_out,
        out_shape=[jax.ShapeDtypeStruct((t, DN_WIDTH), BF16),
                   jax.ShapeDtypeStruct((n_groups, DN_DIM, DN_WIDTH), F32)] + job.outs,
        scratch_shapes=[pltpu.VMEM((DN_DIM, DN_WIDTH), F32)] + job_sems,
        compiler_params=_cparams(("arbitrary",)),
    )(xq, xq, xz, xba, *params, *job.ins)
    return res[0], res[1], list(res[2:])


def _dn_backward(xq, xz, xba, params, s_all, d_out, name, job=None):
    t = xq.shape[0]
    rows = min(DN_GROUP * DN_CHUNK, t)
    n_groups = t // rows
    rev = lambda i: n_groups - 1 - i
    job = job or _no_job()
    job_in, job_out, job_sems = job.specs()

    def body(*refs):
        x_ref, halo_ref, z_ref, ba_ref, cw_ref, al_ref, dt_ref, gn_ref, s_ref, do_ref = refs[:10]
        ji = 10 + len(job_in)
        dx_ref, dz_ref, dba_ref, dcw_ref, dal_ref, ddt_ref, dgn_ref = refs[ji:ji + 7]
        jo = ji + 7 + len(job_out)
        ds_ref, dhalo_ref = refs[jo], refs[jo + 1]
        i = pl.program_id(0)
        n = n_groups - 1 - i
        job.run_at(i, n_groups, refs[10:ji], refs[ji + 7:jo], refs[jo + 2:])

        @pl.when(i == 0)
        def _():
            ds_ref[...] = jnp.zeros_like(ds_ref)
            dhalo_ref[...] = jnp.zeros_like(dhalo_ref)
            for r in (dcw_ref, dal_ref, ddt_ref, dgn_ref):
                r[...] = jnp.zeros_like(r)

        halo = jnp.where(n > 0, halo_ref[...], 0.0)
        xx = jnp.concatenate([halo, x_ref[...]], axis=0)
        _, pull = jax.vjp(_dn_group, xx, z_ref[...], ba_ref[...], s_ref[0], cw_ref[...], al_ref[...], dt_ref[...],
                          gn_ref[...])
        dxx, dz, dba, ds, dcw, dal, ddt, dgn = pull((do_ref[...].astype(F32), ds_ref[...]))
        dx_ref[...] = jnp.concatenate([dxx[HALO:rows], dxx[rows:] + dhalo_ref[...]], axis=0).astype(dx_ref.dtype)
        dhalo_ref[...] = dxx[:HALO]
        dz_ref[...] = dz.astype(dz_ref.dtype)
        dba_ref[...] = dba.astype(dba_ref.dtype)
        ds_ref[...] = ds
        dcw_ref[...] += dcw
        dal_ref[...] += dal
        ddt_ref[...] += ddt
        dgn_ref[...] += dgn

    assert len(params) == 4
    res = pl.pallas_call(
        body, name=name, grid=(n_groups,),
        in_specs=_dn_specs(rev, rows) + [_full_spec(p) for p in params]
        + [pl.BlockSpec((1, DN_DIM, DN_WIDTH), lambda i: (rev(i), 0, 0)),
           pl.BlockSpec((rows, DN_WIDTH), lambda i: (rev(i), 0))] + job_in,
        out_specs=[pl.BlockSpec((rows, 3 * DN_WIDTH), lambda i: (rev(i), 0)),
                   pl.BlockSpec((rows, DN_WIDTH), lambda i: (rev(i), 0)),
                   pl.BlockSpec((rows, LANES), lambda i: (rev(i), 0))] + [_full_spec(p) for p in params] + job_out,
        out_shape=[jax.ShapeDtypeStruct(xq.shape, BF16), jax.ShapeDtypeStruct(xz.shape, BF16),
                   jax.ShapeDtypeStruct(xba.shape, BF16)] + [jax.ShapeDtypeStruct(p.shape, F32) for p in params] + job.outs,
        scratch_shapes=[pltpu.VMEM((DN_DIM, DN_WIDTH), F32), pltpu.VMEM((HALO, 3 * DN_WIDTH), F32)] + job_sems,
        compiler_params=_cparams(("arbitrary",)),
    )(xq, xq, xz, xba, *params, s_all, d_out, *job.ins)
    return tuple(res[:7]), list(res[7:])


def _swa_forward(xs, window, dilation, name):
    t = xs.shape[0]
    d, l = dilation, t // dilation
    nb = l // SWA_BLOCK
    view = xs.reshape(l, d * 3 * SWA_WIDTH)
    blk = (SWA_BLOCK, SWA_WIDTH)

    def body(q_ref, kp_ref, kc_ref, vp_ref, vc_ref, o_ref, l_ref):
        blocks = [r[...].astype(F32) for r in (q_ref, kp_ref, kc_ref, vp_ref, vc_ref)]
        o, lse = _swa_block(*blocks, pl.program_id(1) == 0, window, dilation)
        o_ref[...] = o
        l_ref[...] = lse

    prev = lambda n: jnp.maximum(n - 1, 0)
    o, lse = pl.pallas_call(
        body, name=name, grid=(d, nb),
        in_specs=[pl.BlockSpec(blk, lambda r, n: (n, 3 * r)), pl.BlockSpec(blk, lambda r, n: (prev(n), 3 * r + 1)),
                  pl.BlockSpec(blk, lambda r, n: (n, 3 * r + 1)), pl.BlockSpec(blk, lambda r, n: (prev(n), 3 * r + 2)),
                  pl.BlockSpec(blk, lambda r, n: (n, 3 * r + 2))],
        out_specs=[pl.BlockSpec(blk, lambda r, n: (n, r)), pl.BlockSpec((SWA_BLOCK, LANES), lambda r, n: (n, r))],
        out_shape=[jax.ShapeDtypeStruct((l, d * SWA_WIDTH), F32), jax.ShapeDtypeStruct((l, d * LANES), F32)],
        compiler_params=_cparams(("parallel", "parallel")),
    )(view, view, view, view, view)
    return o.reshape(t, SWA_WIDTH), lse.reshape(t, LANES)


def _swa_backward(xs, d_o, d_lse, acc, window, dilation, name):
    t = xs.shape[0]
    d, l = dilation, t // dilation
    nb = l // SWA_BLOCK
    view = xs.reshape(l, d * 3 * SWA_WIDTH)
    blk = (SWA_BLOCK, SWA_WIDTH)
    has_acc = acc is not None

    def body(*refs):
        q_ref, kp_ref, kc_ref, vp_ref, vc_ref, do_ref, dl_ref = refs[:7]
        acc_refs = refs[7:10] if has_acc else None
        dq_ref, dk_ref, dv_ref, ck_ref, cv_ref = refs[-5:]
        i = pl.program_id(1)
        n = nb - 1 - i

        @pl.when(i == 0)
        def _():
            ck_ref[...] = jnp.zeros_like(ck_ref)
            cv_ref[...] = jnp.zeros_like(cv_ref)

        f = functools.partial(_swa_block, first=n == 0, window=window, dilation=dilation)
        _, pull = jax.vjp(f, *[r[...].astype(F32) for r in (q_ref, kp_ref, kc_ref, vp_ref, vc_ref)])
        dq, dkp, dkc, dvp, dvc = pull((do_ref[...], dl_ref[...]))
        dk = dkc + ck_ref[...]
        dv = dvc + cv_ref[...]
        if has_acc:
            dq, dk, dv = dq + acc_refs[0][...], dk + acc_refs[1][...], dv + acc_refs[2][...]
        dq_ref[...] = dq
        dk_ref[...] = dk
        dv_ref[...] = dv
        ck_ref[...] = dkp
        cv_ref[...] = dvp

    cur = lambda i: nb - 1 - i
    prev = lambda i: jnp.maximum(nb - 2 - i, 0)
    own = pl.BlockSpec(blk, lambda r, i: (cur(i), r))
    accs = [a.reshape(l, d * SWA_WIDTH) for a in acc] if has_acc else []
    outs = pl.pallas_call(
        body, name=name, grid=(d, nb),
        in_specs=[pl.BlockSpec(blk, lambda r, i: (cur(i), 3 * r)), pl.BlockSpec(blk, lambda r, i: (prev(i), 3 * r + 1)),
                  pl.BlockSpec(blk, lambda r, i: (cur(i), 3 * r + 1)), pl.BlockSpec(blk, lambda r, i: (prev(i), 3 * r + 2)),
                  pl.BlockSpec(blk, lambda r, i: (cur(i), 3 * r + 2)), own,
                  pl.BlockSpec((SWA_BLOCK, LANES), lambda r, i: (cur(i), r))] + [own] * len(accs),
        out_specs=[own] * 3, out_shape=[jax.ShapeDtypeStruct((l, d * SWA_WIDTH), F32)] * 3,
        scratch_shapes=[pltpu.VMEM(blk, F32), pltpu.VMEM(blk, F32)],
        compiler_params=_cparams(("parallel", "arbitrary")),
    )(view, view, view, view, view, d_o.reshape(l, d * SWA_WIDTH), d_lse.reshape(l, d * LANES), *accs)
    return tuple(o.reshape(t, SWA_WIDTH) for o in outs)


def _loss_head(h, target, gain, name, tile=256):
    t, d = h.shape
    tile = min(tile, t)

    def body(h_ref, t_ref, g_ref, loss_ref, dh_ref, dg_ref):
        def f(hv, gv):
            err = _rms(hv, gv) - t_ref[...]
            return 0.5 * jnp.sum(jnp.mean(err * err, axis=-1, keepdims=True), axis=0, keepdims=True)

        val, pull = jax.vjp(f, h_ref[...], g_ref[...])
        dh, dg = pull(jnp.ones((1, 1), F32))
        dh_ref[...] = dh

        @pl.when(pl.program_id(0) == 0)
        def _():
            loss_ref[...] = jnp.zeros_like(loss_ref)
            dg_ref[...] = jnp.zeros_like(dg_ref)

        loss_ref[...] += jnp.broadcast_to(val, loss_ref.shape)
        dg_ref[...] += dg

    return pl.pallas_call(
        body, name=name, grid=(t // tile,),
        in_specs=[pl.BlockSpec((tile, d), lambda i: (i, 0)), pl.BlockSpec((tile, d), lambda i: (i, 0)), _full_spec(gain)],
        out_specs=[pl.BlockSpec((1, LANES), lambda i: (0, 0)), pl.BlockSpec((tile, d), lambda i: (i, 0)), _full_spec(gain)],
        out_shape=[jax.ShapeDtypeStruct((1, LANES), F32), jax.ShapeDtypeStruct((t, d), F32),
                   jax.ShapeDtypeStruct(gain.shape, F32)],
        compiler_params=_cparams(("arbitrary",)),
    )(h, target, gain)


def _split_w_in(w_in):
    cuts = [0]
    for s in IN_SIZES:
        cuts.append(cuts[-1] + s)
    qkv, z = w_in[:, cuts[0]:cuts[1]], w_in[:, cuts[1]:cuts[2]]
    ba = jnp.pad(w_in[:, cuts[2]:cuts[4]], ((0, 0), (0, LANES - 2 * DN_HEADS)))
    return qkv, z, ba, w_in[:, cuts[4]:cuts[5]], w_in[:, cuts[5]:cuts[6]]


def _lane_pad(v, offset):
    return jnp.pad(v.reshape(1, -1), ((0, 0), (offset, LANES - offset - v.shape[0])))


def _layer_params(sm, i, conv_w):
    return dict(
        ffn1_norm=sm["ffn1_norm"][i][None], mix_norm=sm["mix_norm"][i][None], xa_norm=sm["xa_norm"][i][None],
        xa_mem_norm=sm["xa_mem_norm"][i][None], ffn2_norm=sm["ffn2_norm"][i][None],
        dn=(conv_w, _lane_pad(sm["dn_a_log"][i], DN_HEADS), _lane_pad(sm["dn_dt_bias"][i], DN_HEADS),
            sm["dn_out_norm"][i][None]),
        sg=(sm["sg_norm_gain"][i][None], sm["sg_norm_bias"][i][None], sm["sg_w_spatial"][i],
            jnp.pad(sm["sg_b_spatial"][i], ((0, 8 - SG_GROUPS), (0, 0)))),
    )


def _ffn_fwd(h, gain, w_gu, w_d, tag):
    n = _rows(_f_rms, [h], [gain], [(h.shape[1], BF16)], tile=512, name=f"{tag}_norm")[0]
    gu = _mm(n, w_gu, NN, out_dtype=BF16, name=f"{tag}_gate_up")
    out = _mm(gu, w_d, NN, out_dtype=F32, res=h, scale=0.5, swiglu_a=True, name=f"{tag}_down")
    return out, (h, n, gu)


def _ffn_bwd(dh, saved, gain, w_gu, w_d, tag):
    h, n, gu = saved
    f = w_d.shape[0]
    d_gate, d_up = _mm_swiglu_bwd(dh, w_d, gu, 0.5, name=f"{tag}_down_dx")
    dw_d = _mm(gu, dh, TN, out_dtype=BF16, scale=0.5, swiglu_a=True, name=f"{tag}_down_dw")
    dn = _mm(d_gate, w_gu, NT, out_dtype=F32, name=f"{tag}_gate_dx")
    dn = _mm(d_up, w_gu, NT, out_dtype=F32, res=dn, b_k0=f, name=f"{tag}_up_dx")
    dw_gu = jnp.concatenate([_mm(n, d_gate, TN, out_dtype=BF16, name=f"{tag}_gate_dw"),
                             _mm(n, d_up, TN, out_dtype=BF16, name=f"{tag}_up_dw")], axis=1)
    dh_in, dgain = _rows_vjp(_f_rms, [h], [gain], [dn], diff=[True], grad_dtypes=[F32], tile=512, add=[dh],
                             name=f"{tag}_norm_bwd")
    return dh_in, dgain, dw_gu, dw_d


def _mixer_fwd(h, p, w_in, w_out, tag, job=None):
    d = h.shape[1]
    n = _rows(_f_rms, [h], [p["mix_norm"]], [(d, BF16)], tile=512, name=f"{tag}_norm")[0]
    w_parts = _split_w_in(w_in)
    xq, xz, xba, xs, xg = (_mm(n, w, NN, out_dtype=BF16 if j == 3 else F32, name=f"{tag}_in{j}") for j, w in enumerate(w_parts))
    oa, s_all, job_out = _dn_forward(xq, xz, xba, p["dn"], name=f"{tag}_dn", job=job)
    swa = [_swa_forward(xs, wnd, dil, name=f"{tag}_swa{j}") for j, (wnd, dil) in enumerate(SWA_PATTERNS)]
    ob = _rows(_f_swa_mix, [o for o, _ in swa] + [l for _, l in swa], [], [(SWA_WIDTH, BF16)], tile=512,
               name=f"{tag}_swa_mix")[0]
    oc = _rows(_f_gmlp, [xg], list(p["sg"]), [(SG_WIDTH, BF16)], tile=256, name=f"{tag}_gmlp")[0]
    merged = jnp.concatenate([oa, ob, oc], axis=1)
    out = _mm(merged, w_out, NN, out_dtype=F32, res=h, name=f"{tag}_out")
    return out, (h, n, xq, xz, xba, xs, xg, s_all, swa, merged), job_out


def _mixer_bwd(dh, saved, p, w_in, w_out, tag, job=None):
    h, n, xq, xz, xba, xs, xg, s_all, swa, merged = saved
    dw_out = _mm(merged, dh, TN, out_dtype=BF16, name=f"{tag}_out_dw")
    doa = _mm(dh, w_out[:DN_WIDTH], NT, out_dtype=F32, name=f"{tag}_out_dxa")
    dob = _mm(dh, w_out[DN_WIDTH:DN_WIDTH + SWA_WIDTH], NT, out_dtype=F32, name=f"{tag}_out_dxb")
    doc = _mm(dh, w_out[DN_WIDTH + SWA_WIDTH:], NT, out_dtype=F32, name=f"{tag}_out_dxc")
    res = _rows_vjp(_f_gmlp, [xg], list(p["sg"]), [doc], diff=[True], grad_dtypes=[BF16], tile=256, name=f"{tag}_gmlp_bwd")
    dxg, d_sg = res[0], res[1:]
    mix_in = [o for o, _ in swa] + [l for _, l in swa]
    d_mix = _rows_vjp(_f_swa_mix, mix_in, [], [dob], diff=[True] * 6, grad_dtypes=[F32] * 6, tile=512,
                      name=f"{tag}_swa_mix_bwd")
    acc = None
    for j, (wnd, dil) in enumerate(SWA_PATTERNS):
        acc = _swa_backward(xs, d_mix[j], d_mix[3 + j], acc, wnd, dil, name=f"{tag}_swa{j}_bwd")
    dxs = jnp.concatenate([a.astype(BF16) for a in acc], axis=1)
    res, job_out = _dn_backward(xq, xz, xba, p["dn"], s_all, doa, name=f"{tag}_dn_bwd", job=job)
    (dxq, dxz, dxba), d_dn = res[:3], res[3:]
    w_parts = _split_w_in(w_in)
    dn = None
    dws = []
    for j, (dx, w) in enumerate(zip((dxq, dxz, dxba, dxs, dxg), w_parts, strict=True)):
        dn = _mm(dx, w, NT, out_dtype=F32, res=dn, name=f"{tag}_in{j}_dx")
        dws.append(_mm(n, dx, TN, out_dtype=BF16, name=f"{tag}_in{j}_dw"))
    dws[2] = dws[2][:, :2 * DN_HEADS]
    dw_in = jnp.concatenate(dws, axis=1)
    dh_in, dgain = _rows_vjp(_f_rms, [h], [p["mix_norm"]], [dn], diff=[True], grad_dtypes=[F32], tile=512, add=[dh],
                             name=f"{tag}_norm_bwd")
    return dh_in, dgain, dw_in, dw_out, d_dn, d_sg, job_out


def _xattn_fwd(h, mem, p, w_q, w_kv, w_o, tag):
    d = h.shape[1]
    n = _rows(_f_rms, [h], [p["xa_norm"]], [(d, BF16)], tile=512, name=f"{tag}_norm")[0]
    mn = _rows(_f_rms, [mem], [p["xa_mem_norm"]], [(d, BF16)], tile=512, name=f"{tag}_mem_norm")[0]
    q = _mm(n, w_q, NN, out_dtype=BF16, name=f"{tag}_q")
    kv = _mm(mn, w_kv, NN, out_dtype=BF16, name=f"{tag}_kv")
    o = _rows(_f_xattn, [q], [kv], [(d, BF16)], tile=256, name=f"{tag}_core")[0]
    out = _mm(o, w_o, NN, out_dtype=F32, res=h, name=f"{tag}_o")
    return out, (h, n, mn, q, kv, o)


def _xattn_bwd(dh, saved, mem, p, w_q, w_kv, w_o, tag):
    h, n, mn, q, kv, o = saved
    do = _mm(dh, w_o, NT, out_dtype=BF16, name=f"{tag}_o_dx")
    dw_o = _mm(o, dh, TN, out_dtype=BF16, name=f"{tag}_o_dw")
    dq, dkv = _rows_vjp(_f_xattn, [q], [kv], [do], diff=[True], grad_dtypes=[BF16], tile=256, name=f"{tag}_core_bwd")
    dn = _mm(dq, w_q, NT, out_dtype=F32, name=f"{tag}_q_dx")
    dw_q = _mm(n, dq, TN, out_dtype=BF16, name=f"{tag}_q_dw")
    dmn = _mm(dkv, w_kv, NT, out_dtype=F32, name=f"{tag}_kv_dx")
    dw_kv = _mm(mn, dkv, TN, out_dtype=BF16, name=f"{tag}_kv_dw")
    dmem_gain = _rows_vjp(_f_rms, [mem], [p["xa_mem_norm"]], [dmn], diff=[False], grad_dtypes=[], tile=512,
                          name=f"{tag}_mem_norm_bwd")[0]
    dh_in, dgain = _rows_vjp(_f_rms, [h], [p["xa_norm"]], [dn], diff=[True], grad_dtypes=[F32], tile=512, add=[dh],
                             name=f"{tag}_norm_bwd")
    return dh_in, dgain, dmem_gain, dw_q, dw_kv, dw_o


def kernel(x, mem, ffn1_norm, ffn1_w_gate_up, ffn1_w_down, mix_norm, mix_w_in, dn_conv_w, dn_a_log, dn_dt_bias, dn_out_norm, sg_norm_gain, sg_norm_bias, sg_w_spatial, sg_b_spatial, mix_w_out, xa_norm, xa_mem_norm, xa_w_q, xa_w_kv, xa_w_o, ffn2_norm, ffn2_w_gate_up, ffn2_w_down, final_norm, loss_target, m_ffn1_norm, m_ffn1_w_gate_up, m_ffn1_w_down, m_mix_norm, m_mix_w_in, m_dn_conv_w, m_dn_a_log, m_dn_dt_bias, m_dn_out_norm, m_sg_norm_gain, m_sg_norm_bias, m_sg_w_spatial, m_sg_b_spatial, m_mix_w_out, m_xa_norm, m_xa_mem_norm, m_xa_w_q, m_xa_w_kv, m_xa_w_o, m_ffn2_norm, m_ffn2_w_gate_up, m_ffn2_w_down, m_final_norm, v_ffn1_norm, v_ffn1_w_gate_up, v_ffn1_w_down, v_mix_norm, v_mix_w_in, v_dn_conv_w, v_dn_a_log, v_dn_dt_bias, v_dn_out_norm, v_sg_norm_gain, v_sg_norm_bias, v_sg_w_spatial, v_sg_b_spatial, v_mix_w_out, v_xa_norm, v_xa_mem_norm, v_xa_w_q, v_xa_w_kv, v_xa_w_o, v_ffn2_norm, v_ffn2_w_gate_up, v_ffn2_w_down, v_final_norm):
    args = dict(locals())
    wts = {k: args[k] for k in WEIGHTS}
    mom_m = {k: args["m_" + k] for k in WEIGHTS}
    mom_v = {k: args["v_" + k] for k in WEIGHTS}
    depth = ffn1_norm.shape[0]
    h = x[0]
    mem2 = mem[0]
    target = loss_target[0]

    assert depth == 2, "core c of a chip is responsible for layer c in the weight and gradient exchanges"
    kinds = {k: "stack" if k == "mix_w_in" else ("row" if BIG_AXIS[k] == 1 else "col") for k in BIG}
    shards = [wts[k].astype(BF16) for k in BIG] + [dn_conv_w]
    shard_kinds = [kinds[k] for k in BIG] + ["stack"]

    def layer_weights(whole):
        w = dict(zip(BIG, whole[:-1], strict=True))
        w["mix_w_in"] = jnp.concatenate([w["mix_w_in"][j] for j in range(N_CHIPS)], axis=1)
        w["dn_conv_w"] = jnp.concatenate([whole[-1][j] for j in range(N_CHIPS)], axis=1)
        return w

    lw = [layer_weights(_run_job(_gather_job(shards, shard_kinds, 0), name="gather_layer0")), None]
    small = {k: wts[k] for k in SMALL}

    saved = []
    for i in range(depth):
        w = lw[i]
        p = _layer_params(small, i, w["dn_conv_w"])
        h, s1 = _ffn_fwd(h, p["ffn1_norm"], w["ffn1_w_gate_up"], w["ffn1_w_down"], f"l{i}_ffn1")
        h, s2, fetched = _mixer_fwd(h, p, w["mix_w_in"], w["mix_w_out"], f"l{i}_mix",
                                    job=_gather_job(shards, shard_kinds, 1) if i == 0 else None)
        if i == 0:
            lw[1] = layer_weights(fetched)
        h, s3 = _xattn_fwd(h, mem2, p, w["xa_w_q"], w["xa_w_kv"], w["xa_w_o"], f"l{i}_xa")
        h, s4 = _ffn_fwd(h, p["ffn2_norm"], w["ffn2_w_gate_up"], w["ffn2_w_down"], f"l{i}_ffn2")
        saved.append((p, s1, s2, s3, s4))
    loss_part, dh, d_final = _loss_head(h, target, final_norm[None], name="loss_head")
    loss = lax.psum(loss_part[0, 0], ("x", "y", "c"))

    g_big = {k: [None] * depth for k in BIG}
    g_small = {k: [None] * depth for k in SMALL if k != "final_norm"}
    g_small["dn_conv_w"] = [None] * depth
    def layer_grad(k, i):
        g = g_big[k][i]
        if kinds[k] == "stack":
            n = wts[k].shape[2]
            g = jnp.stack([g[:, j * n:(j + 1) * n] for j in range(N_CHIPS)])
        return g

    def chip_sums(i):
        mine = [layer_grad(k, i) for k in BIG]
        got = _core_to_core(mine, 1 - i, name=f"give_layer{i}")
        return [_sum2(a, b, BF16, name=f"sum_cores{i}_{k}") for k, a, b in zip(BIG, mine, got, strict=True)]

    quarter_shapes, big_kinds = [wts[k].shape[1:] for k in BIG], [kinds[k] for k in BIG]
    chip_sum, recv = [None] * depth, [None] * depth
    for i in reversed(range(depth)):
        p, s1, s2, s3, s4 = saved[i]
        w = lw[i]
        job = None
        if i == 0:
            chip_sum[1] = chip_sums(1)
            job = _scatter_job(chip_sum[1], quarter_shapes, big_kinds, 1)
        dh, dg, dw_gu, dw_d = _ffn_bwd(dh, s4, p["ffn2_norm"], w["ffn2_w_gate_up"], w["ffn2_w_down"], f"l{i}_ffn2")
        g_small["ffn2_norm"][i], g_big["ffn2_w_gate_up"][i], g_big["ffn2_w_down"][i] = dg[0], dw_gu, dw_d
        dh, dg, dmg, dw_q, dw_kv, dw_o = _xattn_bwd(dh, s3, mem2, p, w["xa_w_q"], w["xa_w_kv"], w["xa_w_o"], f"l{i}_xa")
        g_small["xa_norm"][i], g_small["xa_mem_norm"][i] = dg[0], dmg[0]
        g_big["xa_w_q"][i], g_big["xa_w_kv"][i], g_big["xa_w_o"][i] = dw_q, dw_kv, dw_o
        dh, dg, dw_in, dw_out, d_dn, d_sg, arrived = _mixer_bwd(dh, s2, p, w["mix_w_in"], w["mix_w_out"], f"l{i}_mix", job=job)
        if i == 0:
            recv[1] = arrived
        g_small["mix_norm"][i], g_big["mix_w_in"][i], g_big["mix_w_out"][i] = dg[0], dw_in, dw_out
        g_small["dn_conv_w"][i] = d_dn[0]
        g_small["dn_a_log"][i] = d_dn[1][0, DN_HEADS:2 * DN_HEADS]
        g_small["dn_dt_bias"][i] = d_dn[2][0, DN_HEADS:2 * DN_HEADS]
        g_small["dn_out_norm"][i] = d_dn[3][0]
        g_small["sg_norm_gain"][i], g_small["sg_norm_bias"][i] = d_sg[0][0], d_sg[1][0]
        g_small["sg_w_spatial"][i], g_small["sg_b_spatial"][i] = d_sg[2], d_sg[3][:SG_GROUPS]
        dh, dg, dw_gu, dw_d = _ffn_bwd(dh, s1, p["ffn1_norm"], w["ffn1_w_gate_up"], w["ffn1_w_down"], f"l{i}_ffn1")
        g_small["ffn1_norm"][i], g_big["ffn1_w_gate_up"][i], g_big["ffn1_w_down"][i] = dg[0], dw_gu, dw_d
    grad_x = dh[None]
    g_small = {k: jnp.stack(v) for k, v in g_small.items()}
    g_small["final_norm"] = d_final[0]

    core = lax.axis_index("c")
    chip = 2 * lax.axis_index("x") + lax.axis_index("y")
    chip_sum[0] = chip_sums(0)
    recv[0] = _run_job(_scatter_job(chip_sum[0], quarter_shapes, big_kinds, 0), name="scatter_layer0")

    def own_quarter(k, g):
        if kinds[k] == "stack":
            return lax.dynamic_index_in_dim(g, chip, axis=0, keepdims=False)
        axis = BIG_AXIS[k] - 1
        n = wts[k].shape[BIG_AXIS[k]]
        return lax.dynamic_slice_in_dim(g, chip * n, n, axis=axis)

    finished = []
    for i in range(depth):
        parts = [_sum_slots(r, F32, name=f"sum_chips{i}_{k}", first=own_quarter(k, g))
                 for k, r, g in zip(BIG, recv[i], chip_sum[i], strict=True)]
        moved = _core_to_core(parts, i, name=f"final_layer{i}")
        finished.append([jnp.where(core == i, a, b) for a, b in zip(parts, moved, strict=True)])
    g_fin = {k: jnp.stack([finished[0][t], finished[1][t]]) for t, k in enumerate(BIG)}
    small_names = list(SMALL) + ["dn_conv_w"]
    small_shapes = [g_small[k].shape for k in small_names]
    small_sum = _sum_slots(_gather_all(_pack([g_small[k] for k in small_names], F32, 64), name="gather_small"), F32,
                           name="sum_small")
    gs = dict(zip(small_names, _unpack(small_sum, small_shapes), strict=True))
    n_conv = dn_conv_w.shape[2]
    gs["dn_conv_w"] = lax.dynamic_slice_in_dim(gs["dn_conv_w"], (2 * lax.axis_index("x") + lax.axis_index("y")) * n_conv,
                                               n_conv, axis=2)

    results = {}
    for k in BIG:
        shp = wts[k].shape
        two_d = lambda a, _s=shp: a.reshape(-1, _s[-1])
        res = _adamw(two_d(wts[k]), two_d(g_fin[k]), two_d(mom_m[k]), two_d(mom_v[k]), name=f"adamw_{k}")
        results[k] = [g_fin[k]] + [r.reshape(shp) for r in res]
    sm_shapes = [wts[k].shape for k in small_names]
    pk = lambda d: _pack([d[k] for k in small_names], F32, 64)
    res = [_unpack(r, sm_shapes) for r in _adamw(pk(wts), pk(gs), pk(mom_m), pk(mom_v), name="adamw_small")]
    for i, k in enumerate(small_names):
        results[k] = [gs[k]] + [res[j][i] for j in range(3)]

    out = [loss, grad_x]
    for j in range(4):
        out += [results[k][j] for k in WEIGHTS]
    return tuple(out)
```

```python
import functools
import math

import jax
import jax.numpy as jnp
from jax import lax
from jax.experimental import pallas as pl
from jax.experimental.pallas import tpu as pltpu

F32, BF16 = jnp.float32, jnp.bfloat16
HI = lax.Precision.HIGHEST
NN, NT, TN = ((1,), (0,)), ((1,), (1,)), ((0,), (0,))

NORM_EPS = 1e-6
LANES = 128
V7X_VMEM_BYTES = 64 * 2**20
VMEM_LIMIT = V7X_VMEM_BYTES * 3 // 4
MM_TM, MM_TN, MM_TK = 1024, 1408, 2816

DN_HEADS, DN_DIM, DN_CHUNK, DN_CONV, HALO = 4, 128, 64, 4, 8
DN_GROUP = 4
INV_BLOCK = 16
DN_WIDTH = DN_HEADS * DN_DIM
SWA_HEADS, SWA_DIM, SWA_BLOCK = 4, 64, 128
SWA_WIDTH = SWA_HEADS * SWA_DIM
SWA_PATTERNS = ((128, 1), (512, 4), (2048, 16))
SG_GROUPS, SG_DIM, SG_CHUNK = 4, 64, 128
SG_WIDTH = SG_GROUPS * SG_DIM
XA_HEADS = 4
IN_SIZES = (3 * DN_WIDTH, DN_WIDTH, DN_HEADS, DN_HEADS, 3 * SWA_WIDTH, 2 * SG_WIDTH)
ADAM_LR, ADAM_B1, ADAM_B2, ADAM_EPS, ADAM_WD, ADAM_STEP = 0.001, 0.9, 0.999, 1e-08, 0.01, 10
N_CHIPS, N_DEV = 4, 8
MESH_ID = pl.DeviceIdType.MESH

BIG = ("ffn1_w_gate_up", "ffn1_w_down", "mix_w_in", "mix_w_out", "xa_w_q", "xa_w_kv", "xa_w_o",
       "ffn2_w_gate_up", "ffn2_w_down")
EARLY = ("ffn1_w_gate_up", "ffn1_w_down", "mix_w_in", "mix_w_out")
LATE = ("xa_w_q", "xa_w_kv", "xa_w_o", "ffn2_w_gate_up", "ffn2_w_down")
EARLY_W = EARLY + ("dn_conv_w",)
BIG_AXIS = {"ffn1_w_gate_up": 2, "ffn1_w_down": 1, "mix_w_in": 2, "mix_w_out": 1, "xa_w_q": 1, "xa_w_kv": 2,
            "xa_w_o": 1, "ffn2_w_gate_up": 2, "ffn2_w_down": 1}
SMALL = ("ffn1_norm", "mix_norm", "dn_a_log", "dn_dt_bias", "dn_out_norm", "sg_norm_gain", "sg_norm_bias",
         "sg_w_spatial", "sg_b_spatial", "xa_norm", "xa_mem_norm", "ffn2_norm", "final_norm")
WEIGHTS = ("ffn1_norm", "ffn1_w_gate_up", "ffn1_w_down", "mix_norm", "mix_w_in", "dn_conv_w", "dn_a_log",
           "dn_dt_bias", "dn_out_norm", "sg_norm_gain", "sg_norm_bias", "sg_w_spatial", "sg_b_spatial",
           "mix_w_out", "xa_norm", "xa_mem_norm", "xa_w_q", "xa_w_kv", "xa_w_o", "ffn2_norm", "ffn2_w_gate_up",
           "ffn2_w_down", "final_norm")


@functools.partial(jax.custom_vjp, nondiff_argnums=(2,))
def _dlo(a, b, dims):
    return lax.dot_general(a.astype(BF16), b.astype(BF16), (dims, ((), ())), preferred_element_type=F32)


def _dlo_fwd(a, b, dims):
    return _dlo(a, b, dims), (a, b)


def _dlo_bwd(dims, saved, g):
    a, b = saved
    if dims == NN:
        da, db = _dlo(g, b, NT), _dlo(a, g, TN)
    elif dims == NT:
        da, db = _dlo(g, b, NN), _dlo(g, a, TN)
    else:
        da, db = _dlo(b, g, NT), _dlo(a, g, NN)
    return da.astype(a.dtype), db.astype(b.dtype)


_dlo.defvjp(_dlo_fwd, _dlo_bwd)


def _dhi(a, b, dims):
    return lax.dot_general(a, b, (dims, ((), ())), preferred_element_type=F32, precision=HI)


def _sigmoid(x):
    return 1.0 / (1.0 + jnp.exp(-x))


def _silu(x):
    return x * _sigmoid(x)


def _softplus(x):
    return jnp.maximum(x, 0.0) + jnp.log(1.0 + jnp.exp(-jnp.abs(x)))


def _rms(x, gain):
    x = x.astype(F32)
    return x * lax.rsqrt(jnp.mean(x * x, axis=-1, keepdims=True) + NORM_EPS) * gain


def _tile(n, target, unit=LANES):
    best = None
    for t in range(unit, min(n, target) + 1, unit):
        if n % t == 0:
            best = t
    return best if best is not None else n


def _cparams(sem):
    return pltpu.CompilerParams(dimension_semantics=sem, vmem_limit_bytes=VMEM_LIMIT)


def _mm(a, b, dims, *, out_dtype, name, res=None, scale=1.0, swiglu_a=False, b_k0=0):
    feat = 2 if swiglu_a else 1
    if dims == NN:
        (m, k), n = (a.shape[0], a.shape[1] // feat), b.shape[1]
    elif dims == NT:
        (m, k), n = a.shape, b.shape[0]
    else:
        (k, m), n = (a.shape[0], a.shape[1] // feat), b.shape[1]
    if dims == TN:
        tm, tn, tk = _tile(m, MM_TN), _tile(n, MM_TN), _tile(k, MM_TM // feat)
    else:
        tm, tn, tk = _tile(m, MM_TM // feat, 8), _tile(n, MM_TN), _tile(k, MM_TK // feat)
    nk = k // tk
    assert b_k0 % tk == 0 and (b_k0 == 0 or dims == NT)
    k0 = b_k0 // tk
    if dims == TN:
        a_specs = [pl.BlockSpec((tk, tm), lambda i, j, kk, _o=o * (m // tm): (kk, i + _o)) for o in range(feat)]
    else:
        a_specs = [pl.BlockSpec((tm, tk), lambda i, j, kk, _o=o * nk: (i, kk + _o)) for o in range(feat)]
    b_spec = pl.BlockSpec((tn, tk), lambda i, j, kk: (j, kk + k0)) if dims == NT else pl.BlockSpec((tk, tn), lambda i, j, kk: (kk, j))
    o_spec = pl.BlockSpec((tm, tn), lambda i, j, kk: (i, j))
    has_res = res is not None

    def finish(acc, r_ref, o_ref):
        val = acc * scale if scale != 1.0 else acc
        if has_res:
            val = r_ref[...].astype(F32) + val
        o_ref[...] = val.astype(o_ref.dtype)

    def body(*refs):
        b_ref = refs[feat]
        r_ref = refs[feat + 1] if has_res else None
        a_val = _silu(refs[0][...].astype(F32)) * refs[1][...].astype(F32) if swiglu_a else refs[0][...]
        part = lax.dot_general(a_val.astype(BF16), b_ref[...].astype(BF16), (dims, ((), ())),
                               preferred_element_type=F32)
        if nk == 1:
            finish(part, r_ref, refs[-1])
            return
        o_ref, acc_ref = refs[-2], refs[-1]
        kk = pl.program_id(2)

        @pl.when(kk == 0)
        def _():
            acc_ref[...] = part

        @pl.when(jnp.logical_and(kk > 0, kk < nk - 1))
        def _():
            acc_ref[...] += part

        @pl.when(kk == nk - 1)
        def _():
            finish(acc_ref[...] + part, r_ref, o_ref)

    return pl.pallas_call(
        body, name=name, grid=(m // tm, n // tn, nk),
        in_specs=a_specs + [b_spec] + ([o_spec] if has_res else []), out_specs=o_spec,
        out_shape=jax.ShapeDtypeStruct((m, n), out_dtype),
        scratch_shapes=[pltpu.VMEM((tm, tn), F32)] if nk > 1 else [],
        compiler_params=_cparams(("parallel", "parallel", "arbitrary")),
    )(*([a] * feat + [b] + ([res] if has_res else [])))


def _mm_swiglu_bwd(dh, w_d, gu, scale, name):
    m, k = dh.shape
    f = w_d.shape[0]
    tm, tn = _tile(m, MM_TM // 2, 8), _tile(f, MM_TN)

    def body(dh_ref, w_ref, g_ref, u_ref, dg_ref, du_ref):
        d_act = lax.dot_general(dh_ref[...].astype(BF16), w_ref[...].astype(BF16), (NT, ((), ())),
                                preferred_element_type=F32) * scale
        _, pull = jax.vjp(lambda g, u: _silu(g) * u, g_ref[...].astype(F32), u_ref[...].astype(F32))
        dg, du = pull(d_act)
        dg_ref[...] = dg.astype(dg_ref.dtype)
        du_ref[...] = du.astype(du_ref.dtype)

    tile = pl.BlockSpec((tm, tn), lambda i, j: (i, j))
    return pl.pallas_call(
        body, name=name, grid=(m // tm, f // tn),
        in_specs=[pl.BlockSpec((tm, k), lambda i, j: (i, 0)), pl.BlockSpec((tn, k), lambda i, j: (j, 0)), tile,
                  pl.BlockSpec((tm, tn), lambda i, j: (i, j + f // tn))],
        out_specs=[tile, tile], out_shape=[jax.ShapeDtypeStruct((m, f), BF16)] * 2,
        compiler_params=_cparams(("parallel", "parallel")),
    )(dh, w_d, gu, gu)


def _full_spec(p):
    nd = p.ndim
    return pl.BlockSpec(p.shape, lambda i, _nd=nd: (0,) * _nd)


def _rows(f, rows, params, outs, *, tile, name):
    t = rows[0].shape[0]
    tile = min(tile, t)
    nr, npar = len(rows), len(params)

    def body(*refs):
        vals = f(*[r[...] for r in refs[:nr + npar]])
        for o_ref, v in zip(refs[nr + npar:], vals, strict=True):
            o_ref[...] = v.astype(o_ref.dtype)

    res = pl.pallas_call(
        body, name=name, grid=(t // tile,),
        in_specs=[pl.BlockSpec((tile, r.shape[1]), lambda i: (i, 0)) for r in rows] + [_full_spec(p) for p in params],
        out_specs=[pl.BlockSpec((tile, w), lambda i: (i, 0)) for w, _ in outs],
        out_shape=[jax.ShapeDtypeStruct((t, w), d) for w, d in outs],
        compiler_params=_cparams(("parallel",)),
    )(*rows, *params)
    return tuple(res)


def _rows_vjp(f, rows, params, cts, *, diff, grad_dtypes, tile, name, add=None):
    t = rows[0].shape[0]
    tile = min(tile, t)
    nr, npar, nct = len(rows), len(params), len(cts)
    didx = [i for i, d in enumerate(diff) if d]
    add = [None] * len(didx) if add is None else add
    adds = [a for a in add if a is not None]

    def body(*refs):
        row_refs, par_refs = refs[:nr], refs[nr:nr + npar]
        ct_refs = refs[nr + npar:nr + npar + nct]
        add_refs = list(refs[nr + npar + nct:nr + npar + nct + len(adds)])
        out_refs = refs[nr + npar + nct + len(adds):]
        rv = [r[...] for r in row_refs]
        pv = [p[...].astype(F32) for p in par_refs]

        def g(*args):
            full = list(rv)
            for k, i in enumerate(didx):
                full[i] = args[k]
            return f(*full, *args[len(didx):])

        outs, pull = jax.vjp(g, *[rv[i] for i in didx], *pv)
        grads = pull(tuple(c[...].astype(o.dtype) for c, o in zip(ct_refs, outs, strict=True)))
        for k in range(len(didx)):
            val = grads[k].astype(F32)
            if add[k] is not None:
                val = val + add_refs.pop(0)[...].astype(F32)
            out_refs[k][...] = val.astype(out_refs[k].dtype)

        @pl.when(pl.program_id(0) == 0)
        def _():
            for o_ref in out_refs[len(didx):]:
                o_ref[...] = jnp.zeros_like(o_ref)

        for o_ref, gp in zip(out_refs[len(didx):], grads[len(didx):], strict=True):
            o_ref[...] += gp.astype(F32)

    row_spec = lambda a: pl.BlockSpec((tile, a.shape[1]), lambda i: (i, 0))
    res = pl.pallas_call(
        body, name=name, grid=(t // tile,),
        in_specs=[row_spec(r) for r in rows] + [_full_spec(p) for p in params] + [row_spec(c) for c in cts]
        + [row_spec(a) for a in adds],
        out_specs=[row_spec(rows[i]) for i in didx] + [_full_spec(p) for p in params],
        out_shape=[jax.ShapeDtypeStruct(rows[i].shape, d) for i, d in zip(didx, grad_dtypes, strict=True)]
        + [jax.ShapeDtypeStruct(p.shape, F32) for p in params],
        compiler_params=_cparams(("arbitrary",)),
    )(*rows, *params, *cts, *adds)
    return tuple(res)


def _sum2(a, b, out_dtype, name):
    shape = a.shape
    views = [x.reshape(-1, shape[-1]) for x in (a, b)]
    r, c = views[0].shape
    tile = _tile(r, max(16, (1 << 18) // c), 16)

    def body(a_ref, b_ref, o_ref):
        o_ref[...] = (a_ref[...].astype(F32) + b_ref[...].astype(F32)).astype(o_ref.dtype)

    spec = pl.BlockSpec((tile, c), lambda i: (i, 0))
    return pl.pallas_call(
        body, name=name, grid=(r // tile,), in_specs=[spec] * 2, out_specs=spec,
        out_shape=jax.ShapeDtypeStruct((r, c), out_dtype), compiler_params=_cparams(("parallel",)),
    )(*views).reshape(shape)


def _sum_slots(x, out_dtype, name, first=None):
    n, r, c = x.shape
    tile = _tile(r, max(16, (1 << 18) // c), 16)

    def body(*refs):
        acc = refs[0][...].astype(F32)
        for ref in refs[1:-1]:
            acc = acc + ref[...].astype(F32)
        refs[-1][...] = acc.astype(refs[-1].dtype)

    spec = pl.BlockSpec((tile, c), lambda i: (i, 0))
    return pl.pallas_call(
        body, name=name, grid=(r // tile,),
        in_specs=([spec] if first is not None else [])
        + [pl.BlockSpec((None, tile, c), lambda i, _s=s_: (_s, i, 0)) for s_ in range(n)],
        out_specs=spec, out_shape=jax.ShapeDtypeStruct((r, c), out_dtype), compiler_params=_cparams(("parallel",)),
    )(*(([first] if first is not None else []) + [x] * n))


def _adamw(w, g, m, v, name):
    r, c = w.shape
    tile = _tile(r, max(8, (1 << 18) // c), 8)

    def body(w_ref, g_ref, m_ref, v_ref, d_out, m_out, v_out):
        g = g_ref[...]
        mn = ADAM_B1 * m_ref[...] + (1.0 - ADAM_B1) * g
        vn = ADAM_B2 * v_ref[...] + (1.0 - ADAM_B2) * (g * g)
        m_hat = mn / (1.0 - ADAM_B1 ** ADAM_STEP)
        v_hat = vn / (1.0 - ADAM_B2 ** ADAM_STEP)
        d_out[...] = -ADAM_LR * (m_hat / (jnp.sqrt(v_hat) + ADAM_EPS) + ADAM_WD * w_ref[...])
        m_out[...] = mn
        v_out[...] = vn

    spec = pl.BlockSpec((tile, c), lambda i: (i, 0))
    return pl.pallas_call(
        body, name=name, grid=(r // tile,), in_specs=[spec] * 4, out_specs=[spec] * 3,
        out_shape=[jax.ShapeDtypeStruct((r, c), F32)] * 3, compiler_params=_cparams(("parallel",)),
    )(w, g, m, v)


def _place():
    return lax.axis_index("x"), lax.axis_index("y"), lax.axis_index("c")


def _flip(v, bit):
    return 1 - v if bit else v


_ANY = pl.BlockSpec(memory_space=pl.ANY)


def _quarter(ref, j, shape, kind):
    if kind == "row":
        return ref.at[pl.ds(j * shape[0], shape[0])]
    if kind == "col":
        return ref.at[:, pl.ds(j * shape[1], shape[1])]
    return ref.at[j]


def _whole_shape(shape, kind):
    if kind == "row":
        return (N_CHIPS * shape[0],) + tuple(shape[1:])
    if kind == "col":
        return (shape[0], N_CHIPS * shape[1]) + tuple(shape[2:])
    return (N_CHIPS,) + tuple(shape)


def _dma_sems(*counts):
    return [pltpu.SemaphoreType.DMA((n,)) for n in counts]


class _Job:
    def __init__(self, ins, outs, sems, phases, at):
        self.ins, self.outs, self.sems, self.phases, self.at = list(ins), list(outs), list(sems), list(phases), list(at)

    def specs(self):
        return [_ANY] * len(self.ins), [_ANY] * len(self.outs), _dma_sems(*self.sems)

    def run_at(self, step, n_steps, in_refs, out_refs, sem_refs):
        for phase, frac in zip(self.phases, self.at, strict=True):
            @pl.when(step == int(frac * (n_steps - 1)))
            def _():
                phase(in_refs, out_refs, sem_refs)

    def join(self, other):
        ni, no, ns = len(self.ins), len(self.outs), len(self.sems)
        assert self.at == other.at

        def both(mine, theirs):
            def phase(ins, outs, sems):
                mine(ins[:ni], outs[:no], sems[:ns])
                theirs(ins[ni:], outs[no:], sems[ns:])
            return phase

        return _Job(self.ins + other.ins, self.outs + other.outs, self.sems + other.sems,
                    [both(a, b) for a, b in zip(self.phases, other.phases, strict=True)], self.at)


def _run_job(job, name):
    def body(*refs):
        ni, no = len(job.ins), len(job.outs)
        for phase in job.phases:
            phase(refs[:ni], refs[ni:ni + no], refs[ni + no:])

    in_specs, out_specs, sems = job.specs()
    return pl.pallas_call(body, name=name, in_specs=in_specs, out_specs=out_specs, out_shape=job.outs,
                          scratch_shapes=sems)(*job.ins)


def _remote(src, dst, sems, i, j, peer):
    return pltpu.make_async_remote_copy(src_ref=src, dst_ref=dst, send_sem=sems[i].at[j], recv_sem=sems[i + 1].at[j],
                                        device_id=peer, device_id_type=MESH_ID)


def _gather_job(shards, kinds, layer):
    n = len(shards)
    shapes = [s.shape[1:] for s in shards]
    other = 1 - layer

    def window(outs, t, j):
        return _quarter(outs[t], j, shapes[t], kinds[t])

    def start(ins, outs, sems):
        mx, my, mc = _place()
        me = 2 * mx + my

        @pl.when(mc == layer)
        def _():
            for t in range(n):
                for k in range(1, N_CHIPS):
                    _remote(ins[t].at[layer], window(outs, t, me), sems, 0, 3 * t + k - 1,
                            (_flip(mx, k >> 1), _flip(my, k & 1), layer)).start()

        @pl.when(mc == other)
        def _():
            for t in range(n):
                _remote(ins[t].at[layer], window(outs, t, me), sems, 2, t, (mx, my, layer)).start()

    def relay(ins, outs, sems):
        mx, my, mc = _place()
        me = 2 * mx + my

        @pl.when(mc == layer)
        def _():
            for t in range(n):
                _remote(ins[t].at[layer], window(outs, t, me), sems, 2, t, (mx, my, other)).wait_recv()
                for k in range(1, N_CHIPS):
                    px, py = _flip(mx, k >> 1), _flip(my, k & 1)
                    cp = _remote(ins[t].at[layer], window(outs, t, 2 * px + py), sems, 0, 3 * t + k - 1, (px, py, layer))
                    cp.wait_recv()
                    cp.wait_send()
                _remote(outs[t], outs[t], sems, 4, t, (mx, my, other)).start()

        @pl.when(mc == other)
        def _():
            for t in range(n):
                _remote(ins[t].at[layer], window(outs, t, me), sems, 2, t, (mx, my, layer)).wait_send()

    def finish(ins, outs, sems):
        mx, my, mc = _place()

        @pl.when(mc == layer)
        def _():
            for t in range(n):
                _remote(outs[t], outs[t], sems, 4, t, (mx, my, other)).wait_send()

        @pl.when(mc == other)
        def _():
            for t in range(n):
                _remote(outs[t], outs[t], sems, 4, t, (mx, my, layer)).wait_recv()

    outs = [jax.ShapeDtypeStruct(_whole_shape(sh, kd), s.dtype) for s, sh, kd in zip(shards, shapes, kinds)]
    return _Job(shards, outs, [3 * n, 3 * n, n, n, n, n], [start, relay, finish], [0.0, 0.75, 1.0])


def _scatter_job(gs, shapes, kinds, layer):
    n = len(gs)

    def copies(ins, outs, sems):
        mx, my, _ = _place()
        return [_remote(_quarter(ins[t], 2 * _flip(mx, k >> 1) + _flip(my, k & 1), shapes[t], kinds[t]), outs[t].at[k - 1],
                        sems, 0, 3 * t + k - 1, (_flip(mx, k >> 1), _flip(my, k & 1), layer))
                for t in range(n) for k in range(1, N_CHIPS)]

    def start(ins, outs, sems):
        @pl.when(lax.axis_index("c") == layer)
        def _():
            for cp in copies(ins, outs, sems):
                cp.start()

    def finish(ins, outs, sems):
        @pl.when(lax.axis_index("c") == layer)
        def _():
            for cp in copies(ins, outs, sems):
                cp.wait_recv()
                cp.wait_send()

    outs = [jax.ShapeDtypeStruct((N_CHIPS - 1,) + tuple(sh), g.dtype) for g, sh in zip(gs, shapes)]
    return _Job(gs, outs, [3 * n, 3 * n], [start, finish], [0.0, 1.0])


def _core_to_core(parts, sender, name):
    n = len(parts)

    def move(ins, outs, sems):
        mx, my, mc = _place()

        @pl.when(mc == sender)
        def _():
            copies = [_remote(ins[t], outs[t], sems, 0, t, (mx, my, 1 - sender)) for t in range(n)]
            for cp in copies:
                cp.start()
            for cp in copies:
                cp.wait_send()

        @pl.when(mc != sender)
        def _():
            for t in range(n):
                _remote(ins[t], outs[t], sems, 0, t, (mx, my, sender)).wait_recv()

    return _run_job(_Job(parts, [jax.ShapeDtypeStruct(p.shape, p.dtype) for p in parts], [n, n], [move], [0.0]), name)


def _gather_all(x, name):
    r, w = x.shape

    def body(x_ref, o_ref, send_sems, recv_sems, local_sem):
        mx, my, mc = _place()
        mine = 4 * mx + 2 * my + mc
        local = pltpu.make_async_copy(x_ref, o_ref.at[mine], local_sem)
        local.start()
        copies = []
        for k in range(1, N_DEV):
            peer = (_flip(mx, k >> 2), _flip(my, (k >> 1) & 1), _flip(mc, k & 1))
            copies.append(pltpu.make_async_remote_copy(
                src_ref=x_ref, dst_ref=o_ref.at[mine], send_sem=send_sems.at[k - 1], recv_sem=recv_sems.at[k - 1],
                device_id=peer, device_id_type=MESH_ID))
            copies[-1].start()
        for k in range(1, N_DEV):
            peer = (_flip(mx, k >> 2), _flip(my, (k >> 1) & 1), _flip(mc, k & 1))
            pltpu.make_async_remote_copy(
                src_ref=x_ref, dst_ref=o_ref.at[4 * peer[0] + 2 * peer[1] + peer[2]], send_sem=send_sems.at[k - 1],
                recv_sem=recv_sems.at[k - 1], device_id=peer, device_id_type=MESH_ID).wait_recv()
        for cp in copies:
            cp.wait_send()
        local.wait()

    return pl.pallas_call(
        body, name=name, in_specs=[_ANY], out_specs=_ANY, out_shape=jax.ShapeDtypeStruct((N_DEV, r, w), x.dtype),
        scratch_shapes=[pltpu.SemaphoreType.DMA((N_DEV - 1,)), pltpu.SemaphoreType.DMA((N_DEV - 1,)),
                        pltpu.SemaphoreType.DMA],
    )(x)


def _pack(parts, dtype, row_unit):
    flat = jnp.concatenate([p.astype(dtype).reshape(-1) for p in parts])
    unit = row_unit * LANES
    pad = (-flat.shape[0]) % unit
    if pad:
        flat = jnp.concatenate([flat, jnp.zeros((pad,), dtype)])
    return flat.reshape(-1, LANES)


def _unpack(packed, shapes):
    flat = packed.reshape(-1)
    out, off = [], 0
    for s in shapes:
        n = math.prod(s)
        out.append(flat[off:off + n].reshape(s))
        off += n
    return out


def _f_rms(x, gain):
    return (_rms(x, gain),)


def _f_xattn(q, kv):
    d = q.shape[1]
    hd = d // XA_HEADS
    outs = []
    for h in range(XA_HEADS):
        qh, kh, vh = q[:, h * hd:(h + 1) * hd], kv[:, h * hd:(h + 1) * hd], kv[:, d + h * hd:d + (h + 1) * hd]
        s = _dlo(qh, kh, NT) * (hd ** -0.5)
        s = s - jnp.max(s, axis=-1, keepdims=True)
        p = jnp.exp(s)
        p = p / jnp.sum(p, axis=-1, keepdims=True)
        outs.append(_dlo(p, vh, NN))
    return (jnp.concatenate(outs, axis=1),)


def _f_gmlp(uv, gain, bias, w_sp, b_sp):
    r = uv.shape[0]
    act = jax.nn.gelu(uv.astype(F32))
    u, v = act[:, :SG_WIDTH], act[:, SG_WIDTH:]
    mu = jnp.mean(v, axis=-1, keepdims=True)
    var = jnp.mean(jnp.square(v - mu), axis=-1, keepdims=True)
    v = (v - mu) * lax.rsqrt(var + NORM_EPS) * gain + bias
    row = lax.broadcasted_iota(jnp.int32, (SG_CHUNK, SG_CHUNK), 0)
    col = lax.broadcasted_iota(jnp.int32, (SG_CHUNK, SG_CHUNK), 1)
    lane_grp = lax.broadcasted_iota(jnp.int32, (b_sp.shape[0], SG_WIDTH), 1) // SG_DIM
    grp_row = lax.broadcasted_iota(jnp.int32, (b_sp.shape[0], SG_WIDTH), 0)
    spread = jnp.where(lane_grp == grp_row, 1.0, 0.0).astype(F32)
    bias_t = _dhi(b_sp, spread, TN)
    chunks = []
    for c in range(r // SG_CHUNK):
        vc = v[c * SG_CHUNK:(c + 1) * SG_CHUNK]
        parts = []
        for g in range(SG_GROUPS):
            wg = jnp.where(row >= col, w_sp[g], 0.0)
            parts.append(_dlo(wg, vc[:, g * SG_DIM:(g + 1) * SG_DIM], NN))
        chunks.append(jnp.concatenate(parts, axis=1) + bias_t)
    mixed = jnp.concatenate(chunks, axis=0) if len(chunks) > 1 else chunks[0]
    return (u * mixed,)


def _f_swa_mix(o1, o2, o3, l1, l2, l3):
    outs = []
    for h in range(SWA_HEADS):
        ls = [l[:, h:h + 1] for l in (l1, l2, l3)]
        mx = jnp.maximum(jnp.maximum(ls[0], ls[1]), ls[2])
        es = [jnp.exp(l - mx) for l in ls]
        den = es[0] + es[1] + es[2]
        sl = slice(h * SWA_DIM, (h + 1) * SWA_DIM)
        outs.append((es[0] * o1[:, sl] + es[1] * o2[:, sl] + es[2] * o3[:, sl]) / den)
    return (jnp.concatenate(outs, axis=1),)


def _swa_block(q, kp, kc, vp, vc, first, window, dilation):
    span = window // dilation
    rows = SWA_HEADS * SWA_BLOCK
    ri = lax.broadcasted_iota(jnp.int32, (rows, 2 * SWA_BLOCK), 0)
    kj = lax.broadcasted_iota(jnp.int32, (rows, 2 * SWA_BLOCK), 1)
    head = ri // SWA_BLOCK
    rel = SWA_BLOCK + ri % SWA_BLOCK - kj
    valid = (rel >= 0) & (rel <= span) & jnp.logical_not(jnp.logical_and(first, kj < SWA_BLOCK))
    slope = jnp.exp((head + 1).astype(F32) * (-8.0 / SWA_HEADS * math.log(2.0)))
    bias = slope * (rel * dilation).astype(F32)
    kw = jnp.concatenate([kp, kc], axis=0)
    vw = jnp.concatenate([vp, vc], axis=0)
    lane_head = lax.broadcasted_iota(jnp.int32, (rows, SWA_WIDTH), 1) // SWA_DIM
    own_head = lane_head == lax.broadcasted_iota(jnp.int32, (rows, SWA_WIDTH), 0) // SWA_BLOCK
    q_rows = jnp.where(own_head, jnp.concatenate([q] * SWA_HEADS, axis=0), 0.0)
    s = _dlo(q_rows, kw, NT) * (SWA_DIM ** -0.5) - bias
    s = jnp.where(valid, s, -1e30)
    m = jnp.max(s, axis=-1, keepdims=True)
    p = jnp.exp(s - m)
    den = jnp.sum(p, axis=-1, keepdims=True)
    wide = _dlo(p, vw, NN) / den
    lse_rows = m + jnp.log(den)
    lane = lax.broadcasted_iota(jnp.int32, (SWA_BLOCK, LANES), 1)
    outs, lse = [], jnp.zeros((SWA_BLOCK, LANES), F32)
    for h in range(SWA_HEADS):
        outs.append(wide[h * SWA_BLOCK:(h + 1) * SWA_BLOCK, h * SWA_DIM:(h + 1) * SWA_DIM])
        lse = lse + jnp.where(lane == h, lse_rows[h * SWA_BLOCK:(h + 1) * SWA_BLOCK], 0.0)
    return jnp.concatenate(outs, axis=1), lse


@jax.custom_vjp
def _unit_lower_inv(lower):
    c = lower.shape[0]
    assert DN_CHUNK == 4 * INV_BLOCK and c % DN_CHUNK == 0
    row = lax.broadcasted_iota(jnp.int32, (c, c), 0)
    col = lax.broadcasted_iota(jnp.int32, (c, c), 1)
    eye = jnp.where(row == col, 1.0, 0.0).astype(F32)
    same = (row // INV_BLOCK) == (col // INV_BLOCK)
    pw = -jnp.where(same, lower, 0.0)
    d_inv = eye + pw
    for _ in range(int(math.log2(INV_BLOCK)) - 1):
        pw = _dlo(pw, pw, NN)
        d_inv = d_inv + _dlo(d_inv, pw, NN)
    n1 = _dlo(d_inv, jnp.where(same, 0.0, lower), NN)
    n2 = _dlo(n1, n1, NN)
    rough = _dlo(eye - n1 + n2 - _dlo(n1, n2, NN), d_inv, NN)
    residual = eye - _dhi(eye + lower, rough, NN)
    return rough + _dlo(rough, residual, NN)


def _unit_lower_inv_fwd(lower):
    t_inv = _unit_lower_inv(lower)
    return t_inv, t_inv


def _unit_lower_inv_bwd(t_inv, g):
    return (-_dlo(_dlo(t_inv, g, TN), t_inv, NT),)


_unit_lower_inv.defvjp(_unit_lower_inv_fwd, _unit_lower_inv_bwd)


def _dn_group(xx, z, ba, state, conv_w, a_log, dt_bias, gain):
    rows, c = z.shape[0], DN_CHUNK
    hc = DN_HEADS * c
    acc = conv_w[0:1] * xx[HALO - 3:HALO - 3 + rows]
    for j in range(1, DN_CONV):
        acc = acc + conv_w[j:j + 1] * xx[HALO - 3 + j:HALO - 3 + j + rows]
    qkv = _silu(acc)
    beta_all = _sigmoid(ba)
    g_all = -jnp.exp(a_log) * _softplus(ba + dt_bias)
    row = lax.broadcasted_iota(jnp.int32, (hc, hc), 0)
    col = lax.broadcasted_iota(jnp.int32, (hc, hc), 1)
    same_head = (row // c) == (col // c)
    incl, strict = same_head & (row >= col), same_head & (row > col)
    tri = jnp.where(incl[:c, :c], 1.0, 0.0).astype(F32)

    def stack(piece):
        return jnp.concatenate([piece(h) for h in range(DN_HEADS)], axis=0)

    local = []
    for ci in range(rows // c):
        r0 = ci * c
        gc_all = _dhi(tri, g_all[r0:r0 + c], NN)
        gc_t = gc_all.T
        q = stack(lambda h: qkv[r0:r0 + c, h * DN_DIM:(h + 1) * DN_DIM])
        k = stack(lambda h: qkv[r0:r0 + c, DN_WIDTH + h * DN_DIM:DN_WIDTH + (h + 1) * DN_DIM])
        v = stack(lambda h: qkv[r0:r0 + c, 2 * DN_WIDTH + h * DN_DIM:2 * DN_WIDTH + (h + 1) * DN_DIM])
        q = q * lax.rsqrt(jnp.sum(q * q, axis=-1, keepdims=True) + NORM_EPS) * (DN_DIM ** -0.5)
        k = k * lax.rsqrt(jnp.sum(k * k, axis=-1, keepdims=True) + NORM_EPS)
        beta = stack(lambda h: beta_all[r0:r0 + c, h:h + 1])
        gc = stack(lambda h: gc_all[:, DN_HEADS + h:DN_HEADS + h + 1])
        g_last = stack(lambda h: jnp.broadcast_to(gc_all[c - 1:c, DN_HEADS + h:DN_HEADS + h + 1], (c, 1)))
        gc_row = jnp.concatenate([gc_t[DN_HEADS + h:DN_HEADS + h + 1, :] for h in range(DN_HEADS)], axis=1)
        decay = jnp.where(incl, jnp.exp(jnp.where(incl, gc - gc_row, 0.0)), 0.0)
        kb = k * beta
        t_inv = _unit_lower_inv(jnp.where(strict, _dlo(kb, k, NT) * decay, 0.0))
        e_gc = jnp.exp(gc)
        u = _dlo(t_inv, v * beta, NN)
        w = _dlo(t_inv, kb * e_gc, NN)
        a_qk = jnp.where(incl, _dlo(q, k, NT) * decay, 0.0)
        e_last = jnp.concatenate([jnp.broadcast_to(jnp.exp(gc_all[c - 1:c, DN_HEADS + h:DN_HEADS + h + 1]), (1, DN_DIM))
                                  for h in range(DN_HEADS)], axis=1)
        local.append((jnp.concatenate([w, q * e_gc], axis=0), k * jnp.exp(g_last - gc), u, a_qk, e_last))
    own = (lax.broadcasted_iota(jnp.int32, (hc, DN_WIDTH), 0) // c) == (lax.broadcasted_iota(jnp.int32, (hc, DN_WIDTH), 1) // DN_DIM)

    def own_blocks(m):
        return stack(lambda h: m[h * c:(h + 1) * c, h * DN_DIM:(h + 1) * DN_DIM])

    s = state
    out_rows = []
    for ci in range(rows // c):
        r0 = ci * c
        wq, k_tail, u, a_qk, e_last = local[ci]
        through = _dlo(wq, s, NN)
        v_new = u - own_blocks(through[:hc])
        o = own_blocks(through[hc:]) + _dlo(a_qk, v_new, NN)
        v_wide = jnp.where(own, jnp.concatenate([v_new] * DN_HEADS, axis=1), 0.0)
        s = s * e_last + _dlo(k_tail, v_wide, TN)
        o = o * lax.rsqrt(jnp.mean(o * o, axis=-1, keepdims=True) + NORM_EPS) * gain
        o = jnp.concatenate([o[h * c:(h + 1) * c] for h in range(DN_HEADS)], axis=1)
        out_rows.append(o * _silu(z[r0:r0 + c]))
    out = jnp.concatenate(out_rows, axis=0) if len(out_rows) > 1 else out_rows[0]
    return out, s


def _dn_specs(n_of, rows):
    return [pl.BlockSpec((rows, 3 * DN_WIDTH), lambda i: (n_of(i), 0)),
            pl.BlockSpec((HALO, 3 * DN_WIDTH), lambda i: (jnp.maximum(n_of(i) * (rows // HALO) - 1, 0), 0)),
            pl.BlockSpec((rows, DN_WIDTH), lambda i: (n_of(i), 0)),
            pl.BlockSpec((rows, LANES), lambda i: (n_of(i), 0))]


def _no_job():
    return _Job([], [], [], [], [])


def _dn_forward(xq, xz, xba, params, name, job=None):
    t = xq.shape[0]
    rows = min(DN_GROUP * DN_CHUNK, t)
    n_groups = t // rows
    job = job or _no_job()
    job_in, job_out, job_sems = job.specs()

    def body(*refs):
        x_ref, halo_ref, z_ref, ba_ref, cw_ref, al_ref, dt_ref, gn_ref = refs[:8]
        ji, jo, js = 8 + len(job_in), 10 + len(job_in) + len(job_out), 11 + len(job_in) + len(job_out)
        o_ref, s_all_ref, s_ref = refs[ji], refs[ji + 1], refs[jo]
        n = pl.program_id(0)
        job.run_at(n, n_groups, refs[8:ji], refs[ji + 2:jo], refs[js:])

        @pl.when(n == 0)
        def _():
            s_ref[...] = jnp.zeros_like(s_ref)

        halo = jnp.where(n > 0, halo_ref[...], 0.0)
        xx = jnp.concatenate([halo, x_ref[...]], axis=0)
        s_all_ref[0] = s_ref[...]
        o, s_new = _dn_group(xx, z_ref[...], ba_ref[...], s_ref[...], cw_ref[...], al_ref[...], dt_ref[...], gn_ref[...])
        o_ref[...] = o.astype(o_ref.dtype)
        s_ref[...] = s_new

    assert len(params) == 4
    res = pl.pallas_call(
        body, name=name, grid=(n_groups,),
        in_specs=_dn_specs(lambda i: i, rows) + [_full_spec(p) for p in params] + job_in,
        out_specs=[pl.BlockSpec((rows, DN_WIDTH), lambda i: (i, 0)),
                   pl.BlockSpec((1, DN_DIM, DN_WIDTH), lambda i: (i, 0, 0))] + job_out,
        out_shape=[jax.ShapeDtypeStruct((t, DN_WIDTH), BF16),
                   jax.ShapeDtypeStruct((n_groups, DN_DIM, DN_WIDTH), F32)] + job.outs,
        scratch_shapes=[pltpu.VMEM((DN_DIM, DN_WIDTH), F32)] + job_sems,
        compiler_params=_cparams(("arbitrary",)),
    )(xq, xq, xz, xba, *params, *job.ins)
    return res[0], res[1], list(res[2:])


def _dn_backward(xq, xz, xba, params, s_all, d_out, name, job=None):
    t = xq.shape[0]
    rows = min(DN_GROUP * DN_CHUNK, t)
    n_groups = t // rows
    rev = lambda i: n_groups - 1 - i
    job = job or _no_job()
    job_in, job_out, job_sems = job.specs()

    def body(*refs):
        x_ref, halo_ref, z_ref, ba_ref, cw_ref, al_ref, dt_ref, gn_ref, s_ref, do_ref = refs[:10]
        ji = 10 + len(job_in)
        dx_ref, dz_ref, dba_ref, dcw_ref, dal_ref, ddt_ref, dgn_ref = refs[ji:ji + 7]
        jo = ji + 7 + len(job_out)
        ds_ref, dhalo_ref = refs[jo], refs[jo + 1]
        i = pl.program_id(0)
        n = n_groups - 1 - i
        job.run_at(i, n_groups, refs[10:ji], refs[ji + 7:jo], refs[jo + 2:])

        @pl.when(i == 0)
        def _():
            ds_ref[...] = jnp.zeros_like(ds_ref)
            dhalo_ref[...] = jnp.zeros_like(dhalo_ref)
            for r in (dcw_ref, dal_ref, ddt_ref, dgn_ref):
                r[...] = jnp.zeros_like(r)

        halo = jnp.where(n > 0, halo_ref[...], 0.0)
        xx = jnp.concatenate([halo, x_ref[...]], axis=0)
        _, pull = jax.vjp(_dn_group, xx, z_ref[...], ba_ref[...], s_ref[0], cw_ref[...], al_ref[...], dt_ref[...],
                          gn_ref[...])
        dxx, dz, dba, ds, dcw, dal, ddt, dgn = pull((do_ref[...].astype(F32), ds_ref[...]))
        dx_ref[...] = jnp.concatenate([dxx[HALO:rows], dxx[rows:] + dhalo_ref[...]], axis=0).astype(dx_ref.dtype)
        dhalo_ref[...] = dxx[:HALO]
        dz_ref[...] = dz.astype(dz_ref.dtype)
        dba_ref[...] = dba.astype(dba_ref.dtype)
        ds_ref[...] = ds
        dcw_ref[...] += dcw
        dal_ref[...] += dal
        ddt_ref[...] += ddt
        dgn_ref[...] += dgn

    assert len(params) == 4
    res = pl.pallas_call(
        body, name=name, grid=(n_groups,),
        in_specs=_dn_specs(rev, rows) + [_full_spec(p) for p in params]
        + [pl.BlockSpec((1, DN_DIM, DN_WIDTH), lambda i: (rev(i), 0, 0)),
           pl.BlockSpec((rows, DN_WIDTH), lambda i: (rev(i), 0))] + job_in,
        out_specs=[pl.BlockSpec((rows, 3 * DN_WIDTH), lambda i: (rev(i), 0)),
                   pl.BlockSpec((rows, DN_WIDTH), lambda i: (rev(i), 0)),
                   pl.BlockSpec((rows, LANES), lambda i: (rev(i), 0))] + [_full_spec(p) for p in params] + job_out,
        out_shape=[jax.ShapeDtypeStruct(xq.shape, BF16), jax.ShapeDtypeStruct(xz.shape, BF16),
                   jax.ShapeDtypeStruct(xba.shape, BF16)] + [jax.ShapeDtypeStruct(p.shape, F32) for p in params] + job.outs,
        scratch_shapes=[pltpu.VMEM((DN_DIM, DN_WIDTH), F32), pltpu.VMEM((HALO, 3 * DN_WIDTH), F32)] + job_sems,
        compiler_params=_cparams(("arbitrary",)),
    )(xq, xq, xz, xba, *params, s_all, d_out, *job.ins)
    return tuple(res[:7]), list(res[7:])


def _swa_forward(xs, window, dilation, name):
    t = xs.shape[0]
    d, l = dilation, t // dilation
    nb = l // SWA_BLOCK
    view = xs.reshape(l, d * 3 * SWA_WIDTH)
    blk = (SWA_BLOCK, SWA_WIDTH)

    def body(q_ref, kp_ref, kc_ref, vp_ref, vc_ref, o_ref, l_ref):
        blocks = [r[...].astype(F32) for r in (q_ref, kp_ref, kc_ref, vp_ref, vc_ref)]
        o, lse = _swa_block(*blocks, pl.program_id(1) == 0, window, dilation)
        o_ref[...] = o
        l_ref[...] = lse

    prev = lambda n: jnp.maximum(n - 1, 0)
    o, lse = pl.pallas_call(
        body, name=name, grid=(d, nb),
        in_specs=[pl.BlockSpec(blk, lambda r, n: (n, 3 * r)), pl.BlockSpec(blk, lambda r, n: (prev(n), 3 * r + 1)),
                  pl.BlockSpec(blk, lambda r, n: (n, 3 * r + 1)), pl.BlockSpec(blk, lambda r, n: (prev(n), 3 * r + 2)),
                  pl.BlockSpec(blk, lambda r, n: (n, 3 * r + 2))],
        out_specs=[pl.BlockSpec(blk, lambda r, n: (n, r)), pl.BlockSpec((SWA_BLOCK, LANES), lambda r, n: (n, r))],
        out_shape=[jax.ShapeDtypeStruct((l, d * SWA_WIDTH), F32), jax.ShapeDtypeStruct((l, d * LANES), F32)],
        compiler_params=_cparams(("parallel", "parallel")),
    )(view, view, view, view, view)
    return o.reshape(t, SWA_WIDTH), lse.reshape(t, LANES)


def _swa_backward(xs, d_o, d_lse, acc, window, dilation, name):
    t = xs.shape[0]
    d, l = dilation, t // dilation
    nb = l // SWA_BLOCK
    view = xs.reshape(l, d * 3 * SWA_WIDTH)
    blk = (SWA_BLOCK, SWA_WIDTH)
    has_acc = acc is not None

    def body(*refs):
        q_ref, kp_ref, kc_ref, vp_ref, vc_ref, do_ref, dl_ref = refs[:7]
        acc_refs = refs[7:10] if has_acc else None
        dq_ref, dk_ref, dv_ref, ck_ref, cv_ref = refs[-5:]
        i = pl.program_id(1)
        n = nb - 1 - i

        @pl.when(i == 0)
        def _():
            ck_ref[...] = jnp.zeros_like(ck_ref)
            cv_ref[...] = jnp.zeros_like(cv_ref)

        f = functools.partial(_swa_block, first=n == 0, window=window, dilation=dilation)
        _, pull = jax.vjp(f, *[r[...].astype(F32) for r in (q_ref, kp_ref, kc_ref, vp_ref, vc_ref)])
        dq, dkp, dkc, dvp, dvc = pull((do_ref[...], dl_ref[...]))
        dk = dkc + ck_ref[...]
        dv = dvc + cv_ref[...]
        if has_acc:
            dq, dk, dv = dq + acc_refs[0][...], dk + acc_refs[1][...], dv + acc_refs[2][...]
        dq_ref[...] = dq
        dk_ref[...] = dk
        dv_ref[...] = dv
        ck_ref[...] = dkp
        cv_ref[...] = dvp

    cur = lambda i: nb - 1 - i
    prev = lambda i: jnp.maximum(nb - 2 - i, 0)
    own = pl.BlockSpec(blk, lambda r, i: (cur(i), r))
    accs = [a.reshape(l, d * SWA_WIDTH) for a in acc] if has_acc else []
    outs = pl.pallas_call(
        body, name=name, grid=(d, nb),
        in_specs=[pl.BlockSpec(blk, lambda r, i: (cur(i), 3 * r)), pl.BlockSpec(blk, lambda r, i: (prev(i), 3 * r + 1)),
                  pl.BlockSpec(blk, lambda r, i: (cur(i), 3 * r + 1)), pl.BlockSpec(blk, lambda r, i: (prev(i), 3 * r + 2)),
                  pl.BlockSpec(blk, lambda r, i: (cur(i), 3 * r + 2)), own,
                  pl.BlockSpec((SWA_BLOCK, LANES), lambda r, i: (cur(i), r))] + [own] * len(accs),
        out_specs=[own] * 3, out_shape=[jax.ShapeDtypeStruct((l, d * SWA_WIDTH), F32)] * 3,
        scratch_shapes=[pltpu.VMEM(blk, F32), pltpu.VMEM(blk, F32)],
        compiler_params=_cparams(("parallel", "arbitrary")),
    )(view, view, view, view, view, d_o.reshape(l, d * SWA_WIDTH), d_lse.reshape(l, d * LANES), *accs)
    return tuple(o.reshape(t, SWA_WIDTH) for o in outs)


def _loss_head(h, target, gain, name, tile=256):
    t, d = h.shape
    tile = min(tile, t)

    def body(h_ref, t_ref, g_ref, loss_ref, dh_ref, dg_ref):
        def f(hv, gv):
            err = _rms(hv, gv) - t_ref[...]
            return 0.5 * jnp.sum(jnp.mean(err * err, axis=-1, keepdims=True), axis=0, keepdims=True)

        val, pull = jax.vjp(f, h_ref[...], g_ref[...])
        dh, dg = pull(jnp.ones((1, 1), F32))
        dh_ref[...] = dh

        @pl.when(pl.program_id(0) == 0)
        def _():
            loss_ref[...] = jnp.zeros_like(loss_ref)
            dg_ref[...] = jnp.zeros_like(dg_ref)

        loss_ref[...] += jnp.broadcast_to(val, loss_ref.shape)
        dg_ref[...] += dg

    return pl.pallas_call(
        body, name=name, grid=(t // tile,),
        in_specs=[pl.BlockSpec((tile, d), lambda i: (i, 0)), pl.BlockSpec((tile, d), lambda i: (i, 0)), _full_spec(gain)],
        out_specs=[pl.BlockSpec((1, LANES), lambda i: (0, 0)), pl.BlockSpec((tile, d), lambda i: (i, 0)), _full_spec(gain)],
        out_shape=[jax.ShapeDtypeStruct((1, LANES), F32), jax.ShapeDtypeStruct((t, d), F32),
                   jax.ShapeDtypeStruct(gain.shape, F32)],
        compiler_params=_cparams(("arbitrary",)),
    )(h, target, gain)


def _split_w_in(w_in):
    cuts = [0]
    for s in IN_SIZES:
        cuts.append(cuts[-1] + s)
    qkv, z = w_in[:, cuts[0]:cuts[1]], w_in[:, cuts[1]:cuts[2]]
    ba = jnp.pad(w_in[:, cuts[2]:cuts[4]], ((0, 0), (0, LANES - 2 * DN_HEADS)))
    return qkv, z, ba, w_in[:, cuts[4]:cuts[5]], w_in[:, cuts[5]:cuts[6]]


def _lane_pad(v, offset):
    return jnp.pad(v.reshape(1, -1), ((0, 0), (offset, LANES - offset - v.shape[0])))


def _layer_params(sm, i, conv_w):
    return dict(
        ffn1_norm=sm["ffn1_norm"][i][None], mix_norm=sm["mix_norm"][i][None], xa_norm=sm["xa_norm"][i][None],
        xa_mem_norm=sm["xa_mem_norm"][i][None], ffn2_norm=sm["ffn2_norm"][i][None],
        dn=(conv_w, _lane_pad(sm["dn_a_log"][i], DN_HEADS), _lane_pad(sm["dn_dt_bias"][i], DN_HEADS),
            sm["dn_out_norm"][i][None]),
        sg=(sm["sg_norm_gain"][i][None], sm["sg_norm_bias"][i][None], sm["sg_w_spatial"][i],
            jnp.pad(sm["sg_b_spatial"][i], ((0, 8 - SG_GROUPS), (0, 0)))),
    )


def _ffn_fwd(h, gain, w_gu, w_d, tag):
    n = _rows(_f_rms, [h], [gain], [(h.shape[1], BF16)], tile=512, name=f"{tag}_norm")[0]
    gu = _mm(n, w_gu, NN, out_dtype=BF16, name=f"{tag}_gate_up")
    out = _mm(gu, w_d, NN, out_dtype=F32, res=h, scale=0.5, swiglu_a=True, name=f"{tag}_down")
    return out, (h, n, gu)


def _ffn_bwd(dh, saved, gain, w_gu, w_d, tag):
    h, n, gu = saved
    f = w_d.shape[0]
    d_gate, d_up = _mm_swiglu_bwd(dh, w_d, gu, 0.5, name=f"{tag}_down_dx")
    dw_d = _mm(gu, dh, TN, out_dtype=BF16, scale=0.5, swiglu_a=True, name=f"{tag}_down_dw")
    dn = _mm(d_gate, w_gu, NT, out_dtype=F32, name=f"{tag}_gate_dx")
    dn = _mm(d_up, w_gu, NT, out_dtype=F32, res=dn, b_k0=f, name=f"{tag}_up_dx")
    dw_gu = jnp.concatenate([_mm(n, d_gate, TN, out_dtype=BF16, name=f"{tag}_gate_dw"),
                             _mm(n, d_up, TN, out_dtype=BF16, name=f"{tag}_up_dw")], axis=1)
    dh_in, dgain = _rows_vjp(_f_rms, [h], [gain], [dn], diff=[True], grad_dtypes=[F32], tile=512, add=[dh],
                             name=f"{tag}_norm_bwd")
    return dh_in, dgain, dw_gu, dw_d


def _mixer_fwd(h, p, w_in, w_out, tag, job=None):
    d = h.shape[1]
    n = _rows(_f_rms, [h], [p["mix_norm"]], [(d, BF16)], tile=512, name=f"{tag}_norm")[0]
    w_parts = _split_w_in(w_in)
    xq, xz, xba, xs, xg = (_mm(n, w, NN, out_dtype=BF16 if j == 3 else F32, name=f"{tag}_in{j}") for j, w in enumerate(w_parts))
    oa, s_all, job_out = _dn_forward(xq, xz, xba, p["dn"], name=f"{tag}_dn", job=job)
    swa = [_swa_forward(xs, wnd, dil, name=f"{tag}_swa{j}") for j, (wnd, dil) in enumerate(SWA_PATTERNS)]
    ob = _rows(_f_swa_mix, [o for o, _ in swa] + [l for _, l in swa], [], [(SWA_WIDTH, BF16)], tile=512,
               name=f"{tag}_swa_mix")[0]
    oc = _rows(_f_gmlp, [xg], list(p["sg"]), [(SG_WIDTH, BF16)], tile=256, name=f"{tag}_gmlp")[0]
    merged = jnp.concatenate([oa, ob, oc], axis=1)
    out = _mm(merged, w_out, NN, out_dtype=F32, res=h, name=f"{tag}_out")
    return out, (h, n, xq, xz, xba, xs, xg, s_all, swa, merged), job_out


def _mixer_bwd(dh, saved, p, w_in, w_out, tag, job=None):
    h, n, xq, xz, xba, xs, xg, s_all, swa, merged = saved
    dw_out = _mm(merged, dh, TN, out_dtype=BF16, name=f"{tag}_out_dw")
    doa = _mm(dh, w_out[:DN_WIDTH], NT, out_dtype=F32, name=f"{tag}_out_dxa")
    dob = _mm(dh, w_out[DN_WIDTH:DN_WIDTH + SWA_WIDTH], NT, out_dtype=F32, name=f"{tag}_out_dxb")
    doc = _mm(dh, w_out[DN_WIDTH + SWA_WIDTH:], NT, out_dtype=F32, name=f"{tag}_out_dxc")
    res = _rows_vjp(_f_gmlp, [xg], list(p["sg"]), [doc], diff=[True], grad_dtypes=[BF16], tile=256, name=f"{tag}_gmlp_bwd")
    dxg, d_sg = res[0], res[1:]
    mix_in = [o for o, _ in swa] + [l for _, l in swa]
    d_mix = _rows_vjp(_f_swa_mix, mix_in, [], [dob], diff=[True] * 6, grad_dtypes=[F32] * 6, tile=512,
                      name=f"{tag}_swa_mix_bwd")
    acc = None
    for j, (wnd, dil) in enumerate(SWA_PATTERNS):
        acc = _swa_backward(xs, d_mix[j], d_mix[3 + j], acc, wnd, dil, name=f"{tag}_swa{j}_bwd")
    dxs = jnp.concatenate([a.astype(BF16) for a in acc], axis=1)
    res, job_out = _dn_backward(xq, xz, xba, p["dn"], s_all, doa, name=f"{tag}_dn_bwd", job=job)
    (dxq, dxz, dxba), d_dn = res[:3], res[3:]
    w_parts = _split_w_in(w_in)
    dn = None
    dws = []
    for j, (dx, w) in enumerate(zip((dxq, dxz, dxba, dxs, dxg), w_parts, strict=True)):
        dn = _mm(dx, w, NT, out_dtype=F32, res=dn, name=f"{tag}_in{j}_dx")
        dws.append(_mm(n, dx, TN, out_dtype=BF16, name=f"{tag}_in{j}_dw"))
    dws[2] = dws[2][:, :2 * DN_HEADS]
    dw_in = jnp.concatenate(dws, axis=1)
    dh_in, dgain = _rows_vjp(_f_rms, [h], [p["mix_norm"]], [dn], diff=[True], grad_dtypes=[F32], tile=512, add=[dh],
                             name=f"{tag}_norm_bwd")
    return dh_in, dgain, dw_in, dw_out, d_dn, d_sg, job_out


def _xattn_fwd(h, mem, p, w_q, w_kv, w_o, tag):
    d = h.shape[1]
    n = _rows(_f_rms, [h], [p["xa_norm"]], [(d, BF16)], tile=512, name=f"{tag}_norm")[0]
    mn = _rows(_f_rms, [mem], [p["xa_mem_norm"]], [(d, BF16)], tile=512, name=f"{tag}_mem_norm")[0]
    q = _mm(n, w_q, NN, out_dtype=BF16, name=f"{tag}_q")
    kv = _mm(mn, w_kv, NN, out_dtype=BF16, name=f"{tag}_kv")
    o = _rows(_f_xattn, [q], [kv], [(d, BF16)], tile=256, name=f"{tag}_core")[0]
    out = _mm(o, w_o, NN, out_dtype=F32, res=h, name=f"{tag}_o")
    return out, (h, n, mn, q, kv, o)


def _xattn_bwd(dh, saved, mem, p, w_q, w_kv, w_o, tag):
    h, n, mn, q, kv, o = saved
    do = _mm(dh, w_o, NT, out_dtype=BF16, name=f"{tag}_o_dx")
    dw_o = _mm(o, dh, TN, out_dtype=BF16, name=f"{tag}_o_dw")
    dq, dkv = _rows_vjp(_f_xattn, [q], [kv], [do], diff=[True], grad_dtypes=[BF16], tile=256, name=f"{tag}_core_bwd")
    dn = _mm(dq, w_q, NT, out_dtype=F32, name=f"{tag}_q_dx")
    dw_q = _mm(n, dq, TN, out_dtype=BF16, name=f"{tag}_q_dw")
    dmn = _mm(dkv, w_kv, NT, out_dtype=F32, name=f"{tag}_kv_dx")
    dw_kv = _mm(mn, dkv, TN, out_dtype=BF16, name=f"{tag}_kv_dw")
    dmem_gain = _rows_vjp(_f_rms, [mem], [p["xa_mem_norm"]], [dmn], diff=[False], grad_dtypes=[], tile=512,
                          name=f"{tag}_mem_norm_bwd")[0]
    dh_in, dgain = _rows_vjp(_f_rms, [h], [p["xa_norm"]], [dn], diff=[True], grad_dtypes=[F32], tile=512, add=[dh],
                             name=f"{tag}_norm_bwd")
    return dh_in, dgain, dmem_gain, dw_q, dw_kv, dw_o


def kernel(x, mem, ffn1_norm, ffn1_w_gate_up, ffn1_w_down, mix_norm, mix_w_in, dn_conv_w, dn_a_log, dn_dt_bias, dn_out_norm, sg_norm_gain, sg_norm_bias, sg_w_spatial, sg_b_spatial, mix_w_out, xa_norm, xa_mem_norm, xa_w_q, xa_w_kv, xa_w_o, ffn2_norm, ffn2_w_gate_up, ffn2_w_down, final_norm, loss_target, m_ffn1_norm, m_ffn1_w_gate_up, m_ffn1_w_down, m_mix_norm, m_mix_w_in, m_dn_conv_w, m_dn_a_log, m_dn_dt_bias, m_dn_out_norm, m_sg_norm_gain, m_sg_norm_bias, m_sg_w_spatial, m_sg_b_spatial, m_mix_w_out, m_xa_norm, m_xa_mem_norm, m_xa_w_q, m_xa_w_kv, m_xa_w_o, m_ffn2_norm, m_ffn2_w_gate_up, m_ffn2_w_down, m_final_norm, v_ffn1_norm, v_ffn1_w_gate_up, v_ffn1_w_down, v_mix_norm, v_mix_w_in, v_dn_conv_w, v_dn_a_log, v_dn_dt_bias, v_dn_out_norm, v_sg_norm_gain, v_sg_norm_bias, v_sg_w_spatial, v_sg_b_spatial, v_mix_w_out, v_xa_norm, v_xa_mem_norm, v_xa_w_q, v_xa_w_kv, v_xa_w_o, v_ffn2_norm, v_ffn2_w_gate_up, v_ffn2_w_down, v_final_norm):
    args = dict(locals())
    wts = {k: args[k] for k in WEIGHTS}
    mom_m = {k: args["m_" + k] for k in WEIGHTS}
    mom_v = {k: args["v_" + k] for k in WEIGHTS}
    depth = ffn1_norm.shape[0]
    h = x[0]
    mem2 = mem[0]
    target = loss_target[0]

    assert depth == 2, "core c of a chip is responsible for layer c in the weight and gradient exchanges"
    kinds = {k: "stack" if k == "mix_w_in" else ("row" if BIG_AXIS[k] == 1 else "col") for k in BIG}
    shard = {k: wts[k].astype(BF16) for k in BIG}
    shard["dn_conv_w"] = dn_conv_w
    kinds["dn_conv_w"] = "stack"

    def fetch(names, layer):
        return _gather_job([shard[k] for k in names], [kinds[k] for k in names], layer)

    def arrived(names, whole):
        w = dict(zip(names, whole, strict=True))
        for k in ("mix_w_in", "dn_conv_w"):
            if k in w:
                w[k] = jnp.concatenate([w[k][j] for j in range(N_CHIPS)], axis=1)
        return w

    lw = [arrived(EARLY_W, _run_job(fetch(EARLY_W, 0), name="gather_l0_early")), {}]
    small = {k: wts[k] for k in SMALL}

    saved = []
    for i in range(depth):
        w = lw[i]
        p = _layer_params(small, i, w["dn_conv_w"])
        h, s1 = _ffn_fwd(h, p["ffn1_norm"], w["ffn1_w_gate_up"], w["ffn1_w_down"], f"l{i}_ffn1")
        job = fetch(LATE, i).join(fetch(EARLY_W, i + 1)) if i + 1 < depth else fetch(LATE, i)
        h, s2, fetched = _mixer_fwd(h, p, w["mix_w_in"], w["mix_w_out"], f"l{i}_mix", job=job)
        w.update(arrived(LATE, fetched[:len(LATE)]))
        if i + 1 < depth:
            lw[i + 1].update(arrived(EARLY_W, fetched[len(LATE):]))
        h, s3 = _xattn_fwd(h, mem2, p, w["xa_w_q"], w["xa_w_kv"], w["xa_w_o"], f"l{i}_xa")
        h, s4 = _ffn_fwd(h, p["ffn2_norm"], w["ffn2_w_gate_up"], w["ffn2_w_down"], f"l{i}_ffn2")
        saved.append((p, s1, s2, s3, s4))
    loss_part, dh, d_final = _loss_head(h, target, final_norm[None], name="loss_head")
    loss = lax.psum(loss_part[0, 0], ("x", "y", "c"))

    g_big = {k: [None] * depth for k in BIG}
    g_small = {k: [None] * depth for k in SMALL if k != "final_norm"}
    g_small["dn_conv_w"] = [None] * depth
    def layer_grad(k, i):
        g = g_big[k][i]
        if kinds[k] == "stack":
            n = wts[k].shape[2]
            g = jnp.stack([g[:, j * n:(j + 1) * n] for j in range(N_CHIPS)])
        return g

    def chip_sums(names, i, tag):
        mine = [layer_grad(k, i) for k in names]
        got = _core_to_core(mine, 1 - i, name=f"give_{tag}")
        return [_sum2(a, b, BF16, name=f"sum_cores_{tag}_{k}") for k, a, b in zip(names, mine, got, strict=True)]

    def scatter(names, sums, i):
        return _scatter_job(sums, [wts[k].shape[1:] for k in names], [kinds[k] for k in names], i)

    exchanges = []
    for i in reversed(range(depth)):
        p, s1, s2, s3, s4 = saved[i]
        w = lw[i]
        dh, dg, dw_gu, dw_d = _ffn_bwd(dh, s4, p["ffn2_norm"], w["ffn2_w_gate_up"], w["ffn2_w_down"], f"l{i}_ffn2")
        g_small["ffn2_norm"][i], g_big["ffn2_w_gate_up"][i], g_big["ffn2_w_down"][i] = dg[0], dw_gu, dw_d
        dh, dg, dmg, dw_q, dw_kv, dw_o = _xattn_bwd(dh, s3, mem2, p, w["xa_w_q"], w["xa_w_kv"], w["xa_w_o"], f"l{i}_xa")
        g_small["xa_norm"][i], g_small["xa_mem_norm"][i] = dg[0], dmg[0]
        g_big["xa_w_q"][i], g_big["xa_w_kv"][i], g_big["xa_w_o"][i] = dw_q, dw_kv, dw_o
        ready = [(LATE, i, f"l{i}_late")] + ([(EARLY, i + 1, f"l{i + 1}_early")] if i + 1 < depth else [])
        sums = [chip_sums(names, layer, tag) for names, layer, tag in ready]
        job = scatter(ready[0][0], sums[0], ready[0][1])
        for (names, layer, _), cs in zip(ready[1:], sums[1:], strict=True):
            job = job.join(scatter(names, cs, layer))
        dh, dg, dw_in, dw_out, d_dn, d_sg, landed = _mixer_bwd(dh, s2, p, w["mix_w_in"], w["mix_w_out"], f"l{i}_mix", job=job)
        for (names, layer, _), cs in zip(ready, sums, strict=True):
            exchanges.append((names, layer, cs, landed[:len(names)]))
            landed = landed[len(names):]
        g_small["mix_norm"][i], g_big["mix_w_in"][i], g_big["mix_w_out"][i] = dg[0], dw_in, dw_out
        g_small["dn_conv_w"][i] = d_dn[0]
        g_small["dn_a_log"][i] = d_dn[1][0, DN_HEADS:2 * DN_HEADS]
        g_small["dn_dt_bias"][i] = d_dn[2][0, DN_HEADS:2 * DN_HEADS]
        g_small["dn_out_norm"][i] = d_dn[3][0]
        g_small["sg_norm_gain"][i], g_small["sg_norm_bias"][i] = d_sg[0][0], d_sg[1][0]
        g_small["sg_w_spatial"][i], g_small["sg_b_spatial"][i] = d_sg[2], d_sg[3][:SG_GROUPS]
        dh, dg, dw_gu, dw_d = _ffn_bwd(dh, s1, p["ffn1_norm"], w["ffn1_w_gate_up"], w["ffn1_w_down"], f"l{i}_ffn1")
        g_small["ffn1_norm"][i], g_big["ffn1_w_gate_up"][i], g_big["ffn1_w_down"][i] = dg[0], dw_gu, dw_d
    grad_x = dh[None]
    g_small = {k: jnp.stack(v) for k, v in g_small.items()}
    g_small["final_norm"] = d_final[0]

    core = lax.axis_index("c")
    chip = 2 * lax.axis_index("x") + lax.axis_index("y")
    last = chip_sums(EARLY, 0, "l0_early")
    exchanges.append((EARLY, 0, last, _run_job(scatter(EARLY, last, 0), name="scatter_l0_early")))

    def own_quarter(k, g):
        if kinds[k] == "stack":
            return lax.dynamic_index_in_dim(g, chip, axis=0, keepdims=False)
        axis = BIG_AXIS[k] - 1
        n = wts[k].shape[BIG_AXIS[k]]
        return lax.dynamic_slice_in_dim(g, chip * n, n, axis=axis)

    finished = [{} for _ in range(depth)]
    for names, i, sums, landed in exchanges:
        tag = f"l{i}_{names[0]}"
        parts = [_sum_slots(r, F32, name=f"sum_chips_l{i}_{k}", first=own_quarter(k, g))
                 for k, r, g in zip(names, landed, sums, strict=True)]
        moved = _core_to_core(parts, i, name=f"final_{tag}")
        finished[i].update({k: jnp.where(core == i, a, b) for k, a, b in zip(names, parts, moved, strict=True)})
    g_fin = {k: jnp.stack([finished[i][k] for i in range(depth)]) for k in BIG}
    small_names = list(SMALL) + ["dn_conv_w"]
    small_shapes = [g_small[k].shape for k in small_names]
    small_sum = _sum_slots(_gather_all(_pack([g_small[k] for k in small_names], F32, 64), name="gather_small"), F32,
                           name="sum_small")
    gs = dict(zip(small_names, _unpack(small_sum, small_shapes), strict=True))
    n_conv = dn_conv_w.shape[2]
    gs["dn_conv_w"] = lax.dynamic_slice_in_dim(gs["dn_conv_w"], (2 * lax.axis_index("x") + lax.axis_index("y")) * n_conv,
                                               n_conv, axis=2)

    results = {}
    for k in BIG:
        shp = wts[k].shape
        two_d = lambda a, _s=shp: a.reshape(-1, _s[-1])
        res = _adamw(two_d(wts[k]), two_d(g_fin[k]), two_d(mom_m[k]), two_d(mom_v[k]), name=f"adamw_{k}")
        results[k] = [g_fin[k]] + [r.reshape(shp) for r in res]
    sm_shapes = [wts[k].shape for k in small_names]
    pk = lambda d: _pack([d[k] for k in small_names], F32, 64)
    res = [_unpack(r, sm_shapes) for r in _adamw(pk(wts), pk(gs), pk(mom_m), pk(mom_v), name="adamw_small")]
    for i, k in enumerate(small_names):
        results[k] = [gs[k]] + [res[j][i] for j in range(3)]

    out = [loss, grad_x]
    for j in range(4):
        out += [results[k][j] for k in WEIGHTS]
    return tuple(out)
```

```python
import functools
import math

import jax
import jax.numpy as jnp
from jax import lax
from jax.experimental import pallas as pl
from jax.experimental.pallas import tpu as pltpu

F32, BF16 = jnp.float32, jnp.bfloat16
HI = lax.Precision.HIGHEST
NN, NT, TN = ((1,), (0,)), ((1,), (1,)), ((0,), (0,))

NORM_EPS = 1e-6
LANES = 128
V7X_VMEM_BYTES = 64 * 2**20
VMEM_LIMIT = V7X_VMEM_BYTES * 3 // 4
MM_TM, MM_TN, MM_TK = 1024, 1408, 2816

DN_HEADS, DN_DIM, DN_CHUNK, DN_CONV, HALO = 4, 128, 64, 4, 8
DN_GROUP = 4
INV_BLOCK = 16
DN_WIDTH = DN_HEADS * DN_DIM
SWA_HEADS, SWA_DIM, SWA_BLOCK = 4, 64, 128
SWA_WIDTH = SWA_HEADS * SWA_DIM
SWA_PATTERNS = ((128, 1), (512, 4), (2048, 16))
SG_GROUPS, SG_DIM, SG_CHUNK = 4, 64, 128
SG_WIDTH = SG_GROUPS * SG_DIM
XA_HEADS = 4
IN_SIZES = (3 * DN_WIDTH, DN_WIDTH, DN_HEADS, DN_HEADS, 3 * SWA_WIDTH, 2 * SG_WIDTH)
ADAM_LR, ADAM_B1, ADAM_B2, ADAM_EPS, ADAM_WD, ADAM_STEP = 0.001, 0.9, 0.999, 1e-08, 0.01, 10
N_CHIPS, N_DEV = 4, 8
MESH_ID = pl.DeviceIdType.MESH

BIG = ("ffn1_w_gate_up", "ffn1_w_down", "mix_w_in", "mix_w_out", "xa_w_q", "xa_w_kv", "xa_w_o",
       "ffn2_w_gate_up", "ffn2_w_down")
EARLY = ("ffn1_w_gate_up", "ffn1_w_down", "mix_w_in", "mix_w_out")
LATE = ("xa_w_q", "xa_w_kv", "xa_w_o", "ffn2_w_gate_up", "ffn2_w_down")
EARLY_W = EARLY + ("dn_conv_w",)
BIG_AXIS = {"ffn1_w_gate_up": 2, "ffn1_w_down": 1, "mix_w_in": 2, "mix_w_out": 1, "xa_w_q": 1, "xa_w_kv": 2,
            "xa_w_o": 1, "ffn2_w_gate_up": 2, "ffn2_w_down": 1}
SMALL = ("ffn1_norm", "mix_norm", "dn_a_log", "dn_dt_bias", "dn_out_norm", "sg_norm_gain", "sg_norm_bias",
         "sg_w_spatial", "sg_b_spatial", "xa_norm", "xa_mem_norm", "ffn2_norm", "final_norm")
WEIGHTS = ("ffn1_norm", "ffn1_w_gate_up", "ffn1_w_down", "mix_norm", "mix_w_in", "dn_conv_w", "dn_a_log",
           "dn_dt_bias", "dn_out_norm", "sg_norm_gain", "sg_norm_bias", "sg_w_spatial", "sg_b_spatial",
           "mix_w_out", "xa_norm", "xa_mem_norm", "xa_w_q", "xa_w_kv", "xa_w_o", "ffn2_norm", "ffn2_w_gate_up",
           "ffn2_w_down", "final_norm")


@functools.partial(jax.custom_vjp, nondiff_argnums=(2,))
def _dlo(a, b, dims):
    return lax.dot_general(a.astype(BF16), b.astype(BF16), (dims, ((), ())), preferred_element_type=F32)


def _dlo_fwd(a, b, dims):
    return _dlo(a, b, dims), (a, b)


def _dlo_bwd(dims, saved, g):
    a, b = saved
    if dims == NN:
        da, db = _dlo(g, b, NT), _dlo(a, g, TN)
    elif dims == NT:
        da, db = _dlo(g, b, NN), _dlo(g, a, TN)
    else:
        da, db = _dlo(b, g, NT), _dlo(a, g, NN)
    return da.astype(a.dtype), db.astype(b.dtype)


_dlo.defvjp(_dlo_fwd, _dlo_bwd)


def _dhi(a, b, dims):
    return lax.dot_general(a, b, (dims, ((), ())), preferred_element_type=F32, precision=HI)


def _sigmoid(x):
    return 1.0 / (1.0 + jnp.exp(-x))


def _silu(x):
    return x * _sigmoid(x)


def _softplus(x):
    return jnp.maximum(x, 0.0) + jnp.log(1.0 + jnp.exp(-jnp.abs(x)))


def _rms(x, gain):
    x = x.astype(F32)
    return x * lax.rsqrt(jnp.mean(x * x, axis=-1, keepdims=True) + NORM_EPS) * gain


def _tile(n, target, unit=LANES):
    best = None
    for t in range(unit, min(n, target) + 1, unit):
        if n % t == 0:
            best = t
    return best if best is not None else n


def _cparams(sem):
    return pltpu.CompilerParams(dimension_semantics=sem, vmem_limit_bytes=VMEM_LIMIT)


def _mm(a, b, dims, *, out_dtype, name, res=None, scale=1.0, swiglu_a=False, b_k0=0):
    feat = 2 if swiglu_a else 1
    if dims == NN:
        (m, k), n = (a.shape[0], a.shape[1] // feat), b.shape[1]
    elif dims == NT:
        (m, k), n = a.shape, b.shape[0]
    else:
        (k, m), n = (a.shape[0], a.shape[1] // feat), b.shape[1]
    if dims == TN:
        tm, tn, tk = _tile(m, MM_TN), _tile(n, MM_TN), _tile(k, MM_TM // feat)
    else:
        tm, tn, tk = _tile(m, MM_TM // feat, 8), _tile(n, MM_TN), _tile(k, MM_TK // feat)
    nk = k // tk
    assert b_k0 % tk == 0 and (b_k0 == 0 or dims == NT)
    k0 = b_k0 // tk
    if dims == TN:
        a_specs = [pl.BlockSpec((tk, tm), lambda i, j, kk, _o=o * (m // tm): (kk, i + _o)) for o in range(feat)]
    else:
        a_specs = [pl.BlockSpec((tm, tk), lambda i, j, kk, _o=o * nk: (i, kk + _o)) for o in range(feat)]
    b_spec = pl.BlockSpec((tn, tk), lambda i, j, kk: (j, kk + k0)) if dims == NT else pl.BlockSpec((tk, tn), lambda i, j, kk: (kk, j))
    o_spec = pl.BlockSpec((tm, tn), lambda i, j, kk: (i, j))
    has_res = res is not None

    def finish(acc, r_ref, o_ref):
        val = acc * scale if scale != 1.0 else acc
        if has_res:
            val = r_ref[...].astype(F32) + val
        o_ref[...] = val.astype(o_ref.dtype)

    def body(*refs):
        b_ref = refs[feat]
        r_ref = refs[feat + 1] if has_res else None
        a_val = _silu(refs[0][...]) * refs[1][...] if swiglu_a else refs[0][...]
        part = lax.dot_general(a_val.astype(BF16), b_ref[...].astype(BF16), (dims, ((), ())),
                               preferred_element_type=F32)
        if nk == 1:
            finish(part, r_ref, refs[-1])
            return
        o_ref, acc_ref = refs[-2], refs[-1]
        kk = pl.program_id(2)

        @pl.when(kk == 0)
        def _():
            acc_ref[...] = part

        @pl.when(jnp.logical_and(kk > 0, kk < nk - 1))
        def _():
            acc_ref[...] += part

        @pl.when(kk == nk - 1)
        def _():
            finish(acc_ref[...] + part, r_ref, o_ref)

    return pl.pallas_call(
        body, name=name, grid=(m // tm, n // tn, nk),
        in_specs=a_specs + [b_spec] + ([o_spec] if has_res else []), out_specs=o_spec,
        out_shape=jax.ShapeDtypeStruct((m, n), out_dtype),
        scratch_shapes=[pltpu.VMEM((tm, tn), F32)] if nk > 1 else [],
        compiler_params=_cparams(("parallel", "parallel", "arbitrary")),
    )(*([a] * feat + [b] + ([res] if has_res else [])))


def _mm_swiglu_bwd(dh, w_d, gu, scale, name):
    m, k = dh.shape
    f = w_d.shape[0]
    tm, tn = _tile(m, MM_TM // 2, 8), _tile(f, MM_TN)

    def body(dh_ref, w_ref, g_ref, u_ref, dg_ref, du_ref):
        d_act = lax.dot_general(dh_ref[...].astype(BF16), w_ref[...].astype(BF16), (NT, ((), ())),
                                preferred_element_type=F32) * scale
        _, pull = jax.vjp(lambda g, u: _silu(g) * u, g_ref[...].astype(F32), u_ref[...].astype(F32))
        dg, du = pull(d_act)
        dg_ref[...] = dg.astype(dg_ref.dtype)
        du_ref[...] = du.astype(du_ref.dtype)

    tile = pl.BlockSpec((tm, tn), lambda i, j: (i, j))
    return pl.pallas_call(
        body, name=name, grid=(m // tm, f // tn),
        in_specs=[pl.BlockSpec((tm, k), lambda i, j: (i, 0)), pl.BlockSpec((tn, k), lambda i, j: (j, 0)), tile,
                  pl.BlockSpec((tm, tn), lambda i, j: (i, j + f // tn))],
        out_specs=[tile, tile], out_shape=[jax.ShapeDtypeStruct((m, f), BF16)] * 2,
        compiler_params=_cparams(("parallel", "parallel")),
    )(dh, w_d, gu, gu)


def _full_spec(p):
    nd = p.ndim
    return pl.BlockSpec(p.shape, lambda i, _nd=nd: (0,) * _nd)


def _rows(f, rows, params, outs, *, tile, name):
    t = rows[0].shape[0]
    tile = min(tile, t)
    nr, npar = len(rows), len(params)

    def body(*refs):
        vals = f(*[r[...] for r in refs[:nr + npar]])
        for o_ref, v in zip(refs[nr + npar:], vals, strict=True):
            o_ref[...] = v.astype(o_ref.dtype)

    res = pl.pallas_call(
        body, name=name, grid=(t // tile,),
        in_specs=[pl.BlockSpec((tile, r.shape[1]), lambda i: (i, 0)) for r in rows] + [_full_spec(p) for p in params],
        out_specs=[pl.BlockSpec((tile, w), lambda i: (i, 0)) for w, _ in outs],
        out_shape=[jax.ShapeDtypeStruct((t, w), d) for w, d in outs],
        compiler_params=_cparams(("parallel",)),
    )(*rows, *params)
    return tuple(res)


def _rows_vjp(f, rows, params, cts, *, diff, grad_dtypes, tile, name, add=None):
    t = rows[0].shape[0]
    tile = min(tile, t)
    nr, npar, nct = len(rows), len(params), len(cts)
    didx = [i for i, d in enumerate(diff) if d]
    add = [None] * len(didx) if add is None else add
    adds = [a for a in add if a is not None]

    def body(*refs):
        row_refs, par_refs = refs[:nr], refs[nr:nr + npar]
        ct_refs = refs[nr + npar:nr + npar + nct]
        add_refs = list(refs[nr + npar + nct:nr + npar + nct + len(adds)])
        out_refs = refs[nr + npar + nct + len(adds):]
        rv = [r[...] for r in row_refs]
        pv = [p[...].astype(F32) for p in par_refs]

        def g(*args):
            full = list(rv)
            for k, i in enumerate(didx):
                full[i] = args[k]
            return f(*full, *args[len(didx):])

        outs, pull = jax.vjp(g, *[rv[i] for i in didx], *pv)
        grads = pull(tuple(c[...].astype(o.dtype) for c, o in zip(ct_refs, outs, strict=True)))
        for k in range(len(didx)):
            val = grads[k].astype(F32)
            if add[k] is not None:
                val = val + add_refs.pop(0)[...].astype(F32)
            out_refs[k][...] = val.astype(out_refs[k].dtype)

        @pl.when(pl.program_id(0) == 0)
        def _():
            for o_ref in out_refs[len(didx):]:
                o_ref[...] = jnp.zeros_like(o_ref)

        for o_ref, gp in zip(out_refs[len(didx):], grads[len(didx):], strict=True):
            o_ref[...] += gp.astype(F32)

    row_spec = lambda a: pl.BlockSpec((tile, a.shape[1]), lambda i: (i, 0))
    res = pl.pallas_call(
        body, name=name, grid=(t // tile,),
        in_specs=[row_spec(r) for r in rows] + [_full_spec(p) for p in params] + [row_spec(c) for c in cts]
        + [row_spec(a) for a in adds],
        out_specs=[row_spec(rows[i]) for i in didx] + [_full_spec(p) for p in params],
        out_shape=[jax.ShapeDtypeStruct(rows[i].shape, d) for i, d in zip(didx, grad_dtypes, strict=True)]
        + [jax.ShapeDtypeStruct(p.shape, F32) for p in params],
        compiler_params=_cparams(("arbitrary",)),
    )(*rows, *params, *cts, *adds)
    return tuple(res)


def _sum2(a, b, out_dtype, name):
    shape = a.shape
    views = [x.reshape(-1, shape[-1]) for x in (a, b)]
    r, c = views[0].shape
    tile = _tile(r, max(16, (1 << 18) // c), 16)

    def body(a_ref, b_ref, o_ref):
        o_ref[...] = (a_ref[...].astype(F32) + b_ref[...].astype(F32)).astype(o_ref.dtype)

    spec = pl.BlockSpec((tile, c), lambda i: (i, 0))
    return pl.pallas_call(
        body, name=name, grid=(r // tile,), in_specs=[spec] * 2, out_specs=spec,
        out_shape=jax.ShapeDtypeStruct((r, c), out_dtype), compiler_params=_cparams(("parallel",)),
    )(*views).reshape(shape)


def _sum_slots(x, out_dtype, name, first=None):
    n, r, c = x.shape
    tile = _tile(r, max(16, (1 << 18) // c), 16)

    def body(*refs):
        acc = refs[0][...].astype(F32)
        for ref in refs[1:-1]:
            acc = acc + ref[...].astype(F32)
        refs[-1][...] = acc.astype(refs[-1].dtype)

    spec = pl.BlockSpec((tile, c), lambda i: (i, 0))
    return pl.pallas_call(
        body, name=name, grid=(r // tile,),
        in_specs=([spec] if first is not None else [])
        + [pl.BlockSpec((None, tile, c), lambda i, _s=s_: (_s, i, 0)) for s_ in range(n)],
        out_specs=spec, out_shape=jax.ShapeDtypeStruct((r, c), out_dtype), compiler_params=_cparams(("parallel",)),
    )(*(([first] if first is not None else []) + [x] * n))


def _adamw(w, g, m, v, name):
    r, c = w.shape
    tile = _tile(r, max(8, (1 << 18) // c), 8)

    def body(w_ref, g_ref, m_ref, v_ref, d_out, m_out, v_out):
        g = g_ref[...]
        mn = ADAM_B1 * m_ref[...] + (1.0 - ADAM_B1) * g
        vn = ADAM_B2 * v_ref[...] + (1.0 - ADAM_B2) * (g * g)
        m_hat = mn / (1.0 - ADAM_B1 ** ADAM_STEP)
        v_hat = vn / (1.0 - ADAM_B2 ** ADAM_STEP)
        d_out[...] = -ADAM_LR * (m_hat / (jnp.sqrt(v_hat) + ADAM_EPS) + ADAM_WD * w_ref[...])
        m_out[...] = mn
        v_out[...] = vn

    spec = pl.BlockSpec((tile, c), lambda i: (i, 0))
    return pl.pallas_call(
        body, name=name, grid=(r // tile,), in_specs=[spec] * 4, out_specs=[spec] * 3,
        out_shape=[jax.ShapeDtypeStruct((r, c), F32)] * 3, compiler_params=_cparams(("parallel",)),
    )(w, g, m, v)


def _place():
    return lax.axis_index("x"), lax.axis_index("y"), lax.axis_index("c")


def _flip(v, bit):
    return 1 - v if bit else v


_ANY = pl.BlockSpec(memory_space=pl.ANY)


def _quarter(ref, j, shape, kind):
    if kind == "row":
        return ref.at[pl.ds(j * shape[0], shape[0])]
    if kind == "col":
        return ref.at[:, pl.ds(j * shape[1], shape[1])]
    return ref.at[j]


def _whole_shape(shape, kind):
    if kind == "row":
        return (N_CHIPS * shape[0],) + tuple(shape[1:])
    if kind == "col":
        return (shape[0], N_CHIPS * shape[1]) + tuple(shape[2:])
    return (N_CHIPS,) + tuple(shape)


def _dma_sems(*counts):
    return [pltpu.SemaphoreType.DMA((n,)) for n in counts]


class _Job:
    def __init__(self, ins, outs, sems, phases, at):
        self.ins, self.outs, self.sems, self.phases, self.at = list(ins), list(outs), list(sems), list(phases), list(at)

    def specs(self):
        return [_ANY] * len(self.ins), [_ANY] * len(self.outs), _dma_sems(*self.sems)

    def run_at(self, step, n_steps, in_refs, out_refs, sem_refs):
        for phase, frac in zip(self.phases, self.at, strict=True):
            @pl.when(step == int(frac * (n_steps - 1)))
            def _():
                phase(in_refs, out_refs, sem_refs)

    def join(self, other):
        ni, no, ns = len(self.ins), len(self.outs), len(self.sems)
        assert self.at == other.at

        def both(mine, theirs):
            def phase(ins, outs, sems):
                mine(ins[:ni], outs[:no], sems[:ns])
                theirs(ins[ni:], outs[no:], sems[ns:])
            return phase

        return _Job(self.ins + other.ins, self.outs + other.outs, self.sems + other.sems,
                    [both(a, b) for a, b in zip(self.phases, other.phases, strict=True)], self.at)


def _run_job(job, name):
    def body(*refs):
        ni, no = len(job.ins), len(job.outs)
        for phase in job.phases:
            phase(refs[:ni], refs[ni:ni + no], refs[ni + no:])

    in_specs, out_specs, sems = job.specs()
    return pl.pallas_call(body, name=name, in_specs=in_specs, out_specs=out_specs, out_shape=job.outs,
                          scratch_shapes=sems)(*job.ins)


def _remote(src, dst, sems, i, j, peer):
    return pltpu.make_async_remote_copy(src_ref=src, dst_ref=dst, send_sem=sems[i].at[j], recv_sem=sems[i + 1].at[j],
                                        device_id=peer, device_id_type=MESH_ID)


def _gather_job(shards, kinds, layer):
    n = len(shards)
    shapes = [s.shape[1:] for s in shards]
    other = 1 - layer

    def window(outs, t, j):
        return _quarter(outs[t], j, shapes[t], kinds[t])

    def start(ins, outs, sems):
        mx, my, mc = _place()
        me = 2 * mx + my

        @pl.when(mc == layer)
        def _():
            for t in range(n):
                for k in range(1, N_CHIPS):
                    _remote(ins[t].at[layer], window(outs, t, me), sems, 0, 3 * t + k - 1,
                            (_flip(mx, k >> 1), _flip(my, k & 1), layer)).start()

        @pl.when(mc == other)
        def _():
            for t in range(n):
                _remote(ins[t].at[layer], window(outs, t, me), sems, 2, t, (mx, my, layer)).start()

    def relay(ins, outs, sems):
        mx, my, mc = _place()
        me = 2 * mx + my

        @pl.when(mc == layer)
        def _():
            for t in range(n):
                _remote(ins[t].at[layer], window(outs, t, me), sems, 2, t, (mx, my, other)).wait_recv()
                for k in range(1, N_CHIPS):
                    px, py = _flip(mx, k >> 1), _flip(my, k & 1)
                    cp = _remote(ins[t].at[layer], window(outs, t, 2 * px + py), sems, 0, 3 * t + k - 1, (px, py, layer))
                    cp.wait_recv()
                    cp.wait_send()
                _remote(outs[t], outs[t], sems, 4, t, (mx, my, other)).start()

        @pl.when(mc == other)
        def _():
            for t in range(n):
                _remote(ins[t].at[layer], window(outs, t, me), sems, 2, t, (mx, my, layer)).wait_send()

    def finish(ins, outs, sems):
        mx, my, mc = _place()

        @pl.when(mc == layer)
        def _():
            for t in range(n):
                _remote(outs[t], outs[t], sems, 4, t, (mx, my, other)).wait_send()

        @pl.when(mc == other)
        def _():
            for t in range(n):
                _remote(outs[t], outs[t], sems, 4, t, (mx, my, layer)).wait_recv()

    outs = [jax.ShapeDtypeStruct(_whole_shape(sh, kd), s.dtype) for s, sh, kd in zip(shards, shapes, kinds)]
    return _Job(shards, outs, [3 * n, 3 * n, n, n, n, n], [start, relay, finish], [0.0, 0.75, 1.0])


def _scatter_job(gs, shapes, kinds, layer):
    n = len(gs)

    def copies(ins, outs, sems):
        mx, my, _ = _place()
        return [_remote(_quarter(ins[t], 2 * _flip(mx, k >> 1) + _flip(my, k & 1), shapes[t], kinds[t]), outs[t].at[k - 1],
                        sems, 0, 3 * t + k - 1, (_flip(mx, k >> 1), _flip(my, k & 1), layer))
                for t in range(n) for k in range(1, N_CHIPS)]

    def start(ins, outs, sems):
        @pl.when(lax.axis_index("c") == layer)
        def _():
            for cp in copies(ins, outs, sems):
                cp.start()

    def finish(ins, outs, sems):
        @pl.when(lax.axis_index("c") == layer)
        def _():
            for cp in copies(ins, outs, sems):
                cp.wait_recv()
                cp.wait_send()

    outs = [jax.ShapeDtypeStruct((N_CHIPS - 1,) + tuple(sh), g.dtype) for g, sh in zip(gs, shapes)]
    return _Job(gs, outs, [3 * n, 3 * n], [start, finish], [0.0, 1.0])


def _core_to_core(parts, sender, name):
    n = len(parts)

    def move(ins, outs, sems):
        mx, my, mc = _place()

        @pl.when(mc == sender)
        def _():
            copies = [_remote(ins[t], outs[t], sems, 0, t, (mx, my, 1 - sender)) for t in range(n)]
            for cp in copies:
                cp.start()
            for cp in copies:
                cp.wait_send()

        @pl.when(mc != sender)
        def _():
            for t in range(n):
                _remote(ins[t], outs[t], sems, 0, t, (mx, my, sender)).wait_recv()

    return _run_job(_Job(parts, [jax.ShapeDtypeStruct(p.shape, p.dtype) for p in parts], [n, n], [move], [0.0]), name)


def _gather_all(x, name):
    r, w = x.shape

    def body(x_ref, o_ref, send_sems, recv_sems, local_sem):
        mx, my, mc = _place()
        mine = 4 * mx + 2 * my + mc
        local = pltpu.make_async_copy(x_ref, o_ref.at[mine], local_sem)
        local.start()
        copies = []
        for k in range(1, N_DEV):
            peer = (_flip(mx, k >> 2), _flip(my, (k >> 1) & 1), _flip(mc, k & 1))
            copies.append(pltpu.make_async_remote_copy(
                src_ref=x_ref, dst_ref=o_ref.at[mine], send_sem=send_sems.at[k - 1], recv_sem=recv_sems.at[k - 1],
                device_id=peer, device_id_type=MESH_ID))
            copies[-1].start()
        for k in range(1, N_DEV):
            peer = (_flip(mx, k >> 2), _flip(my, (k >> 1) & 1), _flip(mc, k & 1))
            pltpu.make_async_remote_copy(
                src_ref=x_ref, dst_ref=o_ref.at[4 * peer[0] + 2 * peer[1] + peer[2]], send_sem=send_sems.at[k - 1],
                recv_sem=recv_sems.at[k - 1], device_id=peer, device_id_type=MESH_ID).wait_recv()
        for cp in copies:
            cp.wait_send()
        local.wait()

    return pl.pallas_call(
        body, name=name, in_specs=[_ANY], out_specs=_ANY, out_shape=jax.ShapeDtypeStruct((N_DEV, r, w), x.dtype),
        scratch_shapes=[pltpu.SemaphoreType.DMA((N_DEV - 1,)), pltpu.SemaphoreType.DMA((N_DEV - 1,)),
                        pltpu.SemaphoreType.DMA],
    )(x)


def _pack(parts, dtype, row_unit):
    flat = jnp.concatenate([p.astype(dtype).reshape(-1) for p in parts])
    unit = row_unit * LANES
    pad = (-flat.shape[0]) % unit
    if pad:
        flat = jnp.concatenate([flat, jnp.zeros((pad,), dtype)])
    return flat.reshape(-1, LANES)


def _unpack(packed, shapes):
    flat = packed.reshape(-1)
    out, off = [], 0
    for s in shapes:
        n = math.prod(s)
        out.append(flat[off:off + n].reshape(s))
        off += n
    return out


def _f_rms(x, gain):
    return (_rms(x, gain),)


def _f_xattn(q, kv):
    d = q.shape[1]
    hd = d // XA_HEADS
    outs = []
    for h in range(XA_HEADS):
        qh, kh, vh = q[:, h * hd:(h + 1) * hd], kv[:, h * hd:(h + 1) * hd], kv[:, d + h * hd:d + (h + 1) * hd]
        s = _dlo(qh, kh, NT) * (hd ** -0.5)
        s = s - jnp.max(s, axis=-1, keepdims=True)
        p = jnp.exp(s)
        p = p / jnp.sum(p, axis=-1, keepdims=True)
        outs.append(_dlo(p, vh, NN))
    return (jnp.concatenate(outs, axis=1),)


def _f_gmlp(uv, gain, bias, w_sp, b_sp):
    r = uv.shape[0]
    act = jax.nn.gelu(uv.astype(F32))
    u, v = act[:, :SG_WIDTH], act[:, SG_WIDTH:]
    mu = jnp.mean(v, axis=-1, keepdims=True)
    var = jnp.mean(jnp.square(v - mu), axis=-1, keepdims=True)
    v = (v - mu) * lax.rsqrt(var + NORM_EPS) * gain + bias
    row = lax.broadcasted_iota(jnp.int32, (SG_CHUNK, SG_CHUNK), 0)
    col = lax.broadcasted_iota(jnp.int32, (SG_CHUNK, SG_CHUNK), 1)
    lane_grp = lax.broadcasted_iota(jnp.int32, (b_sp.shape[0], SG_WIDTH), 1) // SG_DIM
    grp_row = lax.broadcasted_iota(jnp.int32, (b_sp.shape[0], SG_WIDTH), 0)
    spread = jnp.where(lane_grp == grp_row, 1.0, 0.0).astype(F32)
    bias_t = _dhi(b_sp, spread, TN)
    chunks = []
    for c in range(r // SG_CHUNK):
        vc = v[c * SG_CHUNK:(c + 1) * SG_CHUNK]
        parts = []
        for g in range(SG_GROUPS):
            wg = jnp.where(row >= col, w_sp[g], 0.0)
            parts.append(_dlo(wg, vc[:, g * SG_DIM:(g + 1) * SG_DIM], NN))
        chunks.append(jnp.concatenate(parts, axis=1) + bias_t)
    mixed = jnp.concatenate(chunks, axis=0) if len(chunks) > 1 else chunks[0]
    return (u * mixed,)


def _f_swa_mix(o1, o2, o3, l1, l2, l3):
    outs = []
    for h in range(SWA_HEADS):
        ls = [l[:, h:h + 1] for l in (l1, l2, l3)]
        mx = jnp.maximum(jnp.maximum(ls[0], ls[1]), ls[2])
        es = [jnp.exp(l - mx) for l in ls]
        den = es[0] + es[1] + es[2]
        sl = slice(h * SWA_DIM, (h + 1) * SWA_DIM)
        outs.append((es[0] * o1[:, sl] + es[1] * o2[:, sl] + es[2] * o3[:, sl]) / den)
    return (jnp.concatenate(outs, axis=1),)


def _swa_block(q, kp, kc, vp, vc, first, window, dilation):
    span = window // dilation
    rows = SWA_HEADS * SWA_BLOCK
    ri = lax.broadcasted_iota(jnp.int32, (rows, 2 * SWA_BLOCK), 0)
    kj = lax.broadcasted_iota(jnp.int32, (rows, 2 * SWA_BLOCK), 1)
    head = ri // SWA_BLOCK
    rel = SWA_BLOCK + ri % SWA_BLOCK - kj
    valid = (rel >= 0) & (rel <= span) & jnp.logical_not(jnp.logical_and(first, kj < SWA_BLOCK))
    slope = jnp.exp((head + 1).astype(F32) * (-8.0 / SWA_HEADS * math.log(2.0)))
    bias = slope * (rel * dilation).astype(F32)
    kw = jnp.concatenate([kp, kc], axis=0)
    vw = jnp.concatenate([vp, vc], axis=0)
    lane_head = lax.broadcasted_iota(jnp.int32, (rows, SWA_WIDTH), 1) // SWA_DIM
    own_head = lane_head == lax.broadcasted_iota(jnp.int32, (rows, SWA_WIDTH), 0) // SWA_BLOCK
    q_rows = jnp.where(own_head, jnp.concatenate([q] * SWA_HEADS, axis=0), 0.0)
    s = _dlo(q_rows, kw, NT) * (SWA_DIM ** -0.5) - bias
    s = jnp.where(valid, s, -1e30)
    m = jnp.max(s, axis=-1, keepdims=True)
    p = jnp.exp(s - m)
    den = jnp.sum(p, axis=-1, keepdims=True)
    wide = _dlo(p, vw, NN) / den
    lse_rows = m + jnp.log(den)
    lane = lax.broadcasted_iota(jnp.int32, (SWA_BLOCK, LANES), 1)
    outs, lse = [], jnp.zeros((SWA_BLOCK, LANES), F32)
    for h in range(SWA_HEADS):
        outs.append(wide[h * SWA_BLOCK:(h + 1) * SWA_BLOCK, h * SWA_DIM:(h + 1) * SWA_DIM])
        lse = lse + jnp.where(lane == h, lse_rows[h * SWA_BLOCK:(h + 1) * SWA_BLOCK], 0.0)
    return jnp.concatenate(outs, axis=1), lse


@jax.custom_vjp
def _unit_lower_inv(lower):
    c = lower.shape[0]
    assert DN_CHUNK == 4 * INV_BLOCK and c % DN_CHUNK == 0
    row = lax.broadcasted_iota(jnp.int32, (c, c), 0)
    col = lax.broadcasted_iota(jnp.int32, (c, c), 1)
    eye = jnp.where(row == col, 1.0, 0.0).astype(F32)
    same = (row // INV_BLOCK) == (col // INV_BLOCK)
    pw = -jnp.where(same, lower, 0.0)
    d_inv = eye + pw
    for _ in range(int(math.log2(INV_BLOCK)) - 1):
        pw = _dlo(pw, pw, NN)
        d_inv = d_inv + _dlo(d_inv, pw, NN)
    n1 = _dlo(d_inv, jnp.where(same, 0.0, lower), NN)
    n2 = _dlo(n1, n1, NN)
    rough = _dlo(eye - n1 + n2 - _dlo(n1, n2, NN), d_inv, NN)
    residual = eye - _dhi(eye + lower, rough, NN)
    return rough + _dlo(rough, residual, NN)


def _unit_lower_inv_fwd(lower):
    t_inv = _unit_lower_inv(lower)
    return t_inv, t_inv


def _unit_lower_inv_bwd(t_inv, g):
    return (-_dlo(_dlo(t_inv, g, TN), t_inv, NT),)


_unit_lower_inv.defvjp(_unit_lower_inv_fwd, _unit_lower_inv_bwd)


@jax.custom_vjp
def _known_lower_inv(lower, t_inv):
    return t_inv


def _known_lower_inv_fwd(lower, t_inv):
    return t_inv, t_inv


def _known_lower_inv_bwd(t_inv, g):
    return _unit_lower_inv_bwd(t_inv, g) + (jnp.zeros_like(t_inv),)


_known_lower_inv.defvjp(_known_lower_inv_fwd, _known_lower_inv_bwd)


def _dn_group(xx, z, ba, state, conv_w, a_log, dt_bias, gain, known_inv=None):
    rows, c = z.shape[0], DN_CHUNK
    hc = DN_HEADS * c
    acc = conv_w[0:1] * xx[HALO - 3:HALO - 3 + rows]
    for j in range(1, DN_CONV):
        acc = acc + conv_w[j:j + 1] * xx[HALO - 3 + j:HALO - 3 + j + rows]
    qkv = _silu(acc)
    beta_all = _sigmoid(ba)
    g_all = -jnp.exp(a_log) * _softplus(ba + dt_bias)
    row = lax.broadcasted_iota(jnp.int32, (hc, hc), 0)
    col = lax.broadcasted_iota(jnp.int32, (hc, hc), 1)
    same_head = (row // c) == (col // c)
    incl, strict = same_head & (row >= col), same_head & (row > col)
    tri = jnp.where(incl[:c, :c], 1.0, 0.0).astype(F32)

    def stack(piece):
        return jnp.concatenate([piece(h) for h in range(DN_HEADS)], axis=0)

    local, inverses = [], []
    for ci in range(rows // c):
        r0 = ci * c
        gc_all = _dhi(tri, g_all[r0:r0 + c], NN)
        gc_t = gc_all.T
        q = stack(lambda h: qkv[r0:r0 + c, h * DN_DIM:(h + 1) * DN_DIM])
        k = stack(lambda h: qkv[r0:r0 + c, DN_WIDTH + h * DN_DIM:DN_WIDTH + (h + 1) * DN_DIM])
        v = stack(lambda h: qkv[r0:r0 + c, 2 * DN_WIDTH + h * DN_DIM:2 * DN_WIDTH + (h + 1) * DN_DIM])
        q = q * lax.rsqrt(jnp.sum(q * q, axis=-1, keepdims=True) + NORM_EPS) * (DN_DIM ** -0.5)
        k = k * lax.rsqrt(jnp.sum(k * k, axis=-1, keepdims=True) + NORM_EPS)
        beta = stack(lambda h: beta_all[r0:r0 + c, h:h + 1])
        gc = stack(lambda h: gc_all[:, DN_HEADS + h:DN_HEADS + h + 1])
        g_last = stack(lambda h: jnp.broadcast_to(gc_all[c - 1:c, DN_HEADS + h:DN_HEADS + h + 1], (c, 1)))
        gc_row = jnp.concatenate([gc_t[DN_HEADS + h:DN_HEADS + h + 1, :] for h in range(DN_HEADS)], axis=1)
        decay = jnp.where(incl, jnp.exp(jnp.where(incl, gc - gc_row, 0.0)), 0.0)
        kb = k * beta
        lower = jnp.where(strict, _dlo(kb, k, NT) * decay, 0.0)
        t_inv = _unit_lower_inv(lower) if known_inv is None else _known_lower_inv(lower, known_inv[ci].astype(F32))
        inverses.append(t_inv)
        e_gc = jnp.exp(gc)
        u = _dlo(t_inv, v * beta, NN)
        w = _dlo(t_inv, kb * e_gc, NN)
        a_qk = jnp.where(incl, _dlo(q, k, NT) * decay, 0.0)
        e_last = jnp.concatenate([jnp.broadcast_to(jnp.exp(gc_all[c - 1:c, DN_HEADS + h:DN_HEADS + h + 1]), (1, DN_DIM))
                                  for h in range(DN_HEADS)], axis=1)
        local.append((jnp.concatenate([w, q * e_gc], axis=0), k * jnp.exp(g_last - gc), u, a_qk, e_last))
    own = (lax.broadcasted_iota(jnp.int32, (hc, DN_WIDTH), 0) // c) == (lax.broadcasted_iota(jnp.int32, (hc, DN_WIDTH), 1) // DN_DIM)

    def own_blocks(m):
        return stack(lambda h: m[h * c:(h + 1) * c, h * DN_DIM:(h + 1) * DN_DIM])

    s = state
    out_rows = []
    for ci in range(rows // c):
        r0 = ci * c
        wq, k_tail, u, a_qk, e_last = local[ci]
        through = _dlo(wq, s, NN)
        v_new = u - own_blocks(through[:hc])
        o = own_blocks(through[hc:]) + _dlo(a_qk, v_new, NN)
        v_wide = jnp.where(own, jnp.concatenate([v_new] * DN_HEADS, axis=1), 0.0)
        s = s * e_last + _dlo(k_tail, v_wide, TN)
        o = o * lax.rsqrt(jnp.mean(o * o, axis=-1, keepdims=True) + NORM_EPS) * gain
        o = jnp.concatenate([o[h * c:(h + 1) * c] for h in range(DN_HEADS)], axis=1)
        out_rows.append(o * _silu(z[r0:r0 + c]))
    out = jnp.concatenate(out_rows, axis=0) if len(out_rows) > 1 else out_rows[0]
    return out, s, inverses


def _dn_specs(n_of, rows):
    return [pl.BlockSpec((rows, 3 * DN_WIDTH), lambda i: (n_of(i), 0)),
            pl.BlockSpec((HALO, 3 * DN_WIDTH), lambda i: (jnp.maximum(n_of(i) * (rows // HALO) - 1, 0), 0)),
            pl.BlockSpec((rows, DN_WIDTH), lambda i: (n_of(i), 0)),
            pl.BlockSpec((rows, LANES), lambda i: (n_of(i), 0))]


def _no_job():
    return _Job([], [], [], [], [])


def _dn_forward(xq, xz, xba, params, name, job=None):
    t = xq.shape[0]
    rows = min(DN_GROUP * DN_CHUNK, t)
    n_groups = t // rows
    job = job or _no_job()
    job_in, job_out, job_sems = job.specs()

    def body(*refs):
        x_ref, halo_ref, z_ref, ba_ref, cw_ref, al_ref, dt_ref, gn_ref = refs[:8]
        ji = 8 + len(job_in)
        jo = ji + 3 + len(job_out)
        o_ref, s_all_ref, inv_ref, s_ref = refs[ji], refs[ji + 1], refs[ji + 2], refs[jo]
        n = pl.program_id(0)
        job.run_at(n, n_groups, refs[8:ji], refs[ji + 3:jo], refs[jo + 1:])

        @pl.when(n == 0)
        def _():
            s_ref[...] = jnp.zeros_like(s_ref)

        halo = jnp.where(n > 0, halo_ref[...], 0.0)
        xx = jnp.concatenate([halo, x_ref[...]], axis=0)
        s_all_ref[0] = s_ref[...]
        o, s_new, inverses = _dn_group(xx, z_ref[...], ba_ref[...], s_ref[...], cw_ref[...], al_ref[...], dt_ref[...],
                                       gn_ref[...])
        o_ref[...] = o.astype(o_ref.dtype)
        s_ref[...] = s_new
        for ci, t_inv in enumerate(inverses):
            inv_ref[ci] = t_inv.astype(inv_ref.dtype)

    assert len(params) == 4
    per_group, wide = rows // DN_CHUNK, DN_HEADS * DN_CHUNK
    res = pl.pallas_call(
        body, name=name, grid=(n_groups,),
        in_specs=_dn_specs(lambda i: i, rows) + [_full_spec(p) for p in params] + job_in,
        out_specs=[pl.BlockSpec((rows, DN_WIDTH), lambda i: (i, 0)),
                   pl.BlockSpec((1, DN_DIM, DN_WIDTH), lambda i: (i, 0, 0)),
                   pl.BlockSpec((per_group, wide, wide), lambda i: (i, 0, 0))] + job_out,
        out_shape=[jax.ShapeDtypeStruct((t, DN_WIDTH), BF16),
                   jax.ShapeDtypeStruct((n_groups, DN_DIM, DN_WIDTH), F32),
                   jax.ShapeDtypeStruct((t // DN_CHUNK, wide, wide), BF16)] + job.outs,
        scratch_shapes=[pltpu.VMEM((DN_DIM, DN_WIDTH), F32)] + job_sems,
        compiler_params=_cparams(("arbitrary",)),
    )(xq, xq, xz, xba, *params, *job.ins)
    return res[0], (res[1], res[2]), list(res[3:])


def _dn_backward(xq, xz, xba, params, kept, d_out, name, job=None):
    t = xq.shape[0]
    rows = min(DN_GROUP * DN_CHUNK, t)
    n_groups = t // rows
    rev = lambda i: n_groups - 1 - i
    job = job or _no_job()
    job_in, job_out, job_sems = job.specs()

    def body(*refs):
        x_ref, halo_ref, z_ref, ba_ref, cw_ref, al_ref, dt_ref, gn_ref, s_ref, do_ref, inv_ref = refs[:11]
        ji = 11 + len(job_in)
        dx_ref, dz_ref, dba_ref, dcw_ref, dal_ref, ddt_ref, dgn_ref = refs[ji:ji + 7]
        jo = ji + 7 + len(job_out)
        ds_ref, dhalo_ref = refs[jo], refs[jo + 1]
        i = pl.program_id(0)
        n = n_groups - 1 - i
        job.run_at(i, n_groups, refs[11:ji], refs[ji + 7:jo], refs[jo + 2:])

        @pl.when(i == 0)
        def _():
            ds_ref[...] = jnp.zeros_like(ds_ref)
            dhalo_ref[...] = jnp.zeros_like(dhalo_ref)
            for r in (dcw_ref, dal_ref, ddt_ref, dgn_ref):
                r[...] = jnp.zeros_like(r)

        halo = jnp.where(n > 0, halo_ref[...], 0.0)
        xx = jnp.concatenate([halo, x_ref[...]], axis=0)
        known = inv_ref[...]
        _, pull = jax.vjp(lambda *a: _dn_group(*a, known_inv=known)[:2], xx, z_ref[...], ba_ref[...], s_ref[0],
                          cw_ref[...], al_ref[...], dt_ref[...], gn_ref[...])
        dxx, dz, dba, ds, dcw, dal, ddt, dgn = pull((do_ref[...].astype(F32), ds_ref[...]))
        dx_ref[...] = jnp.concatenate([dxx[HALO:rows], dxx[rows:] + dhalo_ref[...]], axis=0).astype(dx_ref.dtype)
        dhalo_ref[...] = dxx[:HALO]
        dz_ref[...] = dz.astype(dz_ref.dtype)
        dba_ref[...] = dba.astype(dba_ref.dtype)
        ds_ref[...] = ds
        dcw_ref[...] += dcw
        dal_ref[...] += dal
        ddt_ref[...] += ddt
        dgn_ref[...] += dgn

    assert len(params) == 4
    s_all, inv_all = kept
    per_group, wide = rows // DN_CHUNK, DN_HEADS * DN_CHUNK
    res = pl.pallas_call(
        body, name=name, grid=(n_groups,),
        in_specs=_dn_specs(rev, rows) + [_full_spec(p) for p in params]
        + [pl.BlockSpec((1, DN_DIM, DN_WIDTH), lambda i: (rev(i), 0, 0)),
           pl.BlockSpec((rows, DN_WIDTH), lambda i: (rev(i), 0)),
           pl.BlockSpec((per_group, wide, wide), lambda i: (rev(i), 0, 0))] + job_in,
        out_specs=[pl.BlockSpec((rows, 3 * DN_WIDTH), lambda i: (rev(i), 0)),
                   pl.BlockSpec((rows, DN_WIDTH), lambda i: (rev(i), 0)),
                   pl.BlockSpec((rows, LANES), lambda i: (rev(i), 0))] + [_full_spec(p) for p in params] + job_out,
        out_shape=[jax.ShapeDtypeStruct(xq.shape, BF16), jax.ShapeDtypeStruct(xz.shape, BF16),
                   jax.ShapeDtypeStruct(xba.shape, BF16)] + [jax.ShapeDtypeStruct(p.shape, F32) for p in params] + job.outs,
        scratch_shapes=[pltpu.VMEM((DN_DIM, DN_WIDTH), F32), pltpu.VMEM((HALO, 3 * DN_WIDTH), F32)] + job_sems,
        compiler_params=_cparams(("arbitrary",)),
    )(xq, xq, xz, xba, *params, s_all, d_out, inv_all, *job.ins)
    return tuple(res[:7]), list(res[7:])


def _swa_forward(xs, window, dilation, name):
    t = xs.shape[0]
    d, l = dilation, t // dilation
    nb = l // SWA_BLOCK
    view = xs.reshape(l, d * 3 * SWA_WIDTH)
    blk = (SWA_BLOCK, SWA_WIDTH)

    def body(q_ref, kp_ref, kc_ref, vp_ref, vc_ref, o_ref, l_ref):
        blocks = [r[...].astype(F32) for r in (q_ref, kp_ref, kc_ref, vp_ref, vc_ref)]
        o, lse = _swa_block(*blocks, pl.program_id(1) == 0, window, dilation)
        o_ref[...] = o
        l_ref[...] = lse

    prev = lambda n: jnp.maximum(n - 1, 0)
    o, lse = pl.pallas_call(
        body, name=name, grid=(d, nb),
        in_specs=[pl.BlockSpec(blk, lambda r, n: (n, 3 * r)), pl.BlockSpec(blk, lambda r, n: (prev(n), 3 * r + 1)),
                  pl.BlockSpec(blk, lambda r, n: (n, 3 * r + 1)), pl.BlockSpec(blk, lambda r, n: (prev(n), 3 * r + 2)),
                  pl.BlockSpec(blk, lambda r, n: (n, 3 * r + 2))],
        out_specs=[pl.BlockSpec(blk, lambda r, n: (n, r)), pl.BlockSpec((SWA_BLOCK, LANES), lambda r, n: (n, r))],
        out_shape=[jax.ShapeDtypeStruct((l, d * SWA_WIDTH), F32), jax.ShapeDtypeStruct((l, d * LANES), F32)],
        compiler_params=_cparams(("parallel", "parallel")),
    )(view, view, view, view, view)
    return o.reshape(t, SWA_WIDTH), lse.reshape(t, LANES)


def _swa_backward(xs, d_o, d_lse, acc, window, dilation, name):
    t = xs.shape[0]
    d, l = dilation, t // dilation
    nb = l // SWA_BLOCK
    view = xs.reshape(l, d * 3 * SWA_WIDTH)
    blk = (SWA_BLOCK, SWA_WIDTH)
    has_acc = acc is not None

    def body(*refs):
        q_ref, kp_ref, kc_ref, vp_ref, vc_ref, do_ref, dl_ref = refs[:7]
        acc_refs = refs[7:10] if has_acc else None
        dq_ref, dk_ref, dv_ref, ck_ref, cv_ref = refs[-5:]
        i = pl.program_id(1)
        n = nb - 1 - i

        @pl.when(i == 0)
        def _():
            ck_ref[...] = jnp.zeros_like(ck_ref)
            cv_ref[...] = jnp.zeros_like(cv_ref)

        f = functools.partial(_swa_block, first=n == 0, window=window, dilation=dilation)
        _, pull = jax.vjp(f, *[r[...].astype(F32) for r in (q_ref, kp_ref, kc_ref, vp_ref, vc_ref)])
        dq, dkp, dkc, dvp, dvc = pull((do_ref[...], dl_ref[...]))
        dk = dkc + ck_ref[...]
        dv = dvc + cv_ref[...]
        if has_acc:
            dq, dk, dv = dq + acc_refs[0][...], dk + acc_refs[1][...], dv + acc_refs[2][...]
        dq_ref[...] = dq
        dk_ref[...] = dk
        dv_ref[...] = dv
        ck_ref[...] = dkp
        cv_ref[...] = dvp

    cur = lambda i: nb - 1 - i
    prev = lambda i: jnp.maximum(nb - 2 - i, 0)
    own = pl.BlockSpec(blk, lambda r, i: (cur(i), r))
    accs = [a.reshape(l, d * SWA_WIDTH) for a in acc] if has_acc else []
    outs = pl.pallas_call(
        body, name=name, grid=(d, nb),
        in_specs=[pl.BlockSpec(blk, lambda r, i: (cur(i), 3 * r)), pl.BlockSpec(blk, lambda r, i: (prev(i), 3 * r + 1)),
                  pl.BlockSpec(blk, lambda r, i: (cur(i), 3 * r + 1)), pl.BlockSpec(blk, lambda r, i: (prev(i), 3 * r + 2)),
                  pl.BlockSpec(blk, lambda r, i: (cur(i), 3 * r + 2)), own,
                  pl.BlockSpec((SWA_BLOCK, LANES), lambda r, i: (cur(i), r))] + [own] * len(accs),
        out_specs=[own] * 3, out_shape=[jax.ShapeDtypeStruct((l, d * SWA_WIDTH), F32)] * 3,
        scratch_shapes=[pltpu.VMEM(blk, F32), pltpu.VMEM(blk, F32)],
        compiler_params=_cparams(("parallel", "arbitrary")),
    )(view, view, view, view, view, d_o.reshape(l, d * SWA_WIDTH), d_lse.reshape(l, d * LANES), *accs)
    return tuple(o.reshape(t, SWA_WIDTH) for o in outs)


def _loss_head(h, target, gain, name, tile=256):
    t, d = h.shape
    tile = min(tile, t)

    def body(h_ref, t_ref, g_ref, loss_ref, dh_ref, dg_ref):
        def f(hv, gv):
            err = _rms(hv, gv) - t_ref[...]
            return 0.5 * jnp.sum(jnp.mean(err * err, axis=-1, keepdims=True), axis=0, keepdims=True)

        val, pull = jax.vjp(f, h_ref[...], g_ref[...])
        dh, dg = pull(jnp.ones((1, 1), F32))
        dh_ref[...] = dh

        @pl.when(pl.program_id(0) == 0)
        def _():
            loss_ref[...] = jnp.zeros_like(loss_ref)
            dg_ref[...] = jnp.zeros_like(dg_ref)

        loss_ref[...] += jnp.broadcast_to(val, loss_ref.shape)
        dg_ref[...] += dg

    return pl.pallas_call(
        body, name=name, grid=(t // tile,),
        in_specs=[pl.BlockSpec((tile, d), lambda i: (i, 0)), pl.BlockSpec((tile, d), lambda i: (i, 0)), _full_spec(gain)],
        out_specs=[pl.BlockSpec((1, LANES), lambda i: (0, 0)), pl.BlockSpec((tile, d), lambda i: (i, 0)), _full_spec(gain)],
        out_shape=[jax.ShapeDtypeStruct((1, LANES), F32), jax.ShapeDtypeStruct((t, d), F32),
                   jax.ShapeDtypeStruct(gain.shape, F32)],
        compiler_params=_cparams(("arbitrary",)),
    )(h, target, gain)


def _split_w_in(w_in):
    cuts = [0]
    for s in IN_SIZES:
        cuts.append(cuts[-1] + s)
    qkv, z = w_in[:, cuts[0]:cuts[1]], w_in[:, cuts[1]:cuts[2]]
    ba = jnp.pad(w_in[:, cuts[2]:cuts[4]], ((0, 0), (0, LANES - 2 * DN_HEADS)))
    return qkv, z, ba, w_in[:, cuts[4]:cuts[5]], w_in[:, cuts[5]:cuts[6]]


def _lane_pad(v, offset):
    return jnp.pad(v.reshape(1, -1), ((0, 0), (offset, LANES - offset - v.shape[0])))


def _layer_params(sm, i, conv_w):
    return dict(
        ffn1_norm=sm["ffn1_norm"][i][None], mix_norm=sm["mix_norm"][i][None], xa_norm=sm["xa_norm"][i][None],
        xa_mem_norm=sm["xa_mem_norm"][i][None], ffn2_norm=sm["ffn2_norm"][i][None],
        dn=(conv_w, _lane_pad(sm["dn_a_log"][i], DN_HEADS), _lane_pad(sm["dn_dt_bias"][i], DN_HEADS),
            sm["dn_out_norm"][i][None]),
        sg=(sm["sg_norm_gain"][i][None], sm["sg_norm_bias"][i][None], sm["sg_w_spatial"][i],
            jnp.pad(sm["sg_b_spatial"][i], ((0, 8 - SG_GROUPS), (0, 0)))),
    )


def _ffn_fwd(h, gain, w_gu, w_d, tag):
    n = _rows(_f_rms, [h], [gain], [(h.shape[1], BF16)], tile=512, name=f"{tag}_norm")[0]
    gu = _mm(n, w_gu, NN, out_dtype=BF16, name=f"{tag}_gate_up")
    out = _mm(gu, w_d, NN, out_dtype=F32, res=h, scale=0.5, swiglu_a=True, name=f"{tag}_down")
    return out, (h, n, gu)


def _ffn_bwd(dh, saved, gain, w_gu, w_d, tag):
    h, n, gu = saved
    f = w_d.shape[0]
    d_gate, d_up = _mm_swiglu_bwd(dh, w_d, gu, 0.5, name=f"{tag}_down_dx")
    dw_d = _mm(gu, dh, TN, out_dtype=BF16, scale=0.5, swiglu_a=True, name=f"{tag}_down_dw")
    dn = _mm(d_gate, w_gu, NT, out_dtype=F32, name=f"{tag}_gate_dx")
    dn = _mm(d_up, w_gu, NT, out_dtype=F32, res=dn, b_k0=f, name=f"{tag}_up_dx")
    dw_gu = jnp.concatenate([_mm(n, d_gate, TN, out_dtype=BF16, name=f"{tag}_gate_dw"),
                             _mm(n, d_up, TN, out_dtype=BF16, name=f"{tag}_up_dw")], axis=1)
    dh_in, dgain = _rows_vjp(_f_rms, [h], [gain], [dn], diff=[True], grad_dtypes=[F32], tile=512, add=[dh],
                             name=f"{tag}_norm_bwd")
    return dh_in, dgain, dw_gu, dw_d


def _mixer_fwd(h, p, w_in, w_out, tag, job=None):
    d = h.shape[1]
    n = _rows(_f_rms, [h], [p["mix_norm"]], [(d, BF16)], tile=512, name=f"{tag}_norm")[0]
    w_parts = _split_w_in(w_in)
    xq, xz, xba, xs, xg = (_mm(n, w, NN, out_dtype=BF16 if j == 3 else F32, name=f"{tag}_in{j}") for j, w in enumerate(w_parts))
    oa, s_all, job_out = _dn_forward(xq, xz, xba, p["dn"], name=f"{tag}_dn", job=job)
    swa = [_swa_forward(xs, wnd, dil, name=f"{tag}_swa{j}") for j, (wnd, dil) in enumerate(SWA_PATTERNS)]
    ob = _rows(_f_swa_mix, [o for o, _ in swa] + [l for _, l in swa], [], [(SWA_WIDTH, BF16)], tile=512,
               name=f"{tag}_swa_mix")[0]
    oc = _rows(_f_gmlp, [xg], list(p["sg"]), [(SG_WIDTH, BF16)], tile=256, name=f"{tag}_gmlp")[0]
    merged = jnp.concatenate([oa, ob, oc], axis=1)
    out = _mm(merged, w_out, NN, out_dtype=F32, res=h, name=f"{tag}_out")
    return out, (h, n, xq, xz, xba, xs, xg, s_all, swa, merged), job_out


def _mixer_bwd(dh, saved, p, w_in, w_out, tag, job=None):
    h, n, xq, xz, xba, xs, xg, s_all, swa, merged = saved
    dw_out = _mm(merged, dh, TN, out_dtype=BF16, name=f"{tag}_out_dw")
    doa = _mm(dh, w_out[:DN_WIDTH], NT, out_dtype=F32, name=f"{tag}_out_dxa")
    dob = _mm(dh, w_out[DN_WIDTH:DN_WIDTH + SWA_WIDTH], NT, out_dtype=F32, name=f"{tag}_out_dxb")
    doc = _mm(dh, w_out[DN_WIDTH + SWA_WIDTH:], NT, out_dtype=F32, name=f"{tag}_out_dxc")
    res = _rows_vjp(_f_gmlp, [xg], list(p["sg"]), [doc], diff=[True], grad_dtypes=[BF16], tile=256, name=f"{tag}_gmlp_bwd")
    dxg, d_sg = res[0], res[1:]
    mix_in = [o for o, _ in swa] + [l for _, l in swa]
    d_mix = _rows_vjp(_f_swa_mix, mix_in, [], [dob], diff=[True] * 6, grad_dtypes=[F32] * 6, tile=512,
                      name=f"{tag}_swa_mix_bwd")
    acc = None
    for j, (wnd, dil) in enumerate(SWA_PATTERNS):
        acc = _swa_backward(xs, d_mix[j], d_mix[3 + j], acc, wnd, dil, name=f"{tag}_swa{j}_bwd")
    dxs = jnp.concatenate([a.astype(BF16) for a in acc], axis=1)
    res, job_out = _dn_backward(xq, xz, xba, p["dn"], s_all, doa, name=f"{tag}_dn_bwd", job=job)
    (dxq, dxz, dxba), d_dn = res[:3], res[3:]
    w_parts = _split_w_in(w_in)
    dn = None
    dws = []
    for j, (dx, w) in enumerate(zip((dxq, dxz, dxba, dxs, dxg), w_parts, strict=True)):
        dn = _mm(dx, w, NT, out_dtype=F32, res=dn, name=f"{tag}_in{j}_dx")
        dws.append(_mm(n, dx, TN, out_dtype=BF16, name=f"{tag}_in{j}_dw"))
    dws[2] = dws[2][:, :2 * DN_HEADS]
    dw_in = jnp.concatenate(dws, axis=1)
    dh_in, dgain = _rows_vjp(_f_rms, [h], [p["mix_norm"]], [dn], diff=[True], grad_dtypes=[F32], tile=512, add=[dh],
                             name=f"{tag}_norm_bwd")
    return dh_in, dgain, dw_in, dw_out, d_dn, d_sg, job_out


def _xattn_fwd(h, mem, p, w_q, w_kv, w_o, tag):
    d = h.shape[1]
    n = _rows(_f_rms, [h], [p["xa_norm"]], [(d, BF16)], tile=512, name=f"{tag}_norm")[0]
    mn = _rows(_f_rms, [mem], [p["xa_mem_norm"]], [(d, BF16)], tile=512, name=f"{tag}_mem_norm")[0]
    q = _mm(n, w_q, NN, out_dtype=BF16, name=f"{tag}_q")
    kv = _mm(mn, w_kv, NN, out_dtype=BF16, name=f"{tag}_kv")
    o = _rows(_f_xattn, [q], [kv], [(d, BF16)], tile=256, name=f"{tag}_core")[0]
    out = _mm(o, w_o, NN, out_dtype=F32, res=h, name=f"{tag}_o")
    return out, (h, n, mn, q, kv, o)


def _xattn_bwd(dh, saved, mem, p, w_q, w_kv, w_o, tag):
    h, n, mn, q, kv, o = saved
    do = _mm(dh, w_o, NT, out_dtype=BF16, name=f"{tag}_o_dx")
    dw_o = _mm(o, dh, TN, out_dtype=BF16, name=f"{tag}_o_dw")
    dq, dkv = _rows_vjp(_f_xattn, [q], [kv], [do], diff=[True], grad_dtypes=[BF16], tile=256, name=f"{tag}_core_bwd")
    dn = _mm(dq, w_q, NT, out_dtype=F32, name=f"{tag}_q_dx")
    dw_q = _mm(n, dq, TN, out_dtype=BF16, name=f"{tag}_q_dw")
    dmn = _mm(dkv, w_kv, NT, out_dtype=F32, name=f"{tag}_kv_dx")
    dw_kv = _mm(mn, dkv, TN, out_dtype=BF16, name=f"{tag}_kv_dw")
    dmem_gain = _rows_vjp(_f_rms, [mem], [p["xa_mem_norm"]], [dmn], diff=[False], grad_dtypes=[], tile=512,
                          name=f"{tag}_mem_norm_bwd")[0]
    dh_in, dgain = _rows_vjp(_f_rms, [h], [p["xa_norm"]], [dn], diff=[True], grad_dtypes=[F32], tile=512, add=[dh],
                             name=f"{tag}_norm_bwd")
    return dh_in, dgain, dmem_gain, dw_q, dw_kv, dw_o


def kernel(x, mem, ffn1_norm, ffn1_w_gate_up, ffn1_w_down, mix_norm, mix_w_in, dn_conv_w, dn_a_log, dn_dt_bias, dn_out_norm, sg_norm_gain, sg_norm_bias, sg_w_spatial, sg_b_spatial, mix_w_out, xa_norm, xa_mem_norm, xa_w_q, xa_w_kv, xa_w_o, ffn2_norm, ffn2_w_gate_up, ffn2_w_down, final_norm, loss_target, m_ffn1_norm, m_ffn1_w_gate_up, m_ffn1_w_down, m_mix_norm, m_mix_w_in, m_dn_conv_w, m_dn_a_log, m_dn_dt_bias, m_dn_out_norm, m_sg_norm_gain, m_sg_norm_bias, m_sg_w_spatial, m_sg_b_spatial, m_mix_w_out, m_xa_norm, m_xa_mem_norm, m_xa_w_q, m_xa_w_kv, m_xa_w_o, m_ffn2_norm, m_ffn2_w_gate_up, m_ffn2_w_down, m_final_norm, v_ffn1_norm, v_ffn1_w_gate_up, v_ffn1_w_down, v_mix_norm, v_mix_w_in, v_dn_conv_w, v_dn_a_log, v_dn_dt_bias, v_dn_out_norm, v_sg_norm_gain, v_sg_norm_bias, v_sg_w_spatial, v_sg_b_spatial, v_mix_w_out, v_xa_norm, v_xa_mem_norm, v_xa_w_q, v_xa_w_kv, v_xa_w_o, v_ffn2_norm, v_ffn2_w_gate_up, v_ffn2_w_down, v_final_norm):
    args = dict(locals())
    wts = {k: args[k] for k in WEIGHTS}
    mom_m = {k: args["m_" + k] for k in WEIGHTS}
    mom_v = {k: args["v_" + k] for k in WEIGHTS}
    depth = ffn1_norm.shape[0]
    h = x[0]
    mem2 = mem[0]
    target = loss_target[0]

    assert depth == 2, "core c of a chip is responsible for layer c in the weight and gradient exchanges"
    kinds = {k: "stack" if k == "mix_w_in" else ("row" if BIG_AXIS[k] == 1 else "col") for k in BIG}
    shard = {k: wts[k].astype(BF16) for k in BIG}
    shard["dn_conv_w"] = dn_conv_w
    kinds["dn_conv_w"] = "stack"

    def fetch(names, layer):
        return _gather_job([shard[k] for k in names], [kinds[k] for k in names], layer)

    def arrived(names, whole):
        w = dict(zip(names, whole, strict=True))
        for k in ("mix_w_in", "dn_conv_w"):
            if k in w:
                w[k] = jnp.concatenate([w[k][j] for j in range(N_CHIPS)], axis=1)
        return w

    lw = [arrived(EARLY_W, _run_job(fetch(EARLY_W, 0), name="gather_l0_early")), {}]
    small = {k: wts[k] for k in SMALL}

    saved = []
    for i in range(depth):
        w = lw[i]
        p = _layer_params(small, i, w["dn_conv_w"])
        h, s1 = _ffn_fwd(h, p["ffn1_norm"], w["ffn1_w_gate_up"], w["ffn1_w_down"], f"l{i}_ffn1")
        job = fetch(LATE, i).join(fetch(EARLY_W, i + 1)) if i + 1 < depth else fetch(LATE, i)
        h, s2, fetched = _mixer_fwd(h, p, w["mix_w_in"], w["mix_w_out"], f"l{i}_mix", job=job)
        w.update(arrived(LATE, fetched[:len(LATE)]))
        if i + 1 < depth:
            lw[i + 1].update(arrived(EARLY_W, fetched[len(LATE):]))
        h, s3 = _xattn_fwd(h, mem2, p, w["xa_w_q"], w["xa_w_kv"], w["xa_w_o"], f"l{i}_xa")
        h, s4 = _ffn_fwd(h, p["ffn2_norm"], w["ffn2_w_gate_up"], w["ffn2_w_down"], f"l{i}_ffn2")
        saved.append((p, s1, s2, s3, s4))
    loss_part, dh, d_final = _loss_head(h, target, final_norm[None], name="loss_head")
    loss = lax.psum(loss_part[0, 0], ("x", "y", "c"))

    g_big = {k: [None] * depth for k in BIG}
    g_small = {k: [None] * depth for k in SMALL if k != "final_norm"}
    g_small["dn_conv_w"] = [None] * depth
    def layer_grad(k, i):
        g = g_big[k][i]
        if kinds[k] == "stack":
            n = wts[k].shape[2]
            g = jnp.stack([g[:, j * n:(j + 1) * n] for j in range(N_CHIPS)])
        return g

    def chip_sums(names, i, tag):
        mine = [layer_grad(k, i) for k in names]
        got = _core_to_core(mine, 1 - i, name=f"give_{tag}")
        return [_sum2(a, b, BF16, name=f"sum_cores_{tag}_{k}") for k, a, b in zip(names, mine, got, strict=True)]

    def scatter(names, sums, i):
        return _scatter_job(sums, [wts[k].shape[1:] for k in names], [kinds[k] for k in names], i)

    exchanges = []
    for i in reversed(range(depth)):
        p, s1, s2, s3, s4 = saved[i]
        w = lw[i]
        dh, dg, dw_gu, dw_d = _ffn_bwd(dh, s4, p["ffn2_norm"], w["ffn2_w_gate_up"], w["ffn2_w_down"], f"l{i}_ffn2")
        g_small["ffn2_norm"][i], g_big["ffn2_w_gate_up"][i], g_big["ffn2_w_down"][i] = dg[0], dw_gu, dw_d
        dh, dg, dmg, dw_q, dw_kv, dw_o = _xattn_bwd(dh, s3, mem2, p, w["xa_w_q"], w["xa_w_kv"], w["xa_w_o"], f"l{i}_xa")
        g_small["xa_norm"][i], g_small["xa_mem_norm"][i] = dg[0], dmg[0]
        g_big["xa_w_q"][i], g_big["xa_w_kv"][i], g_big["xa_w_o"][i] = dw_q, dw_kv, dw_o
        ready = [(LATE, i, f"l{i}_late")] + ([(EARLY, i + 1, f"l{i + 1}_early")] if i + 1 < depth else [])
        sums = [chip_sums(names, layer, tag) for names, layer, tag in ready]
        job = scatter(ready[0][0], sums[0], ready[0][1])
        for (names, layer, _), cs in zip(ready[1:], sums[1:], strict=True):
            job = job.join(scatter(names, cs, layer))
        dh, dg, dw_in, dw_out, d_dn, d_sg, landed = _mixer_bwd(dh, s2, p, w["mix_w_in"], w["mix_w_out"], f"l{i}_mix", job=job)
        for (names, layer, _), cs in zip(ready, sums, strict=True):
            exchanges.append((names, layer, cs, landed[:len(names)]))
            landed = landed[len(names):]
        g_small["mix_norm"][i], g_big["mix_w_in"][i], g_big["mix_w_out"][i] = dg[0], dw_in, dw_out
        g_small["dn_conv_w"][i] = d_dn[0]
        g_small["dn_a_log"][i] = d_dn[1][0, DN_HEADS:2 * DN_HEADS]
        g_small["dn_dt_bias"][i] = d_dn[2][0, DN_HEADS:2 * DN_HEADS]
        g_small["dn_out_norm"][i] = d_dn[3][0]
        g_small["sg_norm_gain"][i], g_small["sg_norm_bias"][i] = d_sg[0][0], d_sg[1][0]
        g_small["sg_w_spatial"][i], g_small["sg_b_spatial"][i] = d_sg[2], d_sg[3][:SG_GROUPS]
        dh, dg, dw_gu, dw_d = _ffn_bwd(dh, s1, p["ffn1_norm"], w["ffn1_w_gate_up"], w["ffn1_w_down"], f"l{i}_ffn1")
        g_small["ffn1_norm"][i], g_big["ffn1_w_gate_up"][i], g_big["ffn1_w_down"][i] = dg[0], dw_gu, dw_d
    grad_x = dh[None]
    g_small = {k: jnp.stack(v) for k, v in g_small.items()}
    g_small["final_norm"] = d_final[0]

    core = lax.axis_index("c")
    chip = 2 * lax.axis_index("x") + lax.axis_index("y")
    last = chip_sums(EARLY, 0, "l0_early")
    exchanges.append((EARLY, 0, last, _run_job(scatter(EARLY, last, 0), name="scatter_l0_early")))

    def own_quarter(k, g):
        if kinds[k] == "stack":
            return lax.dynamic_index_in_dim(g, chip, axis=0, keepdims=False)
        axis = BIG_AXIS[k] - 1
        n = wts[k].shape[BIG_AXIS[k]]
        return lax.dynamic_slice_in_dim(g, chip * n, n, axis=axis)

    finished = [{} for _ in range(depth)]
    for names, i, sums, landed in exchanges:
        tag = f"l{i}_{names[0]}"
        parts = [_sum_slots(r, F32, name=f"sum_chips_l{i}_{k}", first=own_quarter(k, g))
                 for k, r, g in zip(names, landed, sums, strict=True)]
        moved = _core_to_core(parts, i, name=f"final_{tag}")
        finished[i].update({k: jnp.where(core == i, a, b) for k, a, b in zip(names, parts, moved, strict=True)})
    g_fin = {k: jnp.stack([finished[i][k] for i in range(depth)]) for k in BIG}
    small_names = list(SMALL) + ["dn_conv_w"]
    small_shapes = [g_small[k].shape for k in small_names]
    small_sum = _sum_slots(_gather_all(_pack([g_small[k] for k in small_names], F32, 64), name="gather_small"), F32,
                           name="sum_small")
    gs = dict(zip(small_names, _unpack(small_sum, small_shapes), strict=True))
    n_conv = dn_conv_w.shape[2]
    gs["dn_conv_w"] = lax.dynamic_slice_in_dim(gs["dn_conv_w"], (2 * lax.axis_index("x") + lax.axis_index("y")) * n_conv,
                                               n_conv, axis=2)

    results = {}
    for k in BIG:
        shp = wts[k].shape
        two_d = lambda a, _s=shp: a.reshape(-1, _s[-1])
        res = _adamw(two_d(wts[k]), two_d(g_fin[k]), two_d(mom_m[k]), two_d(mom_v[k]), name=f"adamw_{k}")
        results[k] = [g_fin[k]] + [r.reshape(shp) for r in res]
    sm_shapes = [wts[k].shape for k in small_names]
    pk = lambda d: _pack([d[k] for k in small_names], F32, 64)
    res = [_unpack(r, sm_shapes) for r in _adamw(pk(wts), pk(gs), pk(mom_m), pk(mom_v), name="adamw_small")]
    for i, k in enumerate(small_names):
        results[k] = [gs[k]] + [res[j][i] for j in range(3)]

    out = [loss, grad_x]
    for j in range(4):
        out += [results[k][j] for k in WEIGHTS]
    return tuple(out)
```

```python
import functools
import math

import jax
import jax.numpy as jnp
from jax import lax
from jax.experimental import pallas as pl
from jax.experimental.pallas import tpu as pltpu

F32, BF16 = jnp.float32, jnp.bfloat16
HI = lax.Precision.HIGHEST
NN, NT, TN = ((1,), (0,)), ((1,), (1,)), ((0,), (0,))

NORM_EPS = 1e-6
LANES = 128
V7X_VMEM_BYTES = 64 * 2**20
VMEM_LIMIT = V7X_VMEM_BYTES * 3 // 4
MM_TM, MM_TN, MM_TK = 1024, 1408, 2816

DN_HEADS, DN_DIM, DN_CHUNK, DN_CONV, HALO = 4, 128, 64, 4, 8
DN_GROUP = 4
INV_BLOCK = 16
DN_WIDTH = DN_HEADS * DN_DIM
SWA_HEADS, SWA_DIM, SWA_BLOCK = 4, 64, 128
SWA_WIDTH = SWA_HEADS * SWA_DIM
SWA_PATTERNS = ((128, 1), (512, 4), (2048, 16))
SG_GROUPS, SG_DIM, SG_CHUNK = 4, 64, 128
SG_WIDTH = SG_GROUPS * SG_DIM
XA_HEADS = 4
IN_SIZES = (3 * DN_WIDTH, DN_WIDTH, DN_HEADS, DN_HEADS, 3 * SWA_WIDTH, 2 * SG_WIDTH)
ADAM_LR, ADAM_B1, ADAM_B2, ADAM_EPS, ADAM_WD, ADAM_STEP = 0.001, 0.9, 0.999, 1e-08, 0.01, 10
N_CHIPS, N_DEV = 4, 8
MESH_ID = pl.DeviceIdType.MESH

BIG = ("ffn1_w_gate_up", "ffn1_w_down", "mix_w_in", "mix_w_out", "xa_w_q", "xa_w_kv", "xa_w_o",
       "ffn2_w_gate_up", "ffn2_w_down")
EARLY = ("ffn1_w_gate_up", "ffn1_w_down", "mix_w_in", "mix_w_out")
LATE = ("xa_w_q", "xa_w_kv", "xa_w_o", "ffn2_w_gate_up", "ffn2_w_down")
EARLY_W = EARLY + ("dn_conv_w",)
BIG_AXIS = {"ffn1_w_gate_up": 2, "ffn1_w_down": 1, "mix_w_in": 2, "mix_w_out": 1, "xa_w_q": 1, "xa_w_kv": 2,
            "xa_w_o": 1, "ffn2_w_gate_up": 2, "ffn2_w_down": 1}
SMALL = ("ffn1_norm", "mix_norm", "dn_a_log", "dn_dt_bias", "dn_out_norm", "sg_norm_gain", "sg_norm_bias",
         "sg_w_spatial", "sg_b_spatial", "xa_norm", "xa_mem_norm", "ffn2_norm", "final_norm")
WEIGHTS = ("ffn1_norm", "ffn1_w_gate_up", "ffn1_w_down", "mix_norm", "mix_w_in", "dn_conv_w", "dn_a_log",
           "dn_dt_bias", "dn_out_norm", "sg_norm_gain", "sg_norm_bias", "sg_w_spatial", "sg_b_spatial",
           "mix_w_out", "xa_norm", "xa_mem_norm", "xa_w_q", "xa_w_kv", "xa_w_o", "ffn2_norm", "ffn2_w_gate_up",
           "ffn2_w_down", "final_norm")


@functools.partial(jax.custom_vjp, nondiff_argnums=(2,))
def _dlo(a, b, dims):
    return lax.dot_general(a.astype(BF16), b.astype(BF16), (dims, ((), ())), preferred_element_type=F32)


def _dlo_fwd(a, b, dims):
    return _dlo(a, b, dims), (a, b)


def _dlo_bwd(dims, saved, g):
    a, b = saved
    if dims == NN:
        da, db = _dlo(g, b, NT), _dlo(a, g, TN)
    elif dims == NT:
        da, db = _dlo(g, b, NN), _dlo(g, a, TN)
    else:
        da, db = _dlo(b, g, NT), _dlo(a, g, NN)
    return da.astype(a.dtype), db.astype(b.dtype)


_dlo.defvjp(_dlo_fwd, _dlo_bwd)


def _dhi(a, b, dims):
    return lax.dot_general(a, b, (dims, ((), ())), preferred_element_type=F32, precision=HI)


def _sigmoid(x):
    return 1.0 / (1.0 + jnp.exp(-x))


def _silu(x):
    return x * _sigmoid(x)


def _softplus(x):
    return jnp.maximum(x, 0.0) + jnp.log(1.0 + jnp.exp(-jnp.abs(x)))


def _rms(x, gain):
    x = x.astype(F32)
    return x * lax.rsqrt(jnp.mean(x * x, axis=-1, keepdims=True) + NORM_EPS) * gain


def _tile(n, target, unit=LANES):
    best = None
    for t in range(unit, min(n, target) + 1, unit):
        if n % t == 0:
            best = t
    return best if best is not None else n


def _cparams(sem):
    return pltpu.CompilerParams(dimension_semantics=sem, vmem_limit_bytes=VMEM_LIMIT)


def _mm(a, b, dims, *, out_dtype, name, res=None, scale=1.0, swiglu_a=False, b_k0=0, beside=0, into=None, into_n0=0):
    feat = 2 if swiglu_a else 1
    if dims == NN:
        (m, k), n = (a.shape[0], a.shape[1] // feat), b.shape[1]
    elif dims == NT:
        (m, k), n = a.shape, b.shape[0]
    else:
        (k, m), n = (a.shape[0], a.shape[1] // feat), b.shape[1]
    if dims == TN:
        tm, tn, tk = _tile(m, MM_TN), _tile(n, MM_TN), _tile(k, MM_TM // feat)
    else:
        tm, tn, tk = _tile(m, MM_TM // feat, 8), _tile(n, MM_TN), _tile(k, MM_TK // feat)
    nk = k // tk
    assert b_k0 % tk == 0 and (b_k0 == 0 or dims == NT)
    k0 = b_k0 // tk
    if dims == TN:
        a_specs = [pl.BlockSpec((tk, tm), lambda i, j, kk, _o=o * (m // tm): (kk, i + _o)) for o in range(feat)]
    else:
        a_specs = [pl.BlockSpec((tm, tk), lambda i, j, kk, _o=o * nk: (i, kk + _o)) for o in range(feat)]
    b_spec = pl.BlockSpec((tn, tk), lambda i, j, kk: (j, kk + k0)) if dims == NT else pl.BlockSpec((tk, tn), lambda i, j, kk: (kk, j))
    o_spec = pl.BlockSpec((tm, tn), lambda i, j, kk: (i, j))
    has_res = res is not None

    def finish(acc, r_ref, o_ref):
        val = acc * scale if scale != 1.0 else acc
        if has_res:
            val = r_ref[...].astype(F32) + val
        o_ref[...] = val.astype(o_ref.dtype)

    def body(*refs):
        b_ref = refs[feat]
        r_ref = refs[feat + 1] if has_res else None
        a_val = _silu(refs[0][...]) * refs[1][...] if swiglu_a else refs[0][...]
        part = lax.dot_general(a_val.astype(BF16), b_ref[...].astype(BF16), (dims, ((), ())),
                               preferred_element_type=F32)
        if nk == 1:
            finish(part, r_ref, refs[-1])
            return
        o_ref, acc_ref = refs[-2], refs[-1]
        kk = pl.program_id(2)

        @pl.when(kk == 0)
        def _():
            acc_ref[...] = part

        @pl.when(jnp.logical_and(kk > 0, kk < nk - 1))
        def _():
            acc_ref[...] += part

        @pl.when(kk == nk - 1)
        def _():
            finish(acc_ref[...] + part, r_ref, o_ref)

    assert not (has_res and (beside or into is not None)) and into_n0 % tn == 0
    n_off = into_n0 // tn
    placed = pl.BlockSpec((tm, tn), lambda i, j, kk: (i, j + n_off))
    operands = [a] * feat + [b] + ([res] if has_res else []) + ([into] if into is not None else [])
    return pl.pallas_call(
        body, name=name, grid=(m // tm, n // tn, nk),
        in_specs=a_specs + [b_spec] + ([o_spec] if has_res else []) + ([_ANY] if into is not None else []),
        out_specs=placed,
        out_shape=jax.ShapeDtypeStruct((m, into.shape[1] if into is not None else (beside or n)), out_dtype),
        scratch_shapes=[pltpu.VMEM((tm, tn), F32)] if nk > 1 else [],
        input_output_aliases={len(operands) - 1: 0} if into is not None else {},
        compiler_params=_cparams(("parallel", "parallel", "arbitrary")),
    )(*operands)


def _mm_swiglu_bwd(dh, w_d, gu, scale, name):
    m, k = dh.shape
    f = w_d.shape[0]
    tm, tn = _tile(m, MM_TM // 2, 8), _tile(f, MM_TN)

    def body(dh_ref, w_ref, g_ref, u_ref, dg_ref, du_ref):
        d_act = lax.dot_general(dh_ref[...].astype(BF16), w_ref[...].astype(BF16), (NT, ((), ())),
                                preferred_element_type=F32) * scale
        _, pull = jax.vjp(lambda g, u: _silu(g) * u, g_ref[...].astype(F32), u_ref[...].astype(F32))
        dg, du = pull(d_act)
        dg_ref[...] = dg.astype(dg_ref.dtype)
        du_ref[...] = du.astype(du_ref.dtype)

    tile = pl.BlockSpec((tm, tn), lambda i, j: (i, j))
    return pl.pallas_call(
        body, name=name, grid=(m // tm, f // tn),
        in_specs=[pl.BlockSpec((tm, k), lambda i, j: (i, 0)), pl.BlockSpec((tn, k), lambda i, j: (j, 0)), tile,
                  pl.BlockSpec((tm, tn), lambda i, j: (i, j + f // tn))],
        out_specs=[tile, tile], out_shape=[jax.ShapeDtypeStruct((m, f), BF16)] * 2,
        compiler_params=_cparams(("parallel", "parallel")),
    )(dh, w_d, gu, gu)


def _full_spec(p):
    nd = p.ndim
    return pl.BlockSpec(p.shape, lambda i, _nd=nd: (0,) * _nd)


def _rows(f, rows, params, outs, *, tile, name):
    t = rows[0].shape[0]
    tile = min(tile, t)
    nr, npar = len(rows), len(params)

    def body(*refs):
        vals = f(*[r[...] for r in refs[:nr + npar]])
        for o_ref, v in zip(refs[nr + npar:], vals, strict=True):
            o_ref[...] = v.astype(o_ref.dtype)

    res = pl.pallas_call(
        body, name=name, grid=(t // tile,),
        in_specs=[pl.BlockSpec((tile, r.shape[1]), lambda i: (i, 0)) for r in rows] + [_full_spec(p) for p in params],
        out_specs=[pl.BlockSpec((tile, w), lambda i: (i, 0)) for w, _ in outs],
        out_shape=[jax.ShapeDtypeStruct((t, w), d) for w, d in outs],
        compiler_params=_cparams(("parallel",)),
    )(*rows, *params)
    return tuple(res)


def _rows_vjp(f, rows, params, cts, *, diff, grad_dtypes, tile, name, add=None):
    t = rows[0].shape[0]
    tile = min(tile, t)
    nr, npar, nct = len(rows), len(params), len(cts)
    didx = [i for i, d in enumerate(diff) if d]
    add = [None] * len(didx) if add is None else add
    adds = [a for a in add if a is not None]

    def body(*refs):
        row_refs, par_refs = refs[:nr], refs[nr:nr + npar]
        ct_refs = refs[nr + npar:nr + npar + nct]
        add_refs = list(refs[nr + npar + nct:nr + npar + nct + len(adds)])
        out_refs = refs[nr + npar + nct + len(adds):]
        rv = [r[...] for r in row_refs]
        pv = [p[...].astype(F32) for p in par_refs]

        def g(*args):
            full = list(rv)
            for k, i in enumerate(didx):
                full[i] = args[k]
            return f(*full, *args[len(didx):])

        outs, pull = jax.vjp(g, *[rv[i] for i in didx], *pv)
        grads = pull(tuple(c[...].astype(o.dtype) for c, o in zip(ct_refs, outs, strict=True)))
        for k in range(len(didx)):
            val = grads[k].astype(F32)
            if add[k] is not None:
                val = val + add_refs.pop(0)[...].astype(F32)
            out_refs[k][...] = val.astype(out_refs[k].dtype)

        @pl.when(pl.program_id(0) == 0)
        def _():
            for o_ref in out_refs[len(didx):]:
                o_ref[...] = jnp.zeros_like(o_ref)

        for o_ref, gp in zip(out_refs[len(didx):], grads[len(didx):], strict=True):
            o_ref[...] += gp.astype(F32)

    row_spec = lambda a: pl.BlockSpec((tile, a.shape[1]), lambda i: (i, 0))
    res = pl.pallas_call(
        body, name=name, grid=(t // tile,),
        in_specs=[row_spec(r) for r in rows] + [_full_spec(p) for p in params] + [row_spec(c) for c in cts]
        + [row_spec(a) for a in adds],
        out_specs=[row_spec(rows[i]) for i in didx] + [_full_spec(p) for p in params],
        out_shape=[jax.ShapeDtypeStruct(rows[i].shape, d) for i, d in zip(didx, grad_dtypes, strict=True)]
        + [jax.ShapeDtypeStruct(p.shape, F32) for p in params],
        compiler_params=_cparams(("arbitrary",)),
    )(*rows, *params, *cts, *adds)
    return tuple(res)


def _sum2(a, b, out_dtype, name):
    shape = a.shape
    views = [x.reshape(-1, shape[-1]) for x in (a, b)]
    r, c = views[0].shape
    tile = _tile(r, max(16, (1 << 18) // c), 16)

    def body(a_ref, b_ref, o_ref):
        o_ref[...] = (a_ref[...].astype(F32) + b_ref[...].astype(F32)).astype(o_ref.dtype)

    spec = pl.BlockSpec((tile, c), lambda i: (i, 0))
    return pl.pallas_call(
        body, name=name, grid=(r // tile,), in_specs=[spec] * 2, out_specs=spec,
        out_shape=jax.ShapeDtypeStruct((r, c), out_dtype), compiler_params=_cparams(("parallel",)),
    )(*views).reshape(shape)


def _sum_slots(x, out_dtype, name, first=None):
    n, r, c = x.shape
    tile = _tile(r, max(16, (1 << 18) // c), 16)

    def body(*refs):
        acc = refs[0][...].astype(F32)
        for ref in refs[1:-1]:
            acc = acc + ref[...].astype(F32)
        refs[-1][...] = acc.astype(refs[-1].dtype)

    spec = pl.BlockSpec((tile, c), lambda i: (i, 0))
    return pl.pallas_call(
        body, name=name, grid=(r // tile,),
        in_specs=([spec] if first is not None else [])
        + [pl.BlockSpec((None, tile, c), lambda i, _s=s_: (_s, i, 0)) for s_ in range(n)],
        out_specs=spec, out_shape=jax.ShapeDtypeStruct((r, c), out_dtype), compiler_params=_cparams(("parallel",)),
    )(*(([first] if first is not None else []) + [x] * n))


def _adamw(w, g, m, v, name):
    r, c = w.shape
    tile = _tile(r, max(8, (1 << 18) // c), 8)

    def body(w_ref, g_ref, m_ref, v_ref, d_out, m_out, v_out):
        g = g_ref[...]
        mn = ADAM_B1 * m_ref[...] + (1.0 - ADAM_B1) * g
        vn = ADAM_B2 * v_ref[...] + (1.0 - ADAM_B2) * (g * g)
        m_hat = mn / (1.0 - ADAM_B1 ** ADAM_STEP)
        v_hat = vn / (1.0 - ADAM_B2 ** ADAM_STEP)
        d_out[...] = -ADAM_LR * (m_hat / (jnp.sqrt(v_hat) + ADAM_EPS) + ADAM_WD * w_ref[...])
        m_out[...] = mn
        v_out[...] = vn

    spec = pl.BlockSpec((tile, c), lambda i: (i, 0))
    return pl.pallas_call(
        body, name=name, grid=(r // tile,), in_specs=[spec] * 4, out_specs=[spec] * 3,
        out_shape=[jax.ShapeDtypeStruct((r, c), F32)] * 3, compiler_params=_cparams(("parallel",)),
    )(w, g, m, v)


def _place():
    return lax.axis_index("x"), lax.axis_index("y"), lax.axis_index("c")


def _flip(v, bit):
    return 1 - v if bit else v


_ANY = pl.BlockSpec(memory_space=pl.ANY)


def _quarter(ref, j, shape, kind):
    if kind == "row":
        return ref.at[pl.ds(j * shape[0], shape[0])]
    if kind == "col":
        return ref.at[:, pl.ds(j * shape[1], shape[1])]
    return ref.at[j]


def _whole_shape(shape, kind):
    if kind == "row":
        return (N_CHIPS * shape[0],) + tuple(shape[1:])
    if kind == "col":
        return (shape[0], N_CHIPS * shape[1]) + tuple(shape[2:])
    return (N_CHIPS,) + tuple(shape)


def _dma_sems(*counts):
    return [pltpu.SemaphoreType.DMA((n,)) for n in counts]


class _Job:
    def __init__(self, ins, outs, sems, phases, at):
        self.ins, self.outs, self.sems, self.phases, self.at = list(ins), list(outs), list(sems), list(phases), list(at)

    def specs(self):
        return [_ANY] * len(self.ins), [_ANY] * len(self.outs), _dma_sems(*self.sems)

    def run_at(self, step, n_steps, in_refs, out_refs, sem_refs):
        for phase, frac in zip(self.phases, self.at, strict=True):
            @pl.when(step == int(frac * (n_steps - 1)))
            def _():
                phase(in_refs, out_refs, sem_refs)

    def join(self, other):
        ni, no, ns = len(self.ins), len(self.outs), len(self.sems)
        assert self.at == other.at

        def both(mine, theirs):
            def phase(ins, outs, sems):
                mine(ins[:ni], outs[:no], sems[:ns])
                theirs(ins[ni:], outs[no:], sems[ns:])
            return phase

        return _Job(self.ins + other.ins, self.outs + other.outs, self.sems + other.sems,
                    [both(a, b) for a, b in zip(self.phases, other.phases, strict=True)], self.at)


def _run_job(job, name):
    def body(*refs):
        ni, no = len(job.ins), len(job.outs)
        for phase in job.phases:
            phase(refs[:ni], refs[ni:ni + no], refs[ni + no:])

    in_specs, out_specs, sems = job.specs()
    return pl.pallas_call(body, name=name, in_specs=in_specs, out_specs=out_specs, out_shape=job.outs,
                          scratch_shapes=sems)(*job.ins)


def _remote(src, dst, sems, i, j, peer):
    return pltpu.make_async_remote_copy(src_ref=src, dst_ref=dst, send_sem=sems[i].at[j], recv_sem=sems[i + 1].at[j],
                                        device_id=peer, device_id_type=MESH_ID)


def _gather_job(shards, kinds, layer):
    n = len(shards)
    shapes = [s.shape[1:] for s in shards]
    other = 1 - layer

    def window(outs, t, j):
        return _quarter(outs[t], j, shapes[t], kinds[t])

    def start(ins, outs, sems):
        mx, my, mc = _place()
        me = 2 * mx + my

        @pl.when(mc == layer)
        def _():
            for t in range(n):
                for k in range(1, N_CHIPS):
                    _remote(ins[t].at[layer], window(outs, t, me), sems, 0, 3 * t + k - 1,
                            (_flip(mx, k >> 1), _flip(my, k & 1), layer)).start()

        @pl.when(mc == other)
        def _():
            for t in range(n):
                _remote(ins[t].at[layer], window(outs, t, me), sems, 2, t, (mx, my, layer)).start()

    def relay(ins, outs, sems):
        mx, my, mc = _place()
        me = 2 * mx + my

        @pl.when(mc == layer)
        def _():
            for t in range(n):
                _remote(ins[t].at[layer], window(outs, t, me), sems, 2, t, (mx, my, other)).wait_recv()
                for k in range(1, N_CHIPS):
                    px, py = _flip(mx, k >> 1), _flip(my, k & 1)
                    cp = _remote(ins[t].at[layer], window(outs, t, 2 * px + py), sems, 0, 3 * t + k - 1, (px, py, layer))
                    cp.wait_recv()
                    cp.wait_send()
                _remote(outs[t], outs[t], sems, 4, t, (mx, my, other)).start()

        @pl.when(mc == other)
        def _():
            for t in range(n):
                _remote(ins[t].at[layer], window(outs, t, me), sems, 2, t, (mx, my, layer)).wait_send()

    def finish(ins, outs, sems):
        mx, my, mc = _place()

        @pl.when(mc == layer)
        def _():
            for t in range(n):
                _remote(outs[t], outs[t], sems, 4, t, (mx, my, other)).wait_send()

        @pl.when(mc == other)
        def _():
            for t in range(n):
                _remote(outs[t], outs[t], sems, 4, t, (mx, my, layer)).wait_recv()

    outs = [jax.ShapeDtypeStruct(_whole_shape(sh, kd), s.dtype) for s, sh, kd in zip(shards, shapes, kinds)]
    return _Job(shards, outs, [3 * n, 3 * n, n, n, n, n], [start, relay, finish], [0.0, 0.75, 1.0])


def _scatter_job(gs, shapes, kinds, layer):
    n = len(gs)

    def copies(ins, outs, sems):
        mx, my, _ = _place()
        return [_remote(_quarter(ins[t], 2 * _flip(mx, k >> 1) + _flip(my, k & 1), shapes[t], kinds[t]), outs[t].at[k - 1],
                        sems, 0, 3 * t + k - 1, (_flip(mx, k >> 1), _flip(my, k & 1), layer))
                for t in range(n) for k in range(1, N_CHIPS)]

    def start(ins, outs, sems):
        @pl.when(lax.axis_index("c") == layer)
        def _():
            for cp in copies(ins, outs, sems):
                cp.start()

    def finish(ins, outs, sems):
        @pl.when(lax.axis_index("c") == layer)
        def _():
            for cp in copies(ins, outs, sems):
                cp.wait_recv()
                cp.wait_send()

    outs = [jax.ShapeDtypeStruct((N_CHIPS - 1,) + tuple(sh), g.dtype) for g, sh in zip(gs, shapes)]
    return _Job(gs, outs, [3 * n, 3 * n], [start, finish], [0.0, 1.0])


def _core_to_core(parts, sender, name):
    n = len(parts)

    def move(ins, outs, sems):
        mx, my, mc = _place()

        @pl.when(mc == sender)
        def _():
            copies = [_remote(ins[t], outs[t], sems, 0, t, (mx, my, 1 - sender)) for t in range(n)]
            for cp in copies:
                cp.start()
            for cp in copies:
                cp.wait_send()

        @pl.when(mc != sender)
        def _():
            for t in range(n):
                _remote(ins[t], outs[t], sems, 0, t, (mx, my, sender)).wait_recv()

    return _run_job(_Job(parts, [jax.ShapeDtypeStruct(p.shape, p.dtype) for p in parts], [n, n], [move], [0.0]), name)


def _gather_all(x, name):
    r, w = x.shape

    def body(x_ref, o_ref, send_sems, recv_sems, local_sem):
        mx, my, mc = _place()
        mine = 4 * mx + 2 * my + mc
        local = pltpu.make_async_copy(x_ref, o_ref.at[mine], local_sem)
        local.start()
        copies = []
        for k in range(1, N_DEV):
            peer = (_flip(mx, k >> 2), _flip(my, (k >> 1) & 1), _flip(mc, k & 1))
            copies.append(pltpu.make_async_remote_copy(
                src_ref=x_ref, dst_ref=o_ref.at[mine], send_sem=send_sems.at[k - 1], recv_sem=recv_sems.at[k - 1],
                device_id=peer, device_id_type=MESH_ID))
            copies[-1].start()
        for k in range(1, N_DEV):
            peer = (_flip(mx, k >> 2), _flip(my, (k >> 1) & 1), _flip(mc, k & 1))
            pltpu.make_async_remote_copy(
                src_ref=x_ref, dst_ref=o_ref.at[4 * peer[0] + 2 * peer[1] + peer[2]], send_sem=send_sems.at[k - 1],
                recv_sem=recv_sems.at[k - 1], device_id=peer, device_id_type=MESH_ID).wait_recv()
        for cp in copies:
            cp.wait_send()
        local.wait()

    return pl.pallas_call(
        body, name=name, in_specs=[_ANY], out_specs=_ANY, out_shape=jax.ShapeDtypeStruct((N_DEV, r, w), x.dtype),
        scratch_shapes=[pltpu.SemaphoreType.DMA((N_DEV - 1,)), pltpu.SemaphoreType.DMA((N_DEV - 1,)),
                        pltpu.SemaphoreType.DMA],
    )(x)


def _pack(parts, dtype, row_unit):
    flat = jnp.concatenate([p.astype(dtype).reshape(-1) for p in parts])
    unit = row_unit * LANES
    pad = (-flat.shape[0]) % unit
    if pad:
        flat = jnp.concatenate([flat, jnp.zeros((pad,), dtype)])
    return flat.reshape(-1, LANES)


def _unpack(packed, shapes):
    flat = packed.reshape(-1)
    out, off = [], 0
    for s in shapes:
        n = math.prod(s)
        out.append(flat[off:off + n].reshape(s))
        off += n
    return out


def _f_rms(x, gain):
    return (_rms(x, gain),)


def _f_xattn(q, kv):
    d = q.shape[1]
    hd = d // XA_HEADS
    outs = []
    for h in range(XA_HEADS):
        qh, kh, vh = q[:, h * hd:(h + 1) * hd], kv[:, h * hd:(h + 1) * hd], kv[:, d + h * hd:d + (h + 1) * hd]
        s = _dlo(qh, kh, NT) * (hd ** -0.5)
        s = s - jnp.max(s, axis=-1, keepdims=True)
        p = jnp.exp(s)
        p = p / jnp.sum(p, axis=-1, keepdims=True)
        outs.append(_dlo(p, vh, NN))
    return (jnp.concatenate(outs, axis=1),)


def _f_gmlp(uv, gain, bias, w_sp, b_sp):
    r = uv.shape[0]
    act = jax.nn.gelu(uv.astype(F32))
    u, v = act[:, :SG_WIDTH], act[:, SG_WIDTH:]
    mu = jnp.mean(v, axis=-1, keepdims=True)
    var = jnp.mean(jnp.square(v - mu), axis=-1, keepdims=True)
    v = (v - mu) * lax.rsqrt(var + NORM_EPS) * gain + bias
    row = lax.broadcasted_iota(jnp.int32, (SG_CHUNK, SG_CHUNK), 0)
    col = lax.broadcasted_iota(jnp.int32, (SG_CHUNK, SG_CHUNK), 1)
    lane_grp = lax.broadcasted_iota(jnp.int32, (b_sp.shape[0], SG_WIDTH), 1) // SG_DIM
    grp_row = lax.broadcasted_iota(jnp.int32, (b_sp.shape[0], SG_WIDTH), 0)
    spread = jnp.where(lane_grp == grp_row, 1.0, 0.0).astype(F32)
    bias_t = _dhi(b_sp, spread, TN)
    chunks = []
    for c in range(r // SG_CHUNK):
        vc = v[c * SG_CHUNK:(c + 1) * SG_CHUNK]
        parts = []
        for g in range(SG_GROUPS):
            wg = jnp.where(row >= col, w_sp[g], 0.0)
            parts.append(_dlo(wg, vc[:, g * SG_DIM:(g + 1) * SG_DIM], NN))
        chunks.append(jnp.concatenate(parts, axis=1) + bias_t)
    mixed = jnp.concatenate(chunks, axis=0) if len(chunks) > 1 else chunks[0]
    return (u * mixed,)


def _f_swa_mix(o1, o2, o3, l1, l2, l3):
    outs = []
    for h in range(SWA_HEADS):
        ls = [l[:, h:h + 1] for l in (l1, l2, l3)]
        mx = jnp.maximum(jnp.maximum(ls[0], ls[1]), ls[2])
        es = [jnp.exp(l - mx) for l in ls]
        den = es[0] + es[1] + es[2]
        sl = slice(h * SWA_DIM, (h + 1) * SWA_DIM)
        outs.append((es[0] * o1[:, sl] + es[1] * o2[:, sl] + es[2] * o3[:, sl]) / den)
    return (jnp.concatenate(outs, axis=1),)


def _swa_block(q, kp, kc, vp, vc, first, window, dilation):
    span = window // dilation
    rows = SWA_HEADS * SWA_BLOCK
    ri = lax.broadcasted_iota(jnp.int32, (rows, 2 * SWA_BLOCK), 0)
    kj = lax.broadcasted_iota(jnp.int32, (rows, 2 * SWA_BLOCK), 1)
    head = ri // SWA_BLOCK
    rel = SWA_BLOCK + ri % SWA_BLOCK - kj
    valid = (rel >= 0) & (rel <= span) & jnp.logical_not(jnp.logical_and(first, kj < SWA_BLOCK))
    slope = jnp.exp((head + 1).astype(F32) * (-8.0 / SWA_HEADS * math.log(2.0)))
    bias = slope * (rel * dilation).astype(F32)
    kw = jnp.concatenate([kp, kc], axis=0)
    vw = jnp.concatenate([vp, vc], axis=0)
    lane_head = lax.broadcasted_iota(jnp.int32, (rows, SWA_WIDTH), 1) // SWA_DIM
    own_head = lane_head == lax.broadcasted_iota(jnp.int32, (rows, SWA_WIDTH), 0) // SWA_BLOCK
    q_rows = jnp.where(own_head, jnp.concatenate([q] * SWA_HEADS, axis=0), 0.0)
    s = _dlo(q_rows, kw, NT) * (SWA_DIM ** -0.5) - bias
    s = jnp.where(valid, s, -1e30)
    m = jnp.max(s, axis=-1, keepdims=True)
    p = jnp.exp(s - m)
    den = jnp.sum(p, axis=-1, keepdims=True)
    wide = _dlo(p, vw, NN) / den
    lse_rows = m + jnp.log(den)
    lane = lax.broadcasted_iota(jnp.int32, (SWA_BLOCK, LANES), 1)
    outs, lse = [], jnp.zeros((SWA_BLOCK, LANES), F32)
    for h in range(SWA_HEADS):
        outs.append(wide[h * SWA_BLOCK:(h + 1) * SWA_BLOCK, h * SWA_DIM:(h + 1) * SWA_DIM])
        lse = lse + jnp.where(lane == h, lse_rows[h * SWA_BLOCK:(h + 1) * SWA_BLOCK], 0.0)
    return jnp.concatenate(outs, axis=1), lse


@jax.custom_vjp
def _unit_lower_inv(lower):
    c = lower.shape[0]
    assert DN_CHUNK == 4 * INV_BLOCK and c % DN_CHUNK == 0
    row = lax.broadcasted_iota(jnp.int32, (c, c), 0)
    col = lax.broadcasted_iota(jnp.int32, (c, c), 1)
    eye = jnp.where(row == col, 1.0, 0.0).astype(F32)
    same = (row // INV_BLOCK) == (col // INV_BLOCK)
    pw = -jnp.where(same, lower, 0.0)
    d_inv = eye + pw
    for _ in range(int(math.log2(INV_BLOCK)) - 1):
        pw = _dlo(pw, pw, NN)
        d_inv = d_inv + _dlo(d_inv, pw, NN)
    n1 = _dlo(d_inv, jnp.where(same, 0.0, lower), NN)
    n2 = _dlo(n1, n1, NN)
    rough = _dlo(eye - n1 + n2 - _dlo(n1, n2, NN), d_inv, NN)
    residual = eye - _dhi(eye + lower, rough, NN)
    return rough + _dlo(rough, residual, NN)


def _unit_lower_inv_fwd(lower):
    t_inv = _unit_lower_inv(lower)
    return t_inv, t_inv


def _unit_lower_inv_bwd(t_inv, g):
    return (-_dlo(_dlo(t_inv, g, TN), t_inv, NT),)


_unit_lower_inv.defvjp(_unit_lower_inv_fwd, _unit_lower_inv_bwd)


@jax.custom_vjp
def _known_lower_inv(lower, t_inv):
    return t_inv


def _known_lower_inv_fwd(lower, t_inv):
    return t_inv, t_inv


def _known_lower_inv_bwd(t_inv, g):
    return _unit_lower_inv_bwd(t_inv, g) + (jnp.zeros_like(t_inv),)


_known_lower_inv.defvjp(_known_lower_inv_fwd, _known_lower_inv_bwd)


def _dn_group(xx, z, ba, state, conv_w, a_log, dt_bias, gain, known_inv=None):
    rows, c = z.shape[0], DN_CHUNK
    hc = DN_HEADS * c
    acc = conv_w[0:1] * xx[HALO - 3:HALO - 3 + rows]
    for j in range(1, DN_CONV):
        acc = acc + conv_w[j:j + 1] * xx[HALO - 3 + j:HALO - 3 + j + rows]
    qkv = _silu(acc)
    beta_all = _sigmoid(ba)
    g_all = -jnp.exp(a_log) * _softplus(ba + dt_bias)
    row = lax.broadcasted_iota(jnp.int32, (hc, hc), 0)
    col = lax.broadcasted_iota(jnp.int32, (hc, hc), 1)
    same_head = (row // c) == (col // c)
    incl, strict = same_head & (row >= col), same_head & (row > col)
    tri = jnp.where(incl[:c, :c], 1.0, 0.0).astype(F32)

    def stack(piece):
        return jnp.concatenate([piece(h) for h in range(DN_HEADS)], axis=0)

    local, inverses = [], []
    for ci in range(rows // c):
        r0 = ci * c
        gc_all = _dhi(tri, g_all[r0:r0 + c], NN)
        gc_t = gc_all.T
        q = stack(lambda h: qkv[r0:r0 + c, h * DN_DIM:(h + 1) * DN_DIM])
        k = stack(lambda h: qkv[r0:r0 + c, DN_WIDTH + h * DN_DIM:DN_WIDTH + (h + 1) * DN_DIM])
        v = stack(lambda h: qkv[r0:r0 + c, 2 * DN_WIDTH + h * DN_DIM:2 * DN_WIDTH + (h + 1) * DN_DIM])
        q = q * lax.rsqrt(jnp.sum(q * q, axis=-1, keepdims=True) + NORM_EPS) * (DN_DIM ** -0.5)
        k = k * lax.rsqrt(jnp.sum(k * k, axis=-1, keepdims=True) + NORM_EPS)
        beta = stack(lambda h: beta_all[r0:r0 + c, h:h + 1])
        gc = stack(lambda h: gc_all[:, DN_HEADS + h:DN_HEADS + h + 1])
        g_last = stack(lambda h: jnp.broadcast_to(gc_all[c - 1:c, DN_HEADS + h:DN_HEADS + h + 1], (c, 1)))
        gc_row = jnp.concatenate([gc_t[DN_HEADS + h:DN_HEADS + h + 1, :] for h in range(DN_HEADS)], axis=1)
        decay = jnp.where(incl, jnp.exp(jnp.where(incl, gc - gc_row, 0.0)), 0.0)
        kb = k * beta
        lower = jnp.where(strict, _dlo(kb, k, NT) * decay, 0.0)
        t_inv = _unit_lower_inv(lower) if known_inv is None else _known_lower_inv(lower, known_inv[ci].astype(F32))
        inverses.append(t_inv)
        e_gc = jnp.exp(gc)
        u = _dlo(t_inv, v * beta, NN)
        w = _dlo(t_inv, kb * e_gc, NN)
        a_qk = jnp.where(incl, _dlo(q, k, NT) * decay, 0.0)
        e_last = jnp.concatenate([jnp.broadcast_to(jnp.exp(gc_all[c - 1:c, DN_HEADS + h:DN_HEADS + h + 1]), (1, DN_DIM))
                                  for h in range(DN_HEADS)], axis=1)
        local.append((jnp.concatenate([w, q * e_gc], axis=0), k * jnp.exp(g_last - gc), u, a_qk, e_last))
    own = (lax.broadcasted_iota(jnp.int32, (hc, DN_WIDTH), 0) // c) == (lax.broadcasted_iota(jnp.int32, (hc, DN_WIDTH), 1) // DN_DIM)

    def own_blocks(m):
        return stack(lambda h: m[h * c:(h + 1) * c, h * DN_DIM:(h + 1) * DN_DIM])

    s = state
    out_rows = []
    for ci in range(rows // c):
        r0 = ci * c
        wq, k_tail, u, a_qk, e_last = local[ci]
        through = _dlo(wq, s, NN)
        v_new = u - own_blocks(through[:hc])
        o = own_blocks(through[hc:]) + _dlo(a_qk, v_new, NN)
        v_wide = jnp.where(own, jnp.concatenate([v_new] * DN_HEADS, axis=1), 0.0)
        s = s * e_last + _dlo(k_tail, v_wide, TN)
        o = o * lax.rsqrt(jnp.mean(o * o, axis=-1, keepdims=True) + NORM_EPS) * gain
        o = jnp.concatenate([o[h * c:(h + 1) * c] for h in range(DN_HEADS)], axis=1)
        out_rows.append(o * _silu(z[r0:r0 + c]))
    out = jnp.concatenate(out_rows, axis=0) if len(out_rows) > 1 else out_rows[0]
    return out, s, inverses


def _dn_specs(n_of, rows):
    return [pl.BlockSpec((rows, 3 * DN_WIDTH), lambda i: (n_of(i), 0)),
            pl.BlockSpec((HALO, 3 * DN_WIDTH), lambda i: (jnp.maximum(n_of(i) * (rows // HALO) - 1, 0), 0)),
            pl.BlockSpec((rows, DN_WIDTH), lambda i: (n_of(i), 0)),
            pl.BlockSpec((rows, LANES), lambda i: (n_of(i), 0))]


def _no_job():
    return _Job([], [], [], [], [])


def _dn_forward(xq, xz, xba, params, name, job=None):
    t = xq.shape[0]
    rows = min(DN_GROUP * DN_CHUNK, t)
    n_groups = t // rows
    job = job or _no_job()
    job_in, job_out, job_sems = job.specs()

    def body(*refs):
        x_ref, halo_ref, z_ref, ba_ref, cw_ref, al_ref, dt_ref, gn_ref = refs[:8]
        ji = 8 + len(job_in)
        jo = ji + 3 + len(job_out)
        o_ref, s_all_ref, inv_ref, s_ref = refs[ji], refs[ji + 1], refs[ji + 2], refs[jo]
        n = pl.program_id(0)
        job.run_at(n, n_groups, refs[8:ji], refs[ji + 3:jo], refs[jo + 1:])

        @pl.when(n == 0)
        def _():
            s_ref[...] = jnp.zeros_like(s_ref)

        halo = jnp.where(n > 0, halo_ref[...], 0.0)
        xx = jnp.concatenate([halo, x_ref[...]], axis=0)
        s_all_ref[0] = s_ref[...]
        o, s_new, inverses = _dn_group(xx, z_ref[...], ba_ref[...], s_ref[...], cw_ref[...], al_ref[...], dt_ref[...],
                                       gn_ref[...])
        o_ref[...] = o.astype(o_ref.dtype)
        s_ref[...] = s_new
        for ci, t_inv in enumerate(inverses):
            inv_ref[ci] = t_inv.astype(inv_ref.dtype)

    assert len(params) == 4
    per_group, wide = rows // DN_CHUNK, DN_HEADS * DN_CHUNK
    res = pl.pallas_call(
        body, name=name, grid=(n_groups,),
        in_specs=_dn_specs(lambda i: i, rows) + [_full_spec(p) for p in params] + job_in,
        out_specs=[pl.BlockSpec((rows, DN_WIDTH), lambda i: (i, 0)),
                   pl.BlockSpec((1, DN_DIM, DN_WIDTH), lambda i: (i, 0, 0)),
                   pl.BlockSpec((per_group, wide, wide), lambda i: (i, 0, 0))] + job_out,
        out_shape=[jax.ShapeDtypeStruct((t, DN_WIDTH), BF16),
                   jax.ShapeDtypeStruct((n_groups, DN_DIM, DN_WIDTH), F32),
                   jax.ShapeDtypeStruct((t // DN_CHUNK, wide, wide), BF16)] + job.outs,
        scratch_shapes=[pltpu.VMEM((DN_DIM, DN_WIDTH), F32)] + job_sems,
        compiler_params=_cparams(("arbitrary",)),
    )(xq, xq, xz, xba, *params, *job.ins)
    return res[0], (res[1], res[2]), list(res[3:])


def _dn_backward(xq, xz, xba, params, kept, d_out, name, job=None):
    t = xq.shape[0]
    rows = min(DN_GROUP * DN_CHUNK, t)
    n_groups = t // rows
    rev = lambda i: n_groups - 1 - i
    job = job or _no_job()
    job_in, job_out, job_sems = job.specs()

    def body(*refs):
        x_ref, halo_ref, z_ref, ba_ref, cw_ref, al_ref, dt_ref, gn_ref, s_ref, do_ref, inv_ref = refs[:11]
        ji = 11 + len(job_in)
        dx_ref, dz_ref, dba_ref, dcw_ref, dal_ref, ddt_ref, dgn_ref = refs[ji:ji + 7]
        jo = ji + 7 + len(job_out)
        ds_ref, dhalo_ref = refs[jo], refs[jo + 1]
        i = pl.program_id(0)
        n = n_groups - 1 - i
        job.run_at(i, n_groups, refs[11:ji], refs[ji + 7:jo], refs[jo + 2:])

        @pl.when(i == 0)
        def _():
            ds_ref[...] = jnp.zeros_like(ds_ref)
            dhalo_ref[...] = jnp.zeros_like(dhalo_ref)
            for r in (dcw_ref, dal_ref, ddt_ref, dgn_ref):
                r[...] = jnp.zeros_like(r)

        halo = jnp.where(n > 0, halo_ref[...], 0.0)
        xx = jnp.concatenate([halo, x_ref[...]], axis=0)
        known = inv_ref[...]
        _, pull = jax.vjp(lambda *a: _dn_group(*a, known_inv=known)[:2], xx, z_ref[...], ba_ref[...], s_ref[0],
                          cw_ref[...], al_ref[...], dt_ref[...], gn_ref[...])
        dxx, dz, dba, ds, dcw, dal, ddt, dgn = pull((do_ref[...].astype(F32), ds_ref[...]))
        dx_ref[...] = jnp.concatenate([dxx[HALO:rows], dxx[rows:] + dhalo_ref[...]], axis=0).astype(dx_ref.dtype)
        dhalo_ref[...] = dxx[:HALO]
        dz_ref[...] = dz.astype(dz_ref.dtype)
        dba_ref[...] = dba.astype(dba_ref.dtype)
        ds_ref[...] = ds
        dcw_ref[...] += dcw
        dal_ref[...] += dal
        ddt_ref[...] += ddt
        dgn_ref[...] += dgn

    assert len(params) == 4
    s_all, inv_all = kept
    per_group, wide = rows // DN_CHUNK, DN_HEADS * DN_CHUNK
    res = pl.pallas_call(
        body, name=name, grid=(n_groups,),
        in_specs=_dn_specs(rev, rows) + [_full_spec(p) for p in params]
        + [pl.BlockSpec((1, DN_DIM, DN_WIDTH), lambda i: (rev(i), 0, 0)),
           pl.BlockSpec((rows, DN_WIDTH), lambda i: (rev(i), 0)),
           pl.BlockSpec((per_group, wide, wide), lambda i: (rev(i), 0, 0))] + job_in,
        out_specs=[pl.BlockSpec((rows, 3 * DN_WIDTH), lambda i: (rev(i), 0)),
                   pl.BlockSpec((rows, DN_WIDTH), lambda i: (rev(i), 0)),
                   pl.BlockSpec((rows, LANES), lambda i: (rev(i), 0))] + [_full_spec(p) for p in params] + job_out,
        out_shape=[jax.ShapeDtypeStruct(xq.shape, BF16), jax.ShapeDtypeStruct(xz.shape, BF16),
                   jax.ShapeDtypeStruct(xba.shape, BF16)] + [jax.ShapeDtypeStruct(p.shape, F32) for p in params] + job.outs,
        scratch_shapes=[pltpu.VMEM((DN_DIM, DN_WIDTH), F32), pltpu.VMEM((HALO, 3 * DN_WIDTH), F32)] + job_sems,
        compiler_params=_cparams(("arbitrary",)),
    )(xq, xq, xz, xba, *params, s_all, d_out, inv_all, *job.ins)
    return tuple(res[:7]), list(res[7:])


def _swa_forward(xs, window, dilation, name):
    t = xs.shape[0]
    d, l = dilation, t // dilation
    nb = l // SWA_BLOCK
    view = xs.reshape(l, d * 3 * SWA_WIDTH)
    blk = (SWA_BLOCK, SWA_WIDTH)

    def body(q_ref, kp_ref, kc_ref, vp_ref, vc_ref, o_ref, l_ref):
        blocks = [r[...].astype(F32) for r in (q_ref, kp_ref, kc_ref, vp_ref, vc_ref)]
        o, lse = _swa_block(*blocks, pl.program_id(1) == 0, window, dilation)
        o_ref[...] = o
        l_ref[...] = lse

    prev = lambda n: jnp.maximum(n - 1, 0)
    o, lse = pl.pallas_call(
        body, name=name, grid=(d, nb),
        in_specs=[pl.BlockSpec(blk, lambda r, n: (n, 3 * r)), pl.BlockSpec(blk, lambda r, n: (prev(n), 3 * r + 1)),
                  pl.BlockSpec(blk, lambda r, n: (n, 3 * r + 1)), pl.BlockSpec(blk, lambda r, n: (prev(n), 3 * r + 2)),
                  pl.BlockSpec(blk, lambda r, n: (n, 3 * r + 2))],
        out_specs=[pl.BlockSpec(blk, lambda r, n: (n, r)), pl.BlockSpec((SWA_BLOCK, LANES), lambda r, n: (n, r))],
        out_shape=[jax.ShapeDtypeStruct((l, d * SWA_WIDTH), F32), jax.ShapeDtypeStruct((l, d * LANES), F32)],
        compiler_params=_cparams(("parallel", "parallel")),
    )(view, view, view, view, view)
    return o.reshape(t, SWA_WIDTH), lse.reshape(t, LANES)


def _swa_backward(xs, d_o, d_lse, acc, window, dilation, name):
    t = xs.shape[0]
    d, l = dilation, t // dilation
    nb = l // SWA_BLOCK
    view = xs.reshape(l, d * 3 * SWA_WIDTH)
    blk = (SWA_BLOCK, SWA_WIDTH)
    has_acc = acc is not None

    def body(*refs):
        q_ref, kp_ref, kc_ref, vp_ref, vc_ref, do_ref, dl_ref = refs[:7]
        acc_refs = refs[7:10] if has_acc else None
        dq_ref, dk_ref, dv_ref, ck_ref, cv_ref = refs[-5:]
        i = pl.program_id(1)
        n = nb - 1 - i

        @pl.when(i == 0)
        def _():
            ck_ref[...] = jnp.zeros_like(ck_ref)
            cv_ref[...] = jnp.zeros_like(cv_ref)

        f = functools.partial(_swa_block, first=n == 0, window=window, dilation=dilation)
        _, pull = jax.vjp(f, *[r[...].astype(F32) for r in (q_ref, kp_ref, kc_ref, vp_ref, vc_ref)])
        dq, dkp, dkc, dvp, dvc = pull((do_ref[...], dl_ref[...]))
        dk = dkc + ck_ref[...]
        dv = dvc + cv_ref[...]
        if has_acc:
            dq, dk, dv = dq + acc_refs[0][...], dk + acc_refs[1][...], dv + acc_refs[2][...]
        dq_ref[...] = dq
        dk_ref[...] = dk
        dv_ref[...] = dv
        ck_ref[...] = dkp
        cv_ref[...] = dvp

    cur = lambda i: nb - 1 - i
    prev = lambda i: jnp.maximum(nb - 2 - i, 0)
    own = pl.BlockSpec(blk, lambda r, i: (cur(i), r))
    accs = [a.reshape(l, d * SWA_WIDTH) for a in acc] if has_acc else []
    outs = pl.pallas_call(
        body, name=name, grid=(d, nb),
        in_specs=[pl.BlockSpec(blk, lambda r, i: (cur(i), 3 * r)), pl.BlockSpec(blk, lambda r, i: (prev(i), 3 * r + 1)),
                  pl.BlockSpec(blk, lambda r, i: (cur(i), 3 * r + 1)), pl.BlockSpec(blk, lambda r, i: (prev(i), 3 * r + 2)),
                  pl.BlockSpec(blk, lambda r, i: (cur(i), 3 * r + 2)), own,
                  pl.BlockSpec((SWA_BLOCK, LANES), lambda r, i: (cur(i), r))] + [own] * len(accs),
        out_specs=[own] * 3, out_shape=[jax.ShapeDtypeStruct((l, d * SWA_WIDTH), F32)] * 3,
        scratch_shapes=[pltpu.VMEM(blk, F32), pltpu.VMEM(blk, F32)],
        compiler_params=_cparams(("parallel", "arbitrary")),
    )(view, view, view, view, view, d_o.reshape(l, d * SWA_WIDTH), d_lse.reshape(l, d * LANES), *accs)
    return tuple(o.reshape(t, SWA_WIDTH) for o in outs)


def _loss_head(h, target, gain, name, tile=256):
    t, d = h.shape
    tile = min(tile, t)

    def body(h_ref, t_ref, g_ref, loss_ref, dh_ref, dg_ref):
        def f(hv, gv):
            err = _rms(hv, gv) - t_ref[...]
            return 0.5 * jnp.sum(jnp.mean(err * err, axis=-1, keepdims=True), axis=0, keepdims=True)

        val, pull = jax.vjp(f, h_ref[...], g_ref[...])
        dh, dg = pull(jnp.ones((1, 1), F32))
        dh_ref[...] = dh

        @pl.when(pl.program_id(0) == 0)
        def _():
            loss_ref[...] = jnp.zeros_like(loss_ref)
            dg_ref[...] = jnp.zeros_like(dg_ref)

        loss_ref[...] += jnp.broadcast_to(val, loss_ref.shape)
        dg_ref[...] += dg

    return pl.pallas_call(
        body, name=name, grid=(t // tile,),
        in_specs=[pl.BlockSpec((tile, d), lambda i: (i, 0)), pl.BlockSpec((tile, d), lambda i: (i, 0)), _full_spec(gain)],
        out_specs=[pl.BlockSpec((1, LANES), lambda i: (0, 0)), pl.BlockSpec((tile, d), lambda i: (i, 0)), _full_spec(gain)],
        out_shape=[jax.ShapeDtypeStruct((1, LANES), F32), jax.ShapeDtypeStruct((t, d), F32),
                   jax.ShapeDtypeStruct(gain.shape, F32)],
        compiler_params=_cparams(("arbitrary",)),
    )(h, target, gain)


def _split_w_in(w_in):
    cuts = [0]
    for s in IN_SIZES:
        cuts.append(cuts[-1] + s)
    qkv, z = w_in[:, cuts[0]:cuts[1]], w_in[:, cuts[1]:cuts[2]]
    ba = jnp.pad(w_in[:, cuts[2]:cuts[4]], ((0, 0), (0, LANES - 2 * DN_HEADS)))
    return qkv, z, ba, w_in[:, cuts[4]:cuts[5]], w_in[:, cuts[5]:cuts[6]]


def _lane_pad(v, offset):
    return jnp.pad(v.reshape(1, -1), ((0, 0), (offset, LANES - offset - v.shape[0])))


def _layer_params(sm, i, conv_w):
    return dict(
        ffn1_norm=sm["ffn1_norm"][i][None], mix_norm=sm["mix_norm"][i][None], xa_norm=sm["xa_norm"][i][None],
        xa_mem_norm=sm["xa_mem_norm"][i][None], ffn2_norm=sm["ffn2_norm"][i][None],
        dn=(conv_w, _lane_pad(sm["dn_a_log"][i], DN_HEADS), _lane_pad(sm["dn_dt_bias"][i], DN_HEADS),
            sm["dn_out_norm"][i][None]),
        sg=(sm["sg_norm_gain"][i][None], sm["sg_norm_bias"][i][None], sm["sg_w_spatial"][i],
            jnp.pad(sm["sg_b_spatial"][i], ((0, 8 - SG_GROUPS), (0, 0)))),
    )


def _ffn_fwd(h, gain, w_gu, w_d, tag):
    n = _rows(_f_rms, [h], [gain], [(h.shape[1], BF16)], tile=512, name=f"{tag}_norm")[0]
    gu = _mm(n, w_gu, NN, out_dtype=BF16, name=f"{tag}_gate_up")
    out = _mm(gu, w_d, NN, out_dtype=F32, res=h, scale=0.5, swiglu_a=True, name=f"{tag}_down")
    return out, (h, n, gu)


def _ffn_bwd(dh, saved, gain, w_gu, w_d, tag):
    h, n, gu = saved
    f = w_d.shape[0]
    d_gate, d_up = _mm_swiglu_bwd(dh, w_d, gu, 0.5, name=f"{tag}_down_dx")
    dw_d = _mm(gu, dh, TN, out_dtype=BF16, scale=0.5, swiglu_a=True, name=f"{tag}_down_dw")
    dn = _mm(d_gate, w_gu, NT, out_dtype=F32, name=f"{tag}_gate_dx")
    dn = _mm(d_up, w_gu, NT, out_dtype=F32, res=dn, b_k0=f, name=f"{tag}_up_dx")
    dw_gu = _mm(n, d_gate, TN, out_dtype=BF16, beside=2 * f, name=f"{tag}_gate_dw")
    dw_gu = _mm(n, d_up, TN, out_dtype=BF16, into=dw_gu, into_n0=f, name=f"{tag}_up_dw")
    dh_in, dgain = _rows_vjp(_f_rms, [h], [gain], [dn], diff=[True], grad_dtypes=[F32], tile=512, add=[dh],
                             name=f"{tag}_norm_bwd")
    return dh_in, dgain, dw_gu, dw_d


def _mixer_fwd(h, p, w_in, w_out, tag, job=None):
    d = h.shape[1]
    n = _rows(_f_rms, [h], [p["mix_norm"]], [(d, BF16)], tile=512, name=f"{tag}_norm")[0]
    w_parts = _split_w_in(w_in)
    xq, xz, xba, xs, xg = (_mm(n, w, NN, out_dtype=BF16 if j == 3 else F32, name=f"{tag}_in{j}") for j, w in enumerate(w_parts))
    oa, s_all, job_out = _dn_forward(xq, xz, xba, p["dn"], name=f"{tag}_dn", job=job)
    swa = [_swa_forward(xs, wnd, dil, name=f"{tag}_swa{j}") for j, (wnd, dil) in enumerate(SWA_PATTERNS)]
    ob = _rows(_f_swa_mix, [o for o, _ in swa] + [l for _, l in swa], [], [(SWA_WIDTH, BF16)], tile=512,
               name=f"{tag}_swa_mix")[0]
    oc = _rows(_f_gmlp, [xg], list(p["sg"]), [(SG_WIDTH, BF16)], tile=256, name=f"{tag}_gmlp")[0]
    merged = jnp.concatenate([oa, ob, oc], axis=1)
    out = _mm(merged, w_out, NN, out_dtype=F32, res=h, name=f"{tag}_out")
    return out, (h, n, xq, xz, xba, xs, xg, s_all, swa, merged), job_out


def _mixer_bwd(dh, saved, p, w_in, w_out, tag, job=None):
    h, n, xq, xz, xba, xs, xg, s_all, swa, merged = saved
    dw_out = _mm(merged, dh, TN, out_dtype=BF16, name=f"{tag}_out_dw")
    doa = _mm(dh, w_out[:DN_WIDTH], NT, out_dtype=F32, name=f"{tag}_out_dxa")
    dob = _mm(dh, w_out[DN_WIDTH:DN_WIDTH + SWA_WIDTH], NT, out_dtype=F32, name=f"{tag}_out_dxb")
    doc = _mm(dh, w_out[DN_WIDTH + SWA_WIDTH:], NT, out_dtype=F32, name=f"{tag}_out_dxc")
    res = _rows_vjp(_f_gmlp, [xg], list(p["sg"]), [doc], diff=[True], grad_dtypes=[BF16], tile=256, name=f"{tag}_gmlp_bwd")
    dxg, d_sg = res[0], res[1:]
    mix_in = [o for o, _ in swa] + [l for _, l in swa]
    d_mix = _rows_vjp(_f_swa_mix, mix_in, [], [dob], diff=[True] * 6, grad_dtypes=[F32] * 6, tile=512,
                      name=f"{tag}_swa_mix_bwd")
    acc = None
    for j, (wnd, dil) in enumerate(SWA_PATTERNS):
        acc = _swa_backward(xs, d_mix[j], d_mix[3 + j], acc, wnd, dil, name=f"{tag}_swa{j}_bwd")
    dxs = jnp.concatenate([a.astype(BF16) for a in acc], axis=1)
    res, job_out = _dn_backward(xq, xz, xba, p["dn"], s_all, doa, name=f"{tag}_dn_bwd", job=job)
    (dxq, dxz, dxba), d_dn = res[:3], res[3:]
    w_parts = _split_w_in(w_in)
    dn = None
    dws = []
    for j, (dx, w) in enumerate(zip((dxq, dxz, dxba, dxs, dxg), w_parts, strict=True)):
        dn = _mm(dx, w, NT, out_dtype=F32, res=dn, name=f"{tag}_in{j}_dx")
        dws.append(_mm(n, dx, TN, out_dtype=BF16, name=f"{tag}_in{j}_dw"))
    dws[2] = dws[2][:, :2 * DN_HEADS]
    dw_in = jnp.concatenate(dws, axis=1)
    dh_in, dgain = _rows_vjp(_f_rms, [h], [p["mix_norm"]], [dn], diff=[True], grad_dtypes=[F32], tile=512, add=[dh],
                             name=f"{tag}_norm_bwd")
    return dh_in, dgain, dw_in, dw_out, d_dn, d_sg, job_out


def _xattn_fwd(h, mem, p, w_q, w_kv, w_o, tag):
    d = h.shape[1]
    n = _rows(_f_rms, [h], [p["xa_norm"]], [(d, BF16)], tile=512, name=f"{tag}_norm")[0]
    mn = _rows(_f_rms, [mem], [p["xa_mem_norm"]], [(d, BF16)], tile=512, name=f"{tag}_mem_norm")[0]
    q = _mm(n, w_q, NN, out_dtype=BF16, name=f"{tag}_q")
    kv = _mm(mn, w_kv, NN, out_dtype=BF16, name=f"{tag}_kv")
    o = _rows(_f_xattn, [q], [kv], [(d, BF16)], tile=256, name=f"{tag}_core")[0]
    out = _mm(o, w_o, NN, out_dtype=F32, res=h, name=f"{tag}_o")
    return out, (h, n, mn, q, kv, o)


def _xattn_bwd(dh, saved, mem, p, w_q, w_kv, w_o, tag):
    h, n, mn, q, kv, o = saved
    do = _mm(dh, w_o, NT, out_dtype=BF16, name=f"{tag}_o_dx")
    dw_o = _mm(o, dh, TN, out_dtype=BF16, name=f"{tag}_o_dw")
    dq, dkv = _rows_vjp(_f_xattn, [q], [kv], [do], diff=[True], grad_dtypes=[BF16], tile=256, name=f"{tag}_core_bwd")
    dn = _mm(dq, w_q, NT, out_dtype=F32, name=f"{tag}_q_dx")
    dw_q = _mm(n, dq, TN, out_dtype=BF16, name=f"{tag}_q_dw")
    dmn = _mm(dkv, w_kv, NT, out_dtype=F32, name=f"{tag}_kv_dx")
    dw_kv = _mm(mn, dkv, TN, out_dtype=BF16, name=f"{tag}_kv_dw")
    dmem_gain = _rows_vjp(_f_rms, [mem], [p["xa_mem_norm"]], [dmn], diff=[False], grad_dtypes=[], tile=512,
                          name=f"{tag}_mem_norm_bwd")[0]
    dh_in, dgain = _rows_vjp(_f_rms, [h], [p["xa_norm"]], [dn], diff=[True], grad_dtypes=[F32], tile=512, add=[dh],
                             name=f"{tag}_norm_bwd")
    return dh_in, dgain, dmem_gain, dw_q, dw_kv, dw_o


def kernel(x, mem, ffn1_norm, ffn1_w_gate_up, ffn1_w_down, mix_norm, mix_w_in, dn_conv_w, dn_a_log, dn_dt_bias, dn_out_norm, sg_norm_gain, sg_norm_bias, sg_w_spatial, sg_b_spatial, mix_w_out, xa_norm, xa_mem_norm, xa_w_q, xa_w_kv, xa_w_o, ffn2_norm, ffn2_w_gate_up, ffn2_w_down, final_norm, loss_target, m_ffn1_norm, m_ffn1_w_gate_up, m_ffn1_w_down, m_mix_norm, m_mix_w_in, m_dn_conv_w, m_dn_a_log, m_dn_dt_bias, m_dn_out_norm, m_sg_norm_gain, m_sg_norm_bias, m_sg_w_spatial, m_sg_b_spatial, m_mix_w_out, m_xa_norm, m_xa_mem_norm, m_xa_w_q, m_xa_w_kv, m_xa_w_o, m_ffn2_norm, m_ffn2_w_gate_up, m_ffn2_w_down, m_final_norm, v_ffn1_norm, v_ffn1_w_gate_up, v_ffn1_w_down, v_mix_norm, v_mix_w_in, v_dn_conv_w, v_dn_a_log, v_dn_dt_bias, v_dn_out_norm, v_sg_norm_gain, v_sg_norm_bias, v_sg_w_spatial, v_sg_b_spatial, v_mix_w_out, v_xa_norm, v_xa_mem_norm, v_xa_w_q, v_xa_w_kv, v_xa_w_o, v_ffn2_norm, v_ffn2_w_gate_up, v_ffn2_w_down, v_final_norm):
    args = dict(locals())
    wts = {k: args[k] for k in WEIGHTS}
    mom_m = {k: args["m_" + k] for k in WEIGHTS}
    mom_v = {k: args["v_" + k] for k in WEIGHTS}
    depth = ffn1_norm.shape[0]
    h = x[0]
    mem2 = mem[0]
    target = loss_target[0]

    assert depth == 2, "core c of a chip is responsible for layer c in the weight and gradient exchanges"
    kinds = {k: "stack" if k == "mix_w_in" else ("row" if BIG_AXIS[k] == 1 else "col") for k in BIG}
    shard = {k: wts[k].astype(BF16) for k in BIG}
    shard["dn_conv_w"] = dn_conv_w
    kinds["dn_conv_w"] = "stack"

    def fetch(names, layer):
        return _gather_job([shard[k] for k in names], [kinds[k] for k in names], layer)

    def arrived(names, whole):
        w = dict(zip(names, whole, strict=True))
        for k in ("mix_w_in", "dn_conv_w"):
            if k in w:
                w[k] = jnp.concatenate([w[k][j] for j in range(N_CHIPS)], axis=1)
        return w

    lw = [arrived(EARLY_W, _run_job(fetch(EARLY_W, 0), name="gather_l0_early")), {}]
    small = {k: wts[k] for k in SMALL}

    saved = []
    for i in range(depth):
        w = lw[i]
        p = _layer_params(small, i, w["dn_conv_w"])
        h, s1 = _ffn_fwd(h, p["ffn1_norm"], w["ffn1_w_gate_up"], w["ffn1_w_down"], f"l{i}_ffn1")
        job = fetch(LATE, i).join(fetch(EARLY_W, i + 1)) if i + 1 < depth else fetch(LATE, i)
        h, s2, fetched = _mixer_fwd(h, p, w["mix_w_in"], w["mix_w_out"], f"l{i}_mix", job=job)
        w.update(arrived(LATE, fetched[:len(LATE)]))
        if i + 1 < depth:
            lw[i + 1].update(arrived(EARLY_W, fetched[len(LATE):]))
        h, s3 = _xattn_fwd(h, mem2, p, w["xa_w_q"], w["xa_w_kv"], w["xa_w_o"], f"l{i}_xa")
        h, s4 = _ffn_fwd(h, p["ffn2_norm"], w["ffn2_w_gate_up"], w["ffn2_w_down"], f"l{i}_ffn2")
        saved.append((p, s1, s2, s3, s4))
    loss_part, dh, d_final = _loss_head(h, target, final_norm[None], name="loss_head")
    loss = lax.psum(loss_part[0, 0], ("x", "y", "c"))

    g_big = {k: [None] * depth for k in BIG}
    g_small = {k: [None] * depth for k in SMALL if k != "final_norm"}
    g_small["dn_conv_w"] = [None] * depth
    def layer_grad(k, i):
        g = g_big[k][i]
        if kinds[k] == "stack":
            n = wts[k].shape[2]
            g = jnp.stack([g[:, j * n:(j + 1) * n] for j in range(N_CHIPS)])
        return g

    def chip_sums(names, i, tag):
        mine = [layer_grad(k, i) for k in names]
        got = _core_to_core(mine, 1 - i, name=f"give_{tag}")
        add = lambda: [_sum2(a, b, BF16, name=f"sum_cores_{tag}_{k}") for k, a, b in zip(names, mine, got, strict=True)]
        return lax.cond(lax.axis_index("c") == i, add, lambda: list(mine))

    def scatter(names, sums, i):
        return _scatter_job(sums, [wts[k].shape[1:] for k in names], [kinds[k] for k in names], i)

    exchanges = []
    for i in reversed(range(depth)):
        p, s1, s2, s3, s4 = saved[i]
        w = lw[i]
        dh, dg, dw_gu, dw_d = _ffn_bwd(dh, s4, p["ffn2_norm"], w["ffn2_w_gate_up"], w["ffn2_w_down"], f"l{i}_ffn2")
        g_small["ffn2_norm"][i], g_big["ffn2_w_gate_up"][i], g_big["ffn2_w_down"][i] = dg[0], dw_gu, dw_d
        dh, dg, dmg, dw_q, dw_kv, dw_o = _xattn_bwd(dh, s3, mem2, p, w["xa_w_q"], w["xa_w_kv"], w["xa_w_o"], f"l{i}_xa")
        g_small["xa_norm"][i], g_small["xa_mem_norm"][i] = dg[0], dmg[0]
        g_big["xa_w_q"][i], g_big["xa_w_kv"][i], g_big["xa_w_o"][i] = dw_q, dw_kv, dw_o
        ready = [(LATE, i, f"l{i}_late")] + ([(EARLY, i + 1, f"l{i + 1}_early")] if i + 1 < depth else [])
        sums = [chip_sums(names, layer, tag) for names, layer, tag in ready]
        job = scatter(ready[0][0], sums[0], ready[0][1])
        for (names, layer, _), cs in zip(ready[1:], sums[1:], strict=True):
            job = job.join(scatter(names, cs, layer))
        dh, dg, dw_in, dw_out, d_dn, d_sg, landed = _mixer_bwd(dh, s2, p, w["mix_w_in"], w["mix_w_out"], f"l{i}_mix", job=job)
        for (names, layer, _), cs in zip(ready, sums, strict=True):
            exchanges.append((names, layer, cs, landed[:len(names)]))
            landed = landed[len(names):]
        g_small["mix_norm"][i], g_big["mix_w_in"][i], g_big["mix_w_out"][i] = dg[0], dw_in, dw_out
        g_small["dn_conv_w"][i] = d_dn[0]
        g_small["dn_a_log"][i] = d_dn[1][0, DN_HEADS:2 * DN_HEADS]
        g_small["dn_dt_bias"][i] = d_dn[2][0, DN_HEADS:2 * DN_HEADS]
        g_small["dn_out_norm"][i] = d_dn[3][0]
        g_small["sg_norm_gain"][i], g_small["sg_norm_bias"][i] = d_sg[0][0], d_sg[1][0]
        g_small["sg_w_spatial"][i], g_small["sg_b_spatial"][i] = d_sg[2], d_sg[3][:SG_GROUPS]
        dh, dg, dw_gu, dw_d = _ffn_bwd(dh, s1, p["ffn1_norm"], w["ffn1_w_gate_up"], w["ffn1_w_down"], f"l{i}_ffn1")
        g_small["ffn1_norm"][i], g_big["ffn1_w_gate_up"][i], g_big["ffn1_w_down"][i] = dg[0], dw_gu, dw_d
    grad_x = dh[None]
    g_small = {k: jnp.stack(v) for k, v in g_small.items()}
    g_small["final_norm"] = d_final[0]

    core = lax.axis_index("c")
    chip = 2 * lax.axis_index("x") + lax.axis_index("y")
    last = chip_sums(EARLY, 0, "l0_early")
    exchanges.append((EARLY, 0, last, _run_job(scatter(EARLY, last, 0), name="scatter_l0_early")))

    def own_quarter(k, g):
        if kinds[k] == "stack":
            return lax.dynamic_index_in_dim(g, chip, axis=0, keepdims=False)
        axis = BIG_AXIS[k] - 1
        n = wts[k].shape[BIG_AXIS[k]]
        return lax.dynamic_slice_in_dim(g, chip * n, n, axis=axis)

    finished = [{} for _ in range(depth)]
    for names, i, sums, landed in exchanges:
        tag = f"l{i}_{names[0]}"
        add = lambda: [_sum_slots(r, F32, name=f"sum_chips_l{i}_{k}", first=own_quarter(k, g))
                       for k, r, g in zip(names, landed, sums, strict=True)]
        parts = lax.cond(core == i, add, lambda: [jnp.zeros(wts[k].shape[1:], F32) for k in names])
        moved = _core_to_core(parts, i, name=f"final_{tag}")
        finished[i].update({k: jnp.where(core == i, a, b) for k, a, b in zip(names, parts, moved, strict=True)})
    g_fin = {k: jnp.stack([finished[i][k] for i in range(depth)]) for k in BIG}
    small_names = list(SMALL) + ["dn_conv_w"]
    small_shapes = [g_small[k].shape for k in small_names]
    small_sum = _sum_slots(_gather_all(_pack([g_small[k] for k in small_names], F32, 64), name="gather_small"), F32,
                           name="sum_small")
    gs = dict(zip(small_names, _unpack(small_sum, small_shapes), strict=True))
    n_conv = dn_conv_w.shape[2]
    gs["dn_conv_w"] = lax.dynamic_slice_in_dim(gs["dn_conv_w"], (2 * lax.axis_index("x") + lax.axis_index("y")) * n_conv,
                                               n_conv, axis=2)

    results = {}
    for k in BIG:
        shp = wts[k].shape
        two_d = lambda a, _s=shp: a.reshape(-1, _s[-1])
        res = _adamw(two_d(wts[k]), two_d(g_fin[k]), two_d(mom_m[k]), two_d(mom_v[k]), name=f"adamw_{k}")
        results[k] = [g_fin[k]] + [r.reshape(shp) for r in res]
    sm_shapes = [wts[k].shape for k in small_names]
    pk = lambda d: _pack([d[k] for k in small_names], F32, 64)
    res = [_unpack(r, sm_shapes) for r in _adamw(pk(wts), pk(gs), pk(mom_m), pk(mom_v), name="adamw_small")]
    for i, k in enumerate(small_names):
        results[k] = [gs[k]] + [res[j][i] for j in range(3)]

    out = [loss, grad_x]
    for j in range(4):
        out += [results[k][j] for k in WEIGHTS]
    return tuple(out)
```
